```python
import math
import jax, jax.numpy as jnp
from jax import lax
import numpy as np

D_MODEL = 1024
BATCH = 16
SEQ = 2048
DEPTH = 1
DEC_BATCH = 4
DEC_SEQ = 4096
PAST_LEN = 128

EPS = 1e-6
NEG_INF = -1e30
N_Q_HEADS = 16
N_KV_HEADS = 4
Q_PER_KV = N_Q_HEADS // N_KV_HEADS
HEAD_DIM = 64
ATTN_WIDTH = N_Q_HEADS * HEAD_DIM
KV_WIDTH = N_KV_HEADS * HEAD_DIM
WINDOW = 128
ATTN_BLOCK = 128
ROPE_THETA = 10000.0
D_INNER = 2 * D_MODEL
SSM_HEAD_DIM = 64
N_SSM_HEADS = D_INNER // SSM_HEAD_DIM
N_SSM_GROUPS = 4
HEADS_PER_GROUP = N_SSM_HEADS // N_SSM_GROUPS
D_STATE = 128
BC_WIDTH = N_SSM_GROUPS * D_STATE
CONV_DIM = D_INNER + 2 * BC_WIDTH
CONV_W = 7
CHUNK = 128
N_EXPERT_GROUPS = 4
EXPERTS_PER_GROUP = 8
N_EXPERTS = N_EXPERT_GROUPS * EXPERTS_PER_GROUP
TOP_K = 2
D_EXPERT = 512
ROW_BLOCK = 128
PROJ_SIZES = (ATTN_WIDTH, KV_WIDTH, KV_WIDTH, D_INNER, CONV_DIM, N_SSM_HEADS, N_SSM_HEADS, D_MODEL, D_MODEL)
D_PROJ = sum(PROJ_SIZES)

kernel_name = 'hybrid_swa_ssd_hmoe_encoder'


def rmsnorm(x, gain):
    xf = x.astype(jnp.float32)
    y = xf * lax.rsqrt(jnp.mean(xf * xf, axis=-1, keepdims=True) + EPS)
    return (y * gain.astype(jnp.float32)).astype(x.dtype)


def rope_tables(seq):
    inv = 1.0 / (ROPE_THETA ** (jnp.arange(0, HEAD_DIM, 2, dtype=jnp.float32) / HEAD_DIM))
    ang = jnp.arange(seq, dtype=jnp.float32)[:, None] * inv[None, :]
    return jnp.cos(ang), jnp.sin(ang)


def apply_rope(x, cos, sin):
    half = HEAD_DIM // 2
    shape = (1, x.shape[1]) + (1,) * (x.ndim - 3) + (half,)
    c = cos.reshape(shape)
    s = sin.reshape(shape)
    xf = x.astype(jnp.float32)
    x1, x2 = xf[..., :half], xf[..., half:]
    return jnp.concatenate([x1 * c - x2 * s, x2 * c + x1 * s], axis=-1).astype(x.dtype)


def window_attention(q, k, v, sink):
    b, s = q.shape[0], q.shape[1]
    nb = s // ATTN_BLOCK
    pad = ((0, 0), (ATTN_BLOCK, ATTN_BLOCK), (0, 0), (0, 0))
    kp = jnp.pad(k, pad)
    vp = jnp.pad(v, pad)
    scale = HEAD_DIM ** -0.5
    sink_f = sink.astype(jnp.float32)[None, :, :, None, None]
    key_off = jnp.arange(3 * ATTN_BLOCK) - ATTN_BLOCK
    q_off = jnp.arange(ATTN_BLOCK)

    def one_block(i):
        start = i * ATTN_BLOCK
        qb = lax.dynamic_slice_in_dim(q, start, ATTN_BLOCK, axis=1)
        kb = lax.dynamic_slice_in_dim(kp, start, 3 * ATTN_BLOCK, axis=1)
        vb = lax.dynamic_slice_in_dim(vp, start, 3 * ATTN_BLOCK, axis=1)
        sc = jnp.einsum('bqkgd,bskd->bkgqs', qb, kb).astype(jnp.float32) * scale
        kpos = start + key_off
        qpos = start + q_off
        valid = (jnp.abs(qpos[:, None] - kpos[None, :]) <= WINDOW) & (kpos >= 0)[None, :] & (kpos < s)[None, :]
        sc = jnp.where(valid, sc, NEG_INF)
        m = jnp.maximum(jnp.max(sc, axis=-1, keepdims=True), sink_f)
        p = jnp.exp(sc - m)
        p = p / (jnp.sum(p, axis=-1, keepdims=True) + jnp.exp(sink_f - m))
        return jnp.einsum('bkgqs,bskd->bqkgd', p.astype(vb.dtype), vb)

    out = lax.map(one_block, jnp.arange(nb))
    return jnp.moveaxis(out, 0, 1).reshape(b, s, ATTN_WIDTH)


def ssd_scan(xh, dt, a_head, bm, cm):
    b, s, g, e, p = xh.shape
    n = bm.shape[-1]
    c = s // CHUNK
    X = (xh * dt[..., None]).reshape(b, c, CHUNK, g, e, p)
    Bc = bm.reshape(b, c, CHUNK, g, n)
    Cc = cm.reshape(b, c, CHUNK, g, n)
    a = jnp.moveaxis((dt * a_head).reshape(b, c, CHUNK, g, e), 2, -1)
    a_cum = jnp.cumsum(a, axis=-1)
    diff = a_cum[..., :, None] - a_cum[..., None, :]
    lower = jnp.tril(jnp.ones((CHUNK, CHUNK), dtype=bool))
    decay = jnp.exp(jnp.where(lower, diff, -jnp.inf))
    cb = jnp.einsum('bclgn,bcsgn->bcgls', Cc, Bc)
    y_diag = jnp.einsum('bcgels,bcsgep->bclgep', cb[:, :, :, None] * decay, X)
    decay_to_end = jnp.exp(a_cum[..., -1:] - a_cum)
    chunk_states = jnp.einsum('bclgn,bcgel,bclgep->bcgepn', Bc, decay_to_end, X)
    chunk_decay = jnp.exp(a_cum[..., -1])

    def step(h, inp):
        st, dec = inp
        return dec[..., None, None] * h + st, h

    h0 = jnp.zeros((b, g, e, p, n), jnp.float32)
    _, h_in = lax.scan(step, h0, (jnp.moveaxis(chunk_states, 1, 0), jnp.moveaxis(chunk_decay, 1, 0)))
    h_in = jnp.moveaxis(h_in, 0, 1)
    y_off = jnp.einsum('bclgn,bcgepn,bcgel->bclgep', Cc, h_in, jnp.exp(a_cum))
    return (y_diag + y_off).reshape(b, s, g, e, p)


def ssd_branch(z, xbc, dt_f_raw, dt_b_raw, conv_w, conv_b, a_log_fwd, a_log_bwd,
               dt_bias_fwd, dt_bias_bwd, d_skip, ssm_norm):
    dtype = xbc.dtype
    b, s, _ = xbc.shape
    xbc = lax.conv_general_dilated(xbc, conv_w[:, None, :], window_strides=(1,),
                                   padding=[(CONV_W // 2, CONV_W // 2)],
                                   dimension_numbers=('NWC', 'WIO', 'NWC'),
                                   feature_group_count=CONV_DIM)
    xbc = jax.nn.silu((xbc + conv_b).astype(jnp.float32))
    xs = xbc[..., :D_INNER].reshape(b, s, N_SSM_GROUPS, HEADS_PER_GROUP, SSM_HEAD_DIM)
    bm = xbc[..., D_INNER:D_INNER + BC_WIDTH].reshape(b, s, N_SSM_GROUPS, D_STATE)
    cm = xbc[..., D_INNER + BC_WIDTH:].reshape(b, s, N_SSM_GROUPS, D_STATE)

    def direction(dt_raw, dt_bias, a_log, reverse):
        dt = jax.nn.softplus(dt_raw.astype(jnp.float32) + dt_bias.astype(jnp.float32))
        dt = dt.reshape(b, s, N_SSM_GROUPS, HEADS_PER_GROUP)
        a_head = -jnp.exp(a_log.astype(jnp.float32)).reshape(N_SSM_GROUPS, HEADS_PER_GROUP)
        if reverse:
            y = ssd_scan(jnp.flip(xs, 1), jnp.flip(dt, 1), a_head, jnp.flip(bm, 1), jnp.flip(cm, 1))
            return jnp.flip(y, 1)
        return ssd_scan(xs, dt, a_head, bm, cm)

    y = (direction(dt_f_raw, dt_bias_fwd, a_log_fwd, False)
         + direction(dt_b_raw, dt_bias_bwd, a_log_bwd, True)
         + d_skip.astype(jnp.float32).reshape(N_SSM_GROUPS, HEADS_PER_GROUP)[..., None] * xs)
    y = y.reshape(b, s, D_INNER) * jax.nn.silu(z.astype(jnp.float32))
    return rmsnorm(y, ssm_norm).astype(dtype)


def hier_moe(h, w_router_group, b_router_group, w_router_expert, b_router_expert, w_gate, w_up, w_down):
    b, s, d = h.shape
    t = b * s
    hf = h.reshape(t, d)
    g_prob = jax.nn.softmax((hf @ w_router_group).astype(jnp.float32) + b_router_group.astype(jnp.float32), axis=-1)
    g_idx = jnp.argmax(g_prob, axis=-1)
    g_w = jnp.max(g_prob, axis=-1)
    e_logits = ((hf @ w_router_expert).astype(jnp.float32) + b_router_expert.astype(jnp.float32))
    e_logits = e_logits.reshape(t, N_EXPERT_GROUPS, EXPERTS_PER_GROUP)[jnp.arange(t), g_idx]
    e_prob = jax.nn.softmax(e_logits, axis=-1)
    top_v, top_i = lax.top_k(e_prob, TOP_K)
    weights = g_w[:, None] * top_v / jnp.sum(top_v, axis=-1, keepdims=True)
    expert = g_idx[:, None] * EXPERTS_PER_GROUP + top_i
    flat_e = expert.reshape(-1).astype(jnp.int32)
    flat_w = weights.reshape(-1)
    flat_tok = jnp.repeat(jnp.arange(t, dtype=jnp.int32), TOP_K)
    counts = jnp.bincount(flat_e, length=N_EXPERTS)
    padded = ((counts + ROW_BLOCK - 1) // ROW_BLOCK) * ROW_BLOCK
    pad_end = jnp.cumsum(padded)
    pad_start = pad_end - padded
    raw_start = jnp.cumsum(counts) - counts
    order = jnp.argsort(flat_e)
    sorted_e = flat_e[order]
    dest = pad_start[sorted_e] + jnp.arange(t * TOP_K) - raw_start[sorted_e]
    n_blocks = -(-(t * TOP_K) // ROW_BLOCK) + N_EXPERTS
    n_rows = n_blocks * ROW_BLOCK
    row_tok = jnp.full((n_rows,), t, jnp.int32).at[dest].set(flat_tok[order])
    row_w = jnp.zeros((n_rows,), jnp.float32).at[dest].set(flat_w[order])
    block_e = jnp.minimum(jnp.searchsorted(pad_end, jnp.arange(n_blocks) * ROW_BLOCK, side='right'), N_EXPERTS - 1)
    h_pad = jnp.concatenate([hf, jnp.zeros((1, d), hf.dtype)], axis=0)
    rows = h_pad[row_tok].reshape(n_blocks, ROW_BLOCK, d)

    def expert_block(args):
        xb, e = args
        return (jax.nn.silu(xb @ w_gate[e]) * (xb @ w_up[e])) @ w_down[e]

    out = lax.map(expert_block, (rows, block_e)).reshape(n_rows, d)
    y = jnp.zeros((t + 1, d), jnp.float32).at[row_tok].add(out.astype(jnp.float32) * row_w[:, None])
    return y[:t].reshape(b, s, d).astype(h.dtype)


def encoder_layer(x, cos, sin, norm_mix, w_in, q_norm, k_norm, attn_sink, conv_w, conv_b,
                  a_log_fwd, a_log_bwd, dt_bias_fwd, dt_bias_bwd, d_skip, ssm_norm,
                  w_out_attn, w_out_ssm, w_o, norm_ffn, w_router_group, b_router_group,
                  w_router_expert, b_router_expert, w_gate, w_up, w_down):
    b, s, _ = x.shape
    h = rmsnorm(x, norm_mix)
    proj = h @ w_in
    split_points = np.cumsum(PROJ_SIZES)[:-1].tolist()
    q, k, v, z, xbc, dt_f, dt_b, gate_a, gate_s = jnp.split(proj, split_points, axis=-1)
    q = q.reshape(b, s, N_KV_HEADS, Q_PER_KV, HEAD_DIM)
    k = k.reshape(b, s, N_KV_HEADS, HEAD_DIM)
    v = v.reshape(b, s, N_KV_HEADS, HEAD_DIM)
    q = apply_rope(rmsnorm(q, q_norm), cos, sin)
    k = apply_rope(rmsnorm(k, k_norm), cos, sin)
    attn = window_attention(q, k, v, attn_sink.reshape(N_KV_HEADS, Q_PER_KV))
    a_out = attn @ w_out_attn
    ssm = ssd_branch(z, xbc, dt_f, dt_b, conv_w, conv_b, a_log_fwd, a_log_bwd,
                     dt_bias_fwd, dt_bias_bwd, d_skip, ssm_norm)
    s_out = ssm @ w_out_ssm
    merged = jax.nn.sigmoid(gate_a) * a_out + jax.nn.sigmoid(gate_s) * s_out
    x = x + merged @ w_o
    x = x + hier_moe(rmsnorm(x, norm_ffn), w_router_group, b_router_group, w_router_expert,
                     b_router_expert, w_gate, w_up, w_down)
    return x


def encoder_trunk(x, weights):
    cos, sin = rope_tables(x.shape[1])
    for layer in range(DEPTH):
        x = encoder_layer(x, cos, sin, *[w[layer] for w in weights])
    return x


def setup_inputs(seed: int = 0) -> dict:
    key = jax.random.key(seed)
    ks = jax.random.split(key, 32)
    f32 = jnp.float32
    L = DEPTH

    def nrm(k, shape, scale):
        return jax.random.normal(k, shape, f32) * scale

    def gain(k, shape):
        return 1.0 + 0.02 * jax.random.normal(k, shape, f32)

    def dt_bias(k):
        dt = jnp.exp(jax.random.uniform(k, (L, N_SSM_HEADS), f32, math.log(1e-3), math.log(1e-1)))
        return dt + jnp.log(-jnp.expm1(-dt))

    return {
        'x_prompt': nrm(ks[0], (BATCH, SEQ, D_MODEL), 1.0),
        'x_sample': nrm(ks[1], (DEC_BATCH, DEC_SEQ, D_MODEL), 1.0),
        'norm_mix': gain(ks[2], (L, D_MODEL)),
        'w_in': nrm(ks[3], (L, D_MODEL, D_PROJ), D_MODEL ** -0.5),
        'q_norm': gain(ks[4], (L, HEAD_DIM)),
        'k_norm': gain(ks[5], (L, HEAD_DIM)),
        'attn_sink': nrm(ks[6], (L, N_Q_HEADS), 0.5),
        'conv_w': nrm(ks[7], (L, CONV_W, CONV_DIM), CONV_W ** -0.5),
        'conv_b': nrm(ks[8], (L, CONV_DIM), 0.02),
        'a_log_fwd': jnp.log(jax.random.uniform(ks[9], (L, N_SSM_HEADS), f32, 1.0, 16.0)),
        'a_log_bwd': jnp.log(jax.random.uniform(ks[10], (L, N_SSM_HEADS), f32, 1.0, 16.0)),
        'dt_bias_fwd': dt_bias(ks[11]),
        'dt_bias_bwd': dt_bias(ks[12]),
        'd_skip': gain(ks[13], (L, N_SSM_HEADS)),
        'ssm_norm': gain(ks[14], (L, D_INNER)),
        'w_out_attn': nrm(ks[15], (L, ATTN_WIDTH, D_MODEL), ATTN_WIDTH ** -0.5),
        'w_out_ssm': nrm(ks[16], (L, D_INNER, D_MODEL), D_INNER ** -0.5),
        'w_o': nrm(ks[17], (L, D_MODEL, D_MODEL), D_MODEL ** -0.5),
        'norm_ffn': gain(ks[18], (L, D_MODEL)),
        'w_router_group': nrm(ks[19], (L, D_MODEL, N_EXPERT_GROUPS), D_MODEL ** -0.5),
        'b_router_group': nrm(ks[20], (L, N_EXPERT_GROUPS), 0.01),
        'w_router_expert': nrm(ks[21], (L, D_MODEL, N_EXPERTS), D_MODEL ** -0.5),
        'b_router_expert': nrm(ks[22], (L, N_EXPERTS), 0.01),
        'w_gate': nrm(ks[23], (L, N_EXPERTS, D_MODEL, D_EXPERT), D_MODEL ** -0.5),
        'w_up': nrm(ks[24], (L, N_EXPERTS, D_MODEL, D_EXPERT), D_MODEL ** -0.5),
        'w_down': nrm(ks[25], (L, N_EXPERTS, D_EXPERT, D_MODEL), D_EXPERT ** -0.5),
    }


def reference(x_prompt, x_sample, norm_mix, w_in, q_norm, k_norm, attn_sink, conv_w, conv_b,
              a_log_fwd, a_log_bwd, dt_bias_fwd, dt_bias_bwd, d_skip, ssm_norm,
              w_out_attn, w_out_ssm, w_o, norm_ffn, w_router_group, b_router_group,
              w_router_expert, b_router_expert, w_gate, w_up, w_down):
    weights = (norm_mix, w_in, q_norm, k_norm, attn_sink, conv_w, conv_b,
               a_log_fwd, a_log_bwd, dt_bias_fwd, dt_bias_bwd, d_skip, ssm_norm,
               w_out_attn, w_out_ssm, w_o, norm_ffn, w_router_group, b_router_group,
               w_router_expert, b_router_expert, w_gate, w_up, w_down)
    y_prompt = encoder_trunk(x_prompt, weights)
    y_sample = encoder_trunk(x_sample, weights)
    return (y_prompt, y_sample)
```

```python
import functools

import numpy as np
import jax
import jax.numpy as jnp
from jax import lax
from jax.experimental import pallas as pl
from jax.experimental.pallas import tpu as pltpu

F32 = jnp.float32
BF16 = jnp.bfloat16

D_MODEL = 1024
EPS = 1e-6
NEG_INF = -1e30
N_Q_HEADS = 16
N_KV_HEADS = 4
HEAD_DIM = 64
ATTN_WIDTH = N_Q_HEADS * HEAD_DIM
KV_WIDTH = N_KV_HEADS * HEAD_DIM
ATTN_BLOCK = 128
ROPE_THETA = 10000.0
D_INNER = 2 * D_MODEL
SSM_HEAD_DIM = 64
N_SSM_HEADS = D_INNER // SSM_HEAD_DIM
N_SSM_GROUPS = 4
HEADS_PER_GROUP = N_SSM_HEADS // N_SSM_GROUPS
D_STATE = 128
BC_WIDTH = N_SSM_GROUPS * D_STATE
CONV_DIM = D_INNER + 2 * BC_WIDTH
CONV_W = 7
CHUNK = 128
N_EXPERT_GROUPS = 4
EXPERTS_PER_GROUP = 8
N_EXPERTS = N_EXPERT_GROUPS * EXPERTS_PER_GROUP
D_EXPERT = 512
N_PAIRS = EXPERTS_PER_GROUP * (EXPERTS_PER_GROUP - 1) // 2
N_CLASSES = N_EXPERT_GROUPS * N_PAIRS

LANES = 128
V7X_VMEM_LIMIT_BYTES = 56 * 1024 * 1024

COL_Z = 0
COL_GATE = COL_Z + D_INNER
COL_XS = COL_GATE + 2 * D_MODEL
COL_B = COL_XS + D_INNER
COL_C = COL_B + BC_WIDTH
COL_Q = COL_C + BC_WIDTH
COL_K = COL_Q + ATTN_WIDTH
COL_V = COL_K + KV_WIDTH
COL_DT = COL_V + KV_WIDTH
N_PROJ = COL_DT + LANES

TM_IN = 1024
NJ_IN = 3
TN_IN = N_PROJ // NJ_IN
CH_IN = 512
TM_QK = 512
TM_OUT = 512
CONV_CT = 256
CONV_ROWS = 256
ROW_BLOCK = 128
X2_WIDTH = D_MODEL + LANES


def _params(sem):
    return pltpu.CompilerParams(dimension_semantics=sem, vmem_limit_bytes=V7X_VMEM_LIMIT_BYTES)


def _dot(a, b):
    return jnp.dot(a, b, preferred_element_type=F32)


def _dot_nt(a, b):
    return lax.dot_general(a, b, (((1,), (1,)), ((), ())), preferred_element_type=F32)


def _sigmoid(x):
    return 1.0 / (1.0 + jnp.exp(-x))


def _inproj_kernel(x_ref, g_ref, w_ref, o_ref, dt_ref, h_ref):
    j = pl.program_id(1)

    @pl.when(j == 0)
    def _():
        x = x_ref[...]
        ms = jnp.mean(x * x, axis=-1, keepdims=True)
        h_ref[...] = (x * lax.rsqrt(ms + EPS) * g_ref[...]).astype(BF16)

    for c0 in range(0, TN_IN, CH_IN):
        c1 = min(c0 + CH_IN, TN_IN)
        acc = _dot(h_ref[...], w_ref[:, c0:c1])
        o_ref[:, c0:c1] = acc.astype(BF16)
        if c1 == TN_IN:
            @pl.when(j == NJ_IN - 1)
            def _():
                dt_ref[...] = acc[:, c1 - c0 - LANES:]


def _inproj(x2d, gain, w_bf16):
    t = x2d.shape[0]
    return pl.pallas_call(
        _inproj_kernel,
        grid=(t // TM_IN, NJ_IN),
        in_specs=[
            pl.BlockSpec((TM_IN, D_MODEL), lambda i, j: (i, 0)),
            pl.BlockSpec((1, D_MODEL), lambda i, j: (0, 0)),
            pl.BlockSpec((D_MODEL, TN_IN), lambda i, j: (0, j)),
        ],
        out_specs=[
            pl.BlockSpec((TM_IN, TN_IN), lambda i, j: (i, j)),
            pl.BlockSpec((TM_IN, LANES), lambda i, j: (i, 0)),
        ],
        out_shape=[
            jax.ShapeDtypeStruct((t, N_PROJ), BF16),
            jax.ShapeDtypeStruct((t, LANES), F32),
        ],
        scratch_shapes=[pltpu.VMEM((TM_IN, D_MODEL), BF16)],
        compiler_params=_params(("arbitrary", "arbitrary")),
        name="inproj",
    )(x2d, gain, w_bf16)


def _qkprep_kernel(q_ref, k_ref, v_ref, cos_ref, sin_ref, qg_ref, kg_ref, seg_ref, qo_ref, ko_ref, vo_ref):
    cos = cos_ref[...]
    sin = sin_ref[...]
    seg = seg_ref[...]
    lane = lax.broadcasted_iota(jnp.int32, (TM_QK, LANES), 1)
    first_half = (lane % HEAD_DIM) < (HEAD_DIM // 2)
    low = lane < HEAD_DIM

    def norm_rope(x, gain):
        ss = _dot((x * x).astype(BF16), seg)
        xn = x * lax.rsqrt(ss * (1.0 / HEAD_DIM) + EPS) * gain
        rot = jnp.where(first_half, pltpu.roll(xn, 96, 1), pltpu.roll(xn, 32, 1))
        return xn * cos + rot * sin

    for s in range(ATTN_WIDTH // LANES):
        sl = slice(s * LANES, (s + 1) * LANES)
        y = norm_rope(q_ref[:, sl].astype(F32), qg_ref[...]) * (HEAD_DIM ** -0.5)
        qo_ref[:, sl] = y.astype(BF16)
    for s in range(KV_WIDTH // LANES):
        sl = slice(s * LANES, (s + 1) * LANES)
        y = norm_rope(k_ref[:, sl].astype(F32), kg_ref[...])
        ysw = pltpu.roll(y, HEAD_DIM, 1)
        ko_ref[:, (2 * s) * LANES:(2 * s + 1) * LANES] = jnp.where(low, y, ysw).astype(BF16)
        ko_ref[:, (2 * s + 1) * LANES:(2 * s + 2) * LANES] = jnp.where(low, ysw, y).astype(BF16)
        v = v_ref[:, sl].astype(F32)
        vsw = pltpu.roll(v, HEAD_DIM, 1)
        vo_ref[:, (2 * s) * LANES:(2 * s + 1) * LANES] = jnp.where(low, v, vsw).astype(BF16)
        vo_ref[:, (2 * s + 1) * LANES:(2 * s + 2) * LANES] = jnp.where(low, vsw, v).astype(BF16)


def _qkprep(proj, cos128, sin128, qg128, kg128, seg, seq):
    t = proj.shape[0]
    nseq = seq // TM_QK
    return pl.pallas_call(
        _qkprep_kernel,
        grid=(t // TM_QK,),
        in_specs=[
            pl.BlockSpec((TM_QK, ATTN_WIDTH), lambda i: (i, COL_Q // ATTN_WIDTH)),
            pl.BlockSpec((TM_QK, KV_WIDTH), lambda i: (i, COL_K // KV_WIDTH)),
            pl.BlockSpec((TM_QK, KV_WIDTH), lambda i: (i, COL_V // KV_WIDTH)),
            pl.BlockSpec((TM_QK, LANES), lambda i: (i % nseq, 0)),
            pl.BlockSpec((TM_QK, LANES), lambda i: (i % nseq, 0)),
            pl.BlockSpec((1, LANES), lambda i: (0, 0)),
            pl.BlockSpec((1, LANES), lambda i: (0, 0)),
            pl.BlockSpec((LANES, LANES), lambda i: (0, 0)),
        ],
        out_specs=[
            pl.BlockSpec((TM_QK, ATTN_WIDTH), lambda i: (i, 0)),
            pl.BlockSpec((TM_QK, 2 * KV_WIDTH), lambda i: (i, 0)),
            pl.BlockSpec((TM_QK, 2 * KV_WIDTH), lambda i: (i, 0)),
        ],
        out_shape=[
            jax.ShapeDtypeStruct((t, ATTN_WIDTH), BF16),
            jax.ShapeDtypeStruct((t, 2 * KV_WIDTH), BF16),
            jax.ShapeDtypeStruct((t, 2 * KV_WIDTH), BF16),
        ],
        compiler_params=_params(("arbitrary",)),
        name="qkprep",
    )(proj, proj, proj, cos128, sin128, qg128, kg128, seg)


def _attn_kernel(sink_ref, q_ref, kp_ref, kc_ref, kn_ref, vp_ref, vc_ref, vn_ref, o_ref, *, nq):
    i = pl.program_id(1)
    nk = 3 * ATTN_BLOCK
    qi = lax.broadcasted_iota(jnp.int32, (ATTN_BLOCK, nk), 0)
    si = lax.broadcasted_iota(jnp.int32, (ATTN_BLOCK, nk), 1)
    rel = qi - (si - ATTN_BLOCK)
    band = jnp.where(rel <= ATTN_BLOCK, jnp.where(rel >= -ATTN_BLOCK, 0.0, NEG_INF), NEG_INF)
    band = jnp.where(si < ATTN_BLOCK, jnp.where(i > 0, band, NEG_INF), band)
    bias = jnp.where(si >= 2 * ATTN_BLOCK, jnp.where(i < nq - 1, band, NEG_INF), band)
    low_q = lax.broadcasted_iota(jnp.int32, (ATTN_BLOCK, LANES), 1) < HEAD_DIM
    low_k = lax.broadcasted_iota(jnp.int32, (nk, LANES), 1) < HEAD_DIM
    zero_q = jnp.zeros((ATTN_BLOCK, LANES), BF16)
    zero_k = jnp.zeros((nk, LANES), BF16)

    def soft(qm, kd, head):
        s = _dot_nt(qm, kd) + bias
        snk = sink_ref[head]
        m = jnp.maximum(jnp.max(s, axis=-1, keepdims=True), snk)
        p = jnp.exp(s - m)
        den = jnp.sum(p, axis=-1, keepdims=True) + jnp.exp(snk - m)
        return p.astype(BF16), 1.0 / den

    for h in range(N_KV_HEADS):
        sl = slice(h * LANES, (h + 1) * LANES)
        kd = jnp.concatenate([kp_ref[:, sl], kc_ref[:, sl], kn_ref[:, sl]], axis=0)
        vd = jnp.concatenate([vp_ref[:, sl], vc_ref[:, sl], vn_ref[:, sl]], axis=0)
        v_lo = jnp.where(low_k, vd, zero_k)
        v_hi = jnp.where(low_k, zero_k, vd)
        for j in range(2):
            col = slice((2 * h + j) * LANES, (2 * h + j + 1) * LANES)
            slab = q_ref[:, col]
            pa, ia = soft(jnp.where(low_q, slab, zero_q), kd, 4 * h + 2 * j)
            pb, ib = soft(jnp.where(low_q, zero_q, slab), kd, 4 * h + 2 * j + 1)
            o = _dot(pa, v_lo) + _dot(pb, v_hi)
            o_ref[:, col] = (o * jnp.where(low_q, ia, ib)).astype(BF16)


def _attention(qr, kdup, vdup, sink, batch, seq):
    t = qr.shape[0]
    nq = seq // ATTN_BLOCK

    def prev(b, i, s):
        return (b * nq + jnp.maximum(i - 1, 0), 0)

    def cur(b, i, s):
        return (b * nq + i, 0)

    def nxt(b, i, s):
        return (b * nq + jnp.minimum(i + 1, nq - 1), 0)

    kv_block = (ATTN_BLOCK, 2 * KV_WIDTH)
    grid_spec = pltpu.PrefetchScalarGridSpec(
        num_scalar_prefetch=1,
        grid=(batch, nq),
        in_specs=[
            pl.BlockSpec((ATTN_BLOCK, ATTN_WIDTH), cur),
            pl.BlockSpec(kv_block, prev), pl.BlockSpec(kv_block, cur), pl.BlockSpec(kv_block, nxt),
            pl.BlockSpec(kv_block, prev), pl.BlockSpec(kv_block, cur), pl.BlockSpec(kv_block, nxt),
        ],
        out_specs=pl.BlockSpec((ATTN_BLOCK, ATTN_WIDTH), cur),
    )
    return pl.pallas_call(
        functools.partial(_attn_kernel, nq=nq),
        grid_spec=grid_spec,
        out_shape=jax.ShapeDtypeStruct((t, ATTN_WIDTH), BF16),
        compiler_params=_params(("arbitrary", "arbitrary")),
        name="attention",
    )(sink, qr, kdup, kdup, kdup, vdup, vdup, vdup)


def _conv_kernel(x_ref, w_ref, b_ref, o_ref, pad_ref, *, seq):
    halo = 8
    pad_ref[0:halo, :] = jnp.zeros((halo, CONV_CT), F32)
    pad_ref[seq + halo:seq + 2 * halo, :] = jnp.zeros((halo, CONV_CT), F32)

    def load(r, _):
        r0 = pl.multiple_of(r * CONV_ROWS, CONV_ROWS)
        pad_ref[pl.ds(r0 + halo, CONV_ROWS), :] = x_ref[pl.ds(r0, CONV_ROWS), :].astype(F32)
        return 0

    lax.fori_loop(0, seq // CONV_ROWS, load, 0)
    w = w_ref[...]
    bias = b_ref[...]

    for r in range(seq // CONV_ROWS):
        r0 = r * CONV_ROWS
        acc = jnp.broadcast_to(bias, (CONV_ROWS, CONV_CT))
        for k in range(CONV_W):
            acc = acc + pad_ref[r0 + halo + k - CONV_W // 2:r0 + halo + k - CONV_W // 2 + CONV_ROWS, :] * w[k:k + 1, :]
        o_ref[r0:r0 + CONV_ROWS, :] = (acc * _sigmoid(acc)).astype(BF16)


def _conv(proj, conv_w, conv_b, batch, seq):
    t = proj.shape[0]
    return pl.pallas_call(
        functools.partial(_conv_kernel, seq=seq),
        grid=(batch, CONV_DIM // CONV_CT),
        in_specs=[
            pl.BlockSpec((seq, CONV_CT), lambda b, c: (b, COL_XS // CONV_CT + c)),
            pl.BlockSpec((CONV_W, CONV_CT), lambda b, c: (0, c)),
            pl.BlockSpec((1, CONV_CT), lambda b, c: (0, c)),
        ],
        out_specs=pl.BlockSpec((seq, CONV_CT), lambda b, c: (b, c)),
        out_shape=jax.ShapeDtypeStruct((t, CONV_DIM), BF16),
        scratch_shapes=[pltpu.VMEM((seq + 16, CONV_CT), F32)],
        compiler_params=_params(("arbitrary", "arbitrary")),
        name="conv",
    )(proj, conv_w, conv_b)


def _split3(a):
    a1 = a.astype(BF16)
    r1 = a - a1.astype(F32)
    a2 = r1.astype(BF16)
    a3 = (r1 - a2.astype(F32)).astype(BF16)
    return a1, a2, a3


def _tri_matmul(tri, a):
    a1, a2, a3 = _split3(a)
    return _dot(tri, a1) + _dot(tri, a2) + _dot(tri, a3)


def _softplus(x):
    return jnp.maximum(x, 0.0) + jnp.log(1.0 + jnp.exp(-jnp.abs(x)))


def _dt_and_rate(dt_ref, bias_ref, alog_ref):
    dt = _softplus(dt_ref[...] + bias_ref[...])
    rate = dt * (-jnp.exp(alog_ref[...]))
    return dt, rate


def _head_rows(mat, first, rows):
    n = mat.shape[1]
    return jnp.concatenate(
        [jnp.broadcast_to(mat[first + e:first + e + 1, :], (rows, n)) for e in range(HEADS_PER_GROUP)], axis=0)


def _ssd_bwd_state_kernel(xs_ref, b_ref, dt_ref, bias_ref, alog_ref, tl_ref, hb_ref, st_ref):
    c = pl.program_id(1)

    @pl.when(c == 0)
    def _():
        st_ref[...] = jnp.zeros_like(st_ref)

    hb_ref[0] = st_ref[...].astype(BF16)
    dt, rate = _dt_and_rate(dt_ref, bias_ref, alog_ref)
    pre = _tri_matmul(tl_ref[...], rate)
    pre_t = pre.T
    excl_t = (pre - rate).T
    total = jnp.broadcast_to(pre_t[:, CHUNK - 1:CHUNK], (LANES, CHUNK))
    w_t = dt.T * jnp.exp(excl_t)
    dec = jnp.exp(total)
    off = N_SSM_HEADS
    for g in range(N_SSM_GROUPS):
        xs_t = xs_ref[:, g * 512:(g + 1) * 512].astype(F32).T
        xd = (xs_t * _head_rows(w_t, off + g * HEADS_PER_GROUP, SSM_HEAD_DIM)).astype(BF16)
        upd = _dot(xd, b_ref[:, g * D_STATE:(g + 1) * D_STATE])
        st_ref[g] = _head_rows(dec, off + g * HEADS_PER_GROUP, SSM_HEAD_DIM) * st_ref[g] + upd


def _ssd_bwd_states(xc, dt, bias128, alog128, tri_l, batch, seq):
    nc = seq // CHUNK

    def rev(b, c):
        return (b * nc + nc - 1 - c, 0)

    return pl.pallas_call(
        _ssd_bwd_state_kernel,
        grid=(batch, nc),
        in_specs=[
            pl.BlockSpec((CHUNK, D_INNER), rev),
            pl.BlockSpec((CHUNK, BC_WIDTH), lambda b, c: (b * nc + nc - 1 - c, D_INNER // BC_WIDTH)),
            pl.BlockSpec((CHUNK, LANES), rev),
            pl.BlockSpec((1, LANES), lambda b, c: (0, 0)),
            pl.BlockSpec((1, LANES), lambda b, c: (0, 0)),
            pl.BlockSpec((CHUNK, CHUNK), lambda b, c: (0, 0)),
        ],
        out_specs=pl.BlockSpec((1, N_SSM_GROUPS, 512, D_STATE), lambda b, c: (b * nc + nc - 1 - c, 0, 0, 0)),
        out_shape=jax.ShapeDtypeStruct((batch * nc, N_SSM_GROUPS, 512, D_STATE), BF16),
        scratch_shapes=[pltpu.VMEM((N_SSM_GROUPS, 512, D_STATE), F32)],
        compiler_params=_params(("arbitrary", "arbitrary")),
        name="ssd_bwd_states",
    )(xc, xc, dt, bias128, alog128, tri_l)


def _ssd_main_kernel(xc_ref, z_ref, dt_ref, hb_ref, bias_ref, alog_ref, tl_ref, tu_ref, dskip_ref, gain_ref,
                     o_ref, hf_ref, y_ref):
    c = pl.program_id(1)

    @pl.when(c == 0)
    def _():
        hf_ref[...] = jnp.zeros_like(hf_ref)

    dt, rate = _dt_and_rate(dt_ref, bias_ref, alog_ref)
    lane = lax.broadcasted_iota(jnp.int32, (CHUNK, LANES), 1)
    cum = jnp.where(lane < N_SSM_HEADS, _tri_matmul(tl_ref[...], rate), _tri_matmul(tu_ref[...], rate))
    cum_t = cum.T
    dt_t = dt.T
    row = lax.broadcasted_iota(jnp.int32, (CHUNK, CHUNK), 0)
    col = lax.broadcasted_iota(jnp.int32, (CHUNK, CHUNK), 1)
    lower = row >= col
    upper = row <= col
    low = lane < SSM_HEAD_DIM
    zero_x = jnp.zeros((CHUNK, LANES), BF16)
    nb = N_SSM_HEADS

    def lane_bcast(mat, idx):
        return jnp.broadcast_to(mat[:, idx:idx + 1], (CHUNK, CHUNK))

    def sub_bcast(mat, idx):
        return jnp.broadcast_to(mat[idx:idx + 1, :], (CHUNK, CHUNK))

    def head_matrix(e, cb):
        col_f = lane_bcast(cum, e)
        col_b = lane_bcast(cum, nb + e)
        m_f = jnp.exp(jnp.where(lower, col_f - sub_bcast(cum_t, e), NEG_INF)) * sub_bcast(dt_t, e)
        m_b = jnp.exp(jnp.where(upper, col_b - sub_bcast(cum_t, nb + e), NEG_INF)) * sub_bcast(dt_t, nb + e)
        return ((m_f + m_b) * cb).astype(BF16), col_f, col_b

    for g in range(N_SSM_GROUPS):
        bg = xc_ref[:, D_INNER + g * D_STATE:D_INNER + (g + 1) * D_STATE]
        cg = xc_ref[:, D_INNER + BC_WIDTH + g * D_STATE:D_INNER + BC_WIDTH + (g + 1) * D_STATE]
        cb = _dot_nt(cg, bg)
        y_in_f = _dot_nt(cg, hf_ref[g].astype(BF16))
        y_in_b = _dot_nt(cg, hb_ref[0, g])
        for jp in range(HEADS_PER_GROUP // 2):
            e0 = g * HEADS_PER_GROUP + 2 * jp
            cols = slice(e0 * SSM_HEAD_DIM, e0 * SSM_HEAD_DIM + LANES)
            loc = slice(jp * LANES, (jp + 1) * LANES)
            xs_pair = xc_ref[:, cols]
            m0, cf0, cb0 = head_matrix(e0, cb)
            m1, cf1, cb1 = head_matrix(e0 + 1, cb)
            y = _dot(m0, jnp.where(low, xs_pair, zero_x)) + _dot(m1, jnp.where(low, zero_x, xs_pair))
            y = y + y_in_f[:, loc] * jnp.exp(jnp.where(low, cf0, cf1))
            y = y + y_in_b[:, loc] * jnp.exp(jnp.where(low, cb0, cb1))
            y_ref[:, cols] = y + dskip_ref[:, cols] * xs_pair.astype(F32)

    z = z_ref[...].astype(F32)
    y = y_ref[...] * (z * _sigmoid(z))
    ms = jnp.mean(y * y, axis=-1, keepdims=True)
    o_ref[...] = (y * lax.rsqrt(ms + EPS) * gain_ref[...]).astype(BF16)

    last = jnp.broadcast_to(cum_t[:, CHUNK - 1:CHUNK], (LANES, CHUNK))
    w_t = dt_t * jnp.exp(last - cum_t)
    dec = jnp.exp(last)
    for g in range(N_SSM_GROUPS):
        xs_t = xc_ref[:, g * 512:(g + 1) * 512].astype(F32).T
        xd = (xs_t * _head_rows(w_t, g * HEADS_PER_GROUP, SSM_HEAD_DIM)).astype(BF16)
        upd = _dot(xd, xc_ref[:, D_INNER + g * D_STATE:D_INNER + (g + 1) * D_STATE])
        hf_ref[g] = _head_rows(dec, g * HEADS_PER_GROUP, SSM_HEAD_DIM) * hf_ref[g] + upd


def _ssd_main(xc, proj, dt, hb, bias128, alog128, tri_l, tri_u, dskip, gain, batch, seq):
    t = xc.shape[0]
    nc = seq // CHUNK

    def tok(b, c):
        return (b * nc + c, 0)

    def const(b, c):
        return (0, 0)

    return pl.pallas_call(
        _ssd_main_kernel,
        grid=(batch, nc),
        in_specs=[
            pl.BlockSpec((CHUNK, CONV_DIM), tok),
            pl.BlockSpec((CHUNK, D_INNER), tok),
            pl.BlockSpec((CHUNK, LANES), tok),
            pl.BlockSpec((1, N_SSM_GROUPS, 512, D_STATE), lambda b, c: (b * nc + c, 0, 0, 0)),
            pl.BlockSpec((1, LANES), const),
            pl.BlockSpec((1, LANES), const),
            pl.BlockSpec((CHUNK, CHUNK), const),
            pl.BlockSpec((CHUNK, CHUNK), const),
            pl.BlockSpec((1, D_INNER), const),
            pl.BlockSpec((1, D_INNER), const),
        ],
        out_specs=pl.BlockSpec((CHUNK, D_INNER), tok),
        out_shape=jax.ShapeDtypeStruct((t, D_INNER), BF16),
        scratch_shapes=[pltpu.VMEM((N_SSM_GROUPS, 512, D_STATE), F32), pltpu.VMEM((CHUNK, D_INNER), F32)],
        compiler_params=_params(("arbitrary", "arbitrary")),
        name="ssd_main",
    )(xc, proj, dt, hb, bias128, alog128, tri_l, tri_u, dskip, gain)


def _outproj_kernel(attn_ref, ssm_ref, gate_ref, x_ref, wa_ref, ws_ref, wo_ref, gn_ref, wr1_ref, wr2_ref, br_ref,
                    o_ref):
    a_out = _dot(attn_ref[...], wa_ref[...])
    s_out = _dot(ssm_ref[...], ws_ref[...])
    ga = gate_ref[:, :D_MODEL].astype(F32)
    gs = gate_ref[:, D_MODEL:].astype(F32)
    merged = _sigmoid(ga) * a_out + _sigmoid(gs) * s_out
    x2 = x_ref[...] + _dot(merged.astype(BF16), wo_ref[...])
    o_ref[:, :D_MODEL] = x2

    ms = jnp.mean(x2 * x2, axis=-1, keepdims=True)
    hn = x2 * lax.rsqrt(ms + EPS) * gn_ref[...]
    h1 = hn.astype(BF16)
    h2 = (hn - h1.astype(F32)).astype(BF16)
    lg = _dot(h1, wr1_ref[...]) + _dot(h2, wr1_ref[...]) + _dot(h1, wr2_ref[...]) + br_ref[...]

    lane = lax.broadcasted_iota(jnp.int32, (TM_OUT, LANES), 1).astype(F32)
    big = float(LANES)

    def rmax(v):
        return jnp.max(v, axis=-1, keepdims=True)

    def first_lane(mask):
        return jnp.min(jnp.where(mask, lane, big), axis=-1, keepdims=True)

    gl = jnp.where(lane < N_EXPERT_GROUPS, lg, NEG_INF)
    gmax = rmax(gl)
    g_w = 1.0 / jnp.sum(jnp.exp(gl - gmax), axis=-1, keepdims=True)
    gidx = first_lane(gl == gmax)
    base = N_EXPERT_GROUPS + EXPERTS_PER_GROUP * gidx
    el = jnp.where(lane >= base, jnp.where(lane < base + EXPERTS_PER_GROUP, lg, NEG_INF), NEG_INF)
    m1 = rmax(el)
    i1 = first_lane(el == m1)
    el2 = jnp.where(lane == i1, NEG_INF, el)
    m2 = rmax(el2)
    i2 = first_lane(el2 == m2)
    r = jnp.exp(m2 - m1)
    w1 = g_w / (1.0 + r)
    w2 = w1 * r
    j1 = i1 - base
    j2 = i2 - base
    swap = j1 > j2
    e_lo = jnp.where(swap, j2, j1)
    e_hi = jnp.where(swap, j1, j2)
    w_lo = jnp.where(swap, w2, w1)
    w_hi = jnp.where(swap, w1, w2)
    pair = e_lo * (EXPERTS_PER_GROUP - 1) - e_lo * (e_lo - 1.0) * 0.5 + (e_hi - e_lo - 1.0)
    cls = gidx * N_PAIRS + pair
    o_ref[:, D_MODEL:] = jnp.where(lane == 0.0, cls, jnp.where(lane == 1.0, w_lo, jnp.where(lane == 2.0, w_hi, 0.0)))


def _outproj(attn, ssm, proj, x2d, wa, ws, wo, gn, wr1, wr2, br):
    t = x2d.shape[0]

    def tok(i):
        return (i, 0)

    def const(i):
        return (0, 0)

    return pl.pallas_call(
        _outproj_kernel,
        grid=(t // TM_OUT,),
        in_specs=[
            pl.BlockSpec((TM_OUT, ATTN_WIDTH), tok),
            pl.BlockSpec((TM_OUT, D_INNER), tok),
            pl.BlockSpec((TM_OUT, 2 * D_MODEL), lambda i: (i, COL_GATE // (2 * D_MODEL))),
            pl.BlockSpec((TM_OUT, D_MODEL), tok),
            pl.BlockSpec((ATTN_WIDTH, D_MODEL), const),
            pl.BlockSpec((D_INNER, D_MODEL), const),
            pl.BlockSpec((D_MODEL, D_MODEL), const),
            pl.BlockSpec((1, D_MODEL), const),
            pl.BlockSpec((D_MODEL, LANES), const),
            pl.BlockSpec((D_MODEL, LANES), const),
            pl.BlockSpec((1, LANES), const),
        ],
        out_specs=pl.BlockSpec((TM_OUT, X2_WIDTH), tok),
        out_shape=jax.ShapeDtypeStruct((t, X2_WIDTH), F32),
        compiler_params=_params(("arbitrary",)),
        name="outproj_router",
    )(attn, ssm, proj, x2d, wa, ws, wo, gn, wr1, wr2, br)


def _moe_kernel(ea_ref, eb_ref, nv_ref, tokc_ref, tokn_ref, xa_ref, xb_ref, gn_ref,
                wga_ref, wua_ref, wda_ref, wgb_ref, wub_ref, wdb_ref, oa_ref, ob_ref,
                xg_ref, st_ref, gsem, ssem, *, n_blocks, t_a):
    i = pl.program_id(0)
    slot = i % 2
    other = 1 - slot

    def gather_copy(tok, r, s, src_ref, row):
        return pltpu.make_async_copy(src_ref.at[pl.ds(row, 1), :], xg_ref.at[s, pl.ds(r, 1), :], gsem.at[s])

    def scatter_copy(r, s, dst_ref, row):
        return pltpu.make_async_copy(st_ref.at[s, pl.ds(r, 1), :], dst_ref.at[pl.ds(row, 1), :], ssem.at[s])

    def start_gathers(tok_ref, n, s):
        def body(r, _):
            tok = tok_ref[0, 0, r]

            @pl.when(tok < t_a)
            def _():
                gather_copy(tok, r, s, xa_ref, tok).start()

            @pl.when(tok >= t_a)
            def _():
                gather_copy(tok, r, s, xb_ref, tok - t_a).start()

            return 0

        lax.fori_loop(0, n, body, 0)

    def wait_gathers(n, s):
        def body(r, _):
            gather_copy(0, 0, s, xa_ref, 0).wait()
            return 0

        lax.fori_loop(0, n, body, 0)

    def start_scatters(tok_ref, n, s):
        def body(r, _):
            tok = tok_ref[0, 0, r]

            @pl.when(tok < t_a)
            def _():
                scatter_copy(r, s, oa_ref, tok).start()

            @pl.when(tok >= t_a)
            def _():
                scatter_copy(r, s, ob_ref, tok - t_a).start()

            return 0

        lax.fori_loop(0, n, body, 0)

    def wait_scatters(n, s):
        def body(r, _):
            scatter_copy(0, s, oa_ref, 0).wait()
            return 0

        lax.fori_loop(0, n, body, 0)

    @pl.when(i == 0)
    def _():
        xg_ref[...] = jnp.zeros_like(xg_ref)
        start_gathers(tokc_ref, nv_ref[0], 0)

    @pl.when(i + 1 < n_blocks)
    def _():
        start_gathers(tokn_ref, nv_ref[jnp.minimum(i + 1, n_blocks - 1)], other)

    wait_gathers(nv_ref[i], slot)

    @pl.when(i >= 2)
    def _():
        wait_scatters(nv_ref[jnp.maximum(i - 2, 0)], slot)

    @pl.when(nv_ref[i] > 0)
    def _():
        xr = xg_ref[slot]
        x = xr[:, :D_MODEL]
        w_lo = xr[:, D_MODEL + 1:D_MODEL + 2]
        w_hi = xr[:, D_MODEL + 2:D_MODEL + 3]
        ms = jnp.mean(x * x, axis=-1, keepdims=True)
        hn = (x * lax.rsqrt(ms + EPS) * gn_ref[...]).astype(BF16)

        def expert(wg_ref, wu_ref, wd_ref):
            gte = _dot(hn, wg_ref[0])
            up = _dot(hn, wu_ref[0])
            return _dot((gte * _sigmoid(gte) * up).astype(BF16), wd_ref[0])

        ya = expert(wga_ref, wua_ref, wda_ref)
        yb = expert(wgb_ref, wub_ref, wdb_ref)
        st_ref[slot] = x + w_lo * ya + w_hi * yb

    start_scatters(tokc_ref, nv_ref[i], slot)

    @pl.when(i == n_blocks - 1)
    def _():
        wait_scatters(nv_ref[jnp.maximum(i - 1, 0)], other)
        wait_scatters(nv_ref[i], slot)


def _moe(ea, eb, nvalid, row_tok, x2r_a, x2r_b, gn, wg, wu, wd):
    n_blocks = row_tok.shape[0]
    t_a = x2r_a.shape[0]
    t_b = x2r_b.shape[0]

    def wa(i, ea, eb, nv):
        return (ea[i], 0, 0)

    def wb(i, ea, eb, nv):
        return (eb[i], 0, 0)

    any_spec = pl.BlockSpec(memory_space=pl.ANY)
    grid_spec = pltpu.PrefetchScalarGridSpec(
        num_scalar_prefetch=3,
        grid=(n_blocks,),
        in_specs=[
            pl.BlockSpec((1, 1, ROW_BLOCK), lambda i, ea, eb, nv: (i, 0, 0), memory_space=pltpu.SMEM),
            pl.BlockSpec((1, 1, ROW_BLOCK), lambda i, ea, eb, nv: (jnp.minimum(i + 1, n_blocks - 1), 0, 0),
                         memory_space=pltpu.SMEM),
            any_spec, any_spec,
            pl.BlockSpec((1, D_MODEL), lambda i, ea, eb, nv: (0, 0)),
            pl.BlockSpec((1, D_MODEL, D_EXPERT), wa), pl.BlockSpec((1, D_MODEL, D_EXPERT), wa),
            pl.BlockSpec((1, D_EXPERT, D_MODEL), wa),
            pl.BlockSpec((1, D_MODEL, D_EXPERT), wb), pl.BlockSpec((1, D_MODEL, D_EXPERT), wb),
            pl.BlockSpec((1, D_EXPERT, D_MODEL), wb),
        ],
        out_specs=[any_spec, any_spec],
        scratch_shapes=[
            pltpu.VMEM((2, ROW_BLOCK, X2_WIDTH), F32),
            pltpu.VMEM((2, ROW_BLOCK, D_MODEL), F32),
            pltpu.SemaphoreType.DMA((2,)),
            pltpu.SemaphoreType.DMA((2,)),
        ],
    )
    return pl.pallas_call(
        functools.partial(_moe_kernel, n_blocks=n_blocks, t_a=t_a),
        grid_spec=grid_spec,
        out_shape=[jax.ShapeDtypeStruct((t_a, D_MODEL), F32), jax.ShapeDtypeStruct((t_b, D_MODEL), F32)],
        compiler_params=_params(("arbitrary",)),
        name="moe",
    )(ea, eb, nvalid, row_tok, row_tok, x2r_a, x2r_b, gn, wg, wu, wd, wg, wu, wd)


def _pair_tables():
    lo, hi = [], []
    for a in range(EXPERTS_PER_GROUP):
        for b in range(a + 1, EXPERTS_PER_GROUP):
            lo.append(a)
            hi.append(b)
    return np.asarray(lo, np.int32), np.asarray(hi, np.int32)


def _block_tables(cls):
    t = cls.shape[0]
    n_blocks = t // ROW_BLOCK + N_CLASSES
    sorted_cls, order = lax.sort((cls, jnp.arange(t, dtype=jnp.int32)), num_keys=1)
    starts = jnp.searchsorted(sorted_cls, jnp.arange(N_CLASSES + 1, dtype=jnp.int32), side='left').astype(jnp.int32)
    counts = starts[1:] - starts[:-1]
    nblk = (counts + ROW_BLOCK - 1) // ROW_BLOCK
    blk_end = jnp.cumsum(nblk)
    blk_start = blk_end - nblk
    used = blk_end[-1]
    b = jnp.arange(n_blocks, dtype=jnp.int32)
    b_eff = jnp.minimum(b, used - 1)
    c = jnp.minimum(jnp.searchsorted(blk_end, b_eff, side='right'), N_CLASSES - 1).astype(jnp.int32)
    off = b_eff - blk_start[c]
    src = starts[c] + off * ROW_BLOCK
    nvalid = jnp.where(b < used, jnp.clip(counts[c] - off * ROW_BLOCK, 0, ROW_BLOCK), 0).astype(jnp.int32)
    pair_lo, pair_hi = _pair_tables()
    grp = c // N_PAIRS
    ea = (grp * EXPERTS_PER_GROUP + jnp.asarray(pair_lo)[c % N_PAIRS]).astype(jnp.int32)
    eb = (grp * EXPERTS_PER_GROUP + jnp.asarray(pair_hi)[c % N_PAIRS]).astype(jnp.int32)
    rows = jnp.clip(src[:, None] + jnp.arange(ROW_BLOCK, dtype=jnp.int32)[None, :], 0, t - 1)
    row_tok = order[rows].reshape(n_blocks, 1, ROW_BLOCK)
    return ea, eb, nvalid, row_tok


def _rope_tables(seq):
    inv = 1.0 / (ROPE_THETA ** (jnp.arange(0, HEAD_DIM, 2, dtype=F32) / HEAD_DIM))
    ang = jnp.arange(seq, dtype=F32)[:, None] * inv[None, :]
    cos, sin = jnp.cos(ang), jnp.sin(ang)
    cos128 = jnp.concatenate([cos, cos, cos, cos], axis=-1)
    sin128 = jnp.concatenate([-sin, sin, -sin, sin], axis=-1)
    return cos128, sin128


def _prepare_weights(norm_mix, w_in, q_norm, k_norm, attn_sink, conv_w, conv_b, a_log_fwd, a_log_bwd,
                     dt_bias_fwd, dt_bias_bwd, d_skip, ssm_norm, w_out_attn, w_out_ssm, w_o, norm_ffn,
                     w_router_group, b_router_group, w_router_expert, b_router_expert, w_gate, w_up, w_down):
    o_q = 0
    o_k = o_q + ATTN_WIDTH
    o_v = o_k + KV_WIDTH
    o_z = o_v + KV_WIDTH
    o_xbc = o_z + D_INNER
    o_dtf = o_xbc + CONV_DIM
    o_dtb = o_dtf + N_SSM_HEADS
    o_ga = o_dtb + N_SSM_HEADS
    o_gs = o_ga + D_MODEL
    w = w_in
    w_r = jnp.concatenate([
        w[:, o_z:o_z + D_INNER], w[:, o_ga:o_gs + D_MODEL], w[:, o_xbc:o_xbc + CONV_DIM],
        w[:, o_q:o_q + ATTN_WIDTH], w[:, o_k:o_k + KV_WIDTH], w[:, o_v:o_v + KV_WIDTH],
        w[:, o_dtf:o_dtb + N_SSM_HEADS], jnp.zeros((D_MODEL, LANES - 2 * N_SSM_HEADS), w.dtype)], axis=1)
    pad64 = jnp.zeros((LANES - 2 * N_SSM_HEADS,), F32)
    eye = np.kron(np.eye(2, dtype=np.float32), np.ones((HEAD_DIM, HEAD_DIM), np.float32))
    idx = np.arange(CHUNK)
    w_router = jnp.concatenate([w_router_group, w_router_expert,
                                jnp.zeros((D_MODEL, LANES - N_EXPERT_GROUPS - N_EXPERTS), F32)], axis=1)
    wr1 = w_router.astype(BF16)
    return dict(
        norm_mix=norm_mix.reshape(1, D_MODEL),
        w_in=w_r.astype(BF16),
        qg128=jnp.tile(q_norm, 2).reshape(1, LANES),
        kg128=jnp.tile(k_norm, 2).reshape(1, LANES),
        seg=jnp.asarray(eye, BF16),
        sink=attn_sink.astype(F32),
        conv_w=conv_w,
        conv_b=conv_b.reshape(1, CONV_DIM),
        alog128=jnp.concatenate([a_log_fwd, a_log_bwd, pad64]).reshape(1, LANES),
        bias128=jnp.concatenate([dt_bias_fwd, dt_bias_bwd, pad64]).reshape(1, LANES),
        tri_l=jnp.asarray(idx[:, None] >= idx[None, :], BF16),
        tri_u=jnp.asarray(idx[:, None] <= idx[None, :], BF16),
        dskip=jnp.repeat(d_skip, SSM_HEAD_DIM).reshape(1, D_INNER),
        ssm_norm=ssm_norm.reshape(1, D_INNER),
        wa=w_out_attn.astype(BF16), ws=w_out_ssm.astype(BF16), wo=w_o.astype(BF16),
        norm_ffn=norm_ffn.reshape(1, D_MODEL),
        wr1=wr1, wr2=(w_router - wr1.astype(F32)).astype(BF16),
        br=jnp.concatenate([b_router_group, b_router_expert,
                            jnp.zeros((LANES - N_EXPERT_GROUPS - N_EXPERTS,), F32)]).reshape(1, LANES),
        wg=w_gate.astype(BF16), wu=w_up.astype(BF16), wd=w_down.astype(BF16),
    )


def _mixer(x, p):
    batch, seq, _ = x.shape
    x2d = x.reshape(batch * seq, D_MODEL)
    proj, dt = _inproj(x2d, p['norm_mix'], p['w_in'])
    cos128, sin128 = _rope_tables(seq)
    qr, kdup, vdup = _qkprep(proj, cos128, sin128, p['qg128'], p['kg128'], p['seg'], seq)
    attn = _attention(qr, kdup, vdup, p['sink'], batch, seq)
    xc = _conv(proj, p['conv_w'], p['conv_b'], batch, seq)
    hb = _ssd_bwd_states(xc, dt, p['bias128'], p['alog128'], p['tri_l'], batch, seq)
    ssm = _ssd_main(xc, proj, dt, hb, p['bias128'], p['alog128'], p['tri_l'], p['tri_u'], p['dskip'],
                    p['ssm_norm'], batch, seq)
    return _outproj(attn, ssm, proj, x2d, p['wa'], p['ws'], p['wo'], p['norm_ffn'], p['wr1'], p['wr2'], p['br'])


def kernel(x_prompt, x_sample, norm_mix, w_in, q_norm, k_norm, attn_sink, conv_w, conv_b, a_log_fwd, a_log_bwd,
           dt_bias_fwd, dt_bias_bwd, d_skip, ssm_norm, w_out_attn, w_out_ssm, w_o, norm_ffn, w_router_group,
           b_router_group, w_router_expert, b_router_expert, w_gate, w_up, w_down):
    assert norm_mix.shape[0] == 1, "single-layer encoder"
    p = _prepare_weights(norm_mix[0], w_in[0], q_norm[0], k_norm[0], attn_sink[0], conv_w[0], conv_b[0],
                         a_log_fwd[0], a_log_bwd[0], dt_bias_fwd[0], dt_bias_bwd[0], d_skip[0], ssm_norm[0],
                         w_out_attn[0], w_out_ssm[0], w_o[0], norm_ffn[0], w_router_group[0], b_router_group[0],
                         w_router_expert[0], b_router_expert[0], w_gate[0], w_up[0], w_down[0])
    x2r_a = _mixer(x_prompt, p)
    x2r_b = _mixer(x_sample, p)
    cls = jnp.concatenate([x2r_a[:, D_MODEL], x2r_b[:, D_MODEL]]).astype(jnp.int32)
    ea, eb, nvalid, row_tok = _block_tables(cls)
    y_a, y_b = _moe(ea, eb, nvalid, row_tok, x2r_a, x2r_b, p['norm_ffn'], p['wg'], p['wu'], p['wd'])
    return y_a.reshape(x_prompt.shape), y_b.reshape(x_sample.shape)
```

```python
import functools

import numpy as np
import jax
import jax.numpy as jnp
from jax import lax
from jax.experimental import pallas as pl
from jax.experimental.pallas import tpu as pltpu

F32 = jnp.float32
BF16 = jnp.bfloat16

D_MODEL = 1024
EPS = 1e-6
NEG_INF = -1e30
N_Q_HEADS = 16
N_KV_HEADS = 4
HEAD_DIM = 64
ATTN_WIDTH = N_Q_HEADS * HEAD_DIM
KV_WIDTH = N_KV_HEADS * HEAD_DIM
ATTN_BLOCK = 128
ROPE_THETA = 10000.0
D_INNER = 2 * D_MODEL
SSM_HEAD_DIM = 64
N_SSM_HEADS = D_INNER // SSM_HEAD_DIM
N_SSM_GROUPS = 4
HEADS_PER_GROUP = N_SSM_HEADS // N_SSM_GROUPS
D_STATE = 128
BC_WIDTH = N_SSM_GROUPS * D_STATE
CONV_DIM = D_INNER + 2 * BC_WIDTH
CONV_W = 7
CHUNK = 128
N_EXPERT_GROUPS = 4
EXPERTS_PER_GROUP = 8
N_EXPERTS = N_EXPERT_GROUPS * EXPERTS_PER_GROUP
D_EXPERT = 512
N_PAIRS = EXPERTS_PER_GROUP * (EXPERTS_PER_GROUP - 1) // 2
N_CLASSES = N_EXPERT_GROUPS * N_PAIRS

LANES = 128
V7X_VMEM_LIMIT_BYTES = 56 * 1024 * 1024

COL_Z = 0
COL_GATE = COL_Z + D_INNER
COL_XS = COL_GATE + 2 * D_MODEL
COL_B = COL_XS + D_INNER
COL_C = COL_B + BC_WIDTH
COL_Q = COL_C + BC_WIDTH
COL_K = COL_Q + ATTN_WIDTH
COL_V = COL_K + KV_WIDTH
COL_DT = COL_V + KV_WIDTH
N_PROJ = COL_DT + LANES

TM_IN = 1024
NJ_IN = 3
TN_IN = N_PROJ // NJ_IN
CH_IN = 512
TM_QK = 512
TM_OUT = 512
CONV_CT = 256
CONV_ROWS = 256
ROW_BLOCK = 128
TOKEN_TILE = D_MODEL // LANES


def _params(sem):
    return pltpu.CompilerParams(dimension_semantics=sem, vmem_limit_bytes=V7X_VMEM_LIMIT_BYTES)


def _dot(a, b):
    return jnp.dot(a, b, preferred_element_type=F32)


def _dot_nt(a, b):
    return lax.dot_general(a, b, (((1,), (1,)), ((), ())), preferred_element_type=F32)


def _sigmoid(x):
    return 1.0 / (1.0 + jnp.exp(-x))


def _inproj_kernel(x_ref, g_ref, w_ref, o_ref, dt_ref, h_ref):
    j = pl.program_id(1)

    @pl.when(j == 0)
    def _():
        x = x_ref[...]
        ms = jnp.mean(x * x, axis=-1, keepdims=True)
        h_ref[...] = (x * lax.rsqrt(ms + EPS) * g_ref[...]).astype(BF16)

    for c0 in range(0, TN_IN, CH_IN):
        c1 = min(c0 + CH_IN, TN_IN)
        acc = _dot(h_ref[...], w_ref[:, c0:c1])
        o_ref[:, c0:c1] = acc.astype(BF16)
        if c1 == TN_IN:
            @pl.when(j == NJ_IN - 1)
            def _():
                dt_ref[...] = acc[:, c1 - c0 - LANES:]


def _inproj(x2d, gain, w_bf16):
    t = x2d.shape[0]
    return pl.pallas_call(
        _inproj_kernel,
        grid=(t // TM_IN, NJ_IN),
        in_specs=[
            pl.BlockSpec((TM_IN, D_MODEL), lambda i, j: (i, 0)),
            pl.BlockSpec((1, D_MODEL), lambda i, j: (0, 0)),
            pl.BlockSpec((D_MODEL, TN_IN), lambda i, j: (0, j)),
        ],
        out_specs=[
            pl.BlockSpec((TM_IN, TN_IN), lambda i, j: (i, j)),
            pl.BlockSpec((TM_IN, LANES), lambda i, j: (i, 0)),
        ],
        out_shape=[
            jax.ShapeDtypeStruct((t, N_PROJ), BF16),
            jax.ShapeDtypeStruct((t, LANES), F32),
        ],
        scratch_shapes=[pltpu.VMEM((TM_IN, D_MODEL), BF16)],
        compiler_params=_params(("arbitrary", "arbitrary")),
        name="inproj",
    )(x2d, gain, w_bf16)


def _qkprep_kernel(q_ref, k_ref, v_ref, cos_ref, sin_ref, qg_ref, kg_ref, seg_ref, qo_ref, ko_ref, vo_ref):
    cos = cos_ref[...]
    sin = sin_ref[...]
    seg = seg_ref[...]
    lane = lax.broadcasted_iota(jnp.int32, (TM_QK, LANES), 1)
    first_half = (lane % HEAD_DIM) < (HEAD_DIM // 2)
    low = lane < HEAD_DIM

    def norm_rope(x, gain):
        ss = _dot((x * x).astype(BF16), seg)
        xn = x * lax.rsqrt(ss * (1.0 / HEAD_DIM) + EPS) * gain
        rot = jnp.where(first_half, pltpu.roll(xn, 96, 1), pltpu.roll(xn, 32, 1))
        return xn * cos + rot * sin

    for s in range(ATTN_WIDTH // LANES):
        sl = slice(s * LANES, (s + 1) * LANES)
        y = norm_rope(q_ref[:, sl].astype(F32), qg_ref[...]) * (HEAD_DIM ** -0.5)
        qo_ref[:, sl] = y.astype(BF16)
    for s in range(KV_WIDTH // LANES):
        sl = slice(s * LANES, (s + 1) * LANES)
        y = norm_rope(k_ref[:, sl].astype(F32), kg_ref[...])
        ysw = pltpu.roll(y, HEAD_DIM, 1)
        ko_ref[:, (2 * s) * LANES:(2 * s + 1) * LANES] = jnp.where(low, y, ysw).astype(BF16)
        ko_ref[:, (2 * s + 1) * LANES:(2 * s + 2) * LANES] = jnp.where(low, ysw, y).astype(BF16)
        v = v_ref[:, sl].astype(F32)
        vsw = pltpu.roll(v, HEAD_DIM, 1)
        vo_ref[:, (2 * s) * LANES:(2 * s + 1) * LANES] = jnp.where(low, v, vsw).astype(BF16)
        vo_ref[:, (2 * s + 1) * LANES:(2 * s + 2) * LANES] = jnp.where(low, vsw, v).astype(BF16)


def _qkprep(proj, cos128, sin128, qg128, kg128, seg, seq):
    t = proj.shape[0]
    nseq = seq // TM_QK
    return pl.pallas_call(
        _qkprep_kernel,
        grid=(t // TM_QK,),
        in_specs=[
            pl.BlockSpec((TM_QK, ATTN_WIDTH), lambda i: (i, COL_Q // ATTN_WIDTH)),
            pl.BlockSpec((TM_QK, KV_WIDTH), lambda i: (i, COL_K // KV_WIDTH)),
            pl.BlockSpec((TM_QK, KV_WIDTH), lambda i: (i, COL_V // KV_WIDTH)),
            pl.BlockSpec((TM_QK, LANES), lambda i: (i % nseq, 0)),
            pl.BlockSpec((TM_QK, LANES), lambda i: (i % nseq, 0)),
            pl.BlockSpec((1, LANES), lambda i: (0, 0)),
            pl.BlockSpec((1, LANES), lambda i: (0, 0)),
            pl.BlockSpec((LANES, LANES), lambda i: (0, 0)),
        ],
        out_specs=[
            pl.BlockSpec((TM_QK, ATTN_WIDTH), lambda i: (i, 0)),
            pl.BlockSpec((TM_QK, 2 * KV_WIDTH), lambda i: (i, 0)),
            pl.BlockSpec((TM_QK, 2 * KV_WIDTH), lambda i: (i, 0)),
        ],
        out_shape=[
            jax.ShapeDtypeStruct((t, ATTN_WIDTH), BF16),
            jax.ShapeDtypeStruct((t, 2 * KV_WIDTH), BF16),
            jax.ShapeDtypeStruct((t, 2 * KV_WIDTH), BF16),
        ],
        compiler_params=_params(("arbitrary",)),
        name="qkprep",
    )(proj, proj, proj, cos128, sin128, qg128, kg128, seg)


def _attn_kernel(sink_ref, q_ref, kp_ref, kc_ref, kn_ref, vp_ref, vc_ref, vn_ref, o_ref, *, nq):
    i = pl.program_id(1)
    nk = 3 * ATTN_BLOCK
    qi = lax.broadcasted_iota(jnp.int32, (ATTN_BLOCK, nk), 0)
    si = lax.broadcasted_iota(jnp.int32, (ATTN_BLOCK, nk), 1)
    rel = qi - (si - ATTN_BLOCK)
    band = jnp.where(rel <= ATTN_BLOCK, jnp.where(rel >= -ATTN_BLOCK, 0.0, NEG_INF), NEG_INF)
    band = jnp.where(si < ATTN_BLOCK, jnp.where(i > 0, band, NEG_INF), band)
    bias = jnp.where(si >= 2 * ATTN_BLOCK, jnp.where(i < nq - 1, band, NEG_INF), band)
    low_q = lax.broadcasted_iota(jnp.int32, (ATTN_BLOCK, LANES), 1) < HEAD_DIM
    low_k = lax.broadcasted_iota(jnp.int32, (nk, LANES), 1) < HEAD_DIM
    zero_q = jnp.zeros((ATTN_BLOCK, LANES), BF16)
    zero_k = jnp.zeros((nk, LANES), BF16)

    def soft(qm, kd, head):
        s = _dot_nt(qm, kd) + bias
        snk = sink_ref[head]
        m = jnp.maximum(jnp.max(s, axis=-1, keepdims=True), snk)
        p = jnp.exp(s - m)
        den = jnp.sum(p, axis=-1, keepdims=True) + jnp.exp(snk - m)
        return p.astype(BF16), 1.0 / den

    for h in range(N_KV_HEADS):
        sl = slice(h * LANES, (h + 1) * LANES)
        kd = jnp.concatenate([kp_ref[:, sl], kc_ref[:, sl], kn_ref[:, sl]], axis=0)
        vd = jnp.concatenate([vp_ref[:, sl], vc_ref[:, sl], vn_ref[:, sl]], axis=0)
        v_lo = jnp.where(low_k, vd, zero_k)
        v_hi = jnp.where(low_k, zero_k, vd)
        for j in range(2):
            col = slice((2 * h + j) * LANES, (2 * h + j + 1) * LANES)
            slab = q_ref[:, col]
            pa, ia = soft(jnp.where(low_q, slab, zero_q), kd, 4 * h + 2 * j)
            pb, ib = soft(jnp.where(low_q, zero_q, slab), kd, 4 * h + 2 * j + 1)
            o = _dot(pa, v_lo) + _dot(pb, v_hi)
            o_ref[:, col] = (o * jnp.where(low_q, ia, ib)).astype(BF16)


def _attention(qr, kdup, vdup, sink, batch, seq):
    t = qr.shape[0]
    nq = seq // ATTN_BLOCK

    def prev(b, i, s):
        return (b * nq + jnp.maximum(i - 1, 0), 0)

    def cur(b, i, s):
        return (b * nq + i, 0)

    def nxt(b, i, s):
        return (b * nq + jnp.minimum(i + 1, nq - 1), 0)

    kv_block = (ATTN_BLOCK, 2 * KV_WIDTH)
    grid_spec = pltpu.PrefetchScalarGridSpec(
        num_scalar_prefetch=1,
        grid=(batch, nq),
        in_specs=[
            pl.BlockSpec((ATTN_BLOCK, ATTN_WIDTH), cur),
            pl.BlockSpec(kv_block, prev), pl.BlockSpec(kv_block, cur), pl.BlockSpec(kv_block, nxt),
            pl.BlockSpec(kv_block, prev), pl.BlockSpec(kv_block, cur), pl.BlockSpec(kv_block, nxt),
        ],
        out_specs=pl.BlockSpec((ATTN_BLOCK, ATTN_WIDTH), cur),
    )
    return pl.pallas_call(
        functools.partial(_attn_kernel, nq=nq),
        grid_spec=grid_spec,
        out_shape=jax.ShapeDtypeStruct((t, ATTN_WIDTH), BF16),
        compiler_params=_params(("arbitrary", "arbitrary")),
        name="attention",
    )(sink, qr, kdup, kdup, kdup, vdup, vdup, vdup)


def _conv_kernel(x_ref, w_ref, b_ref, o_ref, pad_ref, *, seq):
    halo = 8
    pad_ref[0:halo, :] = jnp.zeros((halo, CONV_CT), F32)
    pad_ref[seq + halo:seq + 2 * halo, :] = jnp.zeros((halo, CONV_CT), F32)

    def load(r, _):
        r0 = pl.multiple_of(r * CONV_ROWS, CONV_ROWS)
        pad_ref[pl.ds(r0 + halo, CONV_ROWS), :] = x_ref[pl.ds(r0, CONV_ROWS), :].astype(F32)
        return 0

    lax.fori_loop(0, seq // CONV_ROWS, load, 0)
    w = w_ref[...]
    bias = b_ref[...]

    for r in range(seq // CONV_ROWS):
        r0 = r * CONV_ROWS
        acc = jnp.broadcast_to(bias, (CONV_ROWS, CONV_CT))
        for k in range(CONV_W):
            acc = acc + pad_ref[r0 + halo + k - CONV_W // 2:r0 + halo + k - CONV_W // 2 + CONV_ROWS, :] * w[k:k + 1, :]
        o_ref[r0:r0 + CONV_ROWS, :] = (acc * _sigmoid(acc)).astype(BF16)


def _conv(proj, conv_w, conv_b, batch, seq):
    t = proj.shape[0]
    return pl.pallas_call(
        functools.partial(_conv_kernel, seq=seq),
        grid=(batch, CONV_DIM // CONV_CT),
        in_specs=[
            pl.BlockSpec((seq, CONV_CT), lambda b, c: (b, COL_XS // CONV_CT + c)),
            pl.BlockSpec((CONV_W, CONV_CT), lambda b, c: (0, c)),
            pl.BlockSpec((1, CONV_CT), lambda b, c: (0, c)),
        ],
        out_specs=pl.BlockSpec((seq, CONV_CT), lambda b, c: (b, c)),
        out_shape=jax.ShapeDtypeStruct((t, CONV_DIM), BF16),
        scratch_shapes=[pltpu.VMEM((seq + 16, CONV_CT), F32)],
        compiler_params=_params(("arbitrary", "arbitrary")),
        name="conv",
    )(proj, conv_w, conv_b)


def _split3(a):
    a1 = a.astype(BF16)
    r1 = a - a1.astype(F32)
    a2 = r1.astype(BF16)
    a3 = (r1 - a2.astype(F32)).astype(BF16)
    return a1, a2, a3


def _tri_matmul(tri, a):
    a1, a2, a3 = _split3(a)
    return _dot(tri, a1) + _dot(tri, a2) + _dot(tri, a3)


def _softplus(x):
    return jnp.maximum(x, 0.0) + jnp.log(1.0 + jnp.exp(-jnp.abs(x)))


def _dt_and_rate(dt_ref, bias_ref, alog_ref):
    dt = _softplus(dt_ref[...] + bias_ref[...])
    rate = dt * (-jnp.exp(alog_ref[...]))
    return dt, rate


def _head_rows(mat, first, rows):
    n = mat.shape[1]
    return jnp.concatenate(
        [jnp.broadcast_to(mat[first + e:first + e + 1, :], (rows, n)) for e in range(HEADS_PER_GROUP)], axis=0)


def _ssd_bwd_state_kernel(xs_ref, b_ref, dt_ref, bias_ref, alog_ref, tl_ref, hb_ref, st_ref):
    c = pl.program_id(1)

    @pl.when(c == 0)
    def _():
        st_ref[...] = jnp.zeros_like(st_ref)

    hb_ref[0] = st_ref[...].astype(BF16)
    dt, rate = _dt_and_rate(dt_ref, bias_ref, alog_ref)
    pre = _tri_matmul(tl_ref[...], rate)
    pre_t = pre.T
    excl_t = (pre - rate).T
    total = jnp.broadcast_to(pre_t[:, CHUNK - 1:CHUNK], (LANES, CHUNK))
    w_t = dt.T * jnp.exp(excl_t)
    dec = jnp.exp(total)
    off = N_SSM_HEADS
    for g in range(N_SSM_GROUPS):
        xs_t = xs_ref[:, g * 512:(g + 1) * 512].astype(F32).T
        xd = (xs_t * _head_rows(w_t, off + g * HEADS_PER_GROUP, SSM_HEAD_DIM)).astype(BF16)
        upd = _dot(xd, b_ref[:, g * D_STATE:(g + 1) * D_STATE])
        st_ref[g] = _head_rows(dec, off + g * HEADS_PER_GROUP, SSM_HEAD_DIM) * st_ref[g] + upd


def _ssd_bwd_states(xc, dt, bias128, alog128, tri_l, batch, seq):
    nc = seq // CHUNK

    def rev(b, c):
        return (b * nc + nc - 1 - c, 0)

    return pl.pallas_call(
        _ssd_bwd_state_kernel,
        grid=(batch, nc),
        in_specs=[
            pl.BlockSpec((CHUNK, D_INNER), rev),
            pl.BlockSpec((CHUNK, BC_WIDTH), lambda b, c: (b * nc + nc - 1 - c, D_INNER // BC_WIDTH)),
            pl.BlockSpec((CHUNK, LANES), rev),
            pl.BlockSpec((1, LANES), lambda b, c: (0, 0)),
            pl.BlockSpec((1, LANES), lambda b, c: (0, 0)),
            pl.BlockSpec((CHUNK, CHUNK), lambda b, c: (0, 0)),
        ],
        out_specs=pl.BlockSpec((1, N_SSM_GROUPS, 512, D_STATE), lambda b, c: (b * nc + nc - 1 - c, 0, 0, 0)),
        out_shape=jax.ShapeDtypeStruct((batch * nc, N_SSM_GROUPS, 512, D_STATE), BF16),
        scratch_shapes=[pltpu.VMEM((N_SSM_GROUPS, 512, D_STATE), F32)],
        compiler_params=_params(("arbitrary", "arbitrary")),
        name="ssd_bwd_states",
    )(xc, xc, dt, bias128, alog128, tri_l)


def _ssd_main_kernel(xc_ref, z_ref, dt_ref, hb_ref, bias_ref, alog_ref, tl_ref, tu_ref, dskip_ref, gain_ref,
                     o_ref, hf_ref, y_ref):
    c = pl.program_id(1)

    @pl.when(c == 0)
    def _():
        hf_ref[...] = jnp.zeros_like(hf_ref)

    dt, rate = _dt_and_rate(dt_ref, bias_ref, alog_ref)
    lane = lax.broadcasted_iota(jnp.int32, (CHUNK, LANES), 1)
    cum = jnp.where(lane < N_SSM_HEADS, _tri_matmul(tl_ref[...], rate), _tri_matmul(tu_ref[...], rate))
    cum_t = cum.T
    dt_t = dt.T
    row = lax.broadcasted_iota(jnp.int32, (CHUNK, CHUNK), 0)
    col = lax.broadcasted_iota(jnp.int32, (CHUNK, CHUNK), 1)
    lower = row >= col
    upper = row <= col
    low = lane < SSM_HEAD_DIM
    zero_x = jnp.zeros((CHUNK, LANES), BF16)
    nb = N_SSM_HEADS

    def lane_bcast(mat, idx):
        return jnp.broadcast_to(mat[:, idx:idx + 1], (CHUNK, CHUNK))

    def sub_bcast(mat, idx):
        return jnp.broadcast_to(mat[idx:idx + 1, :], (CHUNK, CHUNK))

    def head_matrix(e, cb):
        col_f = lane_bcast(cum, e)
        col_b = lane_bcast(cum, nb + e)
        m_f = jnp.exp(jnp.where(lower, col_f - sub_bcast(cum_t, e), NEG_INF)) * sub_bcast(dt_t, e)
        m_b = jnp.exp(jnp.where(upper, col_b - sub_bcast(cum_t, nb + e), NEG_INF)) * sub_bcast(dt_t, nb + e)
        return ((m_f + m_b) * cb).astype(BF16), col_f, col_b

    for g in range(N_SSM_GROUPS):
        bg = xc_ref[:, D_INNER + g * D_STATE:D_INNER + (g + 1) * D_STATE]
        cg = xc_ref[:, D_INNER + BC_WIDTH + g * D_STATE:D_INNER + BC_WIDTH + (g + 1) * D_STATE]
        cb = _dot_nt(cg, bg)
        y_in_f = _dot_nt(cg, hf_ref[g].astype(BF16))
        y_in_b = _dot_nt(cg, hb_ref[0, g])
        for jp in range(HEADS_PER_GROUP // 2):
            e0 = g * HEADS_PER_GROUP + 2 * jp
            cols = slice(e0 * SSM_HEAD_DIM, e0 * SSM_HEAD_DIM + LANES)
            loc = slice(jp * LANES, (jp + 1) * LANES)
            xs_pair = xc_ref[:, cols]
            m0, cf0, cb0 = head_matrix(e0, cb)
            m1, cf1, cb1 = head_matrix(e0 + 1, cb)
            y = _dot(m0, jnp.where(low, xs_pair, zero_x)) + _dot(m1, jnp.where(low, zero_x, xs_pair))
            y = y + y_in_f[:, loc] * jnp.exp(jnp.where(low, cf0, cf1))
            y = y + y_in_b[:, loc] * jnp.exp(jnp.where(low, cb0, cb1))
            y_ref[:, cols] = y + dskip_ref[:, cols] * xs_pair.astype(F32)

    z = z_ref[...].astype(F32)
    y = y_ref[...] * (z * _sigmoid(z))
    ms = jnp.mean(y * y, axis=-1, keepdims=True)
    o_ref[...] = (y * lax.rsqrt(ms + EPS) * gain_ref[...]).astype(BF16)

    last = jnp.broadcast_to(cum_t[:, CHUNK - 1:CHUNK], (LANES, CHUNK))
    w_t = dt_t * jnp.exp(last - cum_t)
    dec = jnp.exp(last)
    for g in range(N_SSM_GROUPS):
        xs_t = xc_ref[:, g * 512:(g + 1) * 512].astype(F32).T
        xd = (xs_t * _head_rows(w_t, g * HEADS_PER_GROUP, SSM_HEAD_DIM)).astype(BF16)
        upd = _dot(xd, xc_ref[:, D_INNER + g * D_STATE:D_INNER + (g + 1) * D_STATE])
        hf_ref[g] = _head_rows(dec, g * HEADS_PER_GROUP, SSM_HEAD_DIM) * hf_ref[g] + upd


def _ssd_main(xc, proj, dt, hb, bias128, alog128, tri_l, tri_u, dskip, gain, batch, seq):
    t = xc.shape[0]
    nc = seq // CHUNK

    def tok(b, c):
        return (b * nc + c, 0)

    def const(b, c):
        return (0, 0)

    return pl.pallas_call(
        _ssd_main_kernel,
        grid=(batch, nc),
        in_specs=[
            pl.BlockSpec((CHUNK, CONV_DIM), tok),
            pl.BlockSpec((CHUNK, D_INNER), tok),
            pl.BlockSpec((CHUNK, LANES), tok),
            pl.BlockSpec((1, N_SSM_GROUPS, 512, D_STATE), lambda b, c: (b * nc + c, 0, 0, 0)),
            pl.BlockSpec((1, LANES), const),
            pl.BlockSpec((1, LANES), const),
            pl.BlockSpec((CHUNK, CHUNK), const),
            pl.BlockSpec((CHUNK, CHUNK), const),
            pl.BlockSpec((1, D_INNER), const),
            pl.BlockSpec((1, D_INNER), const),
        ],
        out_specs=pl.BlockSpec((CHUNK, D_INNER), tok),
        out_shape=jax.ShapeDtypeStruct((t, D_INNER), BF16),
        scratch_shapes=[pltpu.VMEM((N_SSM_GROUPS, 512, D_STATE), F32), pltpu.VMEM((CHUNK, D_INNER), F32)],
        compiler_params=_params(("arbitrary", "arbitrary")),
        name="ssd_main",
    )(xc, proj, dt, hb, bias128, alog128, tri_l, tri_u, dskip, gain)


def _outproj_kernel(attn_a, ssm_a, gate_a, x_a, attn_b, ssm_b, gate_b, x_b, wa_ref, ws_ref, wo_ref, gn_ref,
                    wr1_ref, wr2_ref, br_ref, o_ref, r_ref, *, n_a):
    i = pl.program_id(0)

    @pl.when(i < n_a)
    def _():
        _outproj_tile(attn_a, ssm_a, gate_a, x_a, wa_ref, ws_ref, wo_ref, gn_ref, wr1_ref, wr2_ref, br_ref,
                      o_ref, r_ref)

    @pl.when(i >= n_a)
    def _():
        _outproj_tile(attn_b, ssm_b, gate_b, x_b, wa_ref, ws_ref, wo_ref, gn_ref, wr1_ref, wr2_ref, br_ref,
                      o_ref, r_ref)


def _outproj_tile(attn_ref, ssm_ref, gate_ref, x_ref, wa_ref, ws_ref, wo_ref, gn_ref, wr1_ref, wr2_ref, br_ref,
                  o_ref, r_ref):
    a_out = _dot(attn_ref[...], wa_ref[...])
    s_out = _dot(ssm_ref[...], ws_ref[...])
    ga = gate_ref[:, :D_MODEL].astype(F32)
    gs = gate_ref[:, D_MODEL:].astype(F32)
    merged = _sigmoid(ga) * a_out + _sigmoid(gs) * s_out
    x2 = x_ref[...] + _dot(merged.astype(BF16), wo_ref[...])
    for j in range(TOKEN_TILE):
        o_ref[pl.ds(j, TM_OUT, stride=TOKEN_TILE), :] = x2[:, j * LANES:(j + 1) * LANES]

    ms = jnp.mean(x2 * x2, axis=-1, keepdims=True)
    hn = x2 * lax.rsqrt(ms + EPS) * gn_ref[...]
    h1 = hn.astype(BF16)
    h2 = (hn - h1.astype(F32)).astype(BF16)
    lg = _dot(h1, wr1_ref[...]) + _dot(h2, wr1_ref[...]) + _dot(h1, wr2_ref[...]) + br_ref[...]

    lane = lax.broadcasted_iota(jnp.int32, (TM_OUT, LANES), 1).astype(F32)
    big = float(LANES)

    def rmax(v):
        return jnp.max(v, axis=-1, keepdims=True)

    def first_lane(mask):
        return jnp.min(jnp.where(mask, lane, big), axis=-1, keepdims=True)

    gl = jnp.where(lane < N_EXPERT_GROUPS, lg, NEG_INF)
    gmax = rmax(gl)
    g_w = 1.0 / jnp.sum(jnp.exp(gl - gmax), axis=-1, keepdims=True)
    gidx = first_lane(gl == gmax)
    base = N_EXPERT_GROUPS + EXPERTS_PER_GROUP * gidx
    el = jnp.where(lane >= base, jnp.where(lane < base + EXPERTS_PER_GROUP, lg, NEG_INF), NEG_INF)
    m1 = rmax(el)
    i1 = first_lane(el == m1)
    el2 = jnp.where(lane == i1, NEG_INF, el)
    m2 = rmax(el2)
    i2 = first_lane(el2 == m2)
    r = jnp.exp(m2 - m1)
    w1 = g_w / (1.0 + r)
    w2 = w1 * r
    j1 = i1 - base
    j2 = i2 - base
    swap = j1 > j2
    e_lo = jnp.where(swap, j2, j1)
    e_hi = jnp.where(swap, j1, j2)
    w_lo = jnp.where(swap, w2, w1)
    w_hi = jnp.where(swap, w1, w2)
    pair = e_lo * (EXPERTS_PER_GROUP - 1) - e_lo * (e_lo - 1.0) * 0.5 + (e_hi - e_lo - 1.0)
    cls = gidx * N_PAIRS + pair
    r_ref[...] = jnp.where(lane == 0.0, cls, jnp.where(lane == 1.0, w_lo, jnp.where(lane == 2.0, w_hi, 0.0)))


def _outproj(group_a, group_b, wa, ws, wo, gn, wr1, wr2, br):
    n_a = group_a[3].shape[0] // TM_OUT
    n_b = group_b[3].shape[0] // TM_OUT
    t = (n_a + n_b) * TM_OUT

    def first(i):
        return (jnp.minimum(i, n_a - 1), 0)

    def second(i):
        return (jnp.maximum(i - n_a, 0), 0)

    def const(i):
        return (0, 0)

    def group_specs(tok):
        return [
            pl.BlockSpec((TM_OUT, ATTN_WIDTH), tok),
            pl.BlockSpec((TM_OUT, D_INNER), tok),
            pl.BlockSpec((TM_OUT, 2 * D_MODEL), lambda i: (tok(i)[0], COL_GATE // (2 * D_MODEL))),
            pl.BlockSpec((TM_OUT, D_MODEL), tok),
        ]

    resident = dict(pipeline_mode=pl.Buffered(1))
    return pl.pallas_call(
        functools.partial(_outproj_kernel, n_a=n_a),
        grid=(n_a + n_b,),
        in_specs=group_specs(first) + group_specs(second) + [
            pl.BlockSpec((ATTN_WIDTH, D_MODEL), const, **resident),
            pl.BlockSpec((D_INNER, D_MODEL), const, **resident),
            pl.BlockSpec((D_MODEL, D_MODEL), const, **resident),
            pl.BlockSpec((1, D_MODEL), const),
            pl.BlockSpec((D_MODEL, LANES), const),
            pl.BlockSpec((D_MODEL, LANES), const),
            pl.BlockSpec((1, LANES), const),
        ],
        out_specs=[
            pl.BlockSpec((TM_OUT * TOKEN_TILE, LANES), lambda i: (i, 0)),
            pl.BlockSpec((TM_OUT, LANES), lambda i: (i, 0)),
        ],
        out_shape=[
            jax.ShapeDtypeStruct((t * TOKEN_TILE, LANES), F32),
            jax.ShapeDtypeStruct((t, LANES), F32),
        ],
        compiler_params=_params(("arbitrary",)),
        name="outproj_router",
    )(*group_a, *group_b, wa, ws, wo, gn, wr1, wr2, br)


def _moe_kernel(ea_ref, eb_ref, nv_ref, tokc_ref, tokn_ref, roww_ref, x_ref, gn_ref,
                wga_ref, wua_ref, wda_ref, wgb_ref, wub_ref, wdb_ref, o_ref,
                xg_ref, st_ref, gsem, ssem, *, n_blocks):
    i = pl.program_id(0)
    slot = i % 2
    other = 1 - slot

    def tile(idx):
        return pl.ds(pl.multiple_of(idx * TOKEN_TILE, TOKEN_TILE), TOKEN_TILE)

    def gather_copy(tok, r, s):
        return pltpu.make_async_copy(x_ref.at[tile(tok), :], xg_ref.at[s, tile(r), :], gsem.at[s])

    def scatter_copy(tok, r, s):
        return pltpu.make_async_copy(st_ref.at[s, tile(r), :], o_ref.at[tile(tok), :], ssem.at[s])

    def for_rows(n, fn):
        n8 = lax.shift_right_logical(n, 3)

        def body8(g, _):
            for u in range(8):
                fn(g * 8 + u)
            return 0

        def body1(r, _):
            fn(r)
            return 0

        lax.fori_loop(0, n8, body8, 0)
        lax.fori_loop(n8 * 8, n, body1, 0)

    def start_gathers(tok_ref, n, s):
        for_rows(n, lambda r: gather_copy(tok_ref[0, 0, r], r, s).start())

    def wait_gathers(n, s):
        for_rows(n, lambda r: gather_copy(0, 0, s).wait())

    def start_scatters(tok_ref, n, s):
        for_rows(n, lambda r: scatter_copy(tok_ref[0, 0, r], r, s).start())

    def wait_scatters(n, s):
        for_rows(n, lambda r: scatter_copy(0, 0, s).wait())

    @pl.when(i == 0)
    def _():
        xg_ref[...] = jnp.zeros_like(xg_ref)
        start_gathers(tokc_ref, nv_ref[0], 0)

    @pl.when(i + 1 < n_blocks)
    def _():
        start_gathers(tokn_ref, nv_ref[jnp.minimum(i + 1, n_blocks - 1)], other)

    wait_gathers(nv_ref[i], slot)

    @pl.when(i >= 2)
    def _():
        wait_scatters(nv_ref[jnp.maximum(i - 2, 0)], slot)

    @pl.when(nv_ref[i] > 0)
    def _():
        x = jnp.concatenate(
            [xg_ref[slot, pl.ds(j, ROW_BLOCK, stride=TOKEN_TILE), :] for j in range(TOKEN_TILE)], axis=1)
        w_cols = jnp.concatenate([roww_ref[0], jnp.zeros((LANES - TOKEN_TILE, ROW_BLOCK), F32)], axis=0).T
        w_lo = w_cols[:, 0:1]
        w_hi = w_cols[:, 1:2]
        ms = jnp.mean(x * x, axis=-1, keepdims=True)
        hn = (x * lax.rsqrt(ms + EPS) * gn_ref[...]).astype(BF16)

        def expert(wg_ref, wu_ref, wd_ref):
            gte = _dot(hn, wg_ref[0])
            up = _dot(hn, wu_ref[0])
            return _dot((gte * _sigmoid(gte) * up).astype(BF16), wd_ref[0])

        ya = expert(wga_ref, wua_ref, wda_ref)
        yb = expert(wgb_ref, wub_ref, wdb_ref)
        out = x + w_lo * ya + w_hi * yb
        for j in range(TOKEN_TILE):
            st_ref[slot, pl.ds(j, ROW_BLOCK, stride=TOKEN_TILE), :] = out[:, j * LANES:(j + 1) * LANES]

    start_scatters(tokc_ref, nv_ref[i], slot)

    @pl.when(i == n_blocks - 1)
    def _():
        wait_scatters(nv_ref[jnp.maximum(i - 1, 0)], other)
        wait_scatters(nv_ref[i], slot)


def _moe(ea, eb, nvalid, row_tok, row_w, x2t, gn, wg, wu, wd):
    n_blocks = row_tok.shape[0]

    def wa(i, ea, eb, nv):
        return (ea[i], 0, 0)

    def wb(i, ea, eb, nv):
        return (eb[i], 0, 0)

    any_spec = pl.BlockSpec(memory_space=pl.ANY)
    grid_spec = pltpu.PrefetchScalarGridSpec(
        num_scalar_prefetch=3,
        grid=(n_blocks,),
        in_specs=[
            pl.BlockSpec((1, 1, ROW_BLOCK), lambda i, ea, eb, nv: (i, 0, 0), memory_space=pltpu.SMEM),
            pl.BlockSpec((1, 1, ROW_BLOCK), lambda i, ea, eb, nv: (jnp.minimum(i + 1, n_blocks - 1), 0, 0),
                         memory_space=pltpu.SMEM),
            pl.BlockSpec((1, TOKEN_TILE, ROW_BLOCK), lambda i, ea, eb, nv: (i, 0, 0)),
            any_spec,
            pl.BlockSpec((1, D_MODEL), lambda i, ea, eb, nv: (0, 0)),
            pl.BlockSpec((1, D_MODEL, D_EXPERT), wa), pl.BlockSpec((1, D_MODEL, D_EXPERT), wa),
            pl.BlockSpec((1, D_EXPERT, D_MODEL), wa),
            pl.BlockSpec((1, D_MODEL, D_EXPERT), wb), pl.BlockSpec((1, D_MODEL, D_EXPERT), wb),
            pl.BlockSpec((1, D_EXPERT, D_MODEL), wb),
        ],
        out_specs=any_spec,
        scratch_shapes=[
            pltpu.VMEM((2, ROW_BLOCK * TOKEN_TILE, LANES), F32),
            pltpu.VMEM((2, ROW_BLOCK * TOKEN_TILE, LANES), F32),
            pltpu.SemaphoreType.DMA((2,)),
            pltpu.SemaphoreType.DMA((2,)),
        ],
    )
    return pl.pallas_call(
        functools.partial(_moe_kernel, n_blocks=n_blocks),
        grid_spec=grid_spec,
        out_shape=jax.ShapeDtypeStruct(x2t.shape, F32),
        compiler_params=_params(("arbitrary",)),
        name="moe",
    )(ea, eb, nvalid, row_tok, row_tok, row_w, x2t, gn, wg, wu, wd, wg, wu, wd)


def _pair_tables():
    lo, hi = [], []
    for a in range(EXPERTS_PER_GROUP):
        for b in range(a + 1, EXPERTS_PER_GROUP):
            lo.append(a)
            hi.append(b)
    return np.asarray(lo, np.int32), np.asarray(hi, np.int32)


def _block_tables(rinfo):
    cls = rinfo[:, 0].astype(jnp.int32)
    t = cls.shape[0]
    n_blocks = t // ROW_BLOCK + N_CLASSES
    sorted_cls, order = lax.sort((cls, jnp.arange(t, dtype=jnp.int32)), num_keys=1)
    starts = jnp.searchsorted(sorted_cls, jnp.arange(N_CLASSES + 1, dtype=jnp.int32), side='left').astype(jnp.int32)
    counts = starts[1:] - starts[:-1]
    nblk = (counts + ROW_BLOCK - 1) // ROW_BLOCK
    blk_end = jnp.cumsum(nblk)
    blk_start = blk_end - nblk
    used = blk_end[-1]
    b = jnp.arange(n_blocks, dtype=jnp.int32)
    b_eff = jnp.minimum(b, used - 1)
    c = jnp.minimum(jnp.searchsorted(blk_end, b_eff, side='right'), N_CLASSES - 1).astype(jnp.int32)
    off = b_eff - blk_start[c]
    src = starts[c] + off * ROW_BLOCK
    nvalid = jnp.where(b < used, jnp.clip(counts[c] - off * ROW_BLOCK, 0, ROW_BLOCK), 0).astype(jnp.int32)
    pair_lo, pair_hi = _pair_tables()
    grp = c // N_PAIRS
    ea = (grp * EXPERTS_PER_GROUP + jnp.asarray(pair_lo)[c % N_PAIRS]).astype(jnp.int32)
    eb = (grp * EXPERTS_PER_GROUP + jnp.asarray(pair_hi)[c % N_PAIRS]).astype(jnp.int32)
    rows = jnp.clip(src[:, None] + jnp.arange(ROW_BLOCK, dtype=jnp.int32)[None, :], 0, t - 1)
    row_tok = order[rows]
    row_w = jnp.zeros((n_blocks, TOKEN_TILE, ROW_BLOCK), F32)
    row_w = row_w.at[:, 0, :].set(rinfo[:, 1][row_tok]).at[:, 1, :].set(rinfo[:, 2][row_tok])
    return ea, eb, nvalid, row_tok.reshape(n_blocks, 1, ROW_BLOCK), row_w


def _rope_tables(seq):
    inv = 1.0 / (ROPE_THETA ** (jnp.arange(0, HEAD_DIM, 2, dtype=F32) / HEAD_DIM))
    ang = jnp.arange(seq, dtype=F32)[:, None] * inv[None, :]
    cos, sin = jnp.cos(ang), jnp.sin(ang)
    cos128 = jnp.concatenate([cos, cos, cos, cos], axis=-1)
    sin128 = jnp.concatenate([-sin, sin, -sin, sin], axis=-1)
    return cos128, sin128


def _prepare_weights(norm_mix, w_in, q_norm, k_norm, attn_sink, conv_w, conv_b, a_log_fwd, a_log_bwd,
                     dt_bias_fwd, dt_bias_bwd, d_skip, ssm_norm, w_out_attn, w_out_ssm, w_o, norm_ffn,
                     w_router_group, b_router_group, w_router_expert, b_router_expert, w_gate, w_up, w_down):
    o_q = 0
    o_k = o_q + ATTN_WIDTH
    o_v = o_k + KV_WIDTH
    o_z = o_v + KV_WIDTH
    o_xbc = o_z + D_INNER
    o_dtf = o_xbc + CONV_DIM
    o_dtb = o_dtf + N_SSM_HEADS
    o_ga = o_dtb + N_SSM_HEADS
    o_gs = o_ga + D_MODEL
    w = w_in
    w_r = jnp.concatenate([
        w[:, o_z:o_z + D_INNER], w[:, o_ga:o_gs + D_MODEL], w[:, o_xbc:o_xbc + CONV_DIM],
        w[:, o_q:o_q + ATTN_WIDTH], w[:, o_k:o_k + KV_WIDTH], w[:, o_v:o_v + KV_WIDTH],
        w[:, o_dtf:o_dtb + N_SSM_HEADS], jnp.zeros((D_MODEL, LANES - 2 * N_SSM_HEADS), w.dtype)], axis=1)
    pad64 = jnp.zeros((LANES - 2 * N_SSM_HEADS,), F32)
    eye = np.kron(np.eye(2, dtype=np.float32), np.ones((HEAD_DIM, HEAD_DIM), np.float32))
    idx = np.arange(CHUNK)
    w_router = jnp.concatenate([w_router_group, w_router_expert,
                                jnp.zeros((D_MODEL, LANES - N_EXPERT_GROUPS - N_EXPERTS), F32)], axis=1)
    wr1 = w_router.astype(BF16)
    return dict(
        norm_mix=norm_mix.reshape(1, D_MODEL),
        w_in=w_r.astype(BF16),
        qg128=jnp.tile(q_norm, 2).reshape(1, LANES),
        kg128=jnp.tile(k_norm, 2).reshape(1, LANES),
        seg=jnp.asarray(eye, BF16),
        sink=attn_sink.astype(F32),
        conv_w=conv_w,
        conv_b=conv_b.reshape(1, CONV_DIM),
        alog128=jnp.concatenate([a_log_fwd, a_log_bwd, pad64]).reshape(1, LANES),
        bias128=jnp.concatenate([dt_bias_fwd, dt_bias_bwd, pad64]).reshape(1, LANES),
        tri_l=jnp.asarray(idx[:, None] >= idx[None, :], BF16),
        tri_u=jnp.asarray(idx[:, None] <= idx[None, :], BF16),
        dskip=jnp.repeat(d_skip, SSM_HEAD_DIM).reshape(1, D_INNER),
        ssm_norm=ssm_norm.reshape(1, D_INNER),
        wa=w_out_attn.astype(BF16), ws=w_out_ssm.astype(BF16), wo=w_o.astype(BF16),
        norm_ffn=norm_ffn.reshape(1, D_MODEL),
        wr1=wr1, wr2=(w_router - wr1.astype(F32)).astype(BF16),
        br=jnp.concatenate([b_router_group, b_router_expert,
                            jnp.zeros((LANES - N_EXPERT_GROUPS - N_EXPERTS,), F32)]).reshape(1, LANES),
        wg=w_gate.astype(BF16), wu=w_up.astype(BF16), wd=w_down.astype(BF16),
    )


def _mixer(x, p):
    batch, seq, _ = x.shape
    x2d = x.reshape(batch * seq, D_MODEL)
    proj, dt = _inproj(x2d, p['norm_mix'], p['w_in'])
    cos128, sin128 = _rope_tables(seq)
    qr, kdup, vdup = _qkprep(proj, cos128, sin128, p['qg128'], p['kg128'], p['seg'], seq)
    attn = _attention(qr, kdup, vdup, p['sink'], batch, seq)
    xc = _conv(proj, p['conv_w'], p['conv_b'], batch, seq)
    hb = _ssd_bwd_states(xc, dt, p['bias128'], p['alog128'], p['tri_l'], batch, seq)
    ssm = _ssd_main(xc, proj, dt, hb, p['bias128'], p['alog128'], p['tri_l'], p['tri_u'], p['dskip'],
                    p['ssm_norm'], batch, seq)
    return attn, ssm, proj, x2d


def kernel(x_prompt, x_sample, norm_mix, w_in, q_norm, k_norm, attn_sink, conv_w, conv_b, a_log_fwd, a_log_bwd,
           dt_bias_fwd, dt_bias_bwd, d_skip, ssm_norm, w_out_attn, w_out_ssm, w_o, norm_ffn, w_router_group,
           b_router_group, w_router_expert, b_router_expert, w_gate, w_up, w_down):
    assert norm_mix.shape[0] == 1, "single-layer encoder"
    p = _prepare_weights(norm_mix[0], w_in[0], q_norm[0], k_norm[0], attn_sink[0], conv_w[0], conv_b[0],
                         a_log_fwd[0], a_log_bwd[0], dt_bias_fwd[0], dt_bias_bwd[0], d_skip[0], ssm_norm[0],
                         w_out_attn[0], w_out_ssm[0], w_o[0], norm_ffn[0], w_router_group[0], b_router_group[0],
                         w_router_expert[0], b_router_expert[0], w_gate[0], w_up[0], w_down[0])
    x2t, rinfo = _outproj(_mixer(x_prompt, p), _mixer(x_sample, p), p['wa'], p['ws'], p['wo'], p['norm_ffn'],
                          p['wr1'], p['wr2'], p['br'])
    ea, eb, nvalid, row_tok, row_w = _block_tables(rinfo)
    y = _moe(ea, eb, nvalid, row_tok, row_w, x2t, p['norm_ffn'], p['wg'], p['wu'], p['wd'])
    t_a = x_prompt.shape[0] * x_prompt.shape[1]
    y = y.reshape(-1, D_MODEL)
    return y[:t_a].reshape(x_prompt.shape), y[t_a:].reshape(x_sample.shape)
```

```python
import functools

import numpy as np
import jax
import jax.numpy as jnp
from jax import lax
from jax.experimental import pallas as pl
from jax.experimental.pallas import tpu as pltpu

F32 = jnp.float32
BF16 = jnp.bfloat16

D_MODEL = 1024
EPS = 1e-6
NEG_INF = -1e30
N_Q_HEADS = 16
N_KV_HEADS = 4
HEAD_DIM = 64
ATTN_WIDTH = N_Q_HEADS * HEAD_DIM
KV_WIDTH = N_KV_HEADS * HEAD_DIM
ATTN_BLOCK = 128
ROPE_THETA = 10000.0
D_INNER = 2 * D_MODEL
SSM_HEAD_DIM = 64
N_SSM_HEADS = D_INNER // SSM_HEAD_DIM
N_SSM_GROUPS = 4
HEADS_PER_GROUP = N_SSM_HEADS // N_SSM_GROUPS
D_STATE = 128
BC_WIDTH = N_SSM_GROUPS * D_STATE
CONV_DIM = D_INNER + 2 * BC_WIDTH
CONV_W = 7
CHUNK = 128
N_EXPERT_GROUPS = 4
EXPERTS_PER_GROUP = 8
N_EXPERTS = N_EXPERT_GROUPS * EXPERTS_PER_GROUP
D_EXPERT = 512
N_PAIRS = EXPERTS_PER_GROUP * (EXPERTS_PER_GROUP - 1) // 2
N_CLASSES = N_EXPERT_GROUPS * N_PAIRS

LANES = 128
V7X_VMEM_LIMIT_BYTES = 56 * 1024 * 1024

COL_Z = 0
COL_GATE = COL_Z + D_INNER
COL_XS = COL_GATE + 2 * D_MODEL
COL_B = COL_XS + D_INNER
COL_C = COL_B + BC_WIDTH
COL_Q = COL_C + BC_WIDTH
COL_K = COL_Q + ATTN_WIDTH
COL_V = COL_K + KV_WIDTH
COL_DT = COL_V + KV_WIDTH
N_PROJ = COL_DT + LANES

TM_IN = 1024
NJ_IN = 3
TN_IN = N_PROJ // NJ_IN
CH_IN = 512
TM_QK = 512
TM_OUT = 512
CONV_CT = 256
CONV_ROWS = 256
ROW_BLOCK = 128
TOKEN_TILE = D_MODEL // LANES


def _params(sem, flags=None):
    return pltpu.CompilerParams(dimension_semantics=sem, vmem_limit_bytes=V7X_VMEM_LIMIT_BYTES, flags=flags)


def _dot(a, b):
    return jnp.dot(a, b, preferred_element_type=F32)


def _dot_nt(a, b):
    return lax.dot_general(a, b, (((1,), (1,)), ((), ())), preferred_element_type=F32)


def _sigmoid(x):
    return 1.0 / (1.0 + jnp.exp(-x))


def _inproj_kernel(x_ref, g_ref, w_ref, o_ref, dt_ref, h_ref):
    j = pl.program_id(1)

    @pl.when(j == 0)
    def _():
        x = x_ref[...]
        ms = jnp.mean(x * x, axis=-1, keepdims=True)
        h_ref[...] = (x * lax.rsqrt(ms + EPS) * g_ref[...]).astype(BF16)

    for c0 in range(0, TN_IN, CH_IN):
        c1 = min(c0 + CH_IN, TN_IN)
        acc = _dot(h_ref[...], w_ref[:, c0:c1])
        o_ref[:, c0:c1] = acc.astype(BF16)
        if c1 == TN_IN:
            @pl.when(j == NJ_IN - 1)
            def _():
                dt_ref[...] = acc[:, c1 - c0 - LANES:]


def _inproj(x2d, gain, w_bf16):
    t = x2d.shape[0]
    return pl.pallas_call(
        _inproj_kernel,
        grid=(t // TM_IN, NJ_IN),
        in_specs=[
            pl.BlockSpec((TM_IN, D_MODEL), lambda i, j: (i, 0)),
            pl.BlockSpec((1, D_MODEL), lambda i, j: (0, 0)),
            pl.BlockSpec((D_MODEL, TN_IN), lambda i, j: (0, j)),
        ],
        out_specs=[
            pl.BlockSpec((TM_IN, TN_IN), lambda i, j: (i, j)),
            pl.BlockSpec((TM_IN, LANES), lambda i, j: (i, 0)),
        ],
        out_shape=[
            jax.ShapeDtypeStruct((t, N_PROJ), BF16),
            jax.ShapeDtypeStruct((t, LANES), F32),
        ],
        scratch_shapes=[pltpu.VMEM((TM_IN, D_MODEL), BF16)],
        compiler_params=_params(("arbitrary", "arbitrary")),
        name="inproj",
    )(x2d, gain, w_bf16)


def _qkprep_kernel(q_ref, k_ref, v_ref, cos_ref, sin_ref, qg_ref, kg_ref, seg_ref, qo_ref, ko_ref, vo_ref):
    cos = cos_ref[...]
    sin = sin_ref[...]
    seg = seg_ref[...]
    lane = lax.broadcasted_iota(jnp.int32, (TM_QK, LANES), 1)
    first_half = (lane % HEAD_DIM) < (HEAD_DIM // 2)
    low = lane < HEAD_DIM

    def norm_rope(x, gain):
        ss = _dot((x * x).astype(BF16), seg)
        xn = x * lax.rsqrt(ss * (1.0 / HEAD_DIM) + EPS) * gain
        rot = jnp.where(first_half, pltpu.roll(xn, 96, 1), pltpu.roll(xn, 32, 1))
        return xn * cos + rot * sin

    for s in range(ATTN_WIDTH // LANES):
        sl = slice(s * LANES, (s + 1) * LANES)
        y = norm_rope(q_ref[:, sl].astype(F32), qg_ref[...]) * (HEAD_DIM ** -0.5)
        qo_ref[:, sl] = y.astype(BF16)
    for s in range(KV_WIDTH // LANES):
        sl = slice(s * LANES, (s + 1) * LANES)
        y = norm_rope(k_ref[:, sl].astype(F32), kg_ref[...])
        ysw = pltpu.roll(y, HEAD_DIM, 1)
        ko_ref[:, (2 * s) * LANES:(2 * s + 1) * LANES] = jnp.where(low, y, ysw).astype(BF16)
        ko_ref[:, (2 * s + 1) * LANES:(2 * s + 2) * LANES] = jnp.where(low, ysw, y).astype(BF16)
        v = v_ref[:, sl].astype(F32)
        vsw = pltpu.roll(v, HEAD_DIM, 1)
        vo_ref[:, (2 * s) * LANES:(2 * s + 1) * LANES] = jnp.where(low, v, vsw).astype(BF16)
        vo_ref[:, (2 * s + 1) * LANES:(2 * s + 2) * LANES] = jnp.where(low, vsw, v).astype(BF16)


def _qkprep(proj, cos128, sin128, qg128, kg128, seg, seq):
    t = proj.shape[0]
    nseq = seq // TM_QK
    return pl.pallas_call(
        _qkprep_kernel,
        grid=(t // TM_QK,),
        in_specs=[
            pl.BlockSpec((TM_QK, ATTN_WIDTH), lambda i: (i, COL_Q // ATTN_WIDTH)),
            pl.BlockSpec((TM_QK, KV_WIDTH), lambda i: (i, COL_K // KV_WIDTH)),
            pl.BlockSpec((TM_QK, KV_WIDTH), lambda i: (i, COL_V // KV_WIDTH)),
            pl.BlockSpec((TM_QK, LANES), lambda i: (i % nseq, 0)),
            pl.BlockSpec((TM_QK, LANES), lambda i: (i % nseq, 0)),
            pl.BlockSpec((1, LANES), lambda i: (0, 0)),
            pl.BlockSpec((1, LANES), lambda i: (0, 0)),
            pl.BlockSpec((LANES, LANES), lambda i: (0, 0)),
        ],
        out_specs=[
            pl.BlockSpec((TM_QK, ATTN_WIDTH), lambda i: (i, 0)),
            pl.BlockSpec((TM_QK, 2 * KV_WIDTH), lambda i: (i, 0)),
            pl.BlockSpec((TM_QK, 2 * KV_WIDTH), lambda i: (i, 0)),
        ],
        out_shape=[
            jax.ShapeDtypeStruct((t, ATTN_WIDTH), BF16),
            jax.ShapeDtypeStruct((t, 2 * KV_WIDTH), BF16),
            jax.ShapeDtypeStruct((t, 2 * KV_WIDTH), BF16),
        ],
        compiler_params=_params(("arbitrary",)),
        name="qkprep",
    )(proj, proj, proj, cos128, sin128, qg128, kg128, seg)


def _attn_kernel(sink_ref, q_ref, kp_ref, kc_ref, kn_ref, vp_ref, vc_ref, vn_ref, o_ref, *, nq):
    i = pl.program_id(1)
    nk = 3 * ATTN_BLOCK
    qi = lax.broadcasted_iota(jnp.int32, (ATTN_BLOCK, nk), 0)
    si = lax.broadcasted_iota(jnp.int32, (ATTN_BLOCK, nk), 1)
    rel = qi - (si - ATTN_BLOCK)
    band = jnp.where(rel <= ATTN_BLOCK, jnp.where(rel >= -ATTN_BLOCK, 0.0, NEG_INF), NEG_INF)
    band = jnp.where(si < ATTN_BLOCK, jnp.where(i > 0, band, NEG_INF), band)
    bias = jnp.where(si >= 2 * ATTN_BLOCK, jnp.where(i < nq - 1, band, NEG_INF), band)
    low_q = lax.broadcasted_iota(jnp.int32, (ATTN_BLOCK, LANES), 1) < HEAD_DIM
    low_k = lax.broadcasted_iota(jnp.int32, (nk, LANES), 1) < HEAD_DIM
    zero_q = jnp.zeros((ATTN_BLOCK, LANES), BF16)
    zero_k = jnp.zeros((nk, LANES), BF16)

    nb = ATTN_BLOCK
    for h in range(N_KV_HEADS):
        sl = slice(h * LANES, (h + 1) * LANES)
        kd = jnp.concatenate([kp_ref[:, sl], kc_ref[:, sl], kn_ref[:, sl]], axis=0)
        vd = jnp.concatenate([vp_ref[:, sl], vc_ref[:, sl], vn_ref[:, sl]], axis=0)
        v_lo = jnp.where(low_k, vd, zero_k)
        v_hi = jnp.where(low_k, zero_k, vd)
        slabs = [q_ref[:, (2 * h + j) * LANES:(2 * h + j + 1) * LANES] for j in range(2)]
        q4 = jnp.concatenate([jnp.where(low_q, s_, zero_q) for s_ in slabs]
                             + [jnp.where(low_q, zero_q, s_) for s_ in slabs], axis=0)
        heads = (4 * h, 4 * h + 2, 4 * h + 1, 4 * h + 3)
        s4 = _dot_nt(q4, kd)
        ps, invs = [], []
        for k, head in enumerate(heads):
            s = s4[k * nb:(k + 1) * nb] + bias
            snk = sink_ref[head]
            m = jnp.maximum(jnp.max(s, axis=-1, keepdims=True), snk)
            p = jnp.exp(s - m)
            den = jnp.sum(p, axis=-1, keepdims=True) + jnp.exp(snk - m)
            ps.append(p.astype(BF16))
            invs.append(1.0 / den)
        o = _dot(jnp.concatenate(ps[:2], axis=0), v_lo) + _dot(jnp.concatenate(ps[2:], axis=0), v_hi)
        for j in range(2):
            oj = o[j * nb:(j + 1) * nb] * jnp.where(low_q, invs[j], invs[2 + j])
            o_ref[:, (2 * h + j) * LANES:(2 * h + j + 1) * LANES] = oj.astype(BF16)


def _attention(qr, kdup, vdup, sink, batch, seq):
    t = qr.shape[0]
    nq = seq // ATTN_BLOCK

    def prev(b, i, s):
        return (b * nq + jnp.maximum(i - 1, 0), 0)

    def cur(b, i, s):
        return (b * nq + i, 0)

    def nxt(b, i, s):
        return (b * nq + jnp.minimum(i + 1, nq - 1), 0)

    kv_block = (ATTN_BLOCK, 2 * KV_WIDTH)
    grid_spec = pltpu.PrefetchScalarGridSpec(
        num_scalar_prefetch=1,
        grid=(batch, nq),
        in_specs=[
            pl.BlockSpec((ATTN_BLOCK, ATTN_WIDTH), cur),
            pl.BlockSpec(kv_block, prev), pl.BlockSpec(kv_block, cur), pl.BlockSpec(kv_block, nxt),
            pl.BlockSpec(kv_block, prev), pl.BlockSpec(kv_block, cur), pl.BlockSpec(kv_block, nxt),
        ],
        out_specs=pl.BlockSpec((ATTN_BLOCK, ATTN_WIDTH), cur),
    )
    return pl.pallas_call(
        functools.partial(_attn_kernel, nq=nq),
        grid_spec=grid_spec,
        out_shape=jax.ShapeDtypeStruct((t, ATTN_WIDTH), BF16),
        compiler_params=_params(("arbitrary", "arbitrary")),
        name="attention",
    )(sink, qr, kdup, kdup, kdup, vdup, vdup, vdup)


def _conv_kernel(x_ref, w_ref, b_ref, o_ref, pad_ref, *, seq):
    halo = 8
    pad_ref[0:halo, :] = jnp.zeros((halo, CONV_CT), F32)
    pad_ref[seq + halo:seq + 2 * halo, :] = jnp.zeros((halo, CONV_CT), F32)

    def load(r, _):
        r0 = pl.multiple_of(r * CONV_ROWS, CONV_ROWS)
        pad_ref[pl.ds(r0 + halo, CONV_ROWS), :] = x_ref[pl.ds(r0, CONV_ROWS), :].astype(F32)
        return 0

    lax.fori_loop(0, seq // CONV_ROWS, load, 0)
    w = w_ref[...]
    bias = b_ref[...]

    for r in range(seq // CONV_ROWS):
        r0 = r * CONV_ROWS
        acc = jnp.broadcast_to(bias, (CONV_ROWS, CONV_CT))
        for k in range(CONV_W):
            acc = acc + pad_ref[r0 + halo + k - CONV_W // 2:r0 + halo + k - CONV_W // 2 + CONV_ROWS, :] * w[k:k + 1, :]
        o_ref[r0:r0 + CONV_ROWS, :] = (acc * _sigmoid(acc)).astype(BF16)


def _conv(proj, conv_w, conv_b, batch, seq):
    t = proj.shape[0]
    return pl.pallas_call(
        functools.partial(_conv_kernel, seq=seq),
        grid=(batch, CONV_DIM // CONV_CT),
        in_specs=[
            pl.BlockSpec((seq, CONV_CT), lambda b, c: (b, COL_XS // CONV_CT + c)),
            pl.BlockSpec((CONV_W, CONV_CT), lambda b, c: (0, c)),
            pl.BlockSpec((1, CONV_CT), lambda b, c: (0, c)),
        ],
        out_specs=pl.BlockSpec((seq, CONV_CT), lambda b, c: (b, c)),
        out_shape=jax.ShapeDtypeStruct((t, CONV_DIM), BF16),
        scratch_shapes=[pltpu.VMEM((seq + 16, CONV_CT), F32)],
        compiler_params=_params(("arbitrary", "arbitrary")),
        name="conv",
    )(proj, conv_w, conv_b)


def _split3(a):
    a1 = a.astype(BF16)
    r1 = a - a1.astype(F32)
    a2 = r1.astype(BF16)
    a3 = (r1 - a2.astype(F32)).astype(BF16)
    return a1, a2, a3


def _tri_matmul(tri, a):
    a1, a2, a3 = _split3(a)
    return _dot(tri, a1) + _dot(tri, a2) + _dot(tri, a3)


def _softplus(x):
    return jnp.maximum(x, 0.0) + jnp.log(1.0 + jnp.exp(-jnp.abs(x)))


LOG2_E = 1.4426950408889634


def _dt_and_rate(dt_ref, bias_ref, alog_ref):
    dt = _softplus(dt_ref[...] + bias_ref[...])
    rate = dt * (-LOG2_E * jnp.exp(alog_ref[...]))
    return dt, rate


def _head_rows(mat, first, rows):
    n = mat.shape[1]
    return jnp.concatenate(
        [jnp.broadcast_to(mat[first + e:first + e + 1, :], (rows, n)) for e in range(HEADS_PER_GROUP)], axis=0)


def _ssd_bwd_state_kernel(xs_ref, b_ref, dt_ref, bias_ref, alog_ref, tl_ref, hb_ref, st_ref):
    c = pl.program_id(1)

    @pl.when(c == 0)
    def _():
        st_ref[...] = jnp.zeros_like(st_ref)

    hb_ref[0] = st_ref[...].astype(BF16)
    dt, rate = _dt_and_rate(dt_ref, bias_ref, alog_ref)
    pre = _tri_matmul(tl_ref[...], rate)
    pre_t = pre.T
    excl_t = (pre - rate).T
    total = jnp.broadcast_to(pre_t[:, CHUNK - 1:CHUNK], (LANES, CHUNK))
    w_t = dt.T * jnp.exp2(excl_t)
    dec = jnp.exp2(total)
    off = N_SSM_HEADS
    for g in range(N_SSM_GROUPS):
        xs_t = xs_ref[:, g * 512:(g + 1) * 512].astype(F32).T
        xd = (xs_t * _head_rows(w_t, off + g * HEADS_PER_GROUP, SSM_HEAD_DIM)).astype(BF16)
        upd = _dot(xd, b_ref[:, g * D_STATE:(g + 1) * D_STATE])
        st_ref[g] = _head_rows(dec, off + g * HEADS_PER_GROUP, SSM_HEAD_DIM) * st_ref[g] + upd


def _ssd_bwd_states(xc, dt, bias128, alog128, tri_l, batch, seq):
    nc = seq // CHUNK

    def rev(b, c):
        return (b * nc + nc - 1 - c, 0)

    return pl.pallas_call(
        _ssd_bwd_state_kernel,
        grid=(batch, nc),
        in_specs=[
            pl.BlockSpec((CHUNK, D_INNER), rev),
            pl.BlockSpec((CHUNK, BC_WIDTH), lambda b, c: (b * nc + nc - 1 - c, D_INNER // BC_WIDTH)),
            pl.BlockSpec((CHUNK, LANES), rev),
            pl.BlockSpec((1, LANES), lambda b, c: (0, 0)),
            pl.BlockSpec((1, LANES), lambda b, c: (0, 0)),
            pl.BlockSpec((CHUNK, CHUNK), lambda b, c: (0, 0)),
        ],
        out_specs=pl.BlockSpec((1, N_SSM_GROUPS, 512, D_STATE), lambda b, c: (b * nc + nc - 1 - c, 0, 0, 0)),
        out_shape=jax.ShapeDtypeStruct((batch * nc, N_SSM_GROUPS, 512, D_STATE), BF16),
        scratch_shapes=[pltpu.VMEM((N_SSM_GROUPS, 512, D_STATE), F32)],
        compiler_params=_params(("arbitrary", "arbitrary")),
        name="ssd_bwd_states",
    )(xc, xc, dt, bias128, alog128, tri_l)


def _ssd_main_kernel(xc_ref, z_ref, dt_ref, hb_ref, bias_ref, alog_ref, tl_ref, tu_ref, dskip_ref, gain_ref,
                     o_ref, hf_ref, y_ref):
    c = pl.program_id(1)

    @pl.when(c == 0)
    def _():
        hf_ref[...] = jnp.zeros_like(hf_ref)

    dt, rate = _dt_and_rate(dt_ref, bias_ref, alog_ref)
    lane = lax.broadcasted_iota(jnp.int32, (CHUNK, LANES), 1)
    cum = jnp.where(lane < N_SSM_HEADS, _tri_matmul(tl_ref[...], rate), _tri_matmul(tu_ref[...], rate))
    cum_t = cum.T
    dt_t = dt.T
    src_t = cum_t - jnp.log2(dt_t)
    row = lax.broadcasted_iota(jnp.int32, (CHUNK, CHUNK), 0)
    col = lax.broadcasted_iota(jnp.int32, (CHUNK, CHUNK), 1)
    lower = row >= col
    diag = row == col
    low = lane < SSM_HEAD_DIM
    zero_x = jnp.zeros((CHUNK, LANES), BF16)
    nb = N_SSM_HEADS

    def lane_bcast(mat, idx):
        return jnp.broadcast_to(mat[:, idx:idx + 1], (CHUNK, CHUNK))

    def sub_bcast(mat, idx):
        return jnp.broadcast_to(mat[idx:idx + 1, :], (CHUNK, CHUNK))

    def head_matrix(e, cb):
        col_f = lane_bcast(cum, e)
        col_b = lane_bcast(cum, nb + e)
        decay = jnp.exp2(jnp.where(lower, col_f - sub_bcast(src_t, e), col_b - sub_bcast(src_t, nb + e)))
        decay = decay + jnp.where(diag, sub_bcast(dt_t, nb + e), 0.0)
        return (decay * cb).astype(BF16), col_f, col_b

    for g in range(N_SSM_GROUPS):
        bg = xc_ref[:, D_INNER + g * D_STATE:D_INNER + (g + 1) * D_STATE]
        cg = xc_ref[:, D_INNER + BC_WIDTH + g * D_STATE:D_INNER + BC_WIDTH + (g + 1) * D_STATE]
        cb = _dot_nt(cg, bg)
        y_in_f = _dot_nt(cg, hf_ref[g].astype(BF16))
        y_in_b = _dot_nt(cg, hb_ref[0, g])
        for jp in range(HEADS_PER_GROUP // 2):
            e0 = g * HEADS_PER_GROUP + 2 * jp
            cols = slice(e0 * SSM_HEAD_DIM, e0 * SSM_HEAD_DIM + LANES)
            loc = slice(jp * LANES, (jp + 1) * LANES)
            xs_pair = xc_ref[:, cols]
            m0, cf0, cb0 = head_matrix(e0, cb)
            m1, cf1, cb1 = head_matrix(e0 + 1, cb)
            y = _dot(m0, jnp.where(low, xs_pair, zero_x)) + _dot(m1, jnp.where(low, zero_x, xs_pair))
            y = y + y_in_f[:, loc] * jnp.exp2(jnp.where(low, cf0, cf1))
            y = y + y_in_b[:, loc] * jnp.exp2(jnp.where(low, cb0, cb1))
            y_ref[:, cols] = y + dskip_ref[:, cols] * xs_pair.astype(F32)

    z = z_ref[...].astype(F32)
    y = y_ref[...] * (z * _sigmoid(z))
    ms = jnp.mean(y * y, axis=-1, keepdims=True)
    o_ref[...] = (y * lax.rsqrt(ms + EPS) * gain_ref[...]).astype(BF16)

    last = jnp.broadcast_to(cum_t[:, CHUNK - 1:CHUNK], (LANES, CHUNK))
    w_t = jnp.exp2(last - src_t)
    dec = jnp.exp2(last)
    for g in range(N_SSM_GROUPS):
        xs_t = xc_ref[:, g * 512:(g + 1) * 512].astype(F32).T
        xd = (xs_t * _head_rows(w_t, g * HEADS_PER_GROUP, SSM_HEAD_DIM)).astype(BF16)
        upd = _dot(xd, xc_ref[:, D_INNER + g * D_STATE:D_INNER + (g + 1) * D_STATE])
        hf_ref[g] = _head_rows(dec, g * HEADS_PER_GROUP, SSM_HEAD_DIM) * hf_ref[g] + upd


def _ssd_main(xc, proj, dt, hb, bias128, alog128, tri_l, tri_u, dskip, gain, batch, seq):
    t = xc.shape[0]
    nc = seq // CHUNK

    def tok(b, c):
        return (b * nc + c, 0)

    def const(b, c):
        return (0, 0)

    return pl.pallas_call(
        _ssd_main_kernel,
        grid=(batch, nc),
        in_specs=[
            pl.BlockSpec((CHUNK, CONV_DIM), tok),
            pl.BlockSpec((CHUNK, D_INNER), tok),
            pl.BlockSpec((CHUNK, LANES), tok),
            pl.BlockSpec((1, N_SSM_GROUPS, 512, D_STATE), lambda b, c: (b * nc + c, 0, 0, 0)),
            pl.BlockSpec((1, LANES), const),
            pl.BlockSpec((1, LANES), const),
            pl.BlockSpec((CHUNK, CHUNK), const),
            pl.BlockSpec((CHUNK, CHUNK), const),
            pl.BlockSpec((1, D_INNER), const),
            pl.BlockSpec((1, D_INNER), const),
        ],
        out_specs=pl.BlockSpec((CHUNK, D_INNER), tok),
        out_shape=jax.ShapeDtypeStruct((t, D_INNER), BF16),
        scratch_shapes=[pltpu.VMEM((N_SSM_GROUPS, 512, D_STATE), F32), pltpu.VMEM((CHUNK, D_INNER), F32)],
        compiler_params=_params(("arbitrary", "arbitrary")),
        name="ssd_main",
    )(xc, proj, dt, hb, bias128, alog128, tri_l, tri_u, dskip, gain)


def _outproj_kernel(attn_a, ssm_a, gate_a, x_a, attn_b, ssm_b, gate_b, x_b, wa_ref, ws_ref, wo_ref, gn_ref,
                    wr1_ref, wr2_ref, br_ref, o_ref, r_ref, *, n_a):
    i = pl.program_id(0)

    @pl.when(i < n_a)
    def _():
        _outproj_tile(attn_a, ssm_a, gate_a, x_a, wa_ref, ws_ref, wo_ref, gn_ref, wr1_ref, wr2_ref, br_ref,
                      o_ref, r_ref)

    @pl.when(i >= n_a)
    def _():
        _outproj_tile(attn_b, ssm_b, gate_b, x_b, wa_ref, ws_ref, wo_ref, gn_ref, wr1_ref, wr2_ref, br_ref,
                      o_ref, r_ref)


def _outproj_tile(attn_ref, ssm_ref, gate_ref, x_ref, wa_ref, ws_ref, wo_ref, gn_ref, wr1_ref, wr2_ref, br_ref,
                  o_ref, r_ref):
    a_out = _dot(attn_ref[...], wa_ref[...])
    s_out = _dot(ssm_ref[...], ws_ref[...])
    ga = gate_ref[:, :D_MODEL].astype(F32)
    gs = gate_ref[:, D_MODEL:].astype(F32)
    merged = _sigmoid(ga) * a_out + _sigmoid(gs) * s_out
    x2 = x_ref[...] + _dot(merged.astype(BF16), wo_ref[...])
    for j in range(TOKEN_TILE):
        o_ref[pl.ds(j, TM_OUT, stride=TOKEN_TILE), :] = x2[:, j * LANES:(j + 1) * LANES]

    ms = jnp.mean(x2 * x2, axis=-1, keepdims=True)
    hn = x2 * lax.rsqrt(ms + EPS) * gn_ref[...]
    h1 = hn.astype(BF16)
    h2 = (hn - h1.astype(F32)).astype(BF16)
    lg = _dot(h1, wr1_ref[...]) + _dot(h2, wr1_ref[...]) + _dot(h1, wr2_ref[...]) + br_ref[...]

    lane = lax.broadcasted_iota(jnp.int32, (TM_OUT, LANES), 1).astype(F32)
    big = float(LANES)

    def rmax(v):
        return jnp.max(v, axis=-1, keepdims=True)

    def first_lane(mask):
        return jnp.min(jnp.where(mask, lane, big), axis=-1, keepdims=True)

    gl = jnp.where(lane < N_EXPERT_GROUPS, lg, NEG_INF)
    gmax = rmax(gl)
    g_w = 1.0 / jnp.sum(jnp.exp(gl - gmax), axis=-1, keepdims=True)
    gidx = first_lane(gl == gmax)
    base = N_EXPERT_GROUPS + EXPERTS_PER_GROUP * gidx
    el = jnp.where(lane >= base, jnp.where(lane < base + EXPERTS_PER_GROUP, lg, NEG_INF), NEG_INF)
    m1 = rmax(el)
    i1 = first_lane(el == m1)
    el2 = jnp.where(lane == i1, NEG_INF, el)
    m2 = rmax(el2)
    i2 = first_lane(el2 == m2)
    r = jnp.exp(m2 - m1)
    w1 = g_w / (1.0 + r)
    w2 = w1 * r
    j1 = i1 - base
    j2 = i2 - base
    swap = j1 > j2
    e_lo = jnp.where(swap, j2, j1)
    e_hi = jnp.where(swap, j1, j2)
    w_lo = jnp.where(swap, w2, w1)
    w_hi = jnp.where(swap, w1, w2)
    pair = e_lo * (EXPERTS_PER_GROUP - 1) - e_lo * (e_lo - 1.0) * 0.5 + (e_hi - e_lo - 1.0)
    cls = gidx * N_PAIRS + pair
    r_ref[...] = jnp.where(lane == 0.0, cls, jnp.where(lane == 1.0, w_lo, jnp.where(lane == 2.0, w_hi, 0.0)))


def _outproj(group_a, group_b, wa, ws, wo, gn, wr1, wr2, br):
    n_a = group_a[3].shape[0] // TM_OUT
    n_b = group_b[3].shape[0] // TM_OUT
    t = (n_a + n_b) * TM_OUT

    def first(i):
        return (jnp.minimum(i, n_a - 1), 0)

    def second(i):
        return (jnp.maximum(i - n_a, 0), 0)

    def const(i):
        return (0, 0)

    def group_specs(tok):
        return [
            pl.BlockSpec((TM_OUT, ATTN_WIDTH), tok),
            pl.BlockSpec((TM_OUT, D_INNER), tok),
            pl.BlockSpec((TM_OUT, 2 * D_MODEL), lambda i: (tok(i)[0], COL_GATE // (2 * D_MODEL))),
            pl.BlockSpec((TM_OUT, D_MODEL), tok),
        ]

    resident = dict(pipeline_mode=pl.Buffered(1))
    return pl.pallas_call(
        functools.partial(_outproj_kernel, n_a=n_a),
        grid=(n_a + n_b,),
        in_specs=group_specs(first) + group_specs(second) + [
            pl.BlockSpec((ATTN_WIDTH, D_MODEL), const, **resident),
            pl.BlockSpec((D_INNER, D_MODEL), const, **resident),
            pl.BlockSpec((D_MODEL, D_MODEL), const, **resident),
            pl.BlockSpec((1, D_MODEL), const),
            pl.BlockSpec((D_MODEL, LANES), const),
            pl.BlockSpec((D_MODEL, LANES), const),
            pl.BlockSpec((1, LANES), const),
        ],
        out_specs=[
            pl.BlockSpec((TM_OUT * TOKEN_TILE, LANES), lambda i: (i, 0)),
            pl.BlockSpec((TM_OUT, LANES), lambda i: (i, 0)),
        ],
        out_shape=[
            jax.ShapeDtypeStruct((t * TOKEN_TILE, LANES), F32),
            jax.ShapeDtypeStruct((t, LANES), F32),
        ],
        compiler_params=_params(("arbitrary",)),
        name="outproj_router",
    )(*group_a, *group_b, wa, ws, wo, gn, wr1, wr2, br)


def _moe_kernel(ea_ref, eb_ref, nv_ref, tokc_ref, tokn_ref, roww_ref, x_ref, gn_ref,
                wga_ref, wua_ref, wda_ref, wgb_ref, wub_ref, wdb_ref, o_ref,
                xg_ref, st_ref, gsem, ssem, *, n_blocks):
    i = pl.program_id(0)
    slot = i % 2
    other = 1 - slot

    def tile(idx):
        return pl.ds(pl.multiple_of(idx * TOKEN_TILE, TOKEN_TILE), TOKEN_TILE)

    def gather_copy(tok, r, s):
        return pltpu.make_async_copy(x_ref.at[tile(tok), :], xg_ref.at[s, tile(r), :], gsem.at[s])

    def scatter_copy(tok, r, s):
        return pltpu.make_async_copy(st_ref.at[s, tile(r), :], o_ref.at[tile(tok), :], ssem.at[s])

    def for_rows(n, fn):
        n8 = lax.shift_right_logical(n, 3)

        def body8(g, _):
            for u in range(8):
                fn(g * 8 + u)
            return 0

        def body1(r, _):
            fn(r)
            return 0

        lax.fori_loop(0, n8, body8, 0)
        lax.fori_loop(n8 * 8, n, body1, 0)

    def start_gathers(tok_ref, n, s):
        for_rows(n, lambda r: gather_copy(tok_ref[0, 0, r], r, s).start())

    def wait_gathers(n, s):
        for_rows(n, lambda r: gather_copy(0, 0, s).wait())

    def start_scatters(tok_ref, n, s):
        for_rows(n, lambda r: scatter_copy(tok_ref[0, 0, r], r, s).start())

    def wait_scatters(n, s):
        for_rows(n, lambda r: scatter_copy(0, 0, s).wait())

    @pl.when(i == 0)
    def _():
        xg_ref[...] = jnp.zeros_like(xg_ref)
        start_gathers(tokc_ref, nv_ref[0], 0)

    @pl.when(i + 1 < n_blocks)
    def _():
        start_gathers(tokn_ref, nv_ref[jnp.minimum(i + 1, n_blocks - 1)], other)

    wait_gathers(nv_ref[i], slot)

    @pl.when(i >= 2)
    def _():
        wait_scatters(nv_ref[jnp.maximum(i - 2, 0)], slot)

    @pl.when(nv_ref[i] > 0)
    def _():
        x = jnp.concatenate(
            [xg_ref[slot, pl.ds(j, ROW_BLOCK, stride=TOKEN_TILE), :] for j in range(TOKEN_TILE)], axis=1)
        w_cols = jnp.concatenate([roww_ref[0], jnp.zeros((LANES - TOKEN_TILE, ROW_BLOCK), F32)], axis=0).T
        w_lo = w_cols[:, 0:1]
        w_hi = w_cols[:, 1:2]
        ms = jnp.mean(x * x, axis=-1, keepdims=True)
        hn = (x * lax.rsqrt(ms + EPS) * gn_ref[...]).astype(BF16)

        def expert(wg_ref, wu_ref, wd_ref):
            gte = _dot(hn, wg_ref[0])
            up = _dot(hn, wu_ref[0])
            return _dot((gte * _sigmoid(gte) * up).astype(BF16), wd_ref[0])

        ya = expert(wga_ref, wua_ref, wda_ref)
        yb = expert(wgb_ref, wub_ref, wdb_ref)
        out = x + w_lo * ya + w_hi * yb
        for j in range(TOKEN_TILE):
            st_ref[slot, pl.ds(j, ROW_BLOCK, stride=TOKEN_TILE), :] = out[:, j * LANES:(j + 1) * LANES]

    start_scatters(tokc_ref, nv_ref[i], slot)

    @pl.when(i == n_blocks - 1)
    def _():
        wait_scatters(nv_ref[jnp.maximum(i - 1, 0)], other)
        wait_scatters(nv_ref[i], slot)


def _moe(ea, eb, nvalid, row_tok, row_w, x2t, gn, wg, wu, wd):
    n_blocks = row_tok.shape[0]

    def wa(i, ea, eb, nv):
        return (ea[i], 0, 0)

    def wb(i, ea, eb, nv):
        return (eb[i], 0, 0)

    any_spec = pl.BlockSpec(memory_space=pl.ANY)
    grid_spec = pltpu.PrefetchScalarGridSpec(
        num_scalar_prefetch=3,
        grid=(n_blocks,),
        in_specs=[
            pl.BlockSpec((1, 1, ROW_BLOCK), lambda i, ea, eb, nv: (i, 0, 0), memory_space=pltpu.SMEM),
            pl.BlockSpec((1, 1, ROW_BLOCK), lambda i, ea, eb, nv: (jnp.minimum(i + 1, n_blocks - 1), 0, 0),
                         memory_space=pltpu.SMEM),
            pl.BlockSpec((1, TOKEN_TILE, ROW_BLOCK), lambda i, ea, eb, nv: (i, 0, 0)),
            any_spec,
            pl.BlockSpec((1, D_MODEL), lambda i, ea, eb, nv: (0, 0)),
            pl.BlockSpec((1, D_MODEL, D_EXPERT), wa), pl.BlockSpec((1, D_MODEL, D_EXPERT), wa),
            pl.BlockSpec((1, D_EXPERT, D_MODEL), wa),
            pl.BlockSpec((1, D_MODEL, D_EXPERT), wb), pl.BlockSpec((1, D_MODEL, D_EXPERT), wb),
            pl.BlockSpec((1, D_EXPERT, D_MODEL), wb),
        ],
        out_specs=any_spec,
        scratch_shapes=[
            pltpu.VMEM((2, ROW_BLOCK * TOKEN_TILE, LANES), F32),
            pltpu.VMEM((2, ROW_BLOCK * TOKEN_TILE, LANES), F32),
            pltpu.SemaphoreType.DMA((2,)),
            pltpu.SemaphoreType.DMA((2,)),
        ],
    )
    return pl.pallas_call(
        functools.partial(_moe_kernel, n_blocks=n_blocks),
        grid_spec=grid_spec,
        out_shape=jax.ShapeDtypeStruct(x2t.shape, F32),
        compiler_params=_params(("arbitrary",)),
        name="moe",
    )(ea, eb, nvalid, row_tok, row_tok, row_w, x2t, gn, wg, wu, wd, wg, wu, wd)


def _untile_kernel(x_ref, o_ref):
    for j in range(TOKEN_TILE):
        o_ref[:, j * LANES:(j + 1) * LANES] = x_ref[pl.ds(j, TM_OUT, stride=TOKEN_TILE), :]


def _untile(y_tiles, first_token, n_tokens):
    first_block = first_token // TM_OUT
    return pl.pallas_call(
        _untile_kernel,
        grid=(n_tokens // TM_OUT,),
        in_specs=[pl.BlockSpec((TM_OUT * TOKEN_TILE, LANES), lambda i: (first_block + i, 0))],
        out_specs=pl.BlockSpec((TM_OUT, D_MODEL), lambda i: (i, 0)),
        out_shape=jax.ShapeDtypeStruct((n_tokens, D_MODEL), F32),
        compiler_params=_params(("arbitrary",)),
        name="untile",
    )(y_tiles)


def _pair_tables():
    lo, hi = [], []
    for a in range(EXPERTS_PER_GROUP):
        for b in range(a + 1, EXPERTS_PER_GROUP):
            lo.append(a)
            hi.append(b)
    return np.asarray(lo, np.int32), np.asarray(hi, np.int32)


def _block_tables(rinfo):
    cls = rinfo[:, 0].astype(jnp.int32)
    t = cls.shape[0]
    n_blocks = t // ROW_BLOCK + N_CLASSES
    sorted_cls, order = lax.sort((cls, jnp.arange(t, dtype=jnp.int32)), num_keys=1)
    starts = jnp.searchsorted(sorted_cls, jnp.arange(N_CLASSES + 1, dtype=jnp.int32), side='left').astype(jnp.int32)
    counts = starts[1:] - starts[:-1]
    nblk = (counts + ROW_BLOCK - 1) // ROW_BLOCK
    blk_end = jnp.cumsum(nblk)
    blk_start = blk_end - nblk
    used = blk_end[-1]
    b = jnp.arange(n_blocks, dtype=jnp.int32)
    b_eff = jnp.minimum(b, used - 1)
    c = jnp.minimum(jnp.searchsorted(blk_end, b_eff, side='right'), N_CLASSES - 1).astype(jnp.int32)
    off = b_eff - blk_start[c]
    src = starts[c] + off * ROW_BLOCK
    nvalid = jnp.where(b < used, jnp.clip(counts[c] - off * ROW_BLOCK, 0, ROW_BLOCK), 0).astype(jnp.int32)
    pair_lo, pair_hi = _pair_tables()
    grp = c // N_PAIRS
    ea = (grp * EXPERTS_PER_GROUP + jnp.asarray(pair_lo)[c % N_PAIRS]).astype(jnp.int32)
    eb = (grp * EXPERTS_PER_GROUP + jnp.asarray(pair_hi)[c % N_PAIRS]).astype(jnp.int32)
    rows = jnp.clip(src[:, None] + jnp.arange(ROW_BLOCK, dtype=jnp.int32)[None, :], 0, t - 1)
    row_tok = order[rows]
    row_w = jnp.zeros((n_blocks, TOKEN_TILE, ROW_BLOCK), F32)
    row_w = row_w.at[:, 0, :].set(rinfo[:, 1][row_tok]).at[:, 1, :].set(rinfo[:, 2][row_tok])
    return ea, eb, nvalid, row_tok.reshape(n_blocks, 1, ROW_BLOCK), row_w


def _rope_tables(seq):
    inv = 1.0 / (ROPE_THETA ** (jnp.arange(0, HEAD_DIM, 2, dtype=F32) / HEAD_DIM))
    ang = jnp.arange(seq, dtype=F32)[:, None] * inv[None, :]
    cos, sin = jnp.cos(ang), jnp.sin(ang)
    cos128 = jnp.concatenate([cos, cos, cos, cos], axis=-1)
    sin128 = jnp.concatenate([-sin, sin, -sin, sin], axis=-1)
    return cos128, sin128


def _prepare_weights(norm_mix, w_in, q_norm, k_norm, attn_sink, conv_w, conv_b, a_log_fwd, a_log_bwd,
                     dt_bias_fwd, dt_bias_bwd, d_skip, ssm_norm, w_out_attn, w_out_ssm, w_o, norm_ffn,
                     w_router_group, b_router_group, w_router_expert, b_router_expert, w_gate, w_up, w_down):
    o_q = 0
    o_k = o_q + ATTN_WIDTH
    o_v = o_k + KV_WIDTH
    o_z = o_v + KV_WIDTH
    o_xbc = o_z + D_INNER
    o_dtf = o_xbc + CONV_DIM
    o_dtb = o_dtf + N_SSM_HEADS
    o_ga = o_dtb + N_SSM_HEADS
    o_gs = o_ga + D_MODEL
    w = w_in
    w_r = jnp.concatenate([
        w[:, o_z:o_z + D_INNER], w[:, o_ga:o_gs + D_MODEL], w[:, o_xbc:o_xbc + CONV_DIM],
        w[:, o_q:o_q + ATTN_WIDTH], w[:, o_k:o_k + KV_WIDTH], w[:, o_v:o_v + KV_WIDTH],
        w[:, o_dtf:o_dtb + N_SSM_HEADS], jnp.zeros((D_MODEL, LANES - 2 * N_SSM_HEADS), w.dtype)], axis=1)
    pad64 = jnp.zeros((LANES - 2 * N_SSM_HEADS,), F32)
    eye = np.kron(np.eye(2, dtype=np.float32), np.ones((HEAD_DIM, HEAD_DIM), np.float32))
    idx = np.arange(CHUNK)
    w_router = jnp.concatenate([w_router_group, w_router_expert,
                                jnp.zeros((D_MODEL, LANES - N_EXPERT_GROUPS - N_EXPERTS), F32)], axis=1)
    wr1 = w_router.astype(BF16)
    return dict(
        norm_mix=norm_mix.reshape(1, D_MODEL),
        w_in=w_r.astype(BF16),
        qg128=jnp.tile(q_norm, 2).reshape(1, LANES),
        kg128=jnp.tile(k_norm, 2).reshape(1, LANES),
        seg=jnp.asarray(eye, BF16),
        sink=attn_sink.astype(F32),
        conv_w=conv_w,
        conv_b=conv_b.reshape(1, CONV_DIM),
        alog128=jnp.concatenate([a_log_fwd, a_log_bwd, pad64]).reshape(1, LANES),
        bias128=jnp.concatenate([dt_bias_fwd, dt_bias_bwd, pad64]).reshape(1, LANES),
        tri_l=jnp.asarray(idx[:, None] >= idx[None, :], BF16),
        tri_u=jnp.asarray(idx[:, None] <= idx[None, :], BF16),
        dskip=jnp.repeat(d_skip, SSM_HEAD_DIM).reshape(1, D_INNER),
        ssm_norm=ssm_norm.reshape(1, D_INNER),
        wa=w_out_attn.astype(BF16), ws=w_out_ssm.astype(BF16), wo=w_o.astype(BF16),
        norm_ffn=norm_ffn.reshape(1, D_MODEL),
        wr1=wr1, wr2=(w_router - wr1.astype(F32)).astype(BF16),
        br=jnp.concatenate([b_router_group, b_router_expert,
                            jnp.zeros((LANES - N_EXPERT_GROUPS - N_EXPERTS,), F32)]).reshape(1, LANES),
        wg=w_gate.astype(BF16), wu=w_up.astype(BF16), wd=w_down.astype(BF16),
    )


def _mixer(x, p):
    batch, seq, _ = x.shape
    x2d = x.reshape(batch * seq, D_MODEL)
    proj, dt = _inproj(x2d, p['norm_mix'], p['w_in'])
    cos128, sin128 = _rope_tables(seq)
    qr, kdup, vdup = _qkprep(proj, cos128, sin128, p['qg128'], p['kg128'], p['seg'], seq)
    attn = _attention(qr, kdup, vdup, p['sink'], batch, seq)
    xc = _conv(proj, p['conv_w'], p['conv_b'], batch, seq)
    hb = _ssd_bwd_states(xc, dt, p['bias128'], p['alog128'], p['tri_l'], batch, seq)
    ssm = _ssd_main(xc, proj, dt, hb, p['bias128'], p['alog128'], p['tri_l'], p['tri_u'], p['dskip'],
                    p['ssm_norm'], batch, seq)
    return attn, ssm, proj, x2d


def kernel(x_prompt, x_sample, norm_mix, w_in, q_norm, k_norm, attn_sink, conv_w, conv_b, a_log_fwd, a_log_bwd,
           dt_bias_fwd, dt_bias_bwd, d_skip, ssm_norm, w_out_attn, w_out_ssm, w_o, norm_ffn, w_router_group,
           b_router_group, w_router_expert, b_router_expert, w_gate, w_up, w_down):
    assert norm_mix.shape[0] == 1, "single-layer encoder"
    p = _prepare_weights(norm_mix[0], w_in[0], q_norm[0], k_norm[0], attn_sink[0], conv_w[0], conv_b[0],
                         a_log_fwd[0], a_log_bwd[0], dt_bias_fwd[0], dt_bias_bwd[0], d_skip[0], ssm_norm[0],
                         w_out_attn[0], w_out_ssm[0], w_o[0], norm_ffn[0], w_router_group[0], b_router_group[0],
                         w_router_expert[0], b_router_expert[0], w_gate[0], w_up[0], w_down[0])
    x2t, rinfo = _outproj(_mixer(x_prompt, p), _mixer(x_sample, p), p['wa'], p['ws'], p['wo'], p['norm_ffn'],
                          p['wr1'], p['wr2'], p['br'])
    ea, eb, nvalid, row_tok, row_w = _block_tables(rinfo)
    y = _moe(ea, eb, nvalid, row_tok, row_w, x2t, p['norm_ffn'], p['wg'], p['wu'], p['wd'])
    t_a = x_prompt.shape[0] * x_prompt.shape[1]
    t_b = x_sample.shape[0] * x_sample.shape[1]
    return _untile(y, 0, t_a).reshape(x_prompt.shape), _untile(y, t_a, t_b).reshape(x_sample.shape)
```

```python
import functools

import numpy as np
import jax
import jax.numpy as jnp
from jax import lax
from jax.experimental import pallas as pl
from jax.experimental.pallas import tpu as pltpu

F32 = jnp.float32
BF16 = jnp.bfloat16

D_MODEL = 1024
EPS = 1e-6
NEG_INF = -1e30
LOG2_E = 1.4426950408889634
N_Q_HEADS = 16
N_KV_HEADS = 4
HEAD_DIM = 64
ATTN_WIDTH = N_Q_HEADS * HEAD_DIM
KV_WIDTH = N_KV_HEADS * HEAD_DIM
ATTN_BLOCK = 128
ROPE_THETA = 10000.0
D_INNER = 2 * D_MODEL
SSM_HEAD_DIM = 64
N_SSM_HEADS = D_INNER // SSM_HEAD_DIM
N_SSM_GROUPS = 4
HEADS_PER_GROUP = N_SSM_HEADS // N_SSM_GROUPS
D_STATE = 128
BC_WIDTH = N_SSM_GROUPS * D_STATE
CONV_DIM = D_INNER + 2 * BC_WIDTH
CONV_W = 7
CHUNK = 128
N_EXPERT_GROUPS = 4
EXPERTS_PER_GROUP = 8
N_EXPERTS = N_EXPERT_GROUPS * EXPERTS_PER_GROUP
D_EXPERT = 512
N_PAIRS = EXPERTS_PER_GROUP * (EXPERTS_PER_GROUP - 1) // 2
N_CLASSES = N_EXPERT_GROUPS * N_PAIRS

LANES = 128
V7X_VMEM_LIMIT_BYTES = 56 * 1024 * 1024

COL_Z = 0
COL_GATE = COL_Z + D_INNER
COL_XS = COL_GATE + 2 * D_MODEL
COL_B = COL_XS + D_INNER
COL_C = COL_B + BC_WIDTH
COL_Q = COL_C + BC_WIDTH
COL_K = COL_Q + ATTN_WIDTH
COL_V = COL_K + KV_WIDTH
COL_DT = COL_V + KV_WIDTH
N_PROJ = COL_DT + LANES

TM_IN = 1024
NJ_IN = 3
TN_IN = N_PROJ // NJ_IN
CH_IN = 512
TM_QK = 512
TM_OUT = 512
CONV_CT = 256
CONV_ROWS = 256
CONV_PITCH = 2
ROW_BLOCK = 128
TOKEN_TILE = D_MODEL // LANES


def _params(sem, flags=None):
    return pltpu.CompilerParams(dimension_semantics=sem, vmem_limit_bytes=V7X_VMEM_LIMIT_BYTES, flags=flags)


def _dot(a, b):
    return jnp.dot(a, b, preferred_element_type=F32)


def _dot_nt(a, b):
    return lax.dot_general(a, b, (((1,), (1,)), ((), ())), preferred_element_type=F32)


def _sigmoid(x):
    return 1.0 / (1.0 + jnp.exp(-x))


def _inproj_kernel(x_ref, g_ref, w_ref, o_ref, dt_ref, h_ref):
    j = pl.program_id(1)

    @pl.when(j == 0)
    def _():
        x = x_ref[...]
        ms = jnp.mean(x * x, axis=-1, keepdims=True)
        h_ref[...] = (x * lax.rsqrt(ms + EPS) * g_ref[...]).astype(BF16)

    for c0 in range(0, TN_IN, CH_IN):
        c1 = min(c0 + CH_IN, TN_IN)
        acc = _dot(h_ref[...], w_ref[:, c0:c1])
        o_ref[:, c0:c1] = acc.astype(BF16)
        if c1 == TN_IN:
            @pl.when(j == NJ_IN - 1)
            def _():
                dt_ref[...] = acc[:, c1 - c0 - LANES:]


def _inproj(x2d, gain, w_bf16):
    t = x2d.shape[0]
    return pl.pallas_call(
        _inproj_kernel,
        grid=(t // TM_IN, NJ_IN),
        in_specs=[
            pl.BlockSpec((TM_IN, D_MODEL), lambda i, j: (i, 0)),
            pl.BlockSpec((1, D_MODEL), lambda i, j: (0, 0)),
            pl.BlockSpec((D_MODEL, TN_IN), lambda i, j: (0, j)),
        ],
        out_specs=[
            pl.BlockSpec((TM_IN, TN_IN), lambda i, j: (i, j)),
            pl.BlockSpec((TM_IN, LANES), lambda i, j: (i, 0)),
        ],
        out_shape=[
            jax.ShapeDtypeStruct((t, N_PROJ), BF16),
            jax.ShapeDtypeStruct((t, LANES), F32),
        ],
        scratch_shapes=[pltpu.VMEM((TM_IN, D_MODEL), BF16)],
        compiler_params=_params(("arbitrary", "arbitrary")),
        name="inproj",
    )(x2d, gain, w_bf16)


def _qkprep_kernel(q_ref, k_ref, v_ref, cos_ref, sin_ref, qg_ref, kg_ref, seg_ref, qo_ref, ko_ref, vo_ref):
    cos = cos_ref[...]
    sin = sin_ref[...]
    seg = seg_ref[...]
    lane = lax.broadcasted_iota(jnp.int32, (TM_QK, LANES), 1)
    first_half = (lane % HEAD_DIM) < (HEAD_DIM // 2)
    low = lane < HEAD_DIM

    def norm_rope(x, gain):
        ss = _dot((x * x).astype(BF16), seg)
        xn = x * lax.rsqrt(ss * (1.0 / HEAD_DIM) + EPS) * gain
        rot = jnp.where(first_half, pltpu.roll(xn, 96, 1), pltpu.roll(xn, 32, 1))
        return xn * cos + rot * sin

    for s in range(ATTN_WIDTH // LANES):
        sl = slice(s * LANES, (s + 1) * LANES)
        y = norm_rope(q_ref[:, sl].astype(F32), qg_ref[...]) * (HEAD_DIM ** -0.5 * LOG2_E)
        qo_ref[:, sl] = y.astype(BF16)
    for s in range(KV_WIDTH // LANES):
        sl = slice(s * LANES, (s + 1) * LANES)
        y = norm_rope(k_ref[:, sl].astype(F32), kg_ref[...])
        ysw = pltpu.roll(y, HEAD_DIM, 1)
        ko_ref[:, (2 * s) * LANES:(2 * s + 1) * LANES] = jnp.where(low, y, ysw).astype(BF16)
        ko_ref[:, (2 * s + 1) * LANES:(2 * s + 2) * LANES] = jnp.where(low, ysw, y).astype(BF16)
        v = v_ref[:, sl].astype(F32)
        vsw = pltpu.roll(v, HEAD_DIM, 1)
        vo_ref[:, (2 * s) * LANES:(2 * s + 1) * LANES] = jnp.where(low, v, vsw).astype(BF16)
        vo_ref[:, (2 * s + 1) * LANES:(2 * s + 2) * LANES] = jnp.where(low, vsw, v).astype(BF16)


def _qkprep(proj, cos128, sin128, qg128, kg128, seg, seq):
    t = proj.shape[0]
    nseq = seq // TM_QK
    return pl.pallas_call(
        _qkprep_kernel,
        grid=(t // TM_QK,),
        in_specs=[
            pl.BlockSpec((TM_QK, ATTN_WIDTH), lambda i: (i, COL_Q // ATTN_WIDTH)),
            pl.BlockSpec((TM_QK, KV_WIDTH), lambda i: (i, COL_K // KV_WIDTH)),
            pl.BlockSpec((TM_QK, KV_WIDTH), lambda i: (i, COL_V // KV_WIDTH)),
            pl.BlockSpec((TM_QK, LANES), lambda i: (i % nseq, 0)),
            pl.BlockSpec((TM_QK, LANES), lambda i: (i % nseq, 0)),
            pl.BlockSpec((1, LANES), lambda i: (0, 0)),
            pl.BlockSpec((1, LANES), lambda i: (0, 0)),
            pl.BlockSpec((LANES, LANES), lambda i: (0, 0)),
        ],
        out_specs=[
            pl.BlockSpec((TM_QK, ATTN_WIDTH), lambda i: (i, 0)),
            pl.BlockSpec((TM_QK, 2 * KV_WIDTH), lambda i: (i, 0)),
            pl.BlockSpec((TM_QK, 2 * KV_WIDTH), lambda i: (i, 0)),
        ],
        out_shape=[
            jax.ShapeDtypeStruct((t, ATTN_WIDTH), BF16),
            jax.ShapeDtypeStruct((t, 2 * KV_WIDTH), BF16),
            jax.ShapeDtypeStruct((t, 2 * KV_WIDTH), BF16),
        ],
        compiler_params=_params(("arbitrary",)),
        name="qkprep",
    )(proj, proj, proj, cos128, sin128, qg128, kg128, seg)


def _attn_kernel(sink_ref, q_ref, kp_ref, kc_ref, kn_ref, vp_ref, vc_ref, vn_ref, o_ref, *, nq):
    i = pl.program_id(1)
    nk = 3 * ATTN_BLOCK
    qi = lax.broadcasted_iota(jnp.int32, (ATTN_BLOCK, nk), 0)
    si = lax.broadcasted_iota(jnp.int32, (ATTN_BLOCK, nk), 1)
    rel = qi - (si - ATTN_BLOCK)
    band = jnp.where(rel <= ATTN_BLOCK, jnp.where(rel >= -ATTN_BLOCK, 0.0, NEG_INF), NEG_INF)
    band = jnp.where(si < ATTN_BLOCK, jnp.where(i > 0, band, NEG_INF), band)
    bias = jnp.where(si >= 2 * ATTN_BLOCK, jnp.where(i < nq - 1, band, NEG_INF), band)
    low_q = lax.broadcasted_iota(jnp.int32, (ATTN_BLOCK, LANES), 1) < HEAD_DIM
    low_k = lax.broadcasted_iota(jnp.int32, (nk, LANES), 1) < HEAD_DIM
    zero_q = jnp.zeros((ATTN_BLOCK, LANES), BF16)
    zero_k = jnp.zeros((nk, LANES), BF16)

    nb = ATTN_BLOCK

    def scores(h):
        sl = slice(h * LANES, (h + 1) * LANES)
        kd = jnp.concatenate([kp_ref[:, sl], kc_ref[:, sl], kn_ref[:, sl]], axis=0)
        slabs = [q_ref[:, (2 * h + j) * LANES:(2 * h + j + 1) * LANES] for j in range(2)]
        q4 = jnp.concatenate([jnp.where(low_q, s_, zero_q) for s_ in slabs]
                             + [jnp.where(low_q, zero_q, s_) for s_ in slabs], axis=0)
        return _dot_nt(q4, kd)

    def finish(h, s4):
        sl = slice(h * LANES, (h + 1) * LANES)
        vd = jnp.concatenate([vp_ref[:, sl], vc_ref[:, sl], vn_ref[:, sl]], axis=0)
        v_lo = jnp.where(low_k, vd, zero_k)
        v_hi = jnp.where(low_k, zero_k, vd)
        heads = (4 * h, 4 * h + 2, 4 * h + 1, 4 * h + 3)
        ps, invs = [], []
        for k, head in enumerate(heads):
            s = s4[k * nb:(k + 1) * nb]
            s = jnp.concatenate([s[:, :nb] + bias[:, :nb], s[:, nb:2 * nb], s[:, 2 * nb:] + bias[:, 2 * nb:]], axis=1)
            snk = sink_ref[head] * LOG2_E
            m = jnp.maximum(jnp.max(s, axis=-1, keepdims=True), snk)
            p = jnp.exp2(s - m)
            den = jnp.sum(p, axis=-1, keepdims=True) + jnp.exp2(snk - m)
            ps.append(p.astype(BF16))
            invs.append(1.0 / den)
        o = _dot(jnp.concatenate(ps[:2], axis=0), v_lo) + _dot(jnp.concatenate(ps[2:], axis=0), v_hi)
        for j in range(2):
            oj = o[j * nb:(j + 1) * nb] * jnp.where(low_q, invs[j], invs[2 + j])
            o_ref[:, (2 * h + j) * LANES:(2 * h + j + 1) * LANES] = oj.astype(BF16)

    ahead = 1
    pending = [scores(h) for h in range(ahead)]
    for h in range(N_KV_HEADS):
        if h + ahead < N_KV_HEADS:
            pending.append(scores(h + ahead))
        finish(h, pending.pop(0))


def _attention(qr, kdup, vdup, sink, batch, seq):
    t = qr.shape[0]
    nq = seq // ATTN_BLOCK

    def prev(b, i, s):
        return (b * nq + jnp.maximum(i - 1, 0), 0)

    def cur(b, i, s):
        return (b * nq + i, 0)

    def nxt(b, i, s):
        return (b * nq + jnp.minimum(i + 1, nq - 1), 0)

    kv_block = (ATTN_BLOCK, 2 * KV_WIDTH)
    grid_spec = pltpu.PrefetchScalarGridSpec(
        num_scalar_prefetch=1,
        grid=(batch, nq),
        in_specs=[
            pl.BlockSpec((ATTN_BLOCK, ATTN_WIDTH), cur),
            pl.BlockSpec(kv_block, prev), pl.BlockSpec(kv_block, cur), pl.BlockSpec(kv_block, nxt),
            pl.BlockSpec(kv_block, prev), pl.BlockSpec(kv_block, cur), pl.BlockSpec(kv_block, nxt),
        ],
        out_specs=pl.BlockSpec((ATTN_BLOCK, ATTN_WIDTH), cur),
    )
    return pl.pallas_call(
        functools.partial(_attn_kernel, nq=nq),
        grid_spec=grid_spec,
        out_shape=jax.ShapeDtypeStruct((t, ATTN_WIDTH), BF16),
        compiler_params=_params(("arbitrary", "arbitrary")),
        name="attention",
    )(sink, qr, kdup, kdup, kdup, vdup, vdup, vdup)


def _conv_kernel(x_ref, w_ref, b_ref, o_ref, pad_ref, *, seq):
    halo = 8
    step = CONV_PITCH

    def rows(first, n):
        return pl.ds(step * (first + halo), n, stride=step)

    for h in range(CONV_CT // LANES):
        lanes = slice(h * LANES, (h + 1) * LANES)
        pad_ref[h, rows(-halo, halo), :] = jnp.zeros((halo, LANES), F32)
        pad_ref[h, rows(seq, halo), :] = jnp.zeros((halo, LANES), F32)
        for r in range(seq // CONV_ROWS):
            pad_ref[h, rows(r * CONV_ROWS, CONV_ROWS), :] = x_ref[r * CONV_ROWS:(r + 1) * CONV_ROWS, lanes].astype(F32)
    w = w_ref[...]
    bias = b_ref[...]
    for h in range(CONV_CT // LANES):
        lanes = slice(h * LANES, (h + 1) * LANES)
        for r in range(seq // CONV_ROWS):
            r0 = r * CONV_ROWS
            acc = jnp.broadcast_to(bias[:, lanes], (CONV_ROWS, LANES))
            for k in range(CONV_W):
                acc = acc + pad_ref[h, rows(r0 + k - CONV_W // 2, CONV_ROWS), :] * w[k:k + 1, lanes]
            o_ref[r0:r0 + CONV_ROWS, lanes] = (acc * _sigmoid(acc)).astype(BF16)


def _conv(proj, conv_w, conv_b, batch, seq):
    t = proj.shape[0]
    return pl.pallas_call(
        functools.partial(_conv_kernel, seq=seq),
        grid=(batch, CONV_DIM // CONV_CT),
        in_specs=[
            pl.BlockSpec((seq, CONV_CT), lambda b, c: (b, COL_XS // CONV_CT + c)),
            pl.BlockSpec((CONV_W, CONV_CT), lambda b, c: (0, c)),
            pl.BlockSpec((1, CONV_CT), lambda b, c: (0, c)),
        ],
        out_specs=pl.BlockSpec((seq, CONV_CT), lambda b, c: (b, c)),
        out_shape=jax.ShapeDtypeStruct((t, CONV_DIM), BF16),
        scratch_shapes=[pltpu.VMEM((CONV_CT // LANES, CONV_PITCH * (seq + 16), LANES), F32)],
        compiler_params=_params(("arbitrary", "arbitrary")),
        name="conv",
    )(proj, conv_w, conv_b)


def _split3(a):
    a1 = a.astype(BF16)
    r1 = a - a1.astype(F32)
    a2 = r1.astype(BF16)
    a3 = (r1 - a2.astype(F32)).astype(BF16)
    return a1, a2, a3


def _tri_matmul(tri, a):
    a1, a2, a3 = _split3(a)
    return _dot(tri, a1) + _dot(tri, a2) + _dot(tri, a3)


def _softplus(x):
    return jnp.maximum(x, 0.0) + jnp.log(1.0 + jnp.exp(-jnp.abs(x)))


def _dt_and_rate(dt_ref, bias_ref, alog_ref):
    dt = _softplus(dt_ref[...] + bias_ref[...])
    rate = dt * (-LOG2_E * jnp.exp(alog_ref[...]))
    return dt, rate


def _head_rows(mat, first, rows):
    n = mat.shape[1]
    return jnp.concatenate(
        [jnp.broadcast_to(mat[first + e:first + e + 1, :], (rows, n)) for e in range(HEADS_PER_GROUP)], axis=0)


def _ssd_bwd_state_kernel(xs_ref, b_ref, dt_ref, bias_ref, alog_ref, tl_ref, hb_ref, st_ref):
    c = pl.program_id(1)

    @pl.when(c == 0)
    def _():
        st_ref[...] = jnp.zeros_like(st_ref)

    hb_ref[0] = st_ref[...].astype(BF16)
    dt, rate = _dt_and_rate(dt_ref, bias_ref, alog_ref)
    pre = _tri_matmul(tl_ref[...], rate)
    pre_t = pre.T
    excl_t = (pre - rate).T
    total = jnp.broadcast_to(pre_t[:, CHUNK - 1:CHUNK], (LANES, CHUNK))
    w_t = dt.T * jnp.exp2(excl_t)
    dec = jnp.exp2(total)
    off = N_SSM_HEADS
    for g in range(N_SSM_GROUPS):
        xs_t = xs_ref[:, g * 512:(g + 1) * 512].astype(F32).T
        xd = (xs_t * _head_rows(w_t, off + g * HEADS_PER_GROUP, SSM_HEAD_DIM)).astype(BF16)
        upd = _dot(xd, b_ref[:, g * D_STATE:(g + 1) * D_STATE])
        st_ref[g] = _head_rows(dec, off + g * HEADS_PER_GROUP, SSM_HEAD_DIM) * st_ref[g] + upd


def _ssd_bwd_states(xc, dt, bias128, alog128, tri_l, batch, seq):
    nc = seq // CHUNK

    def rev(b, c):
        return (b * nc + nc - 1 - c, 0)

    return pl.pallas_call(
        _ssd_bwd_state_kernel,
        grid=(batch, nc),
        in_specs=[
            pl.BlockSpec((CHUNK, D_INNER), rev),
            pl.BlockSpec((CHUNK, BC_WIDTH), lambda b, c: (b * nc + nc - 1 - c, D_INNER // BC_WIDTH)),
            pl.BlockSpec((CHUNK, LANES), rev),
            pl.BlockSpec((1, LANES), lambda b, c: (0, 0)),
            pl.BlockSpec((1, LANES), lambda b, c: (0, 0)),
            pl.BlockSpec((CHUNK, CHUNK), lambda b, c: (0, 0)),
        ],
        out_specs=pl.BlockSpec((1, N_SSM_GROUPS, 512, D_STATE), lambda b, c: (b * nc + nc - 1 - c, 0, 0, 0)),
        out_shape=jax.ShapeDtypeStruct((batch * nc, N_SSM_GROUPS, 512, D_STATE), BF16),
        scratch_shapes=[pltpu.VMEM((N_SSM_GROUPS, 512, D_STATE), F32)],
        compiler_params=_params(("arbitrary", "arbitrary")),
        name="ssd_bwd_states",
    )(xc, xc, dt, bias128, alog128, tri_l)


def _ssd_main_kernel(xc_ref, z_ref, dt_ref, hb_ref, bias_ref, alog_ref, tl_ref, tu_ref, dskip_ref, gain_ref,
                     o_ref, hf_ref, y_ref):
    c = pl.program_id(1)

    @pl.when(c == 0)
    def _():
        hf_ref[...] = jnp.zeros_like(hf_ref)

    dt, rate = _dt_and_rate(dt_ref, bias_ref, alog_ref)
    lane = lax.broadcasted_iota(jnp.int32, (CHUNK, LANES), 1)
    cum = jnp.where(lane < N_SSM_HEADS, _tri_matmul(tl_ref[...], rate), _tri_matmul(tu_ref[...], rate))
    cum_t = cum.T
    dt_t = dt.T
    src_t = cum_t - jnp.log2(dt_t)
    row = lax.broadcasted_iota(jnp.int32, (CHUNK, CHUNK), 0)
    col = lax.broadcasted_iota(jnp.int32, (CHUNK, CHUNK), 1)
    lower = row >= col
    diag = row == col
    low = lane < SSM_HEAD_DIM
    zero_x = jnp.zeros((CHUNK, LANES), BF16)
    nb = N_SSM_HEADS

    def lane_bcast(mat, idx):
        return jnp.broadcast_to(mat[:, idx:idx + 1], (CHUNK, CHUNK))

    def sub_bcast(mat, idx):
        return jnp.broadcast_to(mat[idx:idx + 1, :], (CHUNK, CHUNK))

    def head_matrix(e, cb):
        col_f = lane_bcast(cum, e)
        col_b = lane_bcast(cum, nb + e)
        decay = jnp.exp2(jnp.where(lower, col_f - sub_bcast(src_t, e), col_b - sub_bcast(src_t, nb + e)))
        decay = decay + jnp.where(diag, sub_bcast(dt_t, nb + e), 0.0)
        return (decay * cb).astype(BF16), col_f, col_b

    for g in range(N_SSM_GROUPS):
        bg = xc_ref[:, D_INNER + g * D_STATE:D_INNER + (g + 1) * D_STATE]
        cg = xc_ref[:, D_INNER + BC_WIDTH + g * D_STATE:D_INNER + BC_WIDTH + (g + 1) * D_STATE]
        cb = _dot_nt(cg, bg)
        y_in_f = _dot_nt(cg, hf_ref[g].astype(BF16))
        y_in_b = _dot_nt(cg, hb_ref[0, g])
        for jp in range(HEADS_PER_GROUP // 2):
            e0 = g * HEADS_PER_GROUP + 2 * jp
            cols = slice(e0 * SSM_HEAD_DIM, e0 * SSM_HEAD_DIM + LANES)
            loc = slice(jp * LANES, (jp + 1) * LANES)
            xs_pair = xc_ref[:, cols]
            m0, cf0, cb0 = head_matrix(e0, cb)
            m1, cf1, cb1 = head_matrix(e0 + 1, cb)
            y = _dot(m0, jnp.where(low, xs_pair, zero_x)) + _dot(m1, jnp.where(low, zero_x, xs_pair))
            y = y + y_in_f[:, loc] * jnp.exp2(jnp.where(low, cf0, cf1))
            y = y + y_in_b[:, loc] * jnp.exp2(jnp.where(low, cb0, cb1))
            y_ref[:, cols] = y + dskip_ref[:, cols] * xs_pair.astype(F32)

    z = z_ref[...].astype(F32)
    y = y_ref[...] * (z * _sigmoid(z))
    ms = jnp.mean(y * y, axis=-1, keepdims=True)
    o_ref[...] = (y * lax.rsqrt(ms + EPS) * gain_ref[...]).astype(BF16)

    last = jnp.broadcast_to(cum_t[:, CHUNK - 1:CHUNK], (LANES, CHUNK))
    w_t = jnp.exp2(last - src_t)
    dec = jnp.exp2(last)
    for g in range(N_SSM_GROUPS):
        xs_t = xc_ref[:, g * 512:(g + 1) * 512].astype(F32).T
        xd = (xs_t * _head_rows(w_t, g * HEADS_PER_GROUP, SSM_HEAD_DIM)).astype(BF16)
        upd = _dot(xd, xc_ref[:, D_INNER + g * D_STATE:D_INNER + (g + 1) * D_STATE])
        hf_ref[g] = _head_rows(dec, g * HEADS_PER_GROUP, SSM_HEAD_DIM) * hf_ref[g] + upd


def _ssd_main(xc, proj, dt, hb, bias128, alog128, tri_l, tri_u, dskip, gain, batch, seq):
    t = xc.shape[0]
    nc = seq // CHUNK

    def tok(b, c):
        return (b * nc + c, 0)

    def const(b, c):
        return (0, 0)

    return pl.pallas_call(
        _ssd_main_kernel,
        grid=(batch, nc),
        in_specs=[
            pl.BlockSpec((CHUNK, CONV_DIM), tok),
            pl.BlockSpec((CHUNK, D_INNER), tok),
            pl.BlockSpec((CHUNK, LANES), tok),
            pl.BlockSpec((1, N_SSM_GROUPS, 512, D_STATE), lambda b, c: (b * nc + c, 0, 0, 0)),
            pl.BlockSpec((1, LANES), const),
            pl.BlockSpec((1, LANES), const),
            pl.BlockSpec((CHUNK, CHUNK), const),
            pl.BlockSpec((CHUNK, CHUNK), const),
            pl.BlockSpec((1, D_INNER), const),
            pl.BlockSpec((1, D_INNER), const),
        ],
        out_specs=pl.BlockSpec((CHUNK, D_INNER), tok),
        out_shape=jax.ShapeDtypeStruct((t, D_INNER), BF16),
        scratch_shapes=[pltpu.VMEM((N_SSM_GROUPS, 512, D_STATE), F32), pltpu.VMEM((CHUNK, D_INNER), F32)],
        compiler_params=_params(("arbitrary", "arbitrary")),
        name="ssd_main",
    )(xc, proj, dt, hb, bias128, alog128, tri_l, tri_u, dskip, gain)


def _outproj_kernel(attn_a, ssm_a, gate_a, x_a, attn_b, ssm_b, gate_b, x_b, wa_ref, ws_ref, wo_ref, gn_ref,
                    wr1_ref, wr2_ref, br_ref, o_ref, r_ref, *, n_a):
    i = pl.program_id(0)

    @pl.when(i < n_a)
    def _():
        _outproj_tile(attn_a, ssm_a, gate_a, x_a, wa_ref, ws_ref, wo_ref, gn_ref, wr1_ref, wr2_ref, br_ref,
                      o_ref, r_ref)

    @pl.when(i >= n_a)
    def _():
        _outproj_tile(attn_b, ssm_b, gate_b, x_b, wa_ref, ws_ref, wo_ref, gn_ref, wr1_ref, wr2_ref, br_ref,
                      o_ref, r_ref)


def _outproj_tile(attn_ref, ssm_ref, gate_ref, x_ref, wa_ref, ws_ref, wo_ref, gn_ref, wr1_ref, wr2_ref, br_ref,
                  o_ref, r_ref):
    a_out = _dot(attn_ref[...], wa_ref[...])
    s_out = _dot(ssm_ref[...], ws_ref[...])
    ga = gate_ref[:, :D_MODEL].astype(F32)
    gs = gate_ref[:, D_MODEL:].astype(F32)
    merged = _sigmoid(ga) * a_out + _sigmoid(gs) * s_out
    x2 = x_ref[...] + _dot(merged.astype(BF16), wo_ref[...])
    for j in range(TOKEN_TILE):
        o_ref[pl.ds(j, TM_OUT, stride=TOKEN_TILE), :] = x2[:, j * LANES:(j + 1) * LANES]

    ms = jnp.mean(x2 * x2, axis=-1, keepdims=True)
    hn = x2 * lax.rsqrt(ms + EPS) * gn_ref[...]
    h1 = hn.astype(BF16)
    h2 = (hn - h1.astype(F32)).astype(BF16)
    lg = _dot(h1, wr1_ref[...]) + _dot(h2, wr1_ref[...]) + _dot(h1, wr2_ref[...]) + br_ref[...]

    lane = lax.broadcasted_iota(jnp.int32, (TM_OUT, LANES), 1).astype(F32)
    big = float(LANES)

    def rmax(v):
        return jnp.max(v, axis=-1, keepdims=True)

    def first_lane(mask):
        return jnp.min(jnp.where(mask, lane, big), axis=-1, keepdims=True)

    gl = jnp.where(lane < N_EXPERT_GROUPS, lg, NEG_INF)
    gmax = rmax(gl)
    g_w = 1.0 / jnp.sum(jnp.exp(gl - gmax), axis=-1, keepdims=True)
    gidx = first_lane(gl == gmax)
    base = N_EXPERT_GROUPS + EXPERTS_PER_GROUP * gidx
    el = jnp.where(lane >= base, jnp.where(lane < base + EXPERTS_PER_GROUP, lg, NEG_INF), NEG_INF)
    m1 = rmax(el)
    i1 = first_lane(el == m1)
    el2 = jnp.where(lane == i1, NEG_INF, el)
    m2 = rmax(el2)
    i2 = first_lane(el2 == m2)
    r = jnp.exp(m2 - m1)
    w1 = g_w / (1.0 + r)
    w2 = w1 * r
    j1 = i1 - base
    j2 = i2 - base
    swap = j1 > j2
    e_lo = jnp.where(swap, j2, j1)
    e_hi = jnp.where(swap, j1, j2)
    w_lo = jnp.where(swap, w2, w1)
    w_hi = jnp.where(swap, w1, w2)
    pair = e_lo * (EXPERTS_PER_GROUP - 1) - e_lo * (e_lo - 1.0) * 0.5 + (e_hi - e_lo - 1.0)
    cls = gidx * N_PAIRS + pair
    r_ref[...] = jnp.where(lane == 0.0, cls, jnp.where(lane == 1.0, w_lo, jnp.where(lane == 2.0, w_hi, 0.0)))


def _outproj(group_a, group_b, wa, ws, wo, gn, wr1, wr2, br):
    n_a = group_a[3].shape[0] // TM_OUT
    n_b = group_b[3].shape[0] // TM_OUT
    t = (n_a + n_b) * TM_OUT

    def first(i):
        return (jnp.minimum(i, n_a - 1), 0)

    def second(i):
        return (jnp.maximum(i - n_a, 0), 0)

    def const(i):
        return (0, 0)

    def group_specs(tok):
        return [
            pl.BlockSpec((TM_OUT, ATTN_WIDTH), tok),
            pl.BlockSpec((TM_OUT, D_INNER), tok),
            pl.BlockSpec((TM_OUT, 2 * D_MODEL), lambda i: (tok(i)[0], COL_GATE // (2 * D_MODEL))),
            pl.BlockSpec((TM_OUT, D_MODEL), tok),
        ]

    resident = dict(pipeline_mode=pl.Buffered(1))
    return pl.pallas_call(
        functools.partial(_outproj_kernel, n_a=n_a),
        grid=(n_a + n_b,),
        in_specs=group_specs(first) + group_specs(second) + [
            pl.BlockSpec((ATTN_WIDTH, D_MODEL), const, **resident),
            pl.BlockSpec((D_INNER, D_MODEL), const, **resident),
            pl.BlockSpec((D_MODEL, D_MODEL), const, **resident),
            pl.BlockSpec((1, D_MODEL), const),
            pl.BlockSpec((D_MODEL, LANES), const),
            pl.BlockSpec((D_MODEL, LANES), const),
            pl.BlockSpec((1, LANES), const),
        ],
        out_specs=[
            pl.BlockSpec((TM_OUT * TOKEN_TILE, LANES), lambda i: (i, 0)),
            pl.BlockSpec((TM_OUT, LANES), lambda i: (i, 0)),
        ],
        out_shape=[
            jax.ShapeDtypeStruct((t * TOKEN_TILE, LANES), F32),
            jax.ShapeDtypeStruct((t, LANES), F32),
        ],
        compiler_params=_params(("arbitrary",)),
        name="outproj_router",
    )(*group_a, *group_b, wa, ws, wo, gn, wr1, wr2, br)


def _moe_kernel(ea_ref, eb_ref, nv_ref, tokc_ref, tokn_ref, roww_ref, x_ref, gn_ref,
                wga_ref, wua_ref, wda_ref, wgb_ref, wub_ref, wdb_ref, o_ref,
                xg_ref, st_ref, gsem, ssem, *, n_blocks):
    i = pl.program_id(0)
    slot = i % 2
    other = 1 - slot

    def tile(idx):
        return pl.ds(pl.multiple_of(idx * TOKEN_TILE, TOKEN_TILE), TOKEN_TILE)

    def gather_copy(tok, r, s):
        return pltpu.make_async_copy(x_ref.at[tile(tok), :], xg_ref.at[s, tile(r), :], gsem.at[s])

    def scatter_copy(tok, r, s):
        return pltpu.make_async_copy(st_ref.at[s, tile(r), :], o_ref.at[tile(tok), :], ssem.at[s])

    def for_rows(n, fn):
        n8 = lax.shift_right_logical(n, 3)

        def body8(g, _):
            for u in range(8):
                fn(g * 8 + u)
            return 0

        def body1(r, _):
            fn(r)
            return 0

        lax.fori_loop(0, n8, body8, 0)
        lax.fori_loop(n8 * 8, n, body1, 0)

    def start_gathers(tok_ref, n, s):
        for_rows(n, lambda r: gather_copy(tok_ref[0, 0, r], r, s).start())

    def wait_gathers(n, s):
        for_rows(n, lambda r: gather_copy(0, 0, s).wait())

    def start_scatters(tok_ref, n, s):
        for_rows(n, lambda r: scatter_copy(tok_ref[0, 0, r], r, s).start())

    def wait_scatters(n, s):
        for_rows(n, lambda r: scatter_copy(0, 0, s).wait())

    @pl.when(i == 0)
    def _():
        xg_ref[...] = jnp.zeros_like(xg_ref)
        start_gathers(tokc_ref, nv_ref[0], 0)

    @pl.when(i + 1 < n_blocks)
    def _():
        start_gathers(tokn_ref, nv_ref[jnp.minimum(i + 1, n_blocks - 1)], other)

    wait_gathers(nv_ref[i], slot)

    @pl.when(i >= 2)
    def _():
        wait_scatters(nv_ref[jnp.maximum(i - 2, 0)], slot)

    @pl.when(nv_ref[i] > 0)
    def _():
        x = jnp.concatenate(
            [xg_ref[slot, pl.ds(j, ROW_BLOCK, stride=TOKEN_TILE), :] for j in range(TOKEN_TILE)], axis=1)
        w_cols = jnp.concatenate([roww_ref[0], jnp.zeros((LANES - TOKEN_TILE, ROW_BLOCK), F32)], axis=0).T
        w_lo = w_cols[:, 0:1]
        w_hi = w_cols[:, 1:2]
        ms = jnp.mean(x * x, axis=-1, keepdims=True)
        hn = (x * lax.rsqrt(ms + EPS) * gn_ref[...]).astype(BF16)

        def expert(wg_ref, wu_ref, wd_ref):
            gte = _dot(hn, wg_ref[0])
            up = _dot(hn, wu_ref[0])
            return _dot((gte * _sigmoid(gte) * up).astype(BF16), wd_ref[0])

        ya = expert(wga_ref, wua_ref, wda_ref)
        yb = expert(wgb_ref, wub_ref, wdb_ref)
        out = x + w_lo * ya + w_hi * yb
        for j in range(TOKEN_TILE):
            st_ref[slot, pl.ds(j, ROW_BLOCK, stride=TOKEN_TILE), :] = out[:, j * LANES:(j + 1) * LANES]

    start_scatters(tokc_ref, nv_ref[i], slot)

    @pl.when(i == n_blocks - 1)
    def _():
        wait_scatters(nv_ref[jnp.maximum(i - 1, 0)], other)
        wait_scatters(nv_ref[i], slot)


def _moe(ea, eb, nvalid, row_tok, row_w, x2t, gn, wg, wu, wd):
    n_blocks = row_tok.shape[0]

    def wa(i, ea, eb, nv):
        return (ea[i], 0, 0)

    def wb(i, ea, eb, nv):
        return (eb[i], 0, 0)

    any_spec = pl.BlockSpec(memory_space=pl.ANY)
    grid_spec = pltpu.PrefetchScalarGridSpec(
        num_scalar_prefetch=3,
        grid=(n_blocks,),
        in_specs=[
            pl.BlockSpec((1, 1, ROW_BLOCK), lambda i, ea, eb, nv: (i, 0, 0), memory_space=pltpu.SMEM),
            pl.BlockSpec((1, 1, ROW_BLOCK), lambda i, ea, eb, nv: (jnp.minimum(i + 1, n_blocks - 1), 0, 0),
                         memory_space=pltpu.SMEM),
            pl.BlockSpec((1, TOKEN_TILE, ROW_BLOCK), lambda i, ea, eb, nv: (i, 0, 0)),
            any_spec,
            pl.BlockSpec((1, D_MODEL), lambda i, ea, eb, nv: (0, 0)),
            pl.BlockSpec((1, D_MODEL, D_EXPERT), wa), pl.BlockSpec((1, D_MODEL, D_EXPERT), wa),
            pl.BlockSpec((1, D_EXPERT, D_MODEL), wa),
            pl.BlockSpec((1, D_MODEL, D_EXPERT), wb), pl.BlockSpec((1, D_MODEL, D_EXPERT), wb),
            pl.BlockSpec((1, D_EXPERT, D_MODEL), wb),
        ],
        out_specs=any_spec,
        scratch_shapes=[
            pltpu.VMEM((2, ROW_BLOCK * TOKEN_TILE, LANES), F32),
            pltpu.VMEM((2, ROW_BLOCK * TOKEN_TILE, LANES), F32),
            pltpu.SemaphoreType.DMA((2,)),
            pltpu.SemaphoreType.DMA((2,)),
        ],
    )
    return pl.pallas_call(
        functools.partial(_moe_kernel, n_blocks=n_blocks),
        grid_spec=grid_spec,
        out_shape=jax.ShapeDtypeStruct(x2t.shape, F32),
        compiler_params=_params(("arbitrary",)),
        name="moe",
    )(ea, eb, nvalid, row_tok, row_tok, row_w, x2t, gn, wg, wu, wd, wg, wu, wd)


def _untile_kernel(x_ref, o_ref):
    for j in range(TOKEN_TILE):
        o_ref[:, j * LANES:(j + 1) * LANES] = x_ref[pl.ds(j, TM_OUT, stride=TOKEN_TILE), :]


def _untile(y_tiles, first_token, n_tokens):
    first_block = first_token // TM_OUT
    return pl.pallas_call(
        _untile_kernel,
        grid=(n_tokens // TM_OUT,),
        in_specs=[pl.BlockSpec((TM_OUT * TOKEN_TILE, LANES), lambda i: (first_block + i, 0))],
        out_specs=pl.BlockSpec((TM_OUT, D_MODEL), lambda i: (i, 0)),
        out_shape=jax.ShapeDtypeStruct((n_tokens, D_MODEL), F32),
        compiler_params=_params(("arbitrary",)),
        name="untile",
    )(y_tiles)


def _pair_tables():
    lo, hi = [], []
    for a in range(EXPERTS_PER_GROUP):
        for b in range(a + 1, EXPERTS_PER_GROUP):
            lo.append(a)
            hi.append(b)
    return np.asarray(lo, np.int32), np.asarray(hi, np.int32)


def _block_tables(rinfo):
    cls = rinfo[:, 0].astype(jnp.int32)
    t = cls.shape[0]
    n_blocks = t // ROW_BLOCK + N_CLASSES
    sorted_cls, order = lax.sort((cls, jnp.arange(t, dtype=jnp.int32)), num_keys=1)
    class_ids = jnp.arange(N_CLASSES + 1, dtype=jnp.int32)
    starts = jnp.sum((sorted_cls[:, None] < class_ids[None, :]).astype(jnp.int32), axis=0)
    counts = starts[1:] - starts[:-1]
    nblk = (counts + ROW_BLOCK - 1) // ROW_BLOCK
    blk_end = jnp.cumsum(nblk)
    blk_start = blk_end - nblk
    used = blk_end[-1]
    b = jnp.arange(n_blocks, dtype=jnp.int32)
    b_eff = jnp.minimum(b, used - 1)
    c = jnp.sum((blk_end[None, :] <= b_eff[:, None]).astype(jnp.int32), axis=1)
    c = jnp.minimum(c, N_CLASSES - 1)
    off = b_eff - blk_start[c]
    src = starts[c] + off * ROW_BLOCK
    nvalid = jnp.where(b < used, jnp.clip(counts[c] - off * ROW_BLOCK, 0, ROW_BLOCK), 0).astype(jnp.int32)
    pair_lo, pair_hi = _pair_tables()
    grp = c // N_PAIRS
    ea = (grp * EXPERTS_PER_GROUP + jnp.asarray(pair_lo)[c % N_PAIRS]).astype(jnp.int32)
    eb = (grp * EXPERTS_PER_GROUP + jnp.asarray(pair_hi)[c % N_PAIRS]).astype(jnp.int32)
    rows = jnp.clip(src[:, None] + jnp.arange(ROW_BLOCK, dtype=jnp.int32)[None, :], 0, t - 1)
    row_tok = order[rows]
    row_w = jnp.zeros((n_blocks, TOKEN_TILE, ROW_BLOCK), F32)
    row_w = row_w.at[:, 0, :].set(rinfo[:, 1][row_tok]).at[:, 1, :].set(rinfo[:, 2][row_tok])
    return ea, eb, nvalid, row_tok.reshape(n_blocks, 1, ROW_BLOCK), row_w


def _rope_tables(seq):
    inv = 1.0 / (ROPE_THETA ** (jnp.arange(0, HEAD_DIM, 2, dtype=F32) / HEAD_DIM))
    ang = jnp.arange(seq, dtype=F32)[:, None] * inv[None, :]
    cos, sin = jnp.cos(ang), jnp.sin(ang)
    cos128 = jnp.concatenate([cos, cos, cos, cos], axis=-1)
    sin128 = jnp.concatenate([-sin, sin, -sin, sin], axis=-1)
    return cos128, sin128


def _prepare_weights(norm_mix, w_in, q_norm, k_norm, attn_sink, conv_w, conv_b, a_log_fwd, a_log_bwd,
                     dt_bias_fwd, dt_bias_bwd, d_skip, ssm_norm, w_out_attn, w_out_ssm, w_o, norm_ffn,
                     w_router_group, b_router_group, w_router_expert, b_router_expert, w_gate, w_up, w_down):
    o_q = 0
    o_k = o_q + ATTN_WIDTH
    o_v = o_k + KV_WIDTH
    o_z = o_v + KV_WIDTH
    o_xbc = o_z + D_INNER
    o_dtf = o_xbc + CONV_DIM
    o_dtb = o_dtf + N_SSM_HEADS
    o_ga = o_dtb + N_SSM_HEADS
    o_gs = o_ga + D_MODEL
    w = w_in.astype(BF16)
    w_r = jnp.concatenate([
        w[:, o_z:o_z + D_INNER], w[:, o_ga:o_gs + D_MODEL], w[:, o_xbc:o_xbc + CONV_DIM],
        w[:, o_q:o_q + ATTN_WIDTH], w[:, o_k:o_k + KV_WIDTH], w[:, o_v:o_v + KV_WIDTH],
        w[:, o_dtf:o_dtb + N_SSM_HEADS], jnp.zeros((D_MODEL, LANES - 2 * N_SSM_HEADS), w.dtype)], axis=1)
    pad64 = jnp.zeros((LANES - 2 * N_SSM_HEADS,), F32)
    eye = np.kron(np.eye(2, dtype=np.float32), np.ones((HEAD_DIM, HEAD_DIM), np.float32))
    idx = np.arange(CHUNK)
    w_router = jnp.concatenate([w_router_group, w_router_expert,
                                jnp.zeros((D_MODEL, LANES - N_EXPERT_GROUPS - N_EXPERTS), F32)], axis=1)
    wr1 = w_router.astype(BF16)
    return dict(
        norm_mix=norm_mix.reshape(1, D_MODEL),
        w_in=w_r,
        qg128=jnp.tile(q_norm, 2).reshape(1, LANES),
        kg128=jnp.tile(k_norm, 2).reshape(1, LANES),
        seg=jnp.asarray(eye, BF16),
        sink=attn_sink.astype(F32),
        conv_w=conv_w,
        conv_b=conv_b.reshape(1, CONV_DIM),
        alog128=jnp.concatenate([a_log_fwd, a_log_bwd, pad64]).reshape(1, LANES),
        bias128=jnp.concatenate([dt_bias_fwd, dt_bias_bwd, pad64]).reshape(1, LANES),
        tri_l=jnp.asarray(idx[:, None] >= idx[None, :], BF16),
        tri_u=jnp.asarray(idx[:, None] <= idx[None, :], BF16),
        dskip=jnp.repeat(d_skip, SSM_HEAD_DIM).reshape(1, D_INNER),
        ssm_norm=ssm_norm.reshape(1, D_INNER),
        wa=w_out_attn.astype(BF16), ws=w_out_ssm.astype(BF16), wo=w_o.astype(BF16),
        norm_ffn=norm_ffn.reshape(1, D_MODEL),
        wr1=wr1, wr2=(w_router - wr1.astype(F32)).astype(BF16),
        br=jnp.concatenate([b_router_group, b_router_expert,
                            jnp.zeros((LANES - N_EXPERT_GROUPS - N_EXPERTS,), F32)]).reshape(1, LANES),
        wg=w_gate.astype(BF16), wu=w_up.astype(BF16), wd=w_down.astype(BF16),
    )


def _mixer(x, p):
    batch, seq, _ = x.shape
    x2d = x.reshape(batch * seq, D_MODEL)
    proj, dt = _inproj(x2d, p['norm_mix'], p['w_in'])
    cos128, sin128 = _rope_tables(seq)
    qr, kdup, vdup = _qkprep(proj, cos128, sin128, p['qg128'], p['kg128'], p['seg'], seq)
    attn = _attention(qr, kdup, vdup, p['sink'], batch, seq)
    xc = _conv(proj, p['conv_w'], p['conv_b'], batch, seq)
    hb = _ssd_bwd_states(xc, dt, p['bias128'], p['alog128'], p['tri_l'], batch, seq)
    ssm = _ssd_main(xc, proj, dt, hb, p['bias128'], p['alog128'], p['tri_l'], p['tri_u'], p['dskip'],
                    p['ssm_norm'], batch, seq)
    return attn, ssm, proj, x2d


def kernel(x_prompt, x_sample, norm_mix, w_in, q_norm, k_norm, attn_sink, conv_w, conv_b, a_log_fwd, a_log_bwd,
           dt_bias_fwd, dt_bias_bwd, d_skip, ssm_norm, w_out_attn, w_out_ssm, w_o, norm_ffn, w_router_group,
           b_router_group, w_router_expert, b_router_expert, w_gate, w_up, w_down):
    assert norm_mix.shape[0] == 1, "single-layer encoder"
    p = _prepare_weights(norm_mix[0], w_in[0], q_norm[0], k_norm[0], attn_sink[0], conv_w[0], conv_b[0],
                         a_log_fwd[0], a_log_bwd[0], dt_bias_fwd[0], dt_bias_bwd[0], d_skip[0], ssm_norm[0],
                         w_out_attn[0], w_out_ssm[0], w_o[0], norm_ffn[0], w_router_group[0], b_router_group[0],
                         w_router_expert[0], b_router_expert[0], w_gate[0], w_up[0], w_down[0])
    x2t, rinfo = _outproj(_mixer(x_prompt, p), _mixer(x_sample, p), p['wa'], p['ws'], p['wo'], p['norm_ffn'],
                          p['wr1'], p['wr2'], p['br'])
    ea, eb, nvalid, row_tok, row_w = _block_tables(rinfo)
    y = _moe(ea, eb, nvalid, row_tok, row_w, x2t, p['norm_ffn'], p['wg'], p['wu'], p['wd'])
    t_a = x_prompt.shape[0] * x_prompt.shape[1]
    t_b = x_sample.shape[0] * x_sample.shape[1]
    return _untile(y, 0, t_a).reshape(x_prompt.shape), _untile(y, t_a, t_b).reshape(x_sample.shape)
```

```python
import functools

import numpy as np
import jax
import jax.numpy as jnp
from jax import lax
from jax.experimental import pallas as pl
from jax.experimental.pallas import tpu as pltpu

F32 = jnp.float32
BF16 = jnp.bfloat16

D_MODEL = 1024
EPS = 1e-6
NEG_INF = -1e30
LOG2_E = 1.4426950408889634
N_Q_HEADS = 16
N_KV_HEADS = 4
HEAD_DIM = 64
ATTN_WIDTH = N_Q_HEADS * HEAD_DIM
KV_WIDTH = N_KV_HEADS * HEAD_DIM
ATTN_BLOCK = 128
ATTN_QB = 2
ROPE_THETA = 10000.0
D_INNER = 2 * D_MODEL
SSM_HEAD_DIM = 64
N_SSM_HEADS = D_INNER // SSM_HEAD_DIM
N_SSM_GROUPS = 4
HEADS_PER_GROUP = N_SSM_HEADS // N_SSM_GROUPS
D_STATE = 128
BC_WIDTH = N_SSM_GROUPS * D_STATE
CONV_DIM = D_INNER + 2 * BC_WIDTH
CONV_W = 7
CHUNK = 128
SSD_BWD_CHUNKS = 4
SSD_MAIN_CHUNKS = 2
N_EXPERT_GROUPS = 4
EXPERTS_PER_GROUP = 8
N_EXPERTS = N_EXPERT_GROUPS * EXPERTS_PER_GROUP
D_EXPERT = 512
N_PAIRS = EXPERTS_PER_GROUP * (EXPERTS_PER_GROUP - 1) // 2
N_CLASSES = N_EXPERT_GROUPS * N_PAIRS

LANES = 128
V7X_VMEM_LIMIT_BYTES = 56 * 1024 * 1024

COL_Z = 0
COL_GATE = COL_Z + D_INNER
COL_XS = COL_GATE + 2 * D_MODEL
COL_B = COL_XS + D_INNER
COL_C = COL_B + BC_WIDTH
COL_Q = COL_C + BC_WIDTH
COL_K = COL_Q + ATTN_WIDTH
COL_V = COL_K + KV_WIDTH
COL_DT = COL_V + KV_WIDTH
N_PROJ = COL_DT + LANES

TM_IN = 1024
NJ_IN = 3
TN_IN = N_PROJ // NJ_IN
CH_IN = 512
TM_QK = 1024
TM_OUT = 512
TM_UNTILE = 1024
CONV_CT = 512
CONV_ROWS = 256
CONV_PITCH = 2
ROW_BLOCK = 128
TOKEN_TILE = D_MODEL // LANES


def _params(sem, flags=None):
    return pltpu.CompilerParams(dimension_semantics=sem, vmem_limit_bytes=V7X_VMEM_LIMIT_BYTES, flags=flags)


def _dot(a, b):
    return jnp.dot(a, b, preferred_element_type=F32)


def _dot_nt(a, b):
    return lax.dot_general(a, b, (((1,), (1,)), ((), ())), preferred_element_type=F32)


def _sigmoid(x):
    return 1.0 / (1.0 + jnp.exp(-x))


def _inproj_kernel(x_ref, g_ref, w_ref, o_ref, dt_ref, h_ref):
    j = pl.program_id(1)

    @pl.when(j == 0)
    def _():
        x = x_ref[...]
        ms = jnp.mean(x * x, axis=-1, keepdims=True)
        h_ref[...] = (x * lax.rsqrt(ms + EPS) * g_ref[...]).astype(BF16)

    for c0 in range(0, TN_IN, CH_IN):
        c1 = min(c0 + CH_IN, TN_IN)
        acc = _dot(h_ref[...], w_ref[:, c0:c1])
        o_ref[:, c0:c1] = acc.astype(BF16)
        if c1 == TN_IN:
            @pl.when(j == NJ_IN - 1)
            def _():
                dt_ref[...] = acc[:, c1 - c0 - LANES:]


def _inproj(x2d, gain, w_bf16):
    t = x2d.shape[0]
    return pl.pallas_call(
        _inproj_kernel,
        grid=(t // TM_IN, NJ_IN),
        in_specs=[
            pl.BlockSpec((TM_IN, D_MODEL), lambda i, j: (i, 0)),
            pl.BlockSpec((1, D_MODEL), lambda i, j: (0, 0)),
            pl.BlockSpec((D_MODEL, TN_IN), lambda i, j: (0, j)),
        ],
        out_specs=[
            pl.BlockSpec((TM_IN, TN_IN), lambda i, j: (i, j)),
            pl.BlockSpec((TM_IN, LANES), lambda i, j: (i, 0)),
        ],
        out_shape=[
            jax.ShapeDtypeStruct((t, N_PROJ), BF16),
            jax.ShapeDtypeStruct((t, LANES), F32),
        ],
        scratch_shapes=[pltpu.VMEM((TM_IN, D_MODEL), BF16)],
        compiler_params=_params(("arbitrary", "arbitrary")),
        name="inproj",
    )(x2d, gain, w_bf16)


def _qkprep_kernel(q_ref, k_ref, v_ref, cos_ref, sin_ref, qg_ref, kg_ref, seg_ref, qo_ref, ko_ref, vo_ref):
    cos = cos_ref[...]
    sin = sin_ref[...]
    seg = seg_ref[...]
    lane = lax.broadcasted_iota(jnp.int32, (TM_QK, LANES), 1)
    first_half = (lane % HEAD_DIM) < (HEAD_DIM // 2)
    low = lane < HEAD_DIM

    def norm_rope(x, gain):
        ss = _dot((x * x).astype(BF16), seg)
        xn = x * lax.rsqrt(ss * (1.0 / HEAD_DIM) + EPS) * gain
        rot = jnp.where(first_half, pltpu.roll(xn, 96, 1), pltpu.roll(xn, 32, 1))
        return xn * cos + rot * sin

    for s in range(ATTN_WIDTH // LANES):
        sl = slice(s * LANES, (s + 1) * LANES)
        y = norm_rope(q_ref[:, sl].astype(F32), qg_ref[...]) * (HEAD_DIM ** -0.5 * LOG2_E)
        qo_ref[:, sl] = y.astype(BF16)
    for s in range(KV_WIDTH // LANES):
        sl = slice(s * LANES, (s + 1) * LANES)
        y = norm_rope(k_ref[:, sl].astype(F32), kg_ref[...])
        ysw = pltpu.roll(y, HEAD_DIM, 1)
        ko_ref[:, (2 * s) * LANES:(2 * s + 1) * LANES] = jnp.where(low, y, ysw).astype(BF16)
        ko_ref[:, (2 * s + 1) * LANES:(2 * s + 2) * LANES] = jnp.where(low, ysw, y).astype(BF16)
        v = v_ref[:, sl].astype(F32)
        vsw = pltpu.roll(v, HEAD_DIM, 1)
        vo_ref[:, (2 * s) * LANES:(2 * s + 1) * LANES] = jnp.where(low, v, vsw).astype(BF16)
        vo_ref[:, (2 * s + 1) * LANES:(2 * s + 2) * LANES] = jnp.where(low, vsw, v).astype(BF16)


def _qkprep(proj, cos128, sin128, qg128, kg128, seg, seq):
    t = proj.shape[0]
    nseq = seq // TM_QK
    return pl.pallas_call(
        _qkprep_kernel,
        grid=(t // TM_QK,),
        in_specs=[
            pl.BlockSpec((TM_QK, ATTN_WIDTH), lambda i: (i, COL_Q // ATTN_WIDTH)),
            pl.BlockSpec((TM_QK, KV_WIDTH), lambda i: (i, COL_K // KV_WIDTH)),
            pl.BlockSpec((TM_QK, KV_WIDTH), lambda i: (i, COL_V // KV_WIDTH)),
            pl.BlockSpec((TM_QK, LANES), lambda i: (i % nseq, 0)),
            pl.BlockSpec((TM_QK, LANES), lambda i: (i % nseq, 0)),
            pl.BlockSpec((1, LANES), lambda i: (0, 0)),
            pl.BlockSpec((1, LANES), lambda i: (0, 0)),
            pl.BlockSpec((LANES, LANES), lambda i: (0, 0)),
        ],
        out_specs=[
            pl.BlockSpec((TM_QK, ATTN_WIDTH), lambda i: (i, 0)),
            pl.BlockSpec((TM_QK, 2 * KV_WIDTH), lambda i: (i, 0)),
            pl.BlockSpec((TM_QK, 2 * KV_WIDTH), lambda i: (i, 0)),
        ],
        out_shape=[
            jax.ShapeDtypeStruct((t, ATTN_WIDTH), BF16),
            jax.ShapeDtypeStruct((t, 2 * KV_WIDTH), BF16),
            jax.ShapeDtypeStruct((t, 2 * KV_WIDTH), BF16),
        ],
        compiler_params=_params(("arbitrary",)),
        name="qkprep",
    )(proj, proj, proj, cos128, sin128, qg128, kg128, seg)


def _attn_kernel(sink_ref, q_ref, kp_ref, kc_ref, kn_ref, vp_ref, vc_ref, vn_ref, o_ref, *, n_steps):
    i = pl.program_id(1)
    nb = ATTN_BLOCK
    nk = 3 * nb
    qi = lax.broadcasted_iota(jnp.int32, (nb, nk), 0)
    si = lax.broadcasted_iota(jnp.int32, (nb, nk), 1)
    rel = qi - (si - nb)
    band = jnp.where(rel <= nb, jnp.where(rel >= -nb, 0.0, NEG_INF), NEG_INF)
    bias_first = jnp.where(si < nb, jnp.where(i > 0, band, NEG_INF), band)
    bias_last = jnp.where(si >= 2 * nb, jnp.where(i < n_steps - 1, band, NEG_INF), band)
    low_q = lax.broadcasted_iota(jnp.int32, (nb, LANES), 1) < HEAD_DIM
    low_k = lax.broadcasted_iota(jnp.int32, (nk, LANES), 1) < HEAD_DIM
    zero_q = jnp.zeros((nb, LANES), BF16)
    zero_k = jnp.zeros((nk, LANES), BF16)

    def window(p_ref, c_ref, n_ref, h, j):
        sl = slice(h * LANES, (h + 1) * LANES)
        rows = jnp.concatenate([p_ref[:, sl], c_ref[:, sl], n_ref[:, sl]], axis=0)
        return rows[j * nb:j * nb + nk]

    def scores(j, h):
        kd = window(kp_ref, kc_ref, kn_ref, h, j)
        slabs = [q_ref[j * nb:(j + 1) * nb, (2 * h + u) * LANES:(2 * h + u + 1) * LANES] for u in range(2)]
        q4 = jnp.concatenate([jnp.where(low_q, s_, zero_q) for s_ in slabs]
                             + [jnp.where(low_q, zero_q, s_) for s_ in slabs], axis=0)
        return _dot_nt(q4, kd)

    def finish(j, h, s4):
        vd = window(vp_ref, vc_ref, vn_ref, h, j)
        v_lo = jnp.where(low_k, vd, zero_k)
        v_hi = jnp.where(low_k, zero_k, vd)
        heads = (4 * h, 4 * h + 2, 4 * h + 1, 4 * h + 3)
        ps, invs = [], []
        for k, head in enumerate(heads):
            s = s4[k * nb:(k + 1) * nb]
            left = s[:, :nb] + (bias_first if j == 0 else band)[:, :nb]
            right = s[:, 2 * nb:] + (bias_last if j == ATTN_QB - 1 else band)[:, 2 * nb:]
            s = jnp.concatenate([left, s[:, nb:2 * nb], right], axis=1)
            snk = sink_ref[head] * LOG2_E
            m = jnp.maximum(jnp.max(s, axis=-1, keepdims=True), snk)
            p = jnp.exp2(s - m)
            den = jnp.sum(p, axis=-1, keepdims=True) + jnp.exp2(snk - m)
            ps.append(p.astype(BF16))
            invs.append(1.0 / den)
        o = _dot(jnp.concatenate(ps[:2], axis=0), v_lo) + _dot(jnp.concatenate(ps[2:], axis=0), v_hi)
        for u in range(2):
            ou = o[u * nb:(u + 1) * nb] * jnp.where(low_q, invs[u], invs[2 + u])
            o_ref[j * nb:(j + 1) * nb, (2 * h + u) * LANES:(2 * h + u + 1) * LANES] = ou.astype(BF16)

    tasks = [(j, h) for j in range(ATTN_QB) for h in range(N_KV_HEADS)]
    pending = scores(*tasks[0])
    for n, task in enumerate(tasks):
        following = scores(*tasks[n + 1]) if n + 1 < len(tasks) else None
        finish(*task, pending)
        pending = following


def _attention(qr, kdup, vdup, sink, batch, seq):
    t = qr.shape[0]
    nq = seq // ATTN_BLOCK
    n_steps = nq // ATTN_QB
    rows = ATTN_QB * ATTN_BLOCK

    def prev(b, i, s):
        return (b * nq + jnp.maximum(i * ATTN_QB - 1, 0), 0)

    def cur(b, i, s):
        return (b * n_steps + i, 0)

    def nxt(b, i, s):
        return (b * nq + jnp.minimum((i + 1) * ATTN_QB, nq - 1), 0)

    edge = (ATTN_BLOCK, 2 * KV_WIDTH)
    mid = (rows, 2 * KV_WIDTH)
    grid_spec = pltpu.PrefetchScalarGridSpec(
        num_scalar_prefetch=1,
        grid=(batch, n_steps),
        in_specs=[
            pl.BlockSpec((rows, ATTN_WIDTH), cur),
            pl.BlockSpec(edge, prev), pl.BlockSpec(mid, cur), pl.BlockSpec(edge, nxt),
            pl.BlockSpec(edge, prev), pl.BlockSpec(mid, cur), pl.BlockSpec(edge, nxt),
        ],
        out_specs=pl.BlockSpec((rows, ATTN_WIDTH), cur),
    )
    return pl.pallas_call(
        functools.partial(_attn_kernel, n_steps=n_steps),
        grid_spec=grid_spec,
        out_shape=jax.ShapeDtypeStruct((t, ATTN_WIDTH), BF16),
        compiler_params=_params(("arbitrary", "arbitrary")),
        name="attention",
    )(sink, qr, kdup, kdup, kdup, vdup, vdup, vdup)


def _conv_kernel(x_ref, w_ref, b_ref, o_ref, pad_ref, *, seq):
    halo = 8
    step = CONV_PITCH

    def rows(first, n):
        return pl.ds(step * (first + halo), n, stride=step)

    for h in range(CONV_CT // LANES):
        lanes = slice(h * LANES, (h + 1) * LANES)
        pad_ref[h, rows(-halo, halo), :] = jnp.zeros((halo, LANES), F32)
        pad_ref[h, rows(seq, halo), :] = jnp.zeros((halo, LANES), F32)
        for r in range(seq // CONV_ROWS):
            pad_ref[h, rows(r * CONV_ROWS, CONV_ROWS), :] = x_ref[r * CONV_ROWS:(r + 1) * CONV_ROWS, lanes].astype(F32)
    w = w_ref[...]
    bias = b_ref[...]
    for h in range(CONV_CT // LANES):
        lanes = slice(h * LANES, (h + 1) * LANES)
        for r in range(seq // CONV_ROWS):
            r0 = r * CONV_ROWS
            acc = jnp.broadcast_to(bias[:, lanes], (CONV_ROWS, LANES))
            for k in range(CONV_W):
                acc = acc + pad_ref[h, rows(r0 + k - CONV_W // 2, CONV_ROWS), :] * w[k:k + 1, lanes]
            o_ref[r0:r0 + CONV_ROWS, lanes] = (acc * _sigmoid(acc)).astype(BF16)


def _conv(proj, conv_w, conv_b, batch, seq):
    t = proj.shape[0]
    return pl.pallas_call(
        functools.partial(_conv_kernel, seq=seq),
        grid=(batch, CONV_DIM // CONV_CT),
        in_specs=[
            pl.BlockSpec((seq, CONV_CT), lambda b, c: (b, COL_XS // CONV_CT + c)),
            pl.BlockSpec((CONV_W, CONV_CT), lambda b, c: (0, c)),
            pl.BlockSpec((1, CONV_CT), lambda b, c: (0, c)),
        ],
        out_specs=pl.BlockSpec((seq, CONV_CT), lambda b, c: (b, c)),
        out_shape=jax.ShapeDtypeStruct((t, CONV_DIM), BF16),
        scratch_shapes=[pltpu.VMEM((CONV_CT // LANES, CONV_PITCH * (seq + 16), LANES), F32)],
        compiler_params=_params(("arbitrary", "arbitrary")),
        name="conv",
    )(proj, conv_w, conv_b)


def _split3(a):
    a1 = a.astype(BF16)
    r1 = a - a1.astype(F32)
    a2 = r1.astype(BF16)
    a3 = (r1 - a2.astype(F32)).astype(BF16)
    return a1, a2, a3


def _tri_matmul(tri, a):
    a1, a2, a3 = _split3(a)
    return _dot(tri, a1) + _dot(tri, a2) + _dot(tri, a3)


def _softplus(x):
    return jnp.maximum(x, 0.0) + jnp.log(1.0 + jnp.exp(-jnp.abs(x)))


def _dt_and_rate(dt_ref, bias_ref, alog_ref):
    dt = _softplus(dt_ref[...] + bias_ref[...])
    rate = dt * (-LOG2_E * jnp.exp(alog_ref[...]))
    return dt, rate


def _head_rows(mat, first, rows):
    n = mat.shape[1]
    return jnp.concatenate(
        [jnp.broadcast_to(mat[first + e:first + e + 1, :], (rows, n)) for e in range(HEADS_PER_GROUP)], axis=0)


def _ssd_bwd_state_kernel(xs_ref, b_ref, dt_ref, bias_ref, alog_ref, tl_ref, hb_ref, st_ref):
    c = pl.program_id(1)

    @pl.when(c == 0)
    def _():
        st_ref[...] = jnp.zeros_like(st_ref)

    for ci in reversed(range(SSD_BWD_CHUNKS)):
        rows = pl.ds(ci * CHUNK, CHUNK)
        _ssd_bwd_chunk(xs_ref.at[rows, :], b_ref.at[rows, :], dt_ref.at[rows, :], bias_ref, alog_ref, tl_ref,
                       hb_ref.at[ci], st_ref)


def _ssd_bwd_chunk(xs_ref, b_ref, dt_ref, bias_ref, alog_ref, tl_ref, hb_ref, st_ref):
    hb_ref[...] = st_ref[...].astype(BF16)
    dt, rate = _dt_and_rate(dt_ref, bias_ref, alog_ref)
    pre = _tri_matmul(tl_ref[...], rate)
    pre_t = pre.T
    excl_t = (pre - rate).T
    total = jnp.broadcast_to(pre_t[:, CHUNK - 1:CHUNK], (LANES, CHUNK))
    w_t = dt.T * jnp.exp2(excl_t)
    dec = jnp.exp2(total)
    off = N_SSM_HEADS
    for g in range(N_SSM_GROUPS):
        xs_t = xs_ref[:, g * 512:(g + 1) * 512].astype(F32).T
        xd = (xs_t * _head_rows(w_t, off + g * HEADS_PER_GROUP, SSM_HEAD_DIM)).astype(BF16)
        upd = _dot(xd, b_ref[:, g * D_STATE:(g + 1) * D_STATE])
        st_ref[g] = _head_rows(dec, off + g * HEADS_PER_GROUP, SSM_HEAD_DIM) * st_ref[g] + upd


def _ssd_bwd_states(xc, dt, bias128, alog128, tri_l, batch, seq):
    nc = seq // CHUNK
    ns = nc // SSD_BWD_CHUNKS
    rows = SSD_BWD_CHUNKS * CHUNK

    def rev(b, c):
        return (b * ns + ns - 1 - c, 0)

    return pl.pallas_call(
        _ssd_bwd_state_kernel,
        grid=(batch, ns),
        in_specs=[
            pl.BlockSpec((rows, D_INNER), rev),
            pl.BlockSpec((rows, BC_WIDTH), lambda b, c: (b * ns + ns - 1 - c, D_INNER // BC_WIDTH)),
            pl.BlockSpec((rows, LANES), rev),
            pl.BlockSpec((1, LANES), lambda b, c: (0, 0)),
            pl.BlockSpec((1, LANES), lambda b, c: (0, 0)),
            pl.BlockSpec((CHUNK, CHUNK), lambda b, c: (0, 0)),
        ],
        out_specs=pl.BlockSpec((SSD_BWD_CHUNKS, N_SSM_GROUPS, 512, D_STATE),
                               lambda b, c: (b * ns + ns - 1 - c, 0, 0, 0)),
        out_shape=jax.ShapeDtypeStruct((batch * nc, N_SSM_GROUPS, 512, D_STATE), BF16),
        scratch_shapes=[pltpu.VMEM((N_SSM_GROUPS, 512, D_STATE), F32)],
        compiler_params=_params(("arbitrary", "arbitrary")),
        name="ssd_bwd_states",
    )(xc, xc, dt, bias128, alog128, tri_l)


def _ssd_main_kernel(xc_ref, z_ref, dt_ref, hb_ref, bias_ref, alog_ref, tl_ref, tu_ref, dskip_ref, gain_ref,
                     o_ref, hf_ref, y_ref):
    c = pl.program_id(1)

    @pl.when(c == 0)
    def _():
        hf_ref[...] = jnp.zeros_like(hf_ref)

    for ci in range(SSD_MAIN_CHUNKS):
        rows = pl.ds(ci * CHUNK, CHUNK)
        _ssd_main_chunk(xc_ref.at[rows, :], z_ref.at[rows, :], dt_ref.at[rows, :], hb_ref.at[ci], bias_ref, alog_ref,
                        tl_ref, tu_ref, dskip_ref, gain_ref, o_ref.at[rows, :], hf_ref, y_ref)


def _ssd_main_chunk(xc_ref, z_ref, dt_ref, hb_ref, bias_ref, alog_ref, tl_ref, tu_ref, dskip_ref, gain_ref,
                    o_ref, hf_ref, y_ref):
    dt, rate = _dt_and_rate(dt_ref, bias_ref, alog_ref)
    lane = lax.broadcasted_iota(jnp.int32, (CHUNK, LANES), 1)
    cum = jnp.where(lane < N_SSM_HEADS, _tri_matmul(tl_ref[...], rate), _tri_matmul(tu_ref[...], rate))
    cum_t = cum.T
    dt_t = dt.T
    src_t = cum_t - jnp.log2(dt_t)
    row = lax.broadcasted_iota(jnp.int32, (CHUNK, CHUNK), 0)
    col = lax.broadcasted_iota(jnp.int32, (CHUNK, CHUNK), 1)
    lower = row >= col
    diag = row == col
    low = lane < SSM_HEAD_DIM
    zero_x = jnp.zeros((CHUNK, LANES), BF16)
    nb = N_SSM_HEADS

    def lane_bcast(mat, idx):
        return jnp.broadcast_to(mat[:, idx:idx + 1], (CHUNK, CHUNK))

    def sub_bcast(mat, idx):
        return jnp.broadcast_to(mat[idx:idx + 1, :], (CHUNK, CHUNK))

    def head_matrix(e, cb):
        col_f = lane_bcast(cum, e)
        col_b = lane_bcast(cum, nb + e)
        decay = jnp.exp2(jnp.where(lower, col_f - sub_bcast(src_t, e), col_b - sub_bcast(src_t, nb + e)))
        decay = decay + jnp.where(diag, sub_bcast(dt_t, nb + e), 0.0)
        return (decay * cb).astype(BF16), col_f, col_b

    for g in range(N_SSM_GROUPS):
        bg = xc_ref[:, D_INNER + g * D_STATE:D_INNER + (g + 1) * D_STATE]
        cg = xc_ref[:, D_INNER + BC_WIDTH + g * D_STATE:D_INNER + BC_WIDTH + (g + 1) * D_STATE]
        cb = _dot_nt(cg, bg)
        y_in_f = _dot_nt(cg, hf_ref[g].astype(BF16))
        y_in_b = _dot_nt(cg, hb_ref[g])
        for jp in range(HEADS_PER_GROUP // 2):
            e0 = g * HEADS_PER_GROUP + 2 * jp
            cols = slice(e0 * SSM_HEAD_DIM, e0 * SSM_HEAD_DIM + LANES)
            loc = slice(jp * LANES, (jp + 1) * LANES)
            xs_pair = xc_ref[:, cols]
            m0, cf0, cb0 = head_matrix(e0, cb)
            m1, cf1, cb1 = head_matrix(e0 + 1, cb)
            y = _dot(m0, jnp.where(low, xs_pair, zero_x)) + _dot(m1, jnp.where(low, zero_x, xs_pair))
            y = y + y_in_f[:, loc] * jnp.exp2(jnp.where(low, cf0, cf1))
            y = y + y_in_b[:, loc] * jnp.exp2(jnp.where(low, cb0, cb1))
            y_ref[:, cols] = y + dskip_ref[:, cols] * xs_pair.astype(F32)

    z = z_ref[...].astype(F32)
    y = y_ref[...] * (z * _sigmoid(z))
    ms = jnp.mean(y * y, axis=-1, keepdims=True)
    o_ref[...] = (y * lax.rsqrt(ms + EPS) * gain_ref[...]).astype(BF16)

    last = jnp.broadcast_to(cum_t[:, CHUNK - 1:CHUNK], (LANES, CHUNK))
    w_t = jnp.exp2(last - src_t)
    dec = jnp.exp2(last)
    for g in range(N_SSM_GROUPS):
        xs_t = xc_ref[:, g * 512:(g + 1) * 512].astype(F32).T
        xd = (xs_t * _head_rows(w_t, g * HEADS_PER_GROUP, SSM_HEAD_DIM)).astype(BF16)
        upd = _dot(xd, xc_ref[:, D_INNER + g * D_STATE:D_INNER + (g + 1) * D_STATE])
        hf_ref[g] = _head_rows(dec, g * HEADS_PER_GROUP, SSM_HEAD_DIM) * hf_ref[g] + upd


def _ssd_main(xc, proj, dt, hb, bias128, alog128, tri_l, tri_u, dskip, gain, batch, seq):
    t = xc.shape[0]
    ns = seq // CHUNK // SSD_MAIN_CHUNKS
    rows = SSD_MAIN_CHUNKS * CHUNK

    def tok(b, c):
        return (b * ns + c, 0)

    def const(b, c):
        return (0, 0)

    return pl.pallas_call(
        _ssd_main_kernel,
        grid=(batch, ns),
        in_specs=[
            pl.BlockSpec((rows, CONV_DIM), tok),
            pl.BlockSpec((rows, D_INNER), tok),
            pl.BlockSpec((rows, LANES), tok),
            pl.BlockSpec((SSD_MAIN_CHUNKS, N_SSM_GROUPS, 512, D_STATE), lambda b, c: (b * ns + c, 0, 0, 0)),
            pl.BlockSpec((1, LANES), const),
            pl.BlockSpec((1, LANES), const),
            pl.BlockSpec((CHUNK, CHUNK), const),
            pl.BlockSpec((CHUNK, CHUNK), const),
            pl.BlockSpec((1, D_INNER), const),
            pl.BlockSpec((1, D_INNER), const),
        ],
        out_specs=pl.BlockSpec((rows, D_INNER), tok),
        out_shape=jax.ShapeDtypeStruct((t, D_INNER), BF16),
        scratch_shapes=[pltpu.VMEM((N_SSM_GROUPS, 512, D_STATE), F32), pltpu.VMEM((CHUNK, D_INNER), F32)],
        compiler_params=_params(("arbitrary", "arbitrary")),
        name="ssd_main",
    )(xc, proj, dt, hb, bias128, alog128, tri_l, tri_u, dskip, gain)


def _outproj_kernel(attn_a, ssm_a, gate_a, x_a, attn_b, ssm_b, gate_b, x_b, wa_ref, ws_ref, wo_ref, gn_ref,
                    wr1_ref, wr2_ref, br_ref, o_ref, r_ref, *, n_a):
    i = pl.program_id(0)

    @pl.when(i < n_a)
    def _():
        _outproj_tile(attn_a, ssm_a, gate_a, x_a, wa_ref, ws_ref, wo_ref, gn_ref, wr1_ref, wr2_ref, br_ref,
                      o_ref, r_ref)

    @pl.when(i >= n_a)
    def _():
        _outproj_tile(attn_b, ssm_b, gate_b, x_b, wa_ref, ws_ref, wo_ref, gn_ref, wr1_ref, wr2_ref, br_ref,
                      o_ref, r_ref)


def _outproj_tile(attn_ref, ssm_ref, gate_ref, x_ref, wa_ref, ws_ref, wo_ref, gn_ref, wr1_ref, wr2_ref, br_ref,
                  o_ref, r_ref):
    a_out = _dot(attn_ref[...], wa_ref[...])
    s_out = _dot(ssm_ref[...], ws_ref[...])
    ga = gate_ref[:, :D_MODEL].astype(F32)
    gs = gate_ref[:, D_MODEL:].astype(F32)
    merged = _sigmoid(ga) * a_out + _sigmoid(gs) * s_out
    x2 = x_ref[...] + _dot(merged.astype(BF16), wo_ref[...])
    for j in range(TOKEN_TILE):
        o_ref[pl.ds(j, TM_OUT, stride=TOKEN_TILE), :] = x2[:, j * LANES:(j + 1) * LANES]

    ms = jnp.mean(x2 * x2, axis=-1, keepdims=True)
    hn = x2 * lax.rsqrt(ms + EPS) * gn_ref[...]
    h1 = hn.astype(BF16)
    h2 = (hn - h1.astype(F32)).astype(BF16)
    lg = _dot(h1, wr1_ref[...]) + _dot(h2, wr1_ref[...]) + _dot(h1, wr2_ref[...]) + br_ref[...]

    lane = lax.broadcasted_iota(jnp.int32, (TM_OUT, LANES), 1).astype(F32)
    big = float(LANES)

    def rmax(v):
        return jnp.max(v, axis=-1, keepdims=True)

    def first_lane(mask):
        return jnp.min(jnp.where(mask, lane, big), axis=-1, keepdims=True)

    gl = jnp.where(lane < N_EXPERT_GROUPS, lg, NEG_INF)
    gmax = rmax(gl)
    g_w = 1.0 / jnp.sum(jnp.exp(gl - gmax), axis=-1, keepdims=True)
    gidx = first_lane(gl == gmax)
    base = N_EXPERT_GROUPS + EXPERTS_PER_GROUP * gidx
    el = jnp.where(lane >= base, jnp.where(lane < base + EXPERTS_PER_GROUP, lg, NEG_INF), NEG_INF)
    m1 = rmax(el)
    i1 = first_lane(el == m1)
    el2 = jnp.where(lane == i1, NEG_INF, el)
    m2 = rmax(el2)
    i2 = first_lane(el2 == m2)
    r = jnp.exp(m2 - m1)
    w1 = g_w / (1.0 + r)
    w2 = w1 * r
    j1 = i1 - base
    j2 = i2 - base
    swap = j1 > j2
    e_lo = jnp.where(swap, j2, j1)
    e_hi = jnp.where(swap, j1, j2)
    w_lo = jnp.where(swap, w2, w1)
    w_hi = jnp.where(swap, w1, w2)
    pair = e_lo * (EXPERTS_PER_GROUP - 1) - e_lo * (e_lo - 1.0) * 0.5 + (e_hi - e_lo - 1.0)
    cls = gidx * N_PAIRS + pair
    r_ref[...] = jnp.where(lane == 0.0, cls, jnp.where(lane == 1.0, w_lo, jnp.where(lane == 2.0, w_hi, 0.0)))


def _outproj(group_a, group_b, wa, ws, wo, gn, wr1, wr2, br):
    n_a = group_a[3].shape[0] // TM_OUT
    n_b = group_b[3].shape[0] // TM_OUT
    t = (n_a + n_b) * TM_OUT

    def first(i):
        return (jnp.minimum(i, n_a - 1), 0)

    def second(i):
        return (jnp.maximum(i - n_a, 0), 0)

    def const(i):
        return (0, 0)

    def group_specs(tok):
        return [
            pl.BlockSpec((TM_OUT, ATTN_WIDTH), tok),
            pl.BlockSpec((TM_OUT, D_INNER), tok),
            pl.BlockSpec((TM_OUT, 2 * D_MODEL), lambda i: (tok(i)[0], COL_GATE // (2 * D_MODEL))),
            pl.BlockSpec((TM_OUT, D_MODEL), tok),
        ]

    resident = dict(pipeline_mode=pl.Buffered(1))
    return pl.pallas_call(
        functools.partial(_outproj_kernel, n_a=n_a),
        grid=(n_a + n_b,),
        in_specs=group_specs(first) + group_specs(second) + [
            pl.BlockSpec((ATTN_WIDTH, D_MODEL), const, **resident),
            pl.BlockSpec((D_INNER, D_MODEL), const, **resident),
            pl.BlockSpec((D_MODEL, D_MODEL), const, **resident),
            pl.BlockSpec((1, D_MODEL), const),
            pl.BlockSpec((D_MODEL, LANES), const),
            pl.BlockSpec((D_MODEL, LANES), const),
            pl.BlockSpec((1, LANES), const),
        ],
        out_specs=[
            pl.BlockSpec((TM_OUT * TOKEN_TILE, LANES), lambda i: (i, 0)),
            pl.BlockSpec((TM_OUT, LANES), lambda i: (i, 0)),
        ],
        out_shape=[
            jax.ShapeDtypeStruct((t * TOKEN_TILE, LANES), F32),
            jax.ShapeDtypeStruct((t, LANES), F32),
        ],
        compiler_params=_params(("arbitrary",)),
        name="outproj_router",
    )(*group_a, *group_b, wa, ws, wo, gn, wr1, wr2, br)


def _moe_kernel(ea_ref, eb_ref, nv_ref, tokc_ref, tokn_ref, roww_ref, x_ref, gn_ref,
                wga_ref, wua_ref, wda_ref, wgb_ref, wub_ref, wdb_ref, o_ref,
                xg_ref, st_ref, gsem, ssem, *, n_blocks):
    i = pl.program_id(0)
    slot = i % 2
    other = 1 - slot

    def tile(idx):
        return pl.ds(pl.multiple_of(idx * TOKEN_TILE, TOKEN_TILE), TOKEN_TILE)

    def gather_copy(tok, r, s):
        return pltpu.make_async_copy(x_ref.at[tile(tok), :], xg_ref.at[s, tile(r), :], gsem.at[s])

    def scatter_copy(tok, r, s):
        return pltpu.make_async_copy(st_ref.at[s, tile(r), :], o_ref.at[tile(tok), :], ssem.at[s])

    def for_rows(n, fn):
        n8 = lax.shift_right_logical(n, 3)

        def body8(g, _):
            for u in range(8):
                fn(g * 8 + u)
            return 0

        def body1(r, _):
            fn(r)
            return 0

        lax.fori_loop(0, n8, body8, 0)
        lax.fori_loop(n8 * 8, n, body1, 0)

    def start_gathers(tok_ref, n, s):
        for_rows(n, lambda r: gather_copy(tok_ref[0, 0, r], r, s).start())

    def wait_gathers(n, s):
        for_rows(n, lambda r: gather_copy(0, 0, s).wait())

    def start_scatters(tok_ref, n, s):
        for_rows(n, lambda r: scatter_copy(tok_ref[0, 0, r], r, s).start())

    def wait_scatters(n, s):
        for_rows(n, lambda r: scatter_copy(0, 0, s).wait())

    @pl.when(i == 0)
    def _():
        xg_ref[...] = jnp.zeros_like(xg_ref)
        start_gathers(tokc_ref, nv_ref[0], 0)

    @pl.when(i + 1 < n_blocks)
    def _():
        start_gathers(tokn_ref, nv_ref[jnp.minimum(i + 1, n_blocks - 1)], other)

    wait_gathers(nv_ref[i], slot)

    @pl.when(i >= 2)
    def _():
        wait_scatters(nv_ref[jnp.maximum(i - 2, 0)], slot)

    @pl.when(nv_ref[i] > 0)
    def _():
        x = jnp.concatenate(
            [xg_ref[slot, pl.ds(j, ROW_BLOCK, stride=TOKEN_TILE), :] for j in range(TOKEN_TILE)], axis=1)
        w_cols = jnp.concatenate([roww_ref[0], jnp.zeros((LANES - TOKEN_TILE, ROW_BLOCK), F32)], axis=0).T
        w_lo = w_cols[:, 0:1]
        w_hi = w_cols[:, 1:2]
        ms = jnp.mean(x * x, axis=-1, keepdims=True)
        hn = (x * lax.rsqrt(ms + EPS) * gn_ref[...]).astype(BF16)

        def expert(wg_ref, wu_ref, wd_ref):
            gte = _dot(hn, wg_ref[0])
            up = _dot(hn, wu_ref[0])
            return _dot((gte * _sigmoid(gte) * up).astype(BF16), wd_ref[0])

        ya = expert(wga_ref, wua_ref, wda_ref)
        yb = expert(wgb_ref, wub_ref, wdb_ref)
        out = x + w_lo * ya + w_hi * yb
        for j in range(TOKEN_TILE):
            st_ref[slot, pl.ds(j, ROW_BLOCK, stride=TOKEN_TILE), :] = out[:, j * LANES:(j + 1) * LANES]

    start_scatters(tokc_ref, nv_ref[i], slot)

    @pl.when(i == n_blocks - 1)
    def _():
        wait_scatters(nv_ref[jnp.maximum(i - 1, 0)], other)
        wait_scatters(nv_ref[i], slot)


def _moe(ea, eb, nvalid, row_tok, row_w, x2t, gn, wg, wu, wd):
    n_blocks = row_tok.shape[0]

    def wa(i, ea, eb, nv):
        return (ea[i], 0, 0)

    def wb(i, ea, eb, nv):
        return (eb[i], 0, 0)

    any_spec = pl.BlockSpec(memory_space=pl.ANY)
    grid_spec = pltpu.PrefetchScalarGridSpec(
        num_scalar_prefetch=3,
        grid=(n_blocks,),
        in_specs=[
            pl.BlockSpec((1, 1, ROW_BLOCK), lambda i, ea, eb, nv: (i, 0, 0), memory_space=pltpu.SMEM),
            pl.BlockSpec((1, 1, ROW_BLOCK), lambda i, ea, eb, nv: (jnp.minimum(i + 1, n_blocks - 1), 0, 0),
                         memory_space=pltpu.SMEM),
            pl.BlockSpec((1, TOKEN_TILE, ROW_BLOCK), lambda i, ea, eb, nv: (i, 0, 0)),
            any_spec,
            pl.BlockSpec((1, D_MODEL), lambda i, ea, eb, nv: (0, 0)),
            pl.BlockSpec((1, D_MODEL, D_EXPERT), wa), pl.BlockSpec((1, D_MODEL, D_EXPERT), wa),
            pl.BlockSpec((1, D_EXPERT, D_MODEL), wa),
            pl.BlockSpec((1, D_MODEL, D_EXPERT), wb), pl.BlockSpec((1, D_MODEL, D_EXPERT), wb),
            pl.BlockSpec((1, D_EXPERT, D_MODEL), wb),
        ],
        out_specs=any_spec,
        scratch_shapes=[
            pltpu.VMEM((2, ROW_BLOCK * TOKEN_TILE, LANES), F32),
            pltpu.VMEM((2, ROW_BLOCK * TOKEN_TILE, LANES), F32),
            pltpu.SemaphoreType.DMA((2,)),
            pltpu.SemaphoreType.DMA((2,)),
        ],
    )
    return pl.pallas_call(
        functools.partial(_moe_kernel, n_blocks=n_blocks),
        grid_spec=grid_spec,
        out_shape=jax.ShapeDtypeStruct(x2t.shape, F32),
        compiler_params=_params(("arbitrary",)),
        name="moe",
    )(ea, eb, nvalid, row_tok, row_tok, row_w, x2t, gn, wg, wu, wd, wg, wu, wd)


def _untile_kernel(x_ref, o_ref):
    for j in range(TOKEN_TILE):
        o_ref[:, j * LANES:(j + 1) * LANES] = x_ref[pl.ds(j, TM_UNTILE, stride=TOKEN_TILE), :]


def _untile(y_tiles, first_token, n_tokens):
    first_block = first_token // TM_UNTILE
    return pl.pallas_call(
        _untile_kernel,
        grid=(n_tokens // TM_UNTILE,),
        in_specs=[pl.BlockSpec((TM_UNTILE * TOKEN_TILE, LANES), lambda i: (first_block + i, 0))],
        out_specs=pl.BlockSpec((TM_UNTILE, D_MODEL), lambda i: (i, 0)),
        out_shape=jax.ShapeDtypeStruct((n_tokens, D_MODEL), F32),
        compiler_params=_params(("arbitrary",)),
        name="untile",
    )(y_tiles)


def _pair_tables():
    lo, hi = [], []
    for a in range(EXPERTS_PER_GROUP):
        for b in range(a + 1, EXPERTS_PER_GROUP):
            lo.append(a)
            hi.append(b)
    return np.asarray(lo, np.int32), np.asarray(hi, np.int32)


def _block_tables(rinfo):
    cls = rinfo[:, 0].astype(jnp.int32)
    t = cls.shape[0]
    n_blocks = t // ROW_BLOCK + N_CLASSES
    sorted_cls, order = lax.sort((cls, jnp.arange(t, dtype=jnp.int32)), num_keys=1)
    class_ids = jnp.arange(N_CLASSES + 1, dtype=jnp.int32)
    starts = jnp.sum((sorted_cls[:, None] < class_ids[None, :]).astype(jnp.int32), axis=0)
    counts = starts[1:] - starts[:-1]
    nblk = (counts + ROW_BLOCK - 1) // ROW_BLOCK
    blk_end = jnp.cumsum(nblk)
    blk_start = blk_end - nblk
    used = blk_end[-1]
    b = jnp.arange(n_blocks, dtype=jnp.int32)
    b_eff = jnp.minimum(b, used - 1)
    c = jnp.sum((blk_end[None, :] <= b_eff[:, None]).astype(jnp.int32), axis=1)
    c = jnp.minimum(c, N_CLASSES - 1)
    off = b_eff - blk_start[c]
    src = starts[c] + off * ROW_BLOCK
    nvalid = jnp.where(b < used, jnp.clip(counts[c] - off * ROW_BLOCK, 0, ROW_BLOCK), 0).astype(jnp.int32)
    pair_lo, pair_hi = _pair_tables()
    grp = c // N_PAIRS
    ea = (grp * EXPERTS_PER_GROUP + jnp.asarray(pair_lo)[c % N_PAIRS]).astype(jnp.int32)
    eb = (grp * EXPERTS_PER_GROUP + jnp.asarray(pair_hi)[c % N_PAIRS]).astype(jnp.int32)
    rows = jnp.clip(src[:, None] + jnp.arange(ROW_BLOCK, dtype=jnp.int32)[None, :], 0, t - 1)
    row_tok = order[rows]
    row_w = jnp.zeros((n_blocks, TOKEN_TILE, ROW_BLOCK), F32)
    row_w = row_w.at[:, 0, :].set(rinfo[:, 1][row_tok]).at[:, 1, :].set(rinfo[:, 2][row_tok])
    return ea, eb, nvalid, row_tok.reshape(n_blocks, 1, ROW_BLOCK), row_w


def _rope_tables(seq):
    inv = 1.0 / (ROPE_THETA ** (jnp.arange(0, HEAD_DIM, 2, dtype=F32) / HEAD_DIM))
    ang = jnp.arange(seq, dtype=F32)[:, None] * inv[None, :]
    cos, sin = jnp.cos(ang), jnp.sin(ang)
    cos128 = jnp.concatenate([cos, cos, cos, cos], axis=-1)
    sin128 = jnp.concatenate([-sin, sin, -sin, sin], axis=-1)
    return cos128, sin128


def _prepare_weights(norm_mix, w_in, q_norm, k_norm, attn_sink, conv_w, conv_b, a_log_fwd, a_log_bwd,
                     dt_bias_fwd, dt_bias_bwd, d_skip, ssm_norm, w_out_attn, w_out_ssm, w_o, norm_ffn,
                     w_router_group, b_router_group, w_router_expert, b_router_expert, w_gate, w_up, w_down):
    o_q = 0
    o_k = o_q + ATTN_WIDTH
    o_v = o_k + KV_WIDTH
    o_z = o_v + KV_WIDTH
    o_xbc = o_z + D_INNER
    o_dtf = o_xbc + CONV_DIM
    o_dtb = o_dtf + N_SSM_HEADS
    o_ga = o_dtb + N_SSM_HEADS
    o_gs = o_ga + D_MODEL
    w = w_in.astype(BF16)
    w_r = jnp.concatenate([
        w[:, o_z:o_z + D_INNER], w[:, o_ga:o_gs + D_MODEL], w[:, o_xbc:o_xbc + CONV_DIM],
        w[:, o_q:o_q + ATTN_WIDTH], w[:, o_k:o_k + KV_WIDTH], w[:, o_v:o_v + KV_WIDTH],
        w[:, o_dtf:o_dtb + N_SSM_HEADS], jnp.zeros((D_MODEL, LANES - 2 * N_SSM_HEADS), w.dtype)], axis=1)
    pad64 = jnp.zeros((LANES - 2 * N_SSM_HEADS,), F32)
    eye = np.kron(np.eye(2, dtype=np.float32), np.ones((HEAD_DIM, HEAD_DIM), np.float32))
    idx = np.arange(CHUNK)
    w_router = jnp.concatenate([w_router_group, w_router_expert,
                                jnp.zeros((D_MODEL, LANES - N_EXPERT_GROUPS - N_EXPERTS), F32)], axis=1)
    wr1 = w_router.astype(BF16)
    return dict(
        norm_mix=norm_mix.reshape(1, D_MODEL),
        w_in=w_r,
        qg128=jnp.tile(q_norm, 2).reshape(1, LANES),
        kg128=jnp.tile(k_norm, 2).reshape(1, LANES),
        seg=jnp.asarray(eye, BF16),
        sink=attn_sink.astype(F32),
        conv_w=conv_w,
        conv_b=conv_b.reshape(1, CONV_DIM),
        alog128=jnp.concatenate([a_log_fwd, a_log_bwd, pad64]).reshape(1, LANES),
        bias128=jnp.concatenate([dt_bias_fwd, dt_bias_bwd, pad64]).reshape(1, LANES),
        tri_l=jnp.asarray(idx[:, None] >= idx[None, :], BF16),
        tri_u=jnp.asarray(idx[:, None] <= idx[None, :], BF16),
        dskip=jnp.repeat(d_skip, SSM_HEAD_DIM).reshape(1, D_INNER),
        ssm_norm=ssm_norm.reshape(1, D_INNER),
        wa=w_out_attn.astype(BF16), ws=w_out_ssm.astype(BF16), wo=w_o.astype(BF16),
        norm_ffn=norm_ffn.reshape(1, D_MODEL),
        wr1=wr1, wr2=(w_router - wr1.astype(F32)).astype(BF16),
        br=jnp.concatenate([b_router_group, b_router_expert,
                            jnp.zeros((LANES - N_EXPERT_GROUPS - N_EXPERTS,), F32)]).reshape(1, LANES),
        wg=w_gate.astype(BF16), wu=w_up.astype(BF16), wd=w_down.astype(BF16),
    )


def _mixer(x, p):
    batch, seq, _ = x.shape
    x2d = x.reshape(batch * seq, D_MODEL)
    proj, dt = _inproj(x2d, p['norm_mix'], p['w_in'])
    cos128, sin128 = _rope_tables(seq)
    qr, kdup, vdup = _qkprep(proj, cos128, sin128, p['qg128'], p['kg128'], p['seg'], seq)
    attn = _attention(qr, kdup, vdup, p['sink'], batch, seq)
    xc = _conv(proj, p['conv_w'], p['conv_b'], batch, seq)
    hb = _ssd_bwd_states(xc, dt, p['bias128'], p['alog128'], p['tri_l'], batch, seq)
    ssm = _ssd_main(xc, proj, dt, hb, p['bias128'], p['alog128'], p['tri_l'], p['tri_u'], p['dskip'],
                    p['ssm_norm'], batch, seq)
    return attn, ssm, proj, x2d


def kernel(x_prompt, x_sample, norm_mix, w_in, q_norm, k_norm, attn_sink, conv_w, conv_b, a_log_fwd, a_log_bwd,
           dt_bias_fwd, dt_bias_bwd, d_skip, ssm_norm, w_out_attn, w_out_ssm, w_o, norm_ffn, w_router_group,
           b_router_group, w_router_expert, b_router_expert, w_gate, w_up, w_down):
    assert norm_mix.shape[0] == 1, "single-layer encoder"
    p = _prepare_weights(norm_mix[0], w_in[0], q_norm[0], k_norm[0], attn_sink[0], conv_w[0], conv_b[0],
                         a_log_fwd[0], a_log_bwd[0], dt_bias_fwd[0], dt_bias_bwd[0], d_skip[0], ssm_norm[0],
                         w_out_attn[0], w_out_ssm[0], w_o[0], norm_ffn[0], w_router_group[0], b_router_group[0],
                         w_router_expert[0], b_router_expert[0], w_gate[0], w_up[0], w_down[0])
    x2t, rinfo = _outproj(_mixer(x_prompt, p), _mixer(x_sample, p), p['wa'], p['ws'], p['wo'], p['norm_ffn'],
                          p['wr1'], p['wr2'], p['br'])
    ea, eb, nvalid, row_tok, row_w = _block_tables(rinfo)
    y = _moe(ea, eb, nvalid, row_tok, row_w, x2t, p['norm_ffn'], p['wg'], p['wu'], p['wd'])
    t_a = x_prompt.shape[0] * x_prompt.shape[1]
    t_b = x_sample.shape[0] * x_sample.shape[1]
    return _untile(y, 0, t_a).reshape(x_prompt.shape), _untile(y, t_a, t_b).reshape(x_sample.shape)
```

```python
import functools

import numpy as np
import jax
import jax.numpy as jnp
from jax import lax
from jax.experimental import pallas as pl
from jax.experimental.pallas import tpu as pltpu

F32 = jnp.float32
BF16 = jnp.bfloat16

D_MODEL = 1024
EPS = 1e-6
NEG_INF = -1e30
LOG2_E = 1.4426950408889634
N_Q_HEADS = 16
N_KV_HEADS = 4
HEAD_DIM = 64
ATTN_WIDTH = N_Q_HEADS * HEAD_DIM
KV_WIDTH = N_KV_HEADS * HEAD_DIM
ATTN_BLOCK = 128
ATTN_QB = 2
ROPE_THETA = 10000.0
D_INNER = 2 * D_MODEL
SSM_HEAD_DIM = 64
N_SSM_HEADS = D_INNER // SSM_HEAD_DIM
N_SSM_GROUPS = 4
HEADS_PER_GROUP = N_SSM_HEADS // N_SSM_GROUPS
D_STATE = 128
BC_WIDTH = N_SSM_GROUPS * D_STATE
CONV_DIM = D_INNER + 2 * BC_WIDTH
CONV_W = 7
CHUNK = 128
SSD_BWD_CHUNKS = 4
SSD_MAIN_CHUNKS = 2
N_EXPERT_GROUPS = 4
EXPERTS_PER_GROUP = 8
N_EXPERTS = N_EXPERT_GROUPS * EXPERTS_PER_GROUP
D_EXPERT = 512
N_PAIRS = EXPERTS_PER_GROUP * (EXPERTS_PER_GROUP - 1) // 2
N_CLASSES = N_EXPERT_GROUPS * N_PAIRS

LANES = 128
V7X_VMEM_LIMIT_BYTES = 56 * 1024 * 1024

COL_Z = 0
COL_GATE = COL_Z + D_INNER
COL_XS = COL_GATE + 2 * D_MODEL
COL_B = COL_XS + D_INNER
COL_C = COL_B + BC_WIDTH
COL_Q = COL_C + BC_WIDTH
COL_K = COL_Q + ATTN_WIDTH
COL_V = COL_K + KV_WIDTH
COL_DT = COL_V + KV_WIDTH
N_PROJ = COL_DT + LANES

TM_IN = 1024
NJ_IN = 3
TN_IN = N_PROJ // NJ_IN
CH_IN = 512
TM_OUT = 512
TM_UNTILE = 1024
CONV_CT = 512
CONV_ROWS = 256
CONV_PITCH = 2
ROW_BLOCK = 128
TOKEN_TILE = D_MODEL // LANES


def _params(sem, flags=None):
    return pltpu.CompilerParams(dimension_semantics=sem, vmem_limit_bytes=V7X_VMEM_LIMIT_BYTES, flags=flags)


def _dot(a, b):
    return jnp.dot(a, b, preferred_element_type=F32)


def _dot_nt(a, b):
    return lax.dot_general(a, b, (((1,), (1,)), ((), ())), preferred_element_type=F32)


def _sigmoid(x):
    return 1.0 / (1.0 + jnp.exp(-x))


def _inproj_kernel(x_ref, g_ref, w_ref, cos_ref, sin_ref, qg_ref, kg_ref, seg_ref,
                   o_ref, dt_ref, qo_ref, ko_ref, vo_ref, h_ref):
    j = pl.program_id(1)

    @pl.when(j == 0)
    def _():
        x = x_ref[...]
        ms = jnp.mean(x * x, axis=-1, keepdims=True)
        h_ref[...] = (x * lax.rsqrt(ms + EPS) * g_ref[...]).astype(BF16)

    def project(c0, c1):
        acc = _dot(h_ref[...], w_ref[:, c0:c1])
        o_ref[:, c0:c1] = acc.astype(BF16)
        return acc

    @pl.when(j < NJ_IN - 1)
    def _():
        for c0 in range(0, TN_IN, CH_IN):
            project(c0, min(c0 + CH_IN, TN_IN))

    @pl.when(j == NJ_IN - 1)
    def _():
        base = (NJ_IN - 1) * TN_IN
        q0, k0, v0, d0 = COL_Q - base, COL_K - base, COL_V - base, COL_DT - base
        for c0 in range(0, q0, CH_IN):
            project(c0, min(c0 + CH_IN, q0))
        cos = cos_ref[...]
        sin = sin_ref[...]
        seg = seg_ref[...]
        lane = lax.broadcasted_iota(jnp.int32, (TM_IN, LANES), 1)
        first_half = (lane % HEAD_DIM) < (HEAD_DIM // 2)
        low = lane < HEAD_DIM

        def norm_rope(x, gain):
            ss = _dot((x * x).astype(BF16), seg)
            xn = x * lax.rsqrt(ss * (1.0 / HEAD_DIM) + EPS) * gain
            rot = jnp.where(first_half, pltpu.roll(xn, 96, 1), pltpu.roll(xn, 32, 1))
            return xn * cos + rot * sin

        def duplicate(y, dst_ref, s):
            ysw = pltpu.roll(y, HEAD_DIM, 1)
            dst_ref[:, (2 * s) * LANES:(2 * s + 1) * LANES] = jnp.where(low, y, ysw).astype(BF16)
            dst_ref[:, (2 * s + 1) * LANES:(2 * s + 2) * LANES] = jnp.where(low, ysw, y).astype(BF16)

        for c0 in range(q0, k0, CH_IN):
            acc = project(c0, c0 + CH_IN)
            for s in range(CH_IN // LANES):
                y = norm_rope(acc[:, s * LANES:(s + 1) * LANES], qg_ref[...]) * (HEAD_DIM ** -0.5 * LOG2_E)
                dst = c0 - q0 + s * LANES
                qo_ref[:, dst:dst + LANES] = y.astype(BF16)
        acc = project(k0, d0)
        for s in range(KV_WIDTH // LANES):
            duplicate(norm_rope(acc[:, s * LANES:(s + 1) * LANES], kg_ref[...]), ko_ref, s)
            duplicate(acc[:, KV_WIDTH + s * LANES:KV_WIDTH + (s + 1) * LANES], vo_ref, s)
        dt_ref[...] = project(d0, TN_IN)


def _inproj(x2d, gain, w_bf16, cos128, sin128, qg128, kg128, seg, seq):
    t = x2d.shape[0]
    nseq = seq // TM_IN

    def rows(i, j):
        return (i, 0)

    def const(i, j):
        return (0, 0)

    def pos(i, j):
        return (i % nseq, 0)

    return pl.pallas_call(
        _inproj_kernel,
        grid=(t // TM_IN, NJ_IN),
        in_specs=[
            pl.BlockSpec((TM_IN, D_MODEL), rows),
            pl.BlockSpec((1, D_MODEL), const),
            pl.BlockSpec((D_MODEL, TN_IN), lambda i, j: (0, j)),
            pl.BlockSpec((TM_IN, LANES), pos),
            pl.BlockSpec((TM_IN, LANES), pos),
            pl.BlockSpec((1, LANES), const),
            pl.BlockSpec((1, LANES), const),
            pl.BlockSpec((LANES, LANES), const),
        ],
        out_specs=[
            pl.BlockSpec((TM_IN, TN_IN), lambda i, j: (i, j)),
            pl.BlockSpec((TM_IN, LANES), rows),
            pl.BlockSpec((TM_IN, ATTN_WIDTH), rows),
            pl.BlockSpec((TM_IN, 2 * KV_WIDTH), rows),
            pl.BlockSpec((TM_IN, 2 * KV_WIDTH), rows),
        ],
        out_shape=[
            jax.ShapeDtypeStruct((t, N_PROJ), BF16),
            jax.ShapeDtypeStruct((t, LANES), F32),
            jax.ShapeDtypeStruct((t, ATTN_WIDTH), BF16),
            jax.ShapeDtypeStruct((t, 2 * KV_WIDTH), BF16),
            jax.ShapeDtypeStruct((t, 2 * KV_WIDTH), BF16),
        ],
        scratch_shapes=[pltpu.VMEM((TM_IN, D_MODEL), BF16)],
        compiler_params=_params(("arbitrary", "arbitrary")),
        name="inproj",
    )(x2d, gain, w_bf16, cos128, sin128, qg128, kg128, seg)


def _attn_kernel(sink_ref, q_ref, kp_ref, kc_ref, kn_ref, vp_ref, vc_ref, vn_ref, o_ref, *, n_steps):
    i = pl.program_id(1)
    nb = ATTN_BLOCK
    nk = 3 * nb
    qi = lax.broadcasted_iota(jnp.int32, (nb, nk), 0)
    si = lax.broadcasted_iota(jnp.int32, (nb, nk), 1)
    rel = qi - (si - nb)
    band = jnp.where(rel <= nb, jnp.where(rel >= -nb, 0.0, NEG_INF), NEG_INF)
    bias_first = jnp.where(si < nb, jnp.where(i > 0, band, NEG_INF), band)
    bias_last = jnp.where(si >= 2 * nb, jnp.where(i < n_steps - 1, band, NEG_INF), band)
    low_q = lax.broadcasted_iota(jnp.int32, (nb, LANES), 1) < HEAD_DIM
    low_k = lax.broadcasted_iota(jnp.int32, (nk, LANES), 1) < HEAD_DIM
    zero_q = jnp.zeros((nb, LANES), BF16)
    zero_k = jnp.zeros((nk, LANES), BF16)

    def window(p_ref, c_ref, n_ref, h, j):
        sl = slice(h * LANES, (h + 1) * LANES)
        rows = jnp.concatenate([p_ref[:, sl], c_ref[:, sl], n_ref[:, sl]], axis=0)
        return rows[j * nb:j * nb + nk]

    def scores(j, h):
        kd = window(kp_ref, kc_ref, kn_ref, h, j)
        slabs = [q_ref[j * nb:(j + 1) * nb, (2 * h + u) * LANES:(2 * h + u + 1) * LANES] for u in range(2)]
        q4 = jnp.concatenate([jnp.where(low_q, s_, zero_q) for s_ in slabs]
                             + [jnp.where(low_q, zero_q, s_) for s_ in slabs], axis=0)
        return _dot_nt(q4, kd)

    def finish(j, h, s4):
        vd = window(vp_ref, vc_ref, vn_ref, h, j)
        v_lo = jnp.where(low_k, vd, zero_k)
        v_hi = jnp.where(low_k, zero_k, vd)
        heads = (4 * h, 4 * h + 2, 4 * h + 1, 4 * h + 3)
        ps, invs = [], []
        for k, head in enumerate(heads):
            s = s4[k * nb:(k + 1) * nb]
            left = s[:, :nb] + (bias_first if j == 0 else band)[:, :nb]
            right = s[:, 2 * nb:] + (bias_last if j == ATTN_QB - 1 else band)[:, 2 * nb:]
            s = jnp.concatenate([left, s[:, nb:2 * nb], right], axis=1)
            snk = sink_ref[head] * LOG2_E
            m = jnp.maximum(jnp.max(s, axis=-1, keepdims=True), snk)
            p = jnp.exp2(s - m)
            den = jnp.sum(p, axis=-1, keepdims=True) + jnp.exp2(snk - m)
            ps.append(p.astype(BF16))
            invs.append(1.0 / den)
        o = _dot(jnp.concatenate(ps[:2], axis=0), v_lo) + _dot(jnp.concatenate(ps[2:], axis=0), v_hi)
        for u in range(2):
            ou = o[u * nb:(u + 1) * nb] * jnp.where(low_q, invs[u], invs[2 + u])
            o_ref[j * nb:(j + 1) * nb, (2 * h + u) * LANES:(2 * h + u + 1) * LANES] = ou.astype(BF16)

    tasks = [(j, h) for j in range(ATTN_QB) for h in range(N_KV_HEADS)]
    pending = scores(*tasks[0])
    for n, task in enumerate(tasks):
        following = scores(*tasks[n + 1]) if n + 1 < len(tasks) else None
        finish(*task, pending)
        pending = following


def _attention(qr, kdup, vdup, sink, batch, seq):
    t = qr.shape[0]
    nq = seq // ATTN_BLOCK
    n_steps = nq // ATTN_QB
    rows = ATTN_QB * ATTN_BLOCK

    def prev(b, i, s):
        return (b * nq + jnp.maximum(i * ATTN_QB - 1, 0), 0)

    def cur(b, i, s):
        return (b * n_steps + i, 0)

    def nxt(b, i, s):
        return (b * nq + jnp.minimum((i + 1) * ATTN_QB, nq - 1), 0)

    edge = (ATTN_BLOCK, 2 * KV_WIDTH)
    mid = (rows, 2 * KV_WIDTH)
    grid_spec = pltpu.PrefetchScalarGridSpec(
        num_scalar_prefetch=1,
        grid=(batch, n_steps),
        in_specs=[
            pl.BlockSpec((rows, ATTN_WIDTH), cur),
            pl.BlockSpec(edge, prev), pl.BlockSpec(mid, cur), pl.BlockSpec(edge, nxt),
            pl.BlockSpec(edge, prev), pl.BlockSpec(mid, cur), pl.BlockSpec(edge, nxt),
        ],
        out_specs=pl.BlockSpec((rows, ATTN_WIDTH), cur),
    )
    return pl.pallas_call(
        functools.partial(_attn_kernel, n_steps=n_steps),
        grid_spec=grid_spec,
        out_shape=jax.ShapeDtypeStruct((t, ATTN_WIDTH), BF16),
        compiler_params=_params(("arbitrary", "arbitrary")),
        name="attention",
    )(sink, qr, kdup, kdup, kdup, vdup, vdup, vdup)


def _conv_kernel(x_ref, w_ref, b_ref, o_ref, pad_ref, *, seq):
    halo = 8
    step = CONV_PITCH

    def rows(first, n):
        return pl.ds(step * (first + halo), n, stride=step)

    for h in range(CONV_CT // LANES):
        lanes = slice(h * LANES, (h + 1) * LANES)
        pad_ref[h, rows(-halo, halo), :] = jnp.zeros((halo, LANES), F32)
        pad_ref[h, rows(seq, halo), :] = jnp.zeros((halo, LANES), F32)
        for r in range(seq // CONV_ROWS):
            pad_ref[h, rows(r * CONV_ROWS, CONV_ROWS), :] = x_ref[r * CONV_ROWS:(r + 1) * CONV_ROWS, lanes].astype(F32)
    w = w_ref[...]
    bias = b_ref[...]
    for h in range(CONV_CT // LANES):
        lanes = slice(h * LANES, (h + 1) * LANES)
        for r in range(seq // CONV_ROWS):
            r0 = r * CONV_ROWS
            acc = jnp.broadcast_to(bias[:, lanes], (CONV_ROWS, LANES))
            for k in range(CONV_W):
                acc = acc + pad_ref[h, rows(r0 + k - CONV_W // 2, CONV_ROWS), :] * w[k:k + 1, lanes]
            o_ref[r0:r0 + CONV_ROWS, lanes] = (acc * _sigmoid(acc)).astype(BF16)


def _conv(proj, conv_w, conv_b, batch, seq):
    t = proj.shape[0]
    return pl.pallas_call(
        functools.partial(_conv_kernel, seq=seq),
        grid=(batch, CONV_DIM // CONV_CT),
        in_specs=[
            pl.BlockSpec((seq, CONV_CT), lambda b, c: (b, COL_XS // CONV_CT + c)),
            pl.BlockSpec((CONV_W, CONV_CT), lambda b, c: (0, c)),
            pl.BlockSpec((1, CONV_CT), lambda b, c: (0, c)),
        ],
        out_specs=pl.BlockSpec((seq, CONV_CT), lambda b, c: (b, c)),
        out_shape=jax.ShapeDtypeStruct((t, CONV_DIM), BF16),
        scratch_shapes=[pltpu.VMEM((CONV_CT // LANES, CONV_PITCH * (seq + 16), LANES), F32)],
        compiler_params=_params(("arbitrary", "arbitrary")),
        name="conv",
    )(proj, conv_w, conv_b)


def _split3(a):
    a1 = a.astype(BF16)
    r1 = a - a1.astype(F32)
    a2 = r1.astype(BF16)
    a3 = (r1 - a2.astype(F32)).astype(BF16)
    return a1, a2, a3


def _tri_matmul(tri, a):
    a1, a2, a3 = _split3(a)
    return _dot(tri, a1) + _dot(tri, a2) + _dot(tri, a3)


def _softplus(x):
    return jnp.maximum(x, 0.0) + jnp.log(1.0 + jnp.exp(-jnp.abs(x)))


def _dt_and_rate(dt_ref, bias_ref, alog_ref):
    dt = _softplus(dt_ref[...] + bias_ref[...])
    rate = dt * (-LOG2_E * jnp.exp(alog_ref[...]))
    return dt, rate


def _head_rows(mat, first, rows):
    n = mat.shape[1]
    return jnp.concatenate(
        [jnp.broadcast_to(mat[first + e:first + e + 1, :], (rows, n)) for e in range(HEADS_PER_GROUP)], axis=0)


def _ssd_bwd_state_kernel(xs_ref, b_ref, dt_ref, bias_ref, alog_ref, tl_ref, hb_ref, st_ref):
    c = pl.program_id(1)

    @pl.when(c == 0)
    def _():
        st_ref[...] = jnp.zeros_like(st_ref)

    for ci in reversed(range(SSD_BWD_CHUNKS)):
        rows = pl.ds(ci * CHUNK, CHUNK)
        _ssd_bwd_chunk(xs_ref.at[rows, :], b_ref.at[rows, :], dt_ref.at[rows, :], bias_ref, alog_ref, tl_ref,
                       hb_ref.at[ci], st_ref)


def _ssd_bwd_chunk(xs_ref, b_ref, dt_ref, bias_ref, alog_ref, tl_ref, hb_ref, st_ref):
    hb_ref[...] = st_ref[...].astype(BF16)
    dt, rate = _dt_and_rate(dt_ref, bias_ref, alog_ref)
    pre = _tri_matmul(tl_ref[...], rate)
    pre_t = pre.T
    excl_t = (pre - rate).T
    total = jnp.broadcast_to(pre_t[:, CHUNK - 1:CHUNK], (LANES, CHUNK))
    w_t = dt.T * jnp.exp2(excl_t)
    dec = jnp.exp2(total)
    off = N_SSM_HEADS
    for g in range(N_SSM_GROUPS):
        xs_t = xs_ref[:, g * 512:(g + 1) * 512].astype(F32).T
        xd = (xs_t * _head_rows(w_t, off + g * HEADS_PER_GROUP, SSM_HEAD_DIM)).astype(BF16)
        upd = _dot(xd, b_ref[:, g * D_STATE:(g + 1) * D_STATE])
        st_ref[g] = _head_rows(dec, off + g * HEADS_PER_GROUP, SSM_HEAD_DIM) * st_ref[g] + upd


def _ssd_bwd_states(xc, dt, bias128, alog128, tri_l, batch, seq):
    nc = seq // CHUNK
    ns = nc // SSD_BWD_CHUNKS
    rows = SSD_BWD_CHUNKS * CHUNK

    def rev(b, c):
        return (b * ns + ns - 1 - c, 0)

    return pl.pallas_call(
        _ssd_bwd_state_kernel,
        grid=(batch, ns),
        in_specs=[
            pl.BlockSpec((rows, D_INNER), rev),
            pl.BlockSpec((rows, BC_WIDTH), lambda b, c: (b * ns + ns - 1 - c, D_INNER // BC_WIDTH)),
            pl.BlockSpec((rows, LANES), rev),
            pl.BlockSpec((1, LANES), lambda b, c: (0, 0)),
            pl.BlockSpec((1, LANES), lambda b, c: (0, 0)),
            pl.BlockSpec((CHUNK, CHUNK), lambda b, c: (0, 0)),
        ],
        out_specs=pl.BlockSpec((SSD_BWD_CHUNKS, N_SSM_GROUPS, 512, D_STATE),
                               lambda b, c: (b * ns + ns - 1 - c, 0, 0, 0)),
        out_shape=jax.ShapeDtypeStruct((batch * nc, N_SSM_GROUPS, 512, D_STATE), BF16),
        scratch_shapes=[pltpu.VMEM((N_SSM_GROUPS, 512, D_STATE), F32)],
        compiler_params=_params(("arbitrary", "arbitrary")),
        name="ssd_bwd_states",
    )(xc, xc, dt, bias128, alog128, tri_l)


def _ssd_main_kernel(xc_ref, z_ref, dt_ref, hb_ref, bias_ref, alog_ref, tl_ref, tu_ref, dskip_ref, gain_ref,
                     o_ref, hf_ref, y_ref):
    c = pl.program_id(1)

    @pl.when(c == 0)
    def _():
        hf_ref[...] = jnp.zeros_like(hf_ref)

    for ci in range(SSD_MAIN_CHUNKS):
        rows = pl.ds(ci * CHUNK, CHUNK)
        _ssd_main_chunk(xc_ref.at[rows, :], z_ref.at[rows, :], dt_ref.at[rows, :], hb_ref.at[ci], bias_ref, alog_ref,
                        tl_ref, tu_ref, dskip_ref, gain_ref, o_ref.at[rows, :], hf_ref, y_ref)


def _ssd_main_chunk(xc_ref, z_ref, dt_ref, hb_ref, bias_ref, alog_ref, tl_ref, tu_ref, dskip_ref, gain_ref,
                    o_ref, hf_ref, y_ref):
    dt, rate = _dt_and_rate(dt_ref, bias_ref, alog_ref)
    lane = lax.broadcasted_iota(jnp.int32, (CHUNK, LANES), 1)
    cum = jnp.where(lane < N_SSM_HEADS, _tri_matmul(tl_ref[...], rate), _tri_matmul(tu_ref[...], rate))
    cum_t = cum.T
    dt_t = dt.T
    src_t = cum_t - jnp.log2(dt_t)
    row = lax.broadcasted_iota(jnp.int32, (CHUNK, CHUNK), 0)
    col = lax.broadcasted_iota(jnp.int32, (CHUNK, CHUNK), 1)
    lower = row >= col
    diag = row == col
    low = lane < SSM_HEAD_DIM
    zero_x = jnp.zeros((CHUNK, LANES), BF16)
    nb = N_SSM_HEADS

    def lane_bcast(mat, idx):
        return jnp.broadcast_to(mat[:, idx:idx + 1], (CHUNK, CHUNK))

    def sub_bcast(mat, idx):
        return jnp.broadcast_to(mat[idx:idx + 1, :], (CHUNK, CHUNK))

    def head_matrix(e, cb):
        col_f = lane_bcast(cum, e)
        col_b = lane_bcast(cum, nb + e)
        decay = jnp.exp2(jnp.where(lower, col_f - sub_bcast(src_t, e), col_b - sub_bcast(src_t, nb + e)))
        decay = decay + jnp.where(diag, sub_bcast(dt_t, nb + e), 0.0)
        return (decay * cb).astype(BF16), col_f, col_b

    for g in range(N_SSM_GROUPS):
        bg = xc_ref[:, D_INNER + g * D_STATE:D_INNER + (g + 1) * D_STATE]
        cg = xc_ref[:, D_INNER + BC_WIDTH + g * D_STATE:D_INNER + BC_WIDTH + (g + 1) * D_STATE]
        cb = _dot_nt(cg, bg)
        y_in_f = _dot_nt(cg, hf_ref[g].astype(BF16))
        y_in_b = _dot_nt(cg, hb_ref[g])
        for jp in range(HEADS_PER_GROUP // 2):
            e0 = g * HEADS_PER_GROUP + 2 * jp
            cols = slice(e0 * SSM_HEAD_DIM, e0 * SSM_HEAD_DIM + LANES)
            loc = slice(jp * LANES, (jp + 1) * LANES)
            xs_pair = xc_ref[:, cols]
            m0, cf0, cb0 = head_matrix(e0, cb)
            m1, cf1, cb1 = head_matrix(e0 + 1, cb)
            y = _dot(m0, jnp.where(low, xs_pair, zero_x)) + _dot(m1, jnp.where(low, zero_x, xs_pair))
            y = y + y_in_f[:, loc] * jnp.exp2(jnp.where(low, cf0, cf1))
            y = y + y_in_b[:, loc] * jnp.exp2(jnp.where(low, cb0, cb1))
            y_ref[:, cols] = y + dskip_ref[:, cols] * xs_pair.astype(F32)

    z = z_ref[...].astype(F32)
    y = y_ref[...] * (z * _sigmoid(z))
    ms = jnp.mean(y * y, axis=-1, keepdims=True)
    o_ref[...] = (y * lax.rsqrt(ms + EPS) * gain_ref[...]).astype(BF16)

    last = jnp.broadcast_to(cum_t[:, CHUNK - 1:CHUNK], (LANES, CHUNK))
    w_t = jnp.exp2(last - src_t)
    dec = jnp.exp2(last)
    for g in range(N_SSM_GROUPS):
        xs_t = xc_ref[:, g * 512:(g + 1) * 512].astype(F32).T
        xd = (xs_t * _head_rows(w_t, g * HEADS_PER_GROUP, SSM_HEAD_DIM)).astype(BF16)
        upd = _dot(xd, xc_ref[:, D_INNER + g * D_STATE:D_INNER + (g + 1) * D_STATE])
        hf_ref[g] = _head_rows(dec, g * HEADS_PER_GROUP, SSM_HEAD_DIM) * hf_ref[g] + upd


def _ssd_main(xc, proj, dt, hb, bias128, alog128, tri_l, tri_u, dskip, gain, batch, seq):
    t = xc.shape[0]
    ns = seq // CHUNK // SSD_MAIN_CHUNKS
    rows = SSD_MAIN_CHUNKS * CHUNK

    def tok(b, c):
        return (b * ns + c, 0)

    def const(b, c):
        return (0, 0)

    return pl.pallas_call(
        _ssd_main_kernel,
        grid=(batch, ns),
        in_specs=[
            pl.BlockSpec((rows, CONV_DIM), tok),
            pl.BlockSpec((rows, D_INNER), tok),
            pl.BlockSpec((rows, LANES), tok),
            pl.BlockSpec((SSD_MAIN_CHUNKS, N_SSM_GROUPS, 512, D_STATE), lambda b, c: (b * ns + c, 0, 0, 0)),
            pl.BlockSpec((1, LANES), const),
            pl.BlockSpec((1, LANES), const),
            pl.BlockSpec((CHUNK, CHUNK), const),
            pl.BlockSpec((CHUNK, CHUNK), const),
            pl.BlockSpec((1, D_INNER), const),
            pl.BlockSpec((1, D_INNER), const),
        ],
        out_specs=pl.BlockSpec((rows, D_INNER), tok),
        out_shape=jax.ShapeDtypeStruct((t, D_INNER), BF16),
        scratch_shapes=[pltpu.VMEM((N_SSM_GROUPS, 512, D_STATE), F32), pltpu.VMEM((CHUNK, D_INNER), F32)],
        compiler_params=_params(("arbitrary", "arbitrary")),
        name="ssd_main",
    )(xc, proj, dt, hb, bias128, alog128, tri_l, tri_u, dskip, gain)


def _outproj_kernel(attn_a, ssm_a, gate_a, x_a, attn_b, ssm_b, gate_b, x_b, wa_ref, ws_ref, wo_ref, gn_ref,
                    wr1_ref, wr2_ref, br_ref, o_ref, r_ref, *, n_a):
    i = pl.program_id(0)

    @pl.when(i < n_a)
    def _():
        _outproj_tile(attn_a, ssm_a, gate_a, x_a, wa_ref, ws_ref, wo_ref, gn_ref, wr1_ref, wr2_ref, br_ref,
                      o_ref, r_ref)

    @pl.when(i >= n_a)
    def _():
        _outproj_tile(attn_b, ssm_b, gate_b, x_b, wa_ref, ws_ref, wo_ref, gn_ref, wr1_ref, wr2_ref, br_ref,
                      o_ref, r_ref)


def _outproj_tile(attn_ref, ssm_ref, gate_ref, x_ref, wa_ref, ws_ref, wo_ref, gn_ref, wr1_ref, wr2_ref, br_ref,
                  o_ref, r_ref):
    a_out = _dot(attn_ref[...], wa_ref[...])
    s_out = _dot(ssm_ref[...], ws_ref[...])
    ga = gate_ref[:, :D_MODEL].astype(F32)
    gs = gate_ref[:, D_MODEL:].astype(F32)
    merged = _sigmoid(ga) * a_out + _sigmoid(gs) * s_out
    x2 = x_ref[...] + _dot(merged.astype(BF16), wo_ref[...])
    for j in range(TOKEN_TILE):
        o_ref[pl.ds(j, TM_OUT, stride=TOKEN_TILE), :] = x2[:, j * LANES:(j + 1) * LANES]

    ms = jnp.mean(x2 * x2, axis=-1, keepdims=True)
    hn = x2 * lax.rsqrt(ms + EPS) * gn_ref[...]
    h1 = hn.astype(BF16)
    h2 = (hn - h1.astype(F32)).astype(BF16)
    lg = _dot(h1, wr1_ref[...]) + _dot(h2, wr1_ref[...]) + _dot(h1, wr2_ref[...]) + br_ref[...]

    lane = lax.broadcasted_iota(jnp.int32, (TM_OUT, LANES), 1).astype(F32)
    big = float(LANES)

    def rmax(v):
        return jnp.max(v, axis=-1, keepdims=True)

    def first_lane(mask):
        return jnp.min(jnp.where(mask, lane, big), axis=-1, keepdims=True)

    gl = jnp.where(lane < N_EXPERT_GROUPS, lg, NEG_INF)
    gmax = rmax(gl)
    g_w = 1.0 / jnp.sum(jnp.exp(gl - gmax), axis=-1, keepdims=True)
    gidx = first_lane(gl == gmax)
    base = N_EXPERT_GROUPS + EXPERTS_PER_GROUP * gidx
    el = jnp.where(lane >= base, jnp.where(lane < base + EXPERTS_PER_GROUP, lg, NEG_INF), NEG_INF)
    m1 = rmax(el)
    i1 = first_lane(el == m1)
    el2 = jnp.where(lane == i1, NEG_INF, el)
    m2 = rmax(el2)
    i2 = first_lane(el2 == m2)
    r = jnp.exp(m2 - m1)
    w1 = g_w / (1.0 + r)
    w2 = w1 * r
    j1 = i1 - base
    j2 = i2 - base
    swap = j1 > j2
    e_lo = jnp.where(swap, j2, j1)
    e_hi = jnp.where(swap, j1, j2)
    w_lo = jnp.where(swap, w2, w1)
    w_hi = jnp.where(swap, w1, w2)
    pair = e_lo * (EXPERTS_PER_GROUP - 1) - e_lo * (e_lo - 1.0) * 0.5 + (e_hi - e_lo - 1.0)
    cls = gidx * N_PAIRS + pair
    rows = jnp.where(lane == 0.0, cls, jnp.where(lane == 1.0, w_lo, jnp.where(lane == 2.0, w_hi, 0.0)))
    r_ref[...] = rows.T[:TOKEN_TILE, :]


def _outproj(group_a, group_b, wa, ws, wo, gn, wr1, wr2, br):
    n_a = group_a[3].shape[0] // TM_OUT
    n_b = group_b[3].shape[0] // TM_OUT
    t = (n_a + n_b) * TM_OUT

    def first(i):
        return (jnp.minimum(i, n_a - 1), 0)

    def second(i):
        return (jnp.maximum(i - n_a, 0), 0)

    def const(i):
        return (0, 0)

    def group_specs(tok):
        return [
            pl.BlockSpec((TM_OUT, ATTN_WIDTH), tok),
            pl.BlockSpec((TM_OUT, D_INNER), tok),
            pl.BlockSpec((TM_OUT, 2 * D_MODEL), lambda i: (tok(i)[0], COL_GATE // (2 * D_MODEL))),
            pl.BlockSpec((TM_OUT, D_MODEL), tok),
        ]

    resident = dict(pipeline_mode=pl.Buffered(1))
    return pl.pallas_call(
        functools.partial(_outproj_kernel, n_a=n_a),
        grid=(n_a + n_b,),
        in_specs=group_specs(first) + group_specs(second) + [
            pl.BlockSpec((ATTN_WIDTH, D_MODEL), const, **resident),
            pl.BlockSpec((D_INNER, D_MODEL), const, **resident),
            pl.BlockSpec((D_MODEL, D_MODEL), const, **resident),
            pl.BlockSpec((1, D_MODEL), const),
            pl.BlockSpec((D_MODEL, LANES), const),
            pl.BlockSpec((D_MODEL, LANES), const),
            pl.BlockSpec((1, LANES), const),
        ],
        out_specs=[
            pl.BlockSpec((TM_OUT * TOKEN_TILE, LANES), lambda i: (i, 0)),
            pl.BlockSpec((TOKEN_TILE, TM_OUT), lambda i: (0, i)),
        ],
        out_shape=[
            jax.ShapeDtypeStruct((t * TOKEN_TILE, LANES), F32),
            jax.ShapeDtypeStruct((TOKEN_TILE, t), F32),
        ],
        compiler_params=_params(("arbitrary",)),
        name="outproj_router",
    )(*group_a, *group_b, wa, ws, wo, gn, wr1, wr2, br)


def _moe_kernel(ea_ref, eb_ref, nv_ref, tokc_ref, tokn_ref, roww_ref, x_ref, gn_ref,
                wga_ref, wua_ref, wda_ref, wgb_ref, wub_ref, wdb_ref, o_ref,
                xg_ref, st_ref, gsem, ssem, *, n_blocks):
    i = pl.program_id(0)
    slot = i % 2
    other = 1 - slot

    def tile(idx):
        return pl.ds(pl.multiple_of(idx * TOKEN_TILE, TOKEN_TILE), TOKEN_TILE)

    def gather_copy(tok, r, s):
        return pltpu.make_async_copy(x_ref.at[tile(tok), :], xg_ref.at[s, tile(r), :], gsem.at[s])

    def scatter_copy(tok, r, s):
        return pltpu.make_async_copy(st_ref.at[s, tile(r), :], o_ref.at[tile(tok), :], ssem.at[s])

    def for_rows(n, fn):
        n8 = lax.shift_right_logical(n, 3)

        def body8(g, _):
            for u in range(8):
                fn(g * 8 + u)
            return 0

        def body1(r, _):
            fn(r)
            return 0

        lax.fori_loop(0, n8, body8, 0)
        lax.fori_loop(n8 * 8, n, body1, 0)

    def start_gathers(tok_ref, n, s):
        for_rows(n, lambda r: gather_copy(tok_ref[0, 0, r], r, s).start())

    def wait_gathers(n, s):
        for_rows(n, lambda r: gather_copy(0, 0, s).wait())

    def start_scatters(tok_ref, n, s):
        for_rows(n, lambda r: scatter_copy(tok_ref[0, 0, r], r, s).start())

    def wait_scatters(n, s):
        for_rows(n, lambda r: scatter_copy(0, 0, s).wait())

    @pl.when(i == 0)
    def _():
        xg_ref[...] = jnp.zeros_like(xg_ref)
        start_gathers(tokc_ref, nv_ref[0], 0)

    @pl.when(i + 1 < n_blocks)
    def _():
        start_gathers(tokn_ref, nv_ref[jnp.minimum(i + 1, n_blocks - 1)], other)

    wait_gathers(nv_ref[i], slot)

    @pl.when(i >= 2)
    def _():
        wait_scatters(nv_ref[jnp.maximum(i - 2, 0)], slot)

    @pl.when(nv_ref[i] > 0)
    def _():
        x = jnp.concatenate(
            [xg_ref[slot, pl.ds(j, ROW_BLOCK, stride=TOKEN_TILE), :] for j in range(TOKEN_TILE)], axis=1)
        w_cols = jnp.concatenate([roww_ref[0], jnp.zeros((LANES - TOKEN_TILE, ROW_BLOCK), F32)], axis=0).T
        w_lo = w_cols[:, 0:1]
        w_hi = w_cols[:, 1:2]
        ms = jnp.mean(x * x, axis=-1, keepdims=True)
        hn = (x * lax.rsqrt(ms + EPS) * gn_ref[...]).astype(BF16)

        def expert(wg_ref, wu_ref, wd_ref):
            gte = _dot(hn, wg_ref[0])
            up = _dot(hn, wu_ref[0])
            return _dot((gte * _sigmoid(gte) * up).astype(BF16), wd_ref[0])

        ya = expert(wga_ref, wua_ref, wda_ref)
        yb = expert(wgb_ref, wub_ref, wdb_ref)
        out = x + w_lo * ya + w_hi * yb
        for j in range(TOKEN_TILE):
            st_ref[slot, pl.ds(j, ROW_BLOCK, stride=TOKEN_TILE), :] = out[:, j * LANES:(j + 1) * LANES]

    start_scatters(tokc_ref, nv_ref[i], slot)

    @pl.when(i == n_blocks - 1)
    def _():
        wait_scatters(nv_ref[jnp.maximum(i - 1, 0)], other)
        wait_scatters(nv_ref[i], slot)


def _moe(ea, eb, nvalid, row_tok, row_w, x2t, gn, wg, wu, wd):
    n_blocks = row_tok.shape[0]

    def wa(i, ea, eb, nv):
        return (ea[i], 0, 0)

    def wb(i, ea, eb, nv):
        return (eb[i], 0, 0)

    any_spec = pl.BlockSpec(memory_space=pl.ANY)
    grid_spec = pltpu.PrefetchScalarGridSpec(
        num_scalar_prefetch=3,
        grid=(n_blocks,),
        in_specs=[
            pl.BlockSpec((1, 1, ROW_BLOCK), lambda i, ea, eb, nv: (i, 0, 0), memory_space=pltpu.SMEM),
            pl.BlockSpec((1, 1, ROW_BLOCK), lambda i, ea, eb, nv: (jnp.minimum(i + 1, n_blocks - 1), 0, 0),
                         memory_space=pltpu.SMEM),
            pl.BlockSpec((1, TOKEN_TILE, ROW_BLOCK), lambda i, ea, eb, nv: (i, 0, 0)),
            any_spec,
            pl.BlockSpec((1, D_MODEL), lambda i, ea, eb, nv: (0, 0)),
            pl.BlockSpec((1, D_MODEL, D_EXPERT), wa), pl.BlockSpec((1, D_MODEL, D_EXPERT), wa),
            pl.BlockSpec((1, D_EXPERT, D_MODEL), wa),
            pl.BlockSpec((1, D_MODEL, D_EXPERT), wb), pl.BlockSpec((1, D_MODEL, D_EXPERT), wb),
            pl.BlockSpec((1, D_EXPERT, D_MODEL), wb),
        ],
        out_specs=any_spec,
        scratch_shapes=[
            pltpu.VMEM((2, ROW_BLOCK * TOKEN_TILE, LANES), F32),
            pltpu.VMEM((2, ROW_BLOCK * TOKEN_TILE, LANES), F32),
            pltpu.SemaphoreType.DMA((2,)),
            pltpu.SemaphoreType.DMA((2,)),
        ],
    )
    return pl.pallas_call(
        functools.partial(_moe_kernel, n_blocks=n_blocks),
        grid_spec=grid_spec,
        out_shape=jax.ShapeDtypeStruct(x2t.shape, F32),
        compiler_params=_params(("arbitrary",)),
        name="moe",
    )(ea, eb, nvalid, row_tok, row_tok, row_w, x2t, gn, wg, wu, wd, wg, wu, wd)


def _untile_kernel(x_ref, o_ref):
    for j in range(TOKEN_TILE):
        o_ref[:, j * LANES:(j + 1) * LANES] = x_ref[pl.ds(j, TM_UNTILE, stride=TOKEN_TILE), :]


def _untile(y_tiles, first_token, n_tokens):
    first_block = first_token // TM_UNTILE
    return pl.pallas_call(
        _untile_kernel,
        grid=(n_tokens // TM_UNTILE,),
        in_specs=[pl.BlockSpec((TM_UNTILE * TOKEN_TILE, LANES), lambda i: (first_block + i, 0))],
        out_specs=pl.BlockSpec((TM_UNTILE, D_MODEL), lambda i: (i, 0)),
        out_shape=jax.ShapeDtypeStruct((n_tokens, D_MODEL), F32),
        compiler_params=_params(("arbitrary",)),
        name="untile",
    )(y_tiles)


def _pair_tables():
    lo, hi = [], []
    for a in range(EXPERTS_PER_GROUP):
        for b in range(a + 1, EXPERTS_PER_GROUP):
            lo.append(a)
            hi.append(b)
    return np.asarray(lo, np.int32), np.asarray(hi, np.int32)


def _block_tables(rinfo):
    cls = rinfo[0].astype(jnp.int32)
    t = cls.shape[0]
    n_blocks = t // ROW_BLOCK + N_CLASSES
    sorted_cls, order = lax.sort((cls, jnp.arange(t, dtype=jnp.int32)), num_keys=1)
    class_ids = jnp.arange(N_CLASSES + 1, dtype=jnp.int32)
    starts = jnp.sum((sorted_cls[:, None] < class_ids[None, :]).astype(jnp.int32), axis=0)
    counts = starts[1:] - starts[:-1]
    nblk = (counts + ROW_BLOCK - 1) // ROW_BLOCK
    blk_end = jnp.cumsum(nblk)
    blk_start = blk_end - nblk
    used = blk_end[-1]
    b = jnp.arange(n_blocks, dtype=jnp.int32)
    b_eff = jnp.minimum(b, used - 1)
    c = jnp.sum((blk_end[None, :] <= b_eff[:, None]).astype(jnp.int32), axis=1)
    c = jnp.minimum(c, N_CLASSES - 1)
    off = b_eff - blk_start[c]
    src = starts[c] + off * ROW_BLOCK
    nvalid = jnp.where(b < used, jnp.clip(counts[c] - off * ROW_BLOCK, 0, ROW_BLOCK), 0).astype(jnp.int32)
    pair_lo, pair_hi = _pair_tables()
    grp = c // N_PAIRS
    ea = (grp * EXPERTS_PER_GROUP + jnp.asarray(pair_lo)[c % N_PAIRS]).astype(jnp.int32)
    eb = (grp * EXPERTS_PER_GROUP + jnp.asarray(pair_hi)[c % N_PAIRS]).astype(jnp.int32)
    rows = jnp.clip(src[:, None] + jnp.arange(ROW_BLOCK, dtype=jnp.int32)[None, :], 0, t - 1)
    row_tok = order[rows]
    row_w = jnp.concatenate([rinfo[1][row_tok][:, None, :], rinfo[2][row_tok][:, None, :],
                             jnp.zeros((n_blocks, TOKEN_TILE - 2, ROW_BLOCK), F32)], axis=1)
    return ea, eb, nvalid, row_tok.reshape(n_blocks, 1, ROW_BLOCK), row_w


def _rope_tables(seq):
    inv = 1.0 / (ROPE_THETA ** (jnp.arange(0, HEAD_DIM, 2, dtype=F32) / HEAD_DIM))
    ang = jnp.arange(seq, dtype=F32)[:, None] * inv[None, :]
    cos, sin = jnp.cos(ang), jnp.sin(ang)
    cos128 = jnp.concatenate([cos, cos, cos, cos], axis=-1)
    sin128 = jnp.concatenate([-sin, sin, -sin, sin], axis=-1)
    return cos128, sin128


def _prepare_weights(norm_mix, w_in, q_norm, k_norm, attn_sink, conv_w, conv_b, a_log_fwd, a_log_bwd,
                     dt_bias_fwd, dt_bias_bwd, d_skip, ssm_norm, w_out_attn, w_out_ssm, w_o, norm_ffn,
                     w_router_group, b_router_group, w_router_expert, b_router_expert, w_gate, w_up, w_down):
    o_q = 0
    o_k = o_q + ATTN_WIDTH
    o_v = o_k + KV_WIDTH
    o_z = o_v + KV_WIDTH
    o_xbc = o_z + D_INNER
    o_dtf = o_xbc + CONV_DIM
    o_dtb = o_dtf + N_SSM_HEADS
    o_ga = o_dtb + N_SSM_HEADS
    o_gs = o_ga + D_MODEL
    w = w_in.astype(BF16)
    w_r = jnp.concatenate([
        w[:, o_z:o_z + D_INNER], w[:, o_ga:o_gs + D_MODEL], w[:, o_xbc:o_xbc + CONV_DIM],
        w[:, o_q:o_q + ATTN_WIDTH], w[:, o_k:o_k + KV_WIDTH], w[:, o_v:o_v + KV_WIDTH],
        w[:, o_dtf:o_dtb + N_SSM_HEADS], jnp.zeros((D_MODEL, LANES - 2 * N_SSM_HEADS), w.dtype)], axis=1)
    pad64 = jnp.zeros((LANES - 2 * N_SSM_HEADS,), F32)
    eye = np.kron(np.eye(2, dtype=np.float32), np.ones((HEAD_DIM, HEAD_DIM), np.float32))
    idx = np.arange(CHUNK)
    w_router = jnp.concatenate([w_router_group, w_router_expert,
                                jnp.zeros((D_MODEL, LANES - N_EXPERT_GROUPS - N_EXPERTS), F32)], axis=1)
    wr1 = w_router.astype(BF16)
    return dict(
        norm_mix=norm_mix.reshape(1, D_MODEL),
        w_in=w_r,
        qg128=jnp.tile(q_norm, 2).reshape(1, LANES),
        kg128=jnp.tile(k_norm, 2).reshape(1, LANES),
        seg=jnp.asarray(eye, BF16),
        sink=attn_sink.astype(F32),
        conv_w=conv_w,
        conv_b=conv_b.reshape(1, CONV_DIM),
        alog128=jnp.concatenate([a_log_fwd, a_log_bwd, pad64]).reshape(1, LANES),
        bias128=jnp.concatenate([dt_bias_fwd, dt_bias_bwd, pad64]).reshape(1, LANES),
        tri_l=jnp.asarray(idx[:, None] >= idx[None, :], BF16),
        tri_u=jnp.asarray(idx[:, None] <= idx[None, :], BF16),
        dskip=jnp.repeat(d_skip, SSM_HEAD_DIM).reshape(1, D_INNER),
        ssm_norm=ssm_norm.reshape(1, D_INNER),
        wa=w_out_attn.astype(BF16), ws=w_out_ssm.astype(BF16), wo=w_o.astype(BF16),
        norm_ffn=norm_ffn.reshape(1, D_MODEL),
        wr1=wr1, wr2=(w_router - wr1.astype(F32)).astype(BF16),
        br=jnp.concatenate([b_router_group, b_router_expert,
                            jnp.zeros((LANES - N_EXPERT_GROUPS - N_EXPERTS,), F32)]).reshape(1, LANES),
        wg=w_gate.astype(BF16), wu=w_up.astype(BF16), wd=w_down.astype(BF16),
    )


def _mixer(x, p):
    batch, seq, _ = x.shape
    x2d = x.reshape(batch * seq, D_MODEL)
    cos128, sin128 = _rope_tables(seq)
    proj, dt, qr, kdup, vdup = _inproj(x2d, p['norm_mix'], p['w_in'], cos128, sin128, p['qg128'], p['kg128'],
                                       p['seg'], seq)
    attn = _attention(qr, kdup, vdup, p['sink'], batch, seq)
    xc = _conv(proj, p['conv_w'], p['conv_b'], batch, seq)
    hb = _ssd_bwd_states(xc, dt, p['bias128'], p['alog128'], p['tri_l'], batch, seq)
    ssm = _ssd_main(xc, proj, dt, hb, p['bias128'], p['alog128'], p['tri_l'], p['tri_u'], p['dskip'],
                    p['ssm_norm'], batch, seq)
    return attn, ssm, proj, x2d


def kernel(x_prompt, x_sample, norm_mix, w_in, q_norm, k_norm, attn_sink, conv_w, conv_b, a_log_fwd, a_log_bwd,
           dt_bias_fwd, dt_bias_bwd, d_skip, ssm_norm, w_out_attn, w_out_ssm, w_o, norm_ffn, w_router_group,
           b_router_group, w_router_expert, b_router_expert, w_gate, w_up, w_down):
    assert norm_mix.shape[0] == 1, "single-layer encoder"
    p = _prepare_weights(norm_mix[0], w_in[0], q_norm[0], k_norm[0], attn_sink[0], conv_w[0], conv_b[0],
                         a_log_fwd[0], a_log_bwd[0], dt_bias_fwd[0], dt_bias_bwd[0], d_skip[0], ssm_norm[0],
                         w_out_attn[0], w_out_ssm[0], w_o[0], norm_ffn[0], w_router_group[0], b_router_group[0],
                         w_router_expert[0], b_router_expert[0], w_gate[0], w_up[0], w_down[0])
    x2t, rinfo = _outproj(_mixer(x_prompt, p), _mixer(x_sample, p), p['wa'], p['ws'], p['wo'], p['norm_ffn'],
                          p['wr1'], p['wr2'], p['br'])
    ea, eb, nvalid, row_tok, row_w = _block_tables(rinfo)
    y = _moe(ea, eb, nvalid, row_tok, row_w, x2t, p['norm_ffn'], p['wg'], p['wu'], p['wd'])
    t_a = x_prompt.shape[0] * x_prompt.shape[1]
    t_b = x_sample.shape[0] * x_sample.shape[1]
    return _untile(y, 0, t_a).reshape(x_prompt.shape), _untile(y, t_a, t_b).reshape(x_sample.shape)
```

```python
import functools

import numpy as np
import jax
import jax.numpy as jnp
from jax import lax
from jax.experimental import pallas as pl
from jax.experimental.pallas import tpu as pltpu

F32 = jnp.float32
BF16 = jnp.bfloat16

D_MODEL = 1024
EPS = 1e-6
NEG_INF = -1e30
LOG2_E = 1.4426950408889634
N_Q_HEADS = 16
N_KV_HEADS = 4
HEAD_DIM = 64
ATTN_WIDTH = N_Q_HEADS * HEAD_DIM
KV_WIDTH = N_KV_HEADS * HEAD_DIM
ATTN_BLOCK = 128
ATTN_QB = 2
ROPE_THETA = 10000.0
D_INNER = 2 * D_MODEL
SSM_HEAD_DIM = 64
N_SSM_HEADS = D_INNER // SSM_HEAD_DIM
N_SSM_GROUPS = 4
HEADS_PER_GROUP = N_SSM_HEADS // N_SSM_GROUPS
D_STATE = 128
BC_WIDTH = N_SSM_GROUPS * D_STATE
CONV_DIM = D_INNER + 2 * BC_WIDTH
CONV_W = 7
CHUNK = 128
SSD_BWD_CHUNKS = 4
SSD_MAIN_CHUNKS = 2
N_EXPERT_GROUPS = 4
EXPERTS_PER_GROUP = 8
N_EXPERTS = N_EXPERT_GROUPS * EXPERTS_PER_GROUP
D_EXPERT = 512
N_PAIRS = EXPERTS_PER_GROUP * (EXPERTS_PER_GROUP - 1) // 2
N_CLASSES = N_EXPERT_GROUPS * N_PAIRS

LANES = 128
V7X_VMEM_LIMIT_BYTES = 56 * 1024 * 1024

COL_XS = 0
COL_Z = COL_XS + D_INNER
COL_GATE = COL_Z + D_INNER
COL_B = COL_GATE + 2 * D_MODEL
COL_C = COL_B + BC_WIDTH
COL_Q = COL_C + BC_WIDTH
COL_K = COL_Q + ATTN_WIDTH
COL_V = COL_K + KV_WIDTH
COL_DT = COL_V + KV_WIDTH
N_PROJ = COL_DT + LANES

TM_IN = 512
NJ_IN = 3
TN_IN = N_PROJ // NJ_IN
CH_IN = 512
TM_OUT = 512
TM_UNTILE = 1024
CONV_ROWS = 256
CONV_HALO = 8
CONV_PITCH = 2
ROW_BLOCK = 128
TOKEN_TILE = D_MODEL // LANES


def _params(sem, flags=None):
    return pltpu.CompilerParams(dimension_semantics=sem, vmem_limit_bytes=V7X_VMEM_LIMIT_BYTES, flags=flags)


def _dot(a, b):
    return jnp.dot(a, b, preferred_element_type=F32)


def _dot_nt(a, b):
    return lax.dot_general(a, b, (((1,), (1,)), ((), ())), preferred_element_type=F32)


def _sigmoid(x):
    return 1.0 / (1.0 + jnp.exp(-x))


def _column_chunks(tile):
    lo, hi = tile * TN_IN, (tile + 1) * TN_IN
    segments = ((COL_XS, COL_Z, 'conv', 0), (COL_Z, COL_B, 'plain', 0), (COL_B, COL_Q, 'conv', D_INNER),
                (COL_Q, COL_K, 'q', 0), (COL_K, COL_DT, 'kv', 0), (COL_DT, N_PROJ, 'dt', 0))
    chunks = []
    for a, b, kind, dest in segments:
        c = max(a, lo)
        while c < min(b, hi):
            e = min(c + CH_IN, b, hi)
            chunks.append((c - lo, e - lo, kind, dest + c - a))
            c = e
    heavy = [c for c in chunks if c[2] != 'plain']
    plain = [c for c in chunks if c[2] == 'plain']
    mixed = []
    while heavy or plain:
        mixed += heavy[:1] + plain[:1]
        heavy, plain = heavy[1:], plain[1:]
    return mixed


def _inproj_kernel(x_ref, xp_ref, xn_ref, g_ref, w_ref, cw_ref, cb_ref, cos_ref, sin_ref, qg_ref, kg_ref, seg_ref,
                   o_ref, dt_ref, qo_ref, ko_ref, vo_ref, xc_ref, h_ref, pad_ref, *, nseq):
    i = pl.program_id(0)
    j = pl.program_id(1)
    halo = CONV_HALO

    def normed(x):
        ms = jnp.mean(x * x, axis=-1, keepdims=True)
        return x * lax.rsqrt(ms + EPS) * g_ref[...]

    @pl.when(j == 0)
    def _():
        h_ref[:TM_IN, :] = normed(x_ref[...]).astype(BF16)
        first = (i % nseq) == 0
        last = (i % nseq) == nseq - 1
        before = jnp.where(first, 0.0, normed(xp_ref[...]))
        after = jnp.where(last, 0.0, normed(xn_ref[...]))
        h_ref[TM_IN:, :] = jnp.concatenate([before, after], axis=0).astype(BF16)

    def project(c0, c1):
        acc = _dot(h_ref[:TM_IN, :], w_ref[:, c0:c1])
        o_ref[:, c0:c1] = acc.astype(BF16)
        return acc

    def pitch_rows(first, n):
        return pl.ds(CONV_PITCH * (first + halo), n, stride=CONV_PITCH)

    def conv_stage(c0, c1, parity):
        both = _dot(h_ref[...], w_ref[:, c0:c1])
        acc = both[:TM_IN]
        edge = both[TM_IN:]
        o_ref[:, c0:c1] = acc.astype(BF16)
        for s in range((c1 - c0) // LANES):
            lanes = slice(s * LANES, (s + 1) * LANES)
            stage = pad_ref.at[parity, s]
            stage[pitch_rows(-halo, halo), :] = edge[:halo, lanes]
            stage[pitch_rows(0, TM_IN), :] = acc[:, lanes]
            stage[pitch_rows(TM_IN, halo), :] = edge[halo:, lanes]

    def conv_taps(c0, c1, channel0, parity):
        for s in range((c1 - c0) // LANES):
            ch = slice(channel0 + s * LANES, channel0 + (s + 1) * LANES)
            stage = pad_ref.at[parity, s]
            for r0 in range(0, TM_IN, CONV_ROWS):
                out = jnp.broadcast_to(cb_ref[:, ch], (CONV_ROWS, LANES))
                for k in range(CONV_W):
                    out = out + stage[pitch_rows(r0 + k - CONV_W // 2, CONV_ROWS), :] * cw_ref[k:k + 1, ch]
                xc_ref[r0:r0 + CONV_ROWS, ch] = (out * _sigmoid(out)).astype(BF16)

    def norm_rope(x, gain):
        lane = lax.broadcasted_iota(jnp.int32, (TM_IN, LANES), 1)
        first_half = (lane % HEAD_DIM) < (HEAD_DIM // 2)
        ss = _dot((x * x).astype(BF16), seg_ref[...])
        xn = x * lax.rsqrt(ss * (1.0 / HEAD_DIM) + EPS) * gain
        rot = jnp.where(first_half, pltpu.roll(xn, 96, 1), pltpu.roll(xn, 32, 1))
        return xn * cos_ref[...] + rot * sin_ref[...]

    def duplicate(y, dst_ref, s):
        low = lax.broadcasted_iota(jnp.int32, (TM_IN, LANES), 1) < HEAD_DIM
        ysw = pltpu.roll(y, HEAD_DIM, 1)
        dst_ref[:, (2 * s) * LANES:(2 * s + 1) * LANES] = jnp.where(low, y, ysw).astype(BF16)
        dst_ref[:, (2 * s + 1) * LANES:(2 * s + 2) * LANES] = jnp.where(low, ysw, y).astype(BF16)

    def q_epilogue(acc, col0):
        for s in range(acc.shape[1] // LANES):
            y = norm_rope(acc[:, s * LANES:(s + 1) * LANES], qg_ref[...]) * (HEAD_DIM ** -0.5 * LOG2_E)
            qo_ref[:, col0 + s * LANES:col0 + (s + 1) * LANES] = y.astype(BF16)

    def kv_epilogue(acc):
        for s in range(KV_WIDTH // LANES):
            duplicate(norm_rope(acc[:, s * LANES:(s + 1) * LANES], kg_ref[...]), ko_ref, s)
            duplicate(acc[:, KV_WIDTH + s * LANES:KV_WIDTH + (s + 1) * LANES], vo_ref, s)

    def matmul_phase(n, chunk):
        c0, c1, kind, dest = chunk
        if kind == 'conv':
            return conv_stage(c0, c1, n % 2)
        acc = project(c0, c1)
        if kind == 'dt':
            dt_ref[...] = acc
        return acc

    def epilogue_phase(n, chunk, acc):
        c0, c1, kind, dest = chunk
        if kind == 'conv':
            conv_taps(c0, c1, dest, n % 2)
        elif kind == 'q':
            q_epilogue(acc, dest)
        elif kind == 'kv':
            kv_epilogue(acc)

    for tile in range(NJ_IN):
        @pl.when(j == tile)
        def _(tile=tile):
            chunks = _column_chunks(tile)
            pending = matmul_phase(0, chunks[0])
            for n, chunk in enumerate(chunks):
                following = matmul_phase(n + 1, chunks[n + 1]) if n + 1 < len(chunks) else None
                epilogue_phase(n, chunk, pending)
                pending = following


def _inproj(x2d, gain, w_bf16, conv_w, conv_b, cos128, sin128, qg128, kg128, seg, seq):
    t = x2d.shape[0]
    nseq = seq // TM_IN
    per_tile = TM_IN // CONV_HALO
    n_edge = t // CONV_HALO

    def rows(i, j):
        return (i, 0)

    def const(i, j):
        return (0, 0)

    def pos(i, j):
        return (i % nseq, 0)

    return pl.pallas_call(
        functools.partial(_inproj_kernel, nseq=nseq),
        grid=(t // TM_IN, NJ_IN),
        in_specs=[
            pl.BlockSpec((TM_IN, D_MODEL), rows),
            pl.BlockSpec((CONV_HALO, D_MODEL), lambda i, j: (jnp.maximum(i * per_tile - 1, 0), 0)),
            pl.BlockSpec((CONV_HALO, D_MODEL), lambda i, j: (jnp.minimum((i + 1) * per_tile, n_edge - 1), 0)),
            pl.BlockSpec((1, D_MODEL), const),
            pl.BlockSpec((D_MODEL, TN_IN), lambda i, j: (0, j)),
            pl.BlockSpec((CONV_W, CONV_DIM), const),
            pl.BlockSpec((1, CONV_DIM), const),
            pl.BlockSpec((TM_IN, LANES), pos),
            pl.BlockSpec((TM_IN, LANES), pos),
            pl.BlockSpec((1, LANES), const),
            pl.BlockSpec((1, LANES), const),
            pl.BlockSpec((LANES, LANES), const),
        ],
        out_specs=[
            pl.BlockSpec((TM_IN, TN_IN), lambda i, j: (i, j)),
            pl.BlockSpec((TM_IN, LANES), rows),
            pl.BlockSpec((TM_IN, ATTN_WIDTH), rows),
            pl.BlockSpec((TM_IN, 2 * KV_WIDTH), rows),
            pl.BlockSpec((TM_IN, 2 * KV_WIDTH), rows),
            pl.BlockSpec((TM_IN, CONV_DIM), rows),
        ],
        out_shape=[
            jax.ShapeDtypeStruct((t, N_PROJ), BF16),
            jax.ShapeDtypeStruct((t, LANES), F32),
            jax.ShapeDtypeStruct((t, ATTN_WIDTH), BF16),
            jax.ShapeDtypeStruct((t, 2 * KV_WIDTH), BF16),
            jax.ShapeDtypeStruct((t, 2 * KV_WIDTH), BF16),
            jax.ShapeDtypeStruct((t, CONV_DIM), BF16),
        ],
        scratch_shapes=[
            pltpu.VMEM((TM_IN + 2 * CONV_HALO, D_MODEL), BF16),
            pltpu.VMEM((2, CH_IN // LANES, CONV_PITCH * (TM_IN + 2 * CONV_HALO), LANES), F32),
        ],
        compiler_params=_params(("arbitrary", "arbitrary")),
        name="inproj",
    )(x2d, x2d, x2d, gain, w_bf16, conv_w, conv_b, cos128, sin128, qg128, kg128, seg)


def _attn_kernel(sink_ref, q_ref, kp_ref, kc_ref, kn_ref, vp_ref, vc_ref, vn_ref, o_ref, *, n_steps):
    i = pl.program_id(1)
    nb = ATTN_BLOCK
    nk = 3 * nb
    qi = lax.broadcasted_iota(jnp.int32, (nb, nk), 0)
    si = lax.broadcasted_iota(jnp.int32, (nb, nk), 1)
    rel = qi - (si - nb)
    band = jnp.where(rel <= nb, jnp.where(rel >= -nb, 0.0, NEG_INF), NEG_INF)
    bias_first = jnp.where(si < nb, jnp.where(i > 0, band, NEG_INF), band)
    bias_last = jnp.where(si >= 2 * nb, jnp.where(i < n_steps - 1, band, NEG_INF), band)
    low_q = lax.broadcasted_iota(jnp.int32, (nb, LANES), 1) < HEAD_DIM
    low_k = lax.broadcasted_iota(jnp.int32, (nk, LANES), 1) < HEAD_DIM
    zero_q = jnp.zeros((nb, LANES), BF16)
    zero_k = jnp.zeros((nk, LANES), BF16)

    def window(p_ref, c_ref, n_ref, h, j):
        sl = slice(h * LANES, (h + 1) * LANES)
        rows = jnp.concatenate([p_ref[:, sl], c_ref[:, sl], n_ref[:, sl]], axis=0)
        return rows[j * nb:j * nb + nk]

    def scores(j, h):
        kd = window(kp_ref, kc_ref, kn_ref, h, j)
        slabs = [q_ref[j * nb:(j + 1) * nb, (2 * h + u) * LANES:(2 * h + u + 1) * LANES] for u in range(2)]
        q4 = jnp.concatenate([jnp.where(low_q, s_, zero_q) for s_ in slabs]
                             + [jnp.where(low_q, zero_q, s_) for s_ in slabs], axis=0)
        return _dot_nt(q4, kd)

    def finish(j, h, s4):
        vd = window(vp_ref, vc_ref, vn_ref, h, j)
        v_lo = jnp.where(low_k, vd, zero_k)
        v_hi = jnp.where(low_k, zero_k, vd)
        heads = (4 * h, 4 * h + 2, 4 * h + 1, 4 * h + 3)
        ps, invs = [], []
        for k, head in enumerate(heads):
            s = s4[k * nb:(k + 1) * nb]
            left = s[:, :nb] + (bias_first if j == 0 else band)[:, :nb]
            right = s[:, 2 * nb:] + (bias_last if j == ATTN_QB - 1 else band)[:, 2 * nb:]
            s = jnp.concatenate([left, s[:, nb:2 * nb], right], axis=1)
            snk = sink_ref[head] * LOG2_E
            m = jnp.maximum(jnp.max(s, axis=-1, keepdims=True), snk)
            p = jnp.exp2(s - m)
            den = jnp.sum(p, axis=-1, keepdims=True) + jnp.exp2(snk - m)
            ps.append(p.astype(BF16))
            invs.append(1.0 / den)
        o = _dot(jnp.concatenate(ps[:2], axis=0), v_lo) + _dot(jnp.concatenate(ps[2:], axis=0), v_hi)
        for u in range(2):
            ou = o[u * nb:(u + 1) * nb] * jnp.where(low_q, invs[u], invs[2 + u])
            o_ref[j * nb:(j + 1) * nb, (2 * h + u) * LANES:(2 * h + u + 1) * LANES] = ou.astype(BF16)

    tasks = [(j, h) for j in range(ATTN_QB) for h in range(N_KV_HEADS)]
    pending = scores(*tasks[0])
    for n, task in enumerate(tasks):
        following = scores(*tasks[n + 1]) if n + 1 < len(tasks) else None
        finish(*task, pending)
        pending = following


def _attention(qr, kdup, vdup, sink, batch, seq):
    t = qr.shape[0]
    nq = seq // ATTN_BLOCK
    n_steps = nq // ATTN_QB
    rows = ATTN_QB * ATTN_BLOCK

    def prev(b, i, s):
        return (b * nq + jnp.maximum(i * ATTN_QB - 1, 0), 0)

    def cur(b, i, s):
        return (b * n_steps + i, 0)

    def nxt(b, i, s):
        return (b * nq + jnp.minimum((i + 1) * ATTN_QB, nq - 1), 0)

    edge = (ATTN_BLOCK, 2 * KV_WIDTH)
    mid = (rows, 2 * KV_WIDTH)
    grid_spec = pltpu.PrefetchScalarGridSpec(
        num_scalar_prefetch=1,
        grid=(batch, n_steps),
        in_specs=[
            pl.BlockSpec((rows, ATTN_WIDTH), cur),
            pl.BlockSpec(edge, prev), pl.BlockSpec(mid, cur), pl.BlockSpec(edge, nxt),
            pl.BlockSpec(edge, prev), pl.BlockSpec(mid, cur), pl.BlockSpec(edge, nxt),
        ],
        out_specs=pl.BlockSpec((rows, ATTN_WIDTH), cur),
    )
    return pl.pallas_call(
        functools.partial(_attn_kernel, n_steps=n_steps),
        grid_spec=grid_spec,
        out_shape=jax.ShapeDtypeStruct((t, ATTN_WIDTH), BF16),
        compiler_params=_params(("arbitrary", "arbitrary")),
        name="attention",
    )(sink, qr, kdup, kdup, kdup, vdup, vdup, vdup)


def _split3(a):
    a1 = a.astype(BF16)
    r1 = a - a1.astype(F32)
    a2 = r1.astype(BF16)
    a3 = (r1 - a2.astype(F32)).astype(BF16)
    return a1, a2, a3


def _tri_matmul(tri, a):
    a1, a2, a3 = _split3(a)
    return _dot(tri, a1) + _dot(tri, a2) + _dot(tri, a3)


def _softplus(x):
    return jnp.maximum(x, 0.0) + jnp.log(1.0 + jnp.exp(-jnp.abs(x)))


def _dt_and_rate(dt_ref, bias_ref, alog_ref):
    dt = _softplus(dt_ref[...] + bias_ref[...])
    rate = dt * (-LOG2_E * jnp.exp(alog_ref[...]))
    return dt, rate


def _head_rows(mat, first, rows):
    n = mat.shape[1]
    return jnp.concatenate(
        [jnp.broadcast_to(mat[first + e:first + e + 1, :], (rows, n)) for e in range(HEADS_PER_GROUP)], axis=0)


def _ssd_bwd_state_kernel(xs_ref, b_ref, dt_ref, bias_ref, alog_ref, tl_ref, hb_ref, st_ref):
    c = pl.program_id(1)

    @pl.when(c == 0)
    def _():
        st_ref[...] = jnp.zeros_like(st_ref)

    for ci in reversed(range(SSD_BWD_CHUNKS)):
        rows = pl.ds(ci * CHUNK, CHUNK)
        _ssd_bwd_chunk(xs_ref.at[rows, :], b_ref.at[rows, :], dt_ref.at[rows, :], bias_ref, alog_ref, tl_ref,
                       hb_ref.at[ci], st_ref)


def _ssd_bwd_chunk(xs_ref, b_ref, dt_ref, bias_ref, alog_ref, tl_ref, hb_ref, st_ref):
    hb_ref[...] = st_ref[...].astype(BF16)
    dt, rate = _dt_and_rate(dt_ref, bias_ref, alog_ref)
    pre = _tri_matmul(tl_ref[...], rate)
    pre_t = pre.T
    excl_t = (pre - rate).T
    total = jnp.broadcast_to(pre_t[:, CHUNK - 1:CHUNK], (LANES, CHUNK))
    w_t = dt.T * jnp.exp2(excl_t)
    dec = jnp.exp2(total)
    off = N_SSM_HEADS
    for g in range(N_SSM_GROUPS):
        xs_t = xs_ref[:, g * 512:(g + 1) * 512].astype(F32).T
        xd = (xs_t * _head_rows(w_t, off + g * HEADS_PER_GROUP, SSM_HEAD_DIM)).astype(BF16)
        upd = _dot(xd, b_ref[:, g * D_STATE:(g + 1) * D_STATE])
        st_ref[g] = _head_rows(dec, off + g * HEADS_PER_GROUP, SSM_HEAD_DIM) * st_ref[g] + upd


def _ssd_bwd_states(xc, dt, bias128, alog128, tri_l, batch, seq):
    nc = seq // CHUNK
    ns = nc // SSD_BWD_CHUNKS
    rows = SSD_BWD_CHUNKS * CHUNK

    def rev(b, c):
        return (b * ns + ns - 1 - c, 0)

    return pl.pallas_call(
        _ssd_bwd_state_kernel,
        grid=(batch, ns),
        in_specs=[
            pl.BlockSpec((rows, D_INNER), rev),
            pl.BlockSpec((rows, BC_WIDTH), lambda b, c: (b * ns + ns - 1 - c, D_INNER // BC_WIDTH)),
            pl.BlockSpec((rows, LANES), rev),
            pl.BlockSpec((1, LANES), lambda b, c: (0, 0)),
            pl.BlockSpec((1, LANES), lambda b, c: (0, 0)),
            pl.BlockSpec((CHUNK, CHUNK), lambda b, c: (0, 0)),
        ],
        out_specs=pl.BlockSpec((SSD_BWD_CHUNKS, N_SSM_GROUPS, 512, D_STATE),
                               lambda b, c: (b * ns + ns - 1 - c, 0, 0, 0)),
        out_shape=jax.ShapeDtypeStruct((batch * nc, N_SSM_GROUPS, 512, D_STATE), BF16),
        scratch_shapes=[pltpu.VMEM((N_SSM_GROUPS, 512, D_STATE), F32)],
        compiler_params=_params(("arbitrary", "arbitrary")),
        name="ssd_bwd_states",
    )(xc, xc, dt, bias128, alog128, tri_l)


def _ssd_main_kernel(xc_ref, z_ref, dt_ref, hb_ref, bias_ref, alog_ref, tl_ref, tu_ref, dskip_ref, gain_ref,
                     o_ref, hf_ref, y_ref):
    c = pl.program_id(1)

    @pl.when(c == 0)
    def _():
        hf_ref[...] = jnp.zeros_like(hf_ref)

    for ci in range(SSD_MAIN_CHUNKS):
        rows = pl.ds(ci * CHUNK, CHUNK)
        _ssd_main_chunk(xc_ref.at[rows, :], z_ref.at[rows, :], dt_ref.at[rows, :], hb_ref.at[ci], bias_ref, alog_ref,
                        tl_ref, tu_ref, dskip_ref, gain_ref, o_ref.at[rows, :], hf_ref, y_ref)


def _ssd_main_chunk(xc_ref, z_ref, dt_ref, hb_ref, bias_ref, alog_ref, tl_ref, tu_ref, dskip_ref, gain_ref,
                    o_ref, hf_ref, y_ref):
    dt, rate = _dt_and_rate(dt_ref, bias_ref, alog_ref)
    lane = lax.broadcasted_iota(jnp.int32, (CHUNK, LANES), 1)
    cum = jnp.where(lane < N_SSM_HEADS, _tri_matmul(tl_ref[...], rate), _tri_matmul(tu_ref[...], rate))
    cum_t = cum.T
    dt_t = dt.T
    src_t = cum_t - jnp.log2(dt_t)
    row = lax.broadcasted_iota(jnp.int32, (CHUNK, CHUNK), 0)
    col = lax.broadcasted_iota(jnp.int32, (CHUNK, CHUNK), 1)
    lower = row >= col
    diag = row == col
    low = lane < SSM_HEAD_DIM
    zero_x = jnp.zeros((CHUNK, LANES), BF16)
    nb = N_SSM_HEADS

    def lane_bcast(mat, idx):
        return jnp.broadcast_to(mat[:, idx:idx + 1], (CHUNK, CHUNK))

    def sub_bcast(mat, idx):
        return jnp.broadcast_to(mat[idx:idx + 1, :], (CHUNK, CHUNK))

    def head_matrix(e, cb):
        col_f = lane_bcast(cum, e)
        col_b = lane_bcast(cum, nb + e)
        decay = jnp.exp2(jnp.where(lower, col_f - sub_bcast(src_t, e), col_b - sub_bcast(src_t, nb + e)))
        decay = decay + jnp.where(diag, sub_bcast(dt_t, nb + e), 0.0)
        return (decay * cb).astype(BF16), col_f, col_b

    for g in range(N_SSM_GROUPS):
        bg = xc_ref[:, D_INNER + g * D_STATE:D_INNER + (g + 1) * D_STATE]
        cg = xc_ref[:, D_INNER + BC_WIDTH + g * D_STATE:D_INNER + BC_WIDTH + (g + 1) * D_STATE]
        cb = _dot_nt(cg, bg)
        y_in_f = _dot_nt(cg, hf_ref[g].astype(BF16))
        y_in_b = _dot_nt(cg, hb_ref[g])
        for jp in range(HEADS_PER_GROUP // 2):
            e0 = g * HEADS_PER_GROUP + 2 * jp
            cols = slice(e0 * SSM_HEAD_DIM, e0 * SSM_HEAD_DIM + LANES)
            loc = slice(jp * LANES, (jp + 1) * LANES)
            xs_pair = xc_ref[:, cols]
            m0, cf0, cb0 = head_matrix(e0, cb)
            m1, cf1, cb1 = head_matrix(e0 + 1, cb)
            y = _dot(m0, jnp.where(low, xs_pair, zero_x)) + _dot(m1, jnp.where(low, zero_x, xs_pair))
            y = y + y_in_f[:, loc] * jnp.exp2(jnp.where(low, cf0, cf1))
            y = y + y_in_b[:, loc] * jnp.exp2(jnp.where(low, cb0, cb1))
            y_ref[:, cols] = y + dskip_ref[:, cols] * xs_pair.astype(F32)

    z = z_ref[...].astype(F32)
    y = y_ref[...] * (z * _sigmoid(z))
    ms = jnp.mean(y * y, axis=-1, keepdims=True)
    o_ref[...] = (y * lax.rsqrt(ms + EPS) * gain_ref[...]).astype(BF16)

    last = jnp.broadcast_to(cum_t[:, CHUNK - 1:CHUNK], (LANES, CHUNK))
    w_t = jnp.exp2(last - src_t)
    dec = jnp.exp2(last)
    for g in range(N_SSM_GROUPS):
        xs_t = xc_ref[:, g * 512:(g + 1) * 512].astype(F32).T
        xd = (xs_t * _head_rows(w_t, g * HEADS_PER_GROUP, SSM_HEAD_DIM)).astype(BF16)
        upd = _dot(xd, xc_ref[:, D_INNER + g * D_STATE:D_INNER + (g + 1) * D_STATE])
        hf_ref[g] = _head_rows(dec, g * HEADS_PER_GROUP, SSM_HEAD_DIM) * hf_ref[g] + upd


def _ssd_main(xc, proj, dt, hb, bias128, alog128, tri_l, tri_u, dskip, gain, batch, seq):
    t = xc.shape[0]
    ns = seq // CHUNK // SSD_MAIN_CHUNKS
    rows = SSD_MAIN_CHUNKS * CHUNK

    def tok(b, c):
        return (b * ns + c, 0)

    def const(b, c):
        return (0, 0)

    return pl.pallas_call(
        _ssd_main_kernel,
        grid=(batch, ns),
        in_specs=[
            pl.BlockSpec((rows, CONV_DIM), tok),
            pl.BlockSpec((rows, D_INNER), lambda b, c: (b * ns + c, COL_Z // D_INNER)),
            pl.BlockSpec((rows, LANES), tok),
            pl.BlockSpec((SSD_MAIN_CHUNKS, N_SSM_GROUPS, 512, D_STATE), lambda b, c: (b * ns + c, 0, 0, 0)),
            pl.BlockSpec((1, LANES), const),
            pl.BlockSpec((1, LANES), const),
            pl.BlockSpec((CHUNK, CHUNK), const),
            pl.BlockSpec((CHUNK, CHUNK), const),
            pl.BlockSpec((1, D_INNER), const),
            pl.BlockSpec((1, D_INNER), const),
        ],
        out_specs=pl.BlockSpec((rows, D_INNER), tok),
        out_shape=jax.ShapeDtypeStruct((t, D_INNER), BF16),
        scratch_shapes=[pltpu.VMEM((N_SSM_GROUPS, 512, D_STATE), F32), pltpu.VMEM((CHUNK, D_INNER), F32)],
        compiler_params=_params(("arbitrary", "arbitrary")),
        name="ssd_main",
    )(xc, proj, dt, hb, bias128, alog128, tri_l, tri_u, dskip, gain)


def _outproj_kernel(attn_a, ssm_a, gate_a, x_a, attn_b, ssm_b, gate_b, x_b, wa_ref, ws_ref, wo_ref, gn_ref,
                    wr1_ref, wr2_ref, br_ref, o_ref, r_ref, *, n_a):
    i = pl.program_id(0)

    @pl.when(i < n_a)
    def _():
        _outproj_tile(attn_a, ssm_a, gate_a, x_a, wa_ref, ws_ref, wo_ref, gn_ref, wr1_ref, wr2_ref, br_ref,
                      o_ref, r_ref)

    @pl.when(i >= n_a)
    def _():
        _outproj_tile(attn_b, ssm_b, gate_b, x_b, wa_ref, ws_ref, wo_ref, gn_ref, wr1_ref, wr2_ref, br_ref,
                      o_ref, r_ref)


def _outproj_tile(attn_ref, ssm_ref, gate_ref, x_ref, wa_ref, ws_ref, wo_ref, gn_ref, wr1_ref, wr2_ref, br_ref,
                  o_ref, r_ref):
    a_out = _dot(attn_ref[...], wa_ref[...])
    s_out = _dot(ssm_ref[...], ws_ref[...])
    ga = gate_ref[:, :D_MODEL].astype(F32)
    gs = gate_ref[:, D_MODEL:].astype(F32)
    merged = _sigmoid(ga) * a_out + _sigmoid(gs) * s_out
    x2 = x_ref[...] + _dot(merged.astype(BF16), wo_ref[...])
    for j in range(TOKEN_TILE):
        o_ref[pl.ds(j, TM_OUT, stride=TOKEN_TILE), :] = x2[:, j * LANES:(j + 1) * LANES]

    ms = jnp.mean(x2 * x2, axis=-1, keepdims=True)
    hn = x2 * lax.rsqrt(ms + EPS) * gn_ref[...]
    h1 = hn.astype(BF16)
    h2 = (hn - h1.astype(F32)).astype(BF16)
    lg = _dot(h1, wr1_ref[...]) + _dot(h2, wr1_ref[...]) + _dot(h1, wr2_ref[...]) + br_ref[...]

    lane = lax.broadcasted_iota(jnp.int32, (TM_OUT, LANES), 1).astype(F32)
    big = float(LANES)

    def rmax(v):
        return jnp.max(v, axis=-1, keepdims=True)

    def first_lane(mask):
        return jnp.min(jnp.where(mask, lane, big), axis=-1, keepdims=True)

    gl = jnp.where(lane < N_EXPERT_GROUPS, lg, NEG_INF)
    gmax = rmax(gl)
    g_w = 1.0 / jnp.sum(jnp.exp(gl - gmax), axis=-1, keepdims=True)
    gidx = first_lane(gl == gmax)
    base = N_EXPERT_GROUPS + EXPERTS_PER_GROUP * gidx
    el = jnp.where(lane >= base, jnp.where(lane < base + EXPERTS_PER_GROUP, lg, NEG_INF), NEG_INF)
    m1 = rmax(el)
    i1 = first_lane(el == m1)
    el2 = jnp.where(lane == i1, NEG_INF, el)
    m2 = rmax(el2)
    i2 = first_lane(el2 == m2)
    r = jnp.exp(m2 - m1)
    w1 = g_w / (1.0 + r)
    w2 = w1 * r
    j1 = i1 - base
    j2 = i2 - base
    swap = j1 > j2
    e_lo = jnp.where(swap, j2, j1)
    e_hi = jnp.where(swap, j1, j2)
    w_lo = jnp.where(swap, w2, w1)
    w_hi = jnp.where(swap, w1, w2)
    pair = e_lo * (EXPERTS_PER_GROUP - 1) - e_lo * (e_lo - 1.0) * 0.5 + (e_hi - e_lo - 1.0)
    cls = gidx * N_PAIRS + pair
    rows = jnp.where(lane == 0.0, cls, jnp.where(lane == 1.0, w_lo, jnp.where(lane == 2.0, w_hi, 0.0)))
    r_ref[...] = rows.T[:TOKEN_TILE, :]


def _outproj(group_a, group_b, wa, ws, wo, gn, wr1, wr2, br):
    n_a = group_a[3].shape[0] // TM_OUT
    n_b = group_b[3].shape[0] // TM_OUT
    t = (n_a + n_b) * TM_OUT

    def first(i):
        return (jnp.minimum(i, n_a - 1), 0)

    def second(i):
        return (jnp.maximum(i - n_a, 0), 0)

    def const(i):
        return (0, 0)

    def group_specs(tok):
        return [
            pl.BlockSpec((TM_OUT, ATTN_WIDTH), tok),
            pl.BlockSpec((TM_OUT, D_INNER), tok),
            pl.BlockSpec((TM_OUT, 2 * D_MODEL), lambda i: (tok(i)[0], COL_GATE // (2 * D_MODEL))),
            pl.BlockSpec((TM_OUT, D_MODEL), tok),
        ]

    resident = dict(pipeline_mode=pl.Buffered(1))
    return pl.pallas_call(
        functools.partial(_outproj_kernel, n_a=n_a),
        grid=(n_a + n_b,),
        in_specs=group_specs(first) + group_specs(second) + [
            pl.BlockSpec((ATTN_WIDTH, D_MODEL), const, **resident),
            pl.BlockSpec((D_INNER, D_MODEL), const, **resident),
            pl.BlockSpec((D_MODEL, D_MODEL), const, **resident),
            pl.BlockSpec((1, D_MODEL), const),
            pl.BlockSpec((D_MODEL, LANES), const),
            pl.BlockSpec((D_MODEL, LANES), const),
            pl.BlockSpec((1, LANES), const),
        ],
        out_specs=[
            pl.BlockSpec((TM_OUT * TOKEN_TILE, LANES), lambda i: (i, 0)),
            pl.BlockSpec((TOKEN_TILE, TM_OUT), lambda i: (0, i)),
        ],
        out_shape=[
            jax.ShapeDtypeStruct((t * TOKEN_TILE, LANES), F32),
            jax.ShapeDtypeStruct((TOKEN_TILE, t), F32),
        ],
        compiler_params=_params(("arbitrary",)),
        name="outproj_router",
    )(*group_a, *group_b, wa, ws, wo, gn, wr1, wr2, br)


def _moe_kernel(ea_ref, eb_ref, nv_ref, tokc_ref, tokn_ref, roww_ref, x_ref, gn_ref,
                wga_ref, wua_ref, wda_ref, wgb_ref, wub_ref, wdb_ref, o_ref,
                xg_ref, st_ref, gsem, ssem, *, n_blocks):
    i = pl.program_id(0)
    slot = i % 2
    other = 1 - slot

    def tile(idx):
        return pl.ds(pl.multiple_of(idx * TOKEN_TILE, TOKEN_TILE), TOKEN_TILE)

    def gather_copy(tok, r, s):
        return pltpu.make_async_copy(x_ref.at[tile(tok), :], xg_ref.at[s, tile(r), :], gsem.at[s])

    def scatter_copy(tok, r, s):
        return pltpu.make_async_copy(st_ref.at[s, tile(r), :], o_ref.at[tile(tok), :], ssem.at[s])

    def for_rows(n, fn):
        n8 = lax.shift_right_logical(n, 3)

        def body8(g, _):
            for u in range(8):
                fn(g * 8 + u)
            return 0

        def body1(r, _):
            fn(r)
            return 0

        lax.fori_loop(0, n8, body8, 0)
        lax.fori_loop(n8 * 8, n, body1, 0)

    def start_gathers(tok_ref, n, s):
        for_rows(n, lambda r: gather_copy(tok_ref[0, 0, r], r, s).start())

    def wait_gathers(n, s):
        for_rows(n, lambda r: gather_copy(0, 0, s).wait())

    def start_scatters(tok_ref, n, s):
        for_rows(n, lambda r: scatter_copy(tok_ref[0, 0, r], r, s).start())

    def wait_scatters(n, s):
        for_rows(n, lambda r: scatter_copy(0, 0, s).wait())

    @pl.when(i == 0)
    def _():
        xg_ref[...] = jnp.zeros_like(xg_ref)
        start_gathers(tokc_ref, nv_ref[0], 0)

    @pl.when(i + 1 < n_blocks)
    def _():
        start_gathers(tokn_ref, nv_ref[jnp.minimum(i + 1, n_blocks - 1)], other)

    wait_gathers(nv_ref[i], slot)

    @pl.when(i >= 2)
    def _():
        wait_scatters(nv_ref[jnp.maximum(i - 2, 0)], slot)

    @pl.when(nv_ref[i] > 0)
    def _():
        x = jnp.concatenate(
            [xg_ref[slot, pl.ds(j, ROW_BLOCK, stride=TOKEN_TILE), :] for j in range(TOKEN_TILE)], axis=1)
        w_cols = jnp.concatenate([roww_ref[0], jnp.zeros((LANES - TOKEN_TILE, ROW_BLOCK), F32)], axis=0).T
        w_lo = w_cols[:, 0:1]
        w_hi = w_cols[:, 1:2]
        ms = jnp.mean(x * x, axis=-1, keepdims=True)
        hn = (x * lax.rsqrt(ms + EPS) * gn_ref[...]).astype(BF16)

        def expert(wg_ref, wu_ref, wd_ref):
            gte = _dot(hn, wg_ref[0])
            up = _dot(hn, wu_ref[0])
            return _dot((gte * _sigmoid(gte) * up).astype(BF16), wd_ref[0])

        ya = expert(wga_ref, wua_ref, wda_ref)
        yb = expert(wgb_ref, wub_ref, wdb_ref)
        out = x + w_lo * ya + w_hi * yb
        for j in range(TOKEN_TILE):
            st_ref[slot, pl.ds(j, ROW_BLOCK, stride=TOKEN_TILE), :] = out[:, j * LANES:(j + 1) * LANES]

    start_scatters(tokc_ref, nv_ref[i], slot)

    @pl.when(i == n_blocks - 1)
    def _():
        wait_scatters(nv_ref[jnp.maximum(i - 1, 0)], other)
        wait_scatters(nv_ref[i], slot)


def _moe(ea, eb, nvalid, row_tok, row_w, x2t, gn, wg, wu, wd):
    n_blocks = row_tok.shape[0]

    def wa(i, ea, eb, nv):
        return (ea[i], 0, 0)

    def wb(i, ea, eb, nv):
        return (eb[i], 0, 0)

    any_spec = pl.BlockSpec(memory_space=pl.ANY)
    grid_spec = pltpu.PrefetchScalarGridSpec(
        num_scalar_prefetch=3,
        grid=(n_blocks,),
        in_specs=[
            pl.BlockSpec((1, 1, ROW_BLOCK), lambda i, ea, eb, nv: (i, 0, 0), memory_space=pltpu.SMEM),
            pl.BlockSpec((1, 1, ROW_BLOCK), lambda i, ea, eb, nv: (jnp.minimum(i + 1, n_blocks - 1), 0, 0),
                         memory_space=pltpu.SMEM),
            pl.BlockSpec((1, TOKEN_TILE, ROW_BLOCK), lambda i, ea, eb, nv: (i, 0, 0)),
            any_spec,
            pl.BlockSpec((1, D_MODEL), lambda i, ea, eb, nv: (0, 0)),
            pl.BlockSpec((1, D_MODEL, D_EXPERT), wa), pl.BlockSpec((1, D_MODEL, D_EXPERT), wa),
            pl.BlockSpec((1, D_EXPERT, D_MODEL), wa),
            pl.BlockSpec((1, D_MODEL, D_EXPERT), wb), pl.BlockSpec((1, D_MODEL, D_EXPERT), wb),
            pl.BlockSpec((1, D_EXPERT, D_MODEL), wb),
        ],
        out_specs=any_spec,
        scratch_shapes=[
            pltpu.VMEM((2, ROW_BLOCK * TOKEN_TILE, LANES), F32),
            pltpu.VMEM((2, ROW_BLOCK * TOKEN_TILE, LANES), F32),
            pltpu.SemaphoreType.DMA((2,)),
            pltpu.SemaphoreType.DMA((2,)),
        ],
    )
    return pl.pallas_call(
        functools.partial(_moe_kernel, n_blocks=n_blocks),
        grid_spec=grid_spec,
        out_shape=jax.ShapeDtypeStruct(x2t.shape, F32),
        compiler_params=_params(("arbitrary",)),
        name="moe",
    )(ea, eb, nvalid, row_tok, row_tok, row_w, x2t, gn, wg, wu, wd, wg, wu, wd)


def _untile_kernel(x_ref, o_ref):
    for j in range(TOKEN_TILE):
        o_ref[:, j * LANES:(j + 1) * LANES] = x_ref[pl.ds(j, TM_UNTILE, stride=TOKEN_TILE), :]


def _untile(y_tiles, first_token, n_tokens):
    first_block = first_token // TM_UNTILE
    return pl.pallas_call(
        _untile_kernel,
        grid=(n_tokens // TM_UNTILE,),
        in_specs=[pl.BlockSpec((TM_UNTILE * TOKEN_TILE, LANES), lambda i: (first_block + i, 0))],
        out_specs=pl.BlockSpec((TM_UNTILE, D_MODEL), lambda i: (i, 0)),
        out_shape=jax.ShapeDtypeStruct((n_tokens, D_MODEL), F32),
        compiler_params=_params(("arbitrary",)),
        name="untile",
    )(y_tiles)


def _pair_tables():
    lo, hi = [], []
    for a in range(EXPERTS_PER_GROUP):
        for b in range(a + 1, EXPERTS_PER_GROUP):
            lo.append(a)
            hi.append(b)
    return np.asarray(lo, np.int32), np.asarray(hi, np.int32)


def _block_tables(rinfo):
    cls = rinfo[0].astype(jnp.int32)
    t = cls.shape[0]
    n_blocks = t // ROW_BLOCK + N_CLASSES
    sorted_cls, order = lax.sort((cls, jnp.arange(t, dtype=jnp.int32)), num_keys=1)
    class_ids = jnp.arange(N_CLASSES + 1, dtype=jnp.int32)
    starts = jnp.sum((sorted_cls[:, None] < class_ids[None, :]).astype(jnp.int32), axis=0)
    counts = starts[1:] - starts[:-1]
    nblk = (counts + ROW_BLOCK - 1) // ROW_BLOCK
    blk_end = jnp.cumsum(nblk)
    blk_start = blk_end - nblk
    used = blk_end[-1]
    b = jnp.arange(n_blocks, dtype=jnp.int32)
    b_eff = jnp.minimum(b, used - 1)
    c = jnp.sum((blk_end[None, :] <= b_eff[:, None]).astype(jnp.int32), axis=1)
    c = jnp.minimum(c, N_CLASSES - 1)
    off = b_eff - blk_start[c]
    src = starts[c] + off * ROW_BLOCK
    nvalid = jnp.where(b < used, jnp.clip(counts[c] - off * ROW_BLOCK, 0, ROW_BLOCK), 0).astype(jnp.int32)
    pair_lo, pair_hi = _pair_tables()
    grp = c // N_PAIRS
    ea = (grp * EXPERTS_PER_GROUP + jnp.asarray(pair_lo)[c % N_PAIRS]).astype(jnp.int32)
    eb = (grp * EXPERTS_PER_GROUP + jnp.asarray(pair_hi)[c % N_PAIRS]).astype(jnp.int32)
    rows = jnp.clip(src[:, None] + jnp.arange(ROW_BLOCK, dtype=jnp.int32)[None, :], 0, t - 1)
    row_tok = order[rows]
    row_w = jnp.concatenate([rinfo[1][row_tok][:, None, :], rinfo[2][row_tok][:, None, :],
                             jnp.zeros((n_blocks, TOKEN_TILE - 2, ROW_BLOCK), F32)], axis=1)
    return ea, eb, nvalid, row_tok.reshape(n_blocks, 1, ROW_BLOCK), row_w


def _rope_tables(seq):
    inv = 1.0 / (ROPE_THETA ** (jnp.arange(0, HEAD_DIM, 2, dtype=F32) / HEAD_DIM))
    ang = jnp.arange(seq, dtype=F32)[:, None] * inv[None, :]
    cos, sin = jnp.cos(ang), jnp.sin(ang)
    cos128 = jnp.concatenate([cos, cos, cos, cos], axis=-1)
    sin128 = jnp.concatenate([-sin, sin, -sin, sin], axis=-1)
    return cos128, sin128


def _prepare_weights(norm_mix, w_in, q_norm, k_norm, attn_sink, conv_w, conv_b, a_log_fwd, a_log_bwd,
                     dt_bias_fwd, dt_bias_bwd, d_skip, ssm_norm, w_out_attn, w_out_ssm, w_o, norm_ffn,
                     w_router_group, b_router_group, w_router_expert, b_router_expert, w_gate, w_up, w_down):
    o_q = 0
    o_k = o_q + ATTN_WIDTH
    o_v = o_k + KV_WIDTH
    o_z = o_v + KV_WIDTH
    o_xbc = o_z + D_INNER
    o_dtf = o_xbc + CONV_DIM
    o_dtb = o_dtf + N_SSM_HEADS
    o_ga = o_dtb + N_SSM_HEADS
    o_gs = o_ga + D_MODEL
    w = w_in.astype(BF16)
    w_r = jnp.concatenate([
        w[:, o_xbc:o_xbc + D_INNER], w[:, o_z:o_z + D_INNER], w[:, o_ga:o_gs + D_MODEL],
        w[:, o_xbc + D_INNER:o_xbc + CONV_DIM],
        w[:, o_q:o_q + ATTN_WIDTH], w[:, o_k:o_k + KV_WIDTH], w[:, o_v:o_v + KV_WIDTH],
        w[:, o_dtf:o_dtb + N_SSM_HEADS], jnp.zeros((D_MODEL, LANES - 2 * N_SSM_HEADS), w.dtype)], axis=1)
    pad64 = jnp.zeros((LANES - 2 * N_SSM_HEADS,), F32)
    eye = np.kron(np.eye(2, dtype=np.float32), np.ones((HEAD_DIM, HEAD_DIM), np.float32))
    idx = np.arange(CHUNK)
    w_router = jnp.concatenate([w_router_group, w_router_expert,
                                jnp.zeros((D_MODEL, LANES - N_EXPERT_GROUPS - N_EXPERTS), F32)], axis=1)
    wr1 = w_router.astype(BF16)
    return dict(
        norm_mix=norm_mix.reshape(1, D_MODEL),
        w_in=w_r,
        qg128=jnp.tile(q_norm, 2).reshape(1, LANES),
        kg128=jnp.tile(k_norm, 2).reshape(1, LANES),
        seg=jnp.asarray(eye, BF16),
        sink=attn_sink.astype(F32),
        conv_w=conv_w,
        conv_b=conv_b.reshape(1, CONV_DIM),
        alog128=jnp.concatenate([a_log_fwd, a_log_bwd, pad64]).reshape(1, LANES),
        bias128=jnp.concatenate([dt_bias_fwd, dt_bias_bwd, pad64]).reshape(1, LANES),
        tri_l=jnp.asarray(idx[:, None] >= idx[None, :], BF16),
        tri_u=jnp.asarray(idx[:, None] <= idx[None, :], BF16),
        dskip=jnp.repeat(d_skip, SSM_HEAD_DIM).reshape(1, D_INNER),
        ssm_norm=ssm_norm.reshape(1, D_INNER),
        wa=w_out_attn.astype(BF16), ws=w_out_ssm.astype(BF16), wo=w_o.astype(BF16),
        norm_ffn=norm_ffn.reshape(1, D_MODEL),
        wr1=wr1, wr2=(w_router - wr1.astype(F32)).astype(BF16),
        br=jnp.concatenate([b_router_group, b_router_expert,
                            jnp.zeros((LANES - N_EXPERT_GROUPS - N_EXPERTS,), F32)]).reshape(1, LANES),
        wg=w_gate.astype(BF16), wu=w_up.astype(BF16), wd=w_down.astype(BF16),
    )


def _mixer(x, p):
    batch, seq, _ = x.shape
    x2d = x.reshape(batch * seq, D_MODEL)
    cos128, sin128 = _rope_tables(seq)
    proj, dt, qr, kdup, vdup, xc = _inproj(x2d, p['norm_mix'], p['w_in'], p['conv_w'], p['conv_b'], cos128, sin128,
                                           p['qg128'], p['kg128'], p['seg'], seq)
    attn = _attention(qr, kdup, vdup, p['sink'], batch, seq)
    hb = _ssd_bwd_states(xc, dt, p['bias128'], p['alog128'], p['tri_l'], batch, seq)
    ssm = _ssd_main(xc, proj, dt, hb, p['bias128'], p['alog128'], p['tri_l'], p['tri_u'], p['dskip'],
                    p['ssm_norm'], batch, seq)
    return attn, ssm, proj, x2d


def kernel(x_prompt, x_sample, norm_mix, w_in, q_norm, k_norm, attn_sink, conv_w, conv_b, a_log_fwd, a_log_bwd,
           dt_bias_fwd, dt_bias_bwd, d_skip, ssm_norm, w_out_attn, w_out_ssm, w_o, norm_ffn, w_router_group,
           b_router_group, w_router_expert, b_router_expert, w_gate, w_up, w_down):
    assert norm_mix.shape[0] == 1, "single-layer encoder"
    p = _prepare_weights(norm_mix[0], w_in[0], q_norm[0], k_norm[0], attn_sink[0], conv_w[0], conv_b[0],
                         a_log_fwd[0], a_log_bwd[0], dt_bias_fwd[0], dt_bias_bwd[0], d_skip[0], ssm_norm[0],
                         w_out_attn[0], w_out_ssm[0], w_o[0], norm_ffn[0], w_router_group[0], b_router_group[0],
                         w_router_expert[0], b_router_expert[0], w_gate[0], w_up[0], w_down[0])
    x2t, rinfo = _outproj(_mixer(x_prompt, p), _mixer(x_sample, p), p['wa'], p['ws'], p['wo'], p['norm_ffn'],
                          p['wr1'], p['wr2'], p['br'])
    ea, eb, nvalid, row_tok, row_w = _block_tables(rinfo)
    y = _moe(ea, eb, nvalid, row_tok, row_w, x2t, p['norm_ffn'], p['wg'], p['wu'], p['wd'])
    t_a = x_prompt.shape[0] * x_prompt.shape[1]
    t_b = x_sample.shape[0] * x_sample.shape[1]
    return _untile(y, 0, t_a).reshape(x_prompt.shape), _untile(y, t_a, t_b).reshape(x_sample.shape)
```

```python
import functools

import numpy as np
import jax
import jax.numpy as jnp
from jax import lax
from jax.experimental import pallas as pl
from jax.experimental.pallas import tpu as pltpu

F32 = jnp.float32
BF16 = jnp.bfloat16

D_MODEL = 1024
EPS = 1e-6
NEG_INF = -1e30
LOG2_E = 1.4426950408889634
N_Q_HEADS = 16
N_KV_HEADS = 4
HEAD_DIM = 64
ATTN_WIDTH = N_Q_HEADS * HEAD_DIM
KV_WIDTH = N_KV_HEADS * HEAD_DIM
ATTN_BLOCK = 128
ATTN_QB = 4
ROPE_THETA = 10000.0
D_INNER = 2 * D_MODEL
SSM_HEAD_DIM = 64
N_SSM_HEADS = D_INNER // SSM_HEAD_DIM
N_SSM_GROUPS = 4
HEADS_PER_GROUP = N_SSM_HEADS // N_SSM_GROUPS
D_STATE = 128
BC_WIDTH = N_SSM_GROUPS * D_STATE
CONV_DIM = D_INNER + 2 * BC_WIDTH
CONV_W = 7
CHUNK = 128
SSD_BWD_CHUNKS = 8
SSD_MAIN_CHUNKS = 4
N_EXPERT_GROUPS = 4
EXPERTS_PER_GROUP = 8
N_EXPERTS = N_EXPERT_GROUPS * EXPERTS_PER_GROUP
D_EXPERT = 512
N_PAIRS = EXPERTS_PER_GROUP * (EXPERTS_PER_GROUP - 1) // 2
N_CLASSES = N_EXPERT_GROUPS * N_PAIRS

LANES = 128
V7X_VMEM_LIMIT_BYTES = 56 * 1024 * 1024

COL_Z = 0
COL_GATE = COL_Z + D_INNER
COL_XS = COL_GATE + 2 * D_MODEL
COL_B = COL_XS + D_INNER
COL_C = COL_B + BC_WIDTH
COL_Q = COL_C + BC_WIDTH
COL_K = COL_Q + ATTN_WIDTH
COL_V = COL_K + KV_WIDTH
COL_DT = COL_V + KV_WIDTH
N_PROJ = COL_DT + LANES

TM_IN = 1024
NJ_IN = 3
TN_IN = N_PROJ // NJ_IN
CH_IN = 512
TM_OUT = 512
TM_UNTILE = 1024
CONV_CT = 512
CONV_ROWS = 256
CONV_PITCH = 2
ROW_BLOCK = 128
TOKEN_TILE = D_MODEL // LANES


def _params(sem, flags=None):
    return pltpu.CompilerParams(dimension_semantics=sem, vmem_limit_bytes=V7X_VMEM_LIMIT_BYTES, flags=flags)


def _dot(a, b):
    return jnp.dot(a, b, preferred_element_type=F32)


def _dot_nt(a, b):
    return lax.dot_general(a, b, (((1,), (1,)), ((), ())), preferred_element_type=F32)


def _sigmoid(x):
    return 1.0 / (1.0 + jnp.exp(-x))


def _inproj_kernel(x_ref, g_ref, w_ref, cos_ref, sin_ref, qg_ref, kg_ref, seg_ref,
                   o_ref, dt_ref, qo_ref, ko_ref, vo_ref, h_ref):
    j = pl.program_id(1)

    @pl.when(j == 0)
    def _():
        x = x_ref[...]
        ms = jnp.mean(x * x, axis=-1, keepdims=True)
        h_ref[...] = (x * lax.rsqrt(ms + EPS) * g_ref[...]).astype(BF16)

    def project(c0, c1):
        acc = _dot(h_ref[...], w_ref[:, c0:c1])
        o_ref[:, c0:c1] = acc.astype(BF16)
        return acc

    @pl.when(j < NJ_IN - 1)
    def _():
        for c0 in range(0, TN_IN, CH_IN):
            project(c0, min(c0 + CH_IN, TN_IN))

    @pl.when(j == NJ_IN - 1)
    def _():
        base = (NJ_IN - 1) * TN_IN
        q0, k0, v0, d0 = COL_Q - base, COL_K - base, COL_V - base, COL_DT - base
        for c0 in range(0, q0, CH_IN):
            project(c0, min(c0 + CH_IN, q0))
        cos = cos_ref[...]
        sin = sin_ref[...]
        seg = seg_ref[...]
        lane = lax.broadcasted_iota(jnp.int32, (TM_IN, LANES), 1)
        first_half = (lane % HEAD_DIM) < (HEAD_DIM // 2)
        low = lane < HEAD_DIM

        def norm_rope(x, gain):
            ss = _dot((x * x).astype(BF16), seg)
            xn = x * lax.rsqrt(ss * (1.0 / HEAD_DIM) + EPS) * gain
            rot = jnp.where(first_half, pltpu.roll(xn, 96, 1), pltpu.roll(xn, 32, 1))
            return xn * cos + rot * sin

        def duplicate(y, dst_ref, s):
            ysw = pltpu.roll(y, HEAD_DIM, 1)
            dst_ref[:, (2 * s) * LANES:(2 * s + 1) * LANES] = jnp.where(low, y, ysw).astype(BF16)
            dst_ref[:, (2 * s + 1) * LANES:(2 * s + 2) * LANES] = jnp.where(low, ysw, y).astype(BF16)

        for c0 in range(q0, k0, CH_IN):
            acc = project(c0, c0 + CH_IN)
            for s in range(CH_IN // LANES):
                y = norm_rope(acc[:, s * LANES:(s + 1) * LANES], qg_ref[...]) * (HEAD_DIM ** -0.5 * LOG2_E)
                dst = c0 - q0 + s * LANES
                qo_ref[:, dst:dst + LANES] = y.astype(BF16)
        acc = project(k0, d0)
        for s in range(KV_WIDTH // LANES):
            duplicate(norm_rope(acc[:, s * LANES:(s + 1) * LANES], kg_ref[...]), ko_ref, s)
            duplicate(acc[:, KV_WIDTH + s * LANES:KV_WIDTH + (s + 1) * LANES], vo_ref, s)
        dt_ref[...] = project(d0, TN_IN)


def _inproj(x2d, gain, w_bf16, cos128, sin128, qg128, kg128, seg, seq):
    t = x2d.shape[0]
    nseq = seq // TM_IN

    def rows(i, j):
        return (i, 0)

    def const(i, j):
        return (0, 0)

    def pos(i, j):
        return (i % nseq, 0)

    return pl.pallas_call(
        _inproj_kernel,
        grid=(t // TM_IN, NJ_IN),
        in_specs=[
            pl.BlockSpec((TM_IN, D_MODEL), rows),
            pl.BlockSpec((1, D_MODEL), const),
            pl.BlockSpec((D_MODEL, TN_IN), lambda i, j: (0, j)),
            pl.BlockSpec((TM_IN, LANES), pos),
            pl.BlockSpec((TM_IN, LANES), pos),
            pl.BlockSpec((1, LANES), const),
            pl.BlockSpec((1, LANES), const),
            pl.BlockSpec((LANES, LANES), const),
        ],
        out_specs=[
            pl.BlockSpec((TM_IN, TN_IN), lambda i, j: (i, j)),
            pl.BlockSpec((TM_IN, LANES), rows),
            pl.BlockSpec((TM_IN, ATTN_WIDTH), rows),
            pl.BlockSpec((TM_IN, 2 * KV_WIDTH), rows),
            pl.BlockSpec((TM_IN, 2 * KV_WIDTH), rows),
        ],
        out_shape=[
            jax.ShapeDtypeStruct((t, N_PROJ), BF16),
            jax.ShapeDtypeStruct((t, LANES), F32),
            jax.ShapeDtypeStruct((t, ATTN_WIDTH), BF16),
            jax.ShapeDtypeStruct((t, 2 * KV_WIDTH), BF16),
            jax.ShapeDtypeStruct((t, 2 * KV_WIDTH), BF16),
        ],
        scratch_shapes=[pltpu.VMEM((TM_IN, D_MODEL), BF16)],
        compiler_params=_params(("arbitrary", "arbitrary")),
        name="inproj",
    )(x2d, gain, w_bf16, cos128, sin128, qg128, kg128, seg)


def _attn_kernel(sink_ref, q_ref, kp_ref, kc_ref, kn_ref, vp_ref, vc_ref, vn_ref, o_ref, *, n_steps):
    i = pl.program_id(1)
    nb = ATTN_BLOCK
    nk = 3 * nb
    qi = lax.broadcasted_iota(jnp.int32, (nb, nk), 0)
    si = lax.broadcasted_iota(jnp.int32, (nb, nk), 1)
    rel = qi - (si - nb)
    band = jnp.where(rel <= nb, jnp.where(rel >= -nb, 0.0, NEG_INF), NEG_INF)
    bias_first = jnp.where(si < nb, jnp.where(i > 0, band, NEG_INF), band)
    bias_last = jnp.where(si >= 2 * nb, jnp.where(i < n_steps - 1, band, NEG_INF), band)
    low_q = lax.broadcasted_iota(jnp.int32, (nb, LANES), 1) < HEAD_DIM
    low_k = lax.broadcasted_iota(jnp.int32, (nk, LANES), 1) < HEAD_DIM
    zero_q = jnp.zeros((nb, LANES), BF16)
    zero_k = jnp.zeros((nk, LANES), BF16)

    def window(p_ref, c_ref, n_ref, h, j):
        sl = slice(h * LANES, (h + 1) * LANES)
        rows = jnp.concatenate([p_ref[:, sl], c_ref[:, sl], n_ref[:, sl]], axis=0)
        return rows[j * nb:j * nb + nk]

    def scores(j, h):
        kd = window(kp_ref, kc_ref, kn_ref, h, j)
        slabs = [q_ref[j * nb:(j + 1) * nb, (2 * h + u) * LANES:(2 * h + u + 1) * LANES] for u in range(2)]
        q4 = jnp.concatenate([jnp.where(low_q, s_, zero_q) for s_ in slabs]
                             + [jnp.where(low_q, zero_q, s_) for s_ in slabs], axis=0)
        return _dot_nt(q4, kd)

    def finish(j, h, s4):
        vd = window(vp_ref, vc_ref, vn_ref, h, j)
        v_lo = jnp.where(low_k, vd, zero_k)
        v_hi = jnp.where(low_k, zero_k, vd)
        heads = (4 * h, 4 * h + 2, 4 * h + 1, 4 * h + 3)
        ps, invs = [], []
        for k, head in enumerate(heads):
            s = s4[k * nb:(k + 1) * nb]
            left = s[:, :nb] + (bias_first if j == 0 else band)[:, :nb]
            right = s[:, 2 * nb:] + (bias_last if j == ATTN_QB - 1 else band)[:, 2 * nb:]
            s = jnp.concatenate([left, s[:, nb:2 * nb], right], axis=1)
            snk = sink_ref[head] * LOG2_E
            m = jnp.maximum(jnp.max(s, axis=-1, keepdims=True), snk)
            p = jnp.exp2(s - m)
            den = jnp.sum(p, axis=-1, keepdims=True) + jnp.exp2(snk - m)
            ps.append(p.astype(BF16))
            invs.append(1.0 / den)
        o = _dot(jnp.concatenate(ps[:2], axis=0), v_lo) + _dot(jnp.concatenate(ps[2:], axis=0), v_hi)
        for u in range(2):
            ou = o[u * nb:(u + 1) * nb] * jnp.where(low_q, invs[u], invs[2 + u])
            o_ref[j * nb:(j + 1) * nb, (2 * h + u) * LANES:(2 * h + u + 1) * LANES] = ou.astype(BF16)

    tasks = [(j, h) for j in range(ATTN_QB) for h in range(N_KV_HEADS)]
    pending = scores(*tasks[0])
    for n, task in enumerate(tasks):
        following = scores(*tasks[n + 1]) if n + 1 < len(tasks) else None
        finish(*task, pending)
        pending = following


def _attention(qr, kdup, vdup, sink, batch, seq):
    t = qr.shape[0]
    nq = seq // ATTN_BLOCK
    n_steps = nq // ATTN_QB
    rows = ATTN_QB * ATTN_BLOCK

    def prev(b, i, s):
        return (b * nq + jnp.maximum(i * ATTN_QB - 1, 0), 0)

    def cur(b, i, s):
        return (b * n_steps + i, 0)

    def nxt(b, i, s):
        return (b * nq + jnp.minimum((i + 1) * ATTN_QB, nq - 1), 0)

    edge = (ATTN_BLOCK, 2 * KV_WIDTH)
    mid = (rows, 2 * KV_WIDTH)
    grid_spec = pltpu.PrefetchScalarGridSpec(
        num_scalar_prefetch=1,
        grid=(batch, n_steps),
        in_specs=[
            pl.BlockSpec((rows, ATTN_WIDTH), cur),
            pl.BlockSpec(edge, prev), pl.BlockSpec(mid, cur), pl.BlockSpec(edge, nxt),
            pl.BlockSpec(edge, prev), pl.BlockSpec(mid, cur), pl.BlockSpec(edge, nxt),
        ],
        out_specs=pl.BlockSpec((rows, ATTN_WIDTH), cur),
    )
    return pl.pallas_call(
        functools.partial(_attn_kernel, n_steps=n_steps),
        grid_spec=grid_spec,
        out_shape=jax.ShapeDtypeStruct((t, ATTN_WIDTH), BF16),
        compiler_params=_params(("arbitrary", "arbitrary")),
        name="attention",
    )(sink, qr, kdup, kdup, kdup, vdup, vdup, vdup)


def _conv_kernel(x_ref, w_ref, b_ref, o_ref, pad_ref, *, seq):
    halo = 8
    step = CONV_PITCH

    def rows(first, n):
        return pl.ds(step * (first + halo), n, stride=step)

    for h in range(CONV_CT // LANES):
        lanes = slice(h * LANES, (h + 1) * LANES)
        pad_ref[h, rows(-halo, halo), :] = jnp.zeros((halo, LANES), F32)
        pad_ref[h, rows(seq, halo), :] = jnp.zeros((halo, LANES), F32)
        for r in range(seq // CONV_ROWS):
            pad_ref[h, rows(r * CONV_ROWS, CONV_ROWS), :] = x_ref[r * CONV_ROWS:(r + 1) * CONV_ROWS, lanes].astype(F32)
    w = w_ref[...]
    bias = b_ref[...]
    for h in range(CONV_CT // LANES):
        lanes = slice(h * LANES, (h + 1) * LANES)
        for r in range(seq // CONV_ROWS):
            r0 = r * CONV_ROWS
            acc = jnp.broadcast_to(bias[:, lanes], (CONV_ROWS, LANES))
            for k in range(CONV_W):
                acc = acc + pad_ref[h, rows(r0 + k - CONV_W // 2, CONV_ROWS), :] * w[k:k + 1, lanes]
            o_ref[r0:r0 + CONV_ROWS, lanes] = (acc * _sigmoid(acc)).astype(BF16)


def _conv(proj, conv_w, conv_b, batch, seq):
    t = proj.shape[0]
    return pl.pallas_call(
        functools.partial(_conv_kernel, seq=seq),
        grid=(batch, CONV_DIM // CONV_CT),
        in_specs=[
            pl.BlockSpec((seq, CONV_CT), lambda b, c: (b, COL_XS // CONV_CT + c)),
            pl.BlockSpec((CONV_W, CONV_CT), lambda b, c: (0, c)),
            pl.BlockSpec((1, CONV_CT), lambda b, c: (0, c)),
        ],
        out_specs=pl.BlockSpec((seq, CONV_CT), lambda b, c: (b, c)),
        out_shape=jax.ShapeDtypeStruct((t, CONV_DIM), BF16),
        scratch_shapes=[pltpu.VMEM((CONV_CT // LANES, CONV_PITCH * (seq + 16), LANES), F32)],
        compiler_params=_params(("arbitrary", "arbitrary")),
        name="conv",
    )(proj, conv_w, conv_b)


def _split3(a):
    a1 = a.astype(BF16)
    r1 = a - a1.astype(F32)
    a2 = r1.astype(BF16)
    a3 = (r1 - a2.astype(F32)).astype(BF16)
    return a1, a2, a3


def _tri_matmul(tri, a):
    a1, a2, a3 = _split3(a)
    return _dot(tri, a1) + _dot(tri, a2) + _dot(tri, a3)


def _softplus(x):
    return jnp.maximum(x, 0.0) + jnp.log(1.0 + jnp.exp(-jnp.abs(x)))


def _dt_and_rate(dt_ref, bias_ref, alog_ref):
    dt = _softplus(dt_ref[...] + bias_ref[...])
    rate = dt * (-LOG2_E * jnp.exp(alog_ref[...]))
    return dt, rate


def _head_rows(mat, first, rows):
    n = mat.shape[1]
    return jnp.concatenate(
        [jnp.broadcast_to(mat[first + e:first + e + 1, :], (rows, n)) for e in range(HEADS_PER_GROUP)], axis=0)


def _ssd_bwd_state_kernel(xs_ref, b_ref, dt_ref, bias_ref, alog_ref, tl_ref, hb_ref, st_ref):
    c = pl.program_id(1)

    @pl.when(c == 0)
    def _():
        st_ref[...] = jnp.zeros_like(st_ref)

    for ci in reversed(range(SSD_BWD_CHUNKS)):
        rows = pl.ds(ci * CHUNK, CHUNK)
        _ssd_bwd_chunk(xs_ref.at[rows, :], b_ref.at[rows, :], dt_ref.at[rows, :], bias_ref, alog_ref, tl_ref,
                       hb_ref.at[ci], st_ref)


def _ssd_bwd_chunk(xs_ref, b_ref, dt_ref, bias_ref, alog_ref, tl_ref, hb_ref, st_ref):
    hb_ref[...] = st_ref[...].astype(BF16)
    dt, rate = _dt_and_rate(dt_ref, bias_ref, alog_ref)
    pre = _tri_matmul(tl_ref[...], rate)
    pre_t = pre.T
    excl_t = (pre - rate).T
    total = jnp.broadcast_to(pre_t[:, CHUNK - 1:CHUNK], (LANES, CHUNK))
    w_t = dt.T * jnp.exp2(excl_t)
    dec = jnp.exp2(total)
    off = N_SSM_HEADS
    for g in range(N_SSM_GROUPS):
        xs_t = xs_ref[:, g * 512:(g + 1) * 512].astype(F32).T
        xd = (xs_t * _head_rows(w_t, off + g * HEADS_PER_GROUP, SSM_HEAD_DIM)).astype(BF16)
        upd = _dot(xd, b_ref[:, g * D_STATE:(g + 1) * D_STATE])
        st_ref[g] = _head_rows(dec, off + g * HEADS_PER_GROUP, SSM_HEAD_DIM) * st_ref[g] + upd


def _ssd_bwd_states(xc, dt, bias128, alog128, tri_l, batch, seq):
    nc = seq // CHUNK
    ns = nc // SSD_BWD_CHUNKS
    rows = SSD_BWD_CHUNKS * CHUNK

    def rev(b, c):
        return (b * ns + ns - 1 - c, 0)

    return pl.pallas_call(
        _ssd_bwd_state_kernel,
        grid=(batch, ns),
        in_specs=[
            pl.BlockSpec((rows, D_INNER), rev),
            pl.BlockSpec((rows, BC_WIDTH), lambda b, c: (b * ns + ns - 1 - c, D_INNER // BC_WIDTH)),
            pl.BlockSpec((rows, LANES), rev),
            pl.BlockSpec((1, LANES), lambda b, c: (0, 0)),
            pl.BlockSpec((1, LANES), lambda b, c: (0, 0)),
            pl.BlockSpec((CHUNK, CHUNK), lambda b, c: (0, 0)),
        ],
        out_specs=pl.BlockSpec((SSD_BWD_CHUNKS, N_SSM_GROUPS, 512, D_STATE),
                               lambda b, c: (b * ns + ns - 1 - c, 0, 0, 0)),
        out_shape=jax.ShapeDtypeStruct((batch * nc, N_SSM_GROUPS, 512, D_STATE), BF16),
        scratch_shapes=[pltpu.VMEM((N_SSM_GROUPS, 512, D_STATE), F32)],
        compiler_params=_params(("arbitrary", "arbitrary")),
        name="ssd_bwd_states",
    )(xc, xc, dt, bias128, alog128, tri_l)


def _ssd_main_kernel(xc_ref, z_ref, dt_ref, hb_ref, bias_ref, alog_ref, tl_ref, tu_ref, dskip_ref, gain_ref,
                     o_ref, hf_ref, y_ref):
    c = pl.program_id(1)

    @pl.when(c == 0)
    def _():
        hf_ref[...] = jnp.zeros_like(hf_ref)

    for ci in range(SSD_MAIN_CHUNKS):
        rows = pl.ds(ci * CHUNK, CHUNK)
        _ssd_main_chunk(xc_ref.at[rows, :], z_ref.at[rows, :], dt_ref.at[rows, :], hb_ref.at[ci], bias_ref, alog_ref,
                        tl_ref, tu_ref, dskip_ref, gain_ref, o_ref.at[rows, :], hf_ref, y_ref)


def _ssd_main_chunk(xc_ref, z_ref, dt_ref, hb_ref, bias_ref, alog_ref, tl_ref, tu_ref, dskip_ref, gain_ref,
                    o_ref, hf_ref, y_ref):
    dt, rate = _dt_and_rate(dt_ref, bias_ref, alog_ref)
    lane = lax.broadcasted_iota(jnp.int32, (CHUNK, LANES), 1)
    cum = jnp.where(lane < N_SSM_HEADS, _tri_matmul(tl_ref[...], rate), _tri_matmul(tu_ref[...], rate))
    cum_t = cum.T
    dt_t = dt.T
    src_t = cum_t - jnp.log2(dt_t)
    row = lax.broadcasted_iota(jnp.int32, (CHUNK, CHUNK), 0)
    col = lax.broadcasted_iota(jnp.int32, (CHUNK, CHUNK), 1)
    lower = row >= col
    diag = row == col
    low = lane < SSM_HEAD_DIM
    zero_x = jnp.zeros((CHUNK, LANES), BF16)
    nb = N_SSM_HEADS

    def lane_bcast(mat, idx):
        return jnp.broadcast_to(mat[:, idx:idx + 1], (CHUNK, CHUNK))

    def sub_bcast(mat, idx):
        return jnp.broadcast_to(mat[idx:idx + 1, :], (CHUNK, CHUNK))

    def head_matrix(e, cb):
        col_f = lane_bcast(cum, e)
        col_b = lane_bcast(cum, nb + e)
        decay = jnp.exp2(jnp.where(lower, col_f - sub_bcast(src_t, e), col_b - sub_bcast(src_t, nb + e)))
        decay = decay + jnp.where(diag, sub_bcast(dt_t, nb + e), 0.0)
        return (decay * cb).astype(BF16), col_f, col_b

    for g in range(N_SSM_GROUPS):
        bg = xc_ref[:, D_INNER + g * D_STATE:D_INNER + (g + 1) * D_STATE]
        cg = xc_ref[:, D_INNER + BC_WIDTH + g * D_STATE:D_INNER + BC_WIDTH + (g + 1) * D_STATE]
        cb = _dot_nt(cg, bg)
        y_in_f = _dot_nt(cg, hf_ref[g].astype(BF16))
        y_in_b = _dot_nt(cg, hb_ref[g])
        for jp in range(HEADS_PER_GROUP // 2):
            e0 = g * HEADS_PER_GROUP + 2 * jp
            cols = slice(e0 * SSM_HEAD_DIM, e0 * SSM_HEAD_DIM + LANES)
            loc = slice(jp * LANES, (jp + 1) * LANES)
            xs_pair = xc_ref[:, cols]
            m0, cf0, cb0 = head_matrix(e0, cb)
            m1, cf1, cb1 = head_matrix(e0 + 1, cb)
            y = _dot(m0, jnp.where(low, xs_pair, zero_x)) + _dot(m1, jnp.where(low, zero_x, xs_pair))
            y = y + y_in_f[:, loc] * jnp.exp2(jnp.where(low, cf0, cf1))
            y = y + y_in_b[:, loc] * jnp.exp2(jnp.where(low, cb0, cb1))
            y_ref[:, cols] = y + dskip_ref[:, cols] * xs_pair.astype(F32)

    z = z_ref[...].astype(F32)
    y = y_ref[...] * (z * _sigmoid(z))
    ms = jnp.mean(y * y, axis=-1, keepdims=True)
    o_ref[...] = (y * lax.rsqrt(ms + EPS) * gain_ref[...]).astype(BF16)

    last = jnp.broadcast_to(cum_t[:, CHUNK - 1:CHUNK], (LANES, CHUNK))
    w_t = jnp.exp2(last - src_t)
    dec = jnp.exp2(last)
    for g in range(N_SSM_GROUPS):
        xs_t = xc_ref[:, g * 512:(g + 1) * 512].astype(F32).T
        xd = (xs_t * _head_rows(w_t, g * HEADS_PER_GROUP, SSM_HEAD_DIM)).astype(BF16)
        upd = _dot(xd, xc_ref[:, D_INNER + g * D_STATE:D_INNER + (g + 1) * D_STATE])
        hf_ref[g] = _head_rows(dec, g * HEADS_PER_GROUP, SSM_HEAD_DIM) * hf_ref[g] + upd


def _ssd_main(xc, proj, dt, hb, bias128, alog128, tri_l, tri_u, dskip, gain, batch, seq):
    t = xc.shape[0]
    ns = seq // CHUNK // SSD_MAIN_CHUNKS
    rows = SSD_MAIN_CHUNKS * CHUNK

    def tok(b, c):
        return (b * ns + c, 0)

    def const(b, c):
        return (0, 0)

    return pl.pallas_call(
        _ssd_main_kernel,
        grid=(batch, ns),
        in_specs=[
            pl.BlockSpec((rows, CONV_DIM), tok),
            pl.BlockSpec((rows, D_INNER), tok),
            pl.BlockSpec((rows, LANES), tok),
            pl.BlockSpec((SSD_MAIN_CHUNKS, N_SSM_GROUPS, 512, D_STATE), lambda b, c: (b * ns + c, 0, 0, 0)),
            pl.BlockSpec((1, LANES), const),
            pl.BlockSpec((1, LANES), const),
            pl.BlockSpec((CHUNK, CHUNK), const),
            pl.BlockSpec((CHUNK, CHUNK), const),
            pl.BlockSpec((1, D_INNER), const),
            pl.BlockSpec((1, D_INNER), const),
        ],
        out_specs=pl.BlockSpec((rows, D_INNER), tok),
        out_shape=jax.ShapeDtypeStruct((t, D_INNER), BF16),
        scratch_shapes=[pltpu.VMEM((N_SSM_GROUPS, 512, D_STATE), F32), pltpu.VMEM((CHUNK, D_INNER), F32)],
        compiler_params=_params(("arbitrary", "arbitrary")),
        name="ssd_main",
    )(xc, proj, dt, hb, bias128, alog128, tri_l, tri_u, dskip, gain)


def _outproj_kernel(attn_a, ssm_a, gate_a, x_a, attn_b, ssm_b, gate_b, x_b, wa_ref, ws_ref, wo_ref, gn_ref,
                    wr1_ref, wr2_ref, br_ref, o_ref, r_ref, *, n_a):
    i = pl.program_id(0)

    @pl.when(i < n_a)
    def _():
        _outproj_tile(attn_a, ssm_a, gate_a, x_a, wa_ref, ws_ref, wo_ref, gn_ref, wr1_ref, wr2_ref, br_ref,
                      o_ref, r_ref)

    @pl.when(i >= n_a)
    def _():
        _outproj_tile(attn_b, ssm_b, gate_b, x_b, wa_ref, ws_ref, wo_ref, gn_ref, wr1_ref, wr2_ref, br_ref,
                      o_ref, r_ref)


def _outproj_tile(attn_ref, ssm_ref, gate_ref, x_ref, wa_ref, ws_ref, wo_ref, gn_ref, wr1_ref, wr2_ref, br_ref,
                  o_ref, r_ref):
    a_out = _dot(attn_ref[...], wa_ref[...])
    s_out = _dot(ssm_ref[...], ws_ref[...])
    ga = gate_ref[:, :D_MODEL].astype(F32)
    gs = gate_ref[:, D_MODEL:].astype(F32)
    merged = _sigmoid(ga) * a_out + _sigmoid(gs) * s_out
    x2 = x_ref[...] + _dot(merged.astype(BF16), wo_ref[...])
    for j in range(TOKEN_TILE):
        o_ref[pl.ds(j, TM_OUT, stride=TOKEN_TILE), :] = x2[:, j * LANES:(j + 1) * LANES]

    ms = jnp.mean(x2 * x2, axis=-1, keepdims=True)
    hn = x2 * lax.rsqrt(ms + EPS) * gn_ref[...]
    h1 = hn.astype(BF16)
    h2 = (hn - h1.astype(F32)).astype(BF16)
    lg = _dot(h1, wr1_ref[...]) + _dot(h2, wr1_ref[...]) + _dot(h1, wr2_ref[...]) + br_ref[...]

    lane = lax.broadcasted_iota(jnp.int32, (TM_OUT, LANES), 1).astype(F32)
    big = float(LANES)

    def rmax(v):
        return jnp.max(v, axis=-1, keepdims=True)

    def first_lane(mask):
        return jnp.min(jnp.where(mask, lane, big), axis=-1, keepdims=True)

    gl = jnp.where(lane < N_EXPERT_GROUPS, lg, NEG_INF)
    gmax = rmax(gl)
    g_w = 1.0 / jnp.sum(jnp.exp(gl - gmax), axis=-1, keepdims=True)
    gidx = first_lane(gl == gmax)
    base = N_EXPERT_GROUPS + EXPERTS_PER_GROUP * gidx
    el = jnp.where(lane >= base, jnp.where(lane < base + EXPERTS_PER_GROUP, lg, NEG_INF), NEG_INF)
    m1 = rmax(el)
    i1 = first_lane(el == m1)
    el2 = jnp.where(lane == i1, NEG_INF, el)
    m2 = rmax(el2)
    i2 = first_lane(el2 == m2)
    r = jnp.exp(m2 - m1)
    w1 = g_w / (1.0 + r)
    w2 = w1 * r
    j1 = i1 - base
    j2 = i2 - base
    swap = j1 > j2
    e_lo = jnp.where(swap, j2, j1)
    e_hi = jnp.where(swap, j1, j2)
    w_lo = jnp.where(swap, w2, w1)
    w_hi = jnp.where(swap, w1, w2)
    pair = e_lo * (EXPERTS_PER_GROUP - 1) - e_lo * (e_lo - 1.0) * 0.5 + (e_hi - e_lo - 1.0)
    cls = gidx * N_PAIRS + pair
    rows = jnp.where(lane == 0.0, cls, jnp.where(lane == 1.0, w_lo, jnp.where(lane == 2.0, w_hi, 0.0)))
    r_ref[...] = rows.T[:TOKEN_TILE, :]


def _outproj(group_a, group_b, wa, ws, wo, gn, wr1, wr2, br):
    n_a = group_a[3].shape[0] // TM_OUT
    n_b = group_b[3].shape[0] // TM_OUT
    t = (n_a + n_b) * TM_OUT

    def first(i):
        return (jnp.minimum(i, n_a - 1), 0)

    def second(i):
        return (jnp.maximum(i - n_a, 0), 0)

    def const(i):
        return (0, 0)

    def group_specs(tok):
        return [
            pl.BlockSpec((TM_OUT, ATTN_WIDTH), tok),
            pl.BlockSpec((TM_OUT, D_INNER), tok),
            pl.BlockSpec((TM_OUT, 2 * D_MODEL), lambda i: (tok(i)[0], COL_GATE // (2 * D_MODEL))),
            pl.BlockSpec((TM_OUT, D_MODEL), tok),
        ]

    resident = dict(pipeline_mode=pl.Buffered(1))
    return pl.pallas_call(
        functools.partial(_outproj_kernel, n_a=n_a),
        grid=(n_a + n_b,),
        in_specs=group_specs(first) + group_specs(second) + [
            pl.BlockSpec((ATTN_WIDTH, D_MODEL), const, **resident),
            pl.BlockSpec((D_INNER, D_MODEL), const, **resident),
            pl.BlockSpec((D_MODEL, D_MODEL), const, **resident),
            pl.BlockSpec((1, D_MODEL), const),
            pl.BlockSpec((D_MODEL, LANES), const),
            pl.BlockSpec((D_MODEL, LANES), const),
            pl.BlockSpec((1, LANES), const),
        ],
        out_specs=[
            pl.BlockSpec((TM_OUT * TOKEN_TILE, LANES), lambda i: (i, 0)),
            pl.BlockSpec((TOKEN_TILE, TM_OUT), lambda i: (0, i)),
        ],
        out_shape=[
            jax.ShapeDtypeStruct((t * TOKEN_TILE, LANES), F32),
            jax.ShapeDtypeStruct((TOKEN_TILE, t), F32),
        ],
        compiler_params=_params(("arbitrary",)),
        name="outproj_router",
    )(*group_a, *group_b, wa, ws, wo, gn, wr1, wr2, br)


def _moe_kernel(ea_ref, eb_ref, nv_ref, tokc_ref, tokn_ref, roww_ref, x_ref, gn_ref,
                wga_ref, wua_ref, wda_ref, wgb_ref, wub_ref, wdb_ref, o_ref,
                xg_ref, st_ref, gsem, ssem, *, n_blocks):
    i = pl.program_id(0)
    slot = i % 2
    other = 1 - slot

    def tile(idx):
        return pl.ds(pl.multiple_of(idx * TOKEN_TILE, TOKEN_TILE), TOKEN_TILE)

    def gather_copy(tok, r, s):
        return pltpu.make_async_copy(x_ref.at[tile(tok), :], xg_ref.at[s, tile(r), :], gsem.at[s])

    def scatter_copy(tok, r, s):
        return pltpu.make_async_copy(st_ref.at[s, tile(r), :], o_ref.at[tile(tok), :], ssem.at[s])

    def for_rows(n, fn):
        n8 = lax.shift_right_logical(n, 3)

        def body8(g, _):
            for u in range(8):
                fn(g * 8 + u)
            return 0

        def body1(r, _):
            fn(r)
            return 0

        lax.fori_loop(0, n8, body8, 0)
        lax.fori_loop(n8 * 8, n, body1, 0)

    def start_gathers(tok_ref, n, s):
        for_rows(n, lambda r: gather_copy(tok_ref[0, 0, r], r, s).start())

    def wait_gathers(n, s):
        for_rows(n, lambda r: gather_copy(0, 0, s).wait())

    def start_scatters(tok_ref, n, s):
        for_rows(n, lambda r: scatter_copy(tok_ref[0, 0, r], r, s).start())

    def wait_scatters(n, s):
        for_rows(n, lambda r: scatter_copy(0, 0, s).wait())

    @pl.when(i == 0)
    def _():
        xg_ref[...] = jnp.zeros_like(xg_ref)
        start_gathers(tokc_ref, nv_ref[0], 0)

    @pl.when(i + 1 < n_blocks)
    def _():
        start_gathers(tokn_ref, nv_ref[jnp.minimum(i + 1, n_blocks - 1)], other)

    wait_gathers(nv_ref[i], slot)

    @pl.when(i >= 2)
    def _():
        wait_scatters(nv_ref[jnp.maximum(i - 2, 0)], slot)

    @pl.when(nv_ref[i] > 0)
    def _():
        x = jnp.concatenate(
            [xg_ref[slot, pl.ds(j, ROW_BLOCK, stride=TOKEN_TILE), :] for j in range(TOKEN_TILE)], axis=1)
        w_cols = jnp.concatenate([roww_ref[0], jnp.zeros((LANES - TOKEN_TILE, ROW_BLOCK), F32)], axis=0).T
        w_lo = w_cols[:, 0:1]
        w_hi = w_cols[:, 1:2]
        ms = jnp.mean(x * x, axis=-1, keepdims=True)
        hn = (x * lax.rsqrt(ms + EPS) * gn_ref[...]).astype(BF16)

        def expert(wg_ref, wu_ref, wd_ref):
            gte = _dot(hn, wg_ref[0])
            up = _dot(hn, wu_ref[0])
            return _dot((gte * _sigmoid(gte) * up).astype(BF16), wd_ref[0])

        ya = expert(wga_ref, wua_ref, wda_ref)
        yb = expert(wgb_ref, wub_ref, wdb_ref)
        out = x + w_lo * ya + w_hi * yb
        for j in range(TOKEN_TILE):
            st_ref[slot, pl.ds(j, ROW_BLOCK, stride=TOKEN_TILE), :] = out[:, j * LANES:(j + 1) * LANES]

    start_scatters(tokc_ref, nv_ref[i], slot)

    @pl.when(i == n_blocks - 1)
    def _():
        wait_scatters(nv_ref[jnp.maximum(i - 1, 0)], other)
        wait_scatters(nv_ref[i], slot)


def _moe(ea, eb, nvalid, row_tok, row_w, x2t, gn, wg, wu, wd):
    n_blocks = row_tok.shape[0]

    def wa(i, ea, eb, nv):
        return (ea[i], 0, 0)

    def wb(i, ea, eb, nv):
        return (eb[i], 0, 0)

    any_spec = pl.BlockSpec(memory_space=pl.ANY)
    grid_spec = pltpu.PrefetchScalarGridSpec(
        num_scalar_prefetch=3,
        grid=(n_blocks,),
        in_specs=[
            pl.BlockSpec((1, 1, ROW_BLOCK), lambda i, ea, eb, nv: (i, 0, 0), memory_space=pltpu.SMEM),
            pl.BlockSpec((1, 1, ROW_BLOCK), lambda i, ea, eb, nv: (jnp.minimum(i + 1, n_blocks - 1), 0, 0),
                         memory_space=pltpu.SMEM),
            pl.BlockSpec((1, TOKEN_TILE, ROW_BLOCK), lambda i, ea, eb, nv: (i, 0, 0)),
            any_spec,
            pl.BlockSpec((1, D_MODEL), lambda i, ea, eb, nv: (0, 0)),
            pl.BlockSpec((1, D_MODEL, D_EXPERT), wa), pl.BlockSpec((1, D_MODEL, D_EXPERT), wa),
            pl.BlockSpec((1, D_EXPERT, D_MODEL), wa),
            pl.BlockSpec((1, D_MODEL, D_EXPERT), wb), pl.BlockSpec((1, D_MODEL, D_EXPERT), wb),
            pl.BlockSpec((1, D_EXPERT, D_MODEL), wb),
        ],
        out_specs=any_spec,
        scratch_shapes=[
            pltpu.VMEM((2, ROW_BLOCK * TOKEN_TILE, LANES), F32),
            pltpu.VMEM((2, ROW_BLOCK * TOKEN_TILE, LANES), F32),
            pltpu.SemaphoreType.DMA((2,)),
            pltpu.SemaphoreType.DMA((2,)),
        ],
    )
    return pl.pallas_call(
        functools.partial(_moe_kernel, n_blocks=n_blocks),
        grid_spec=grid_spec,
        out_shape=jax.ShapeDtypeStruct(x2t.shape, F32),
        compiler_params=_params(("arbitrary",)),
        name="moe",
    )(ea, eb, nvalid, row_tok, row_tok, row_w, x2t, gn, wg, wu, wd, wg, wu, wd)


def _untile_kernel(x_ref, o_ref):
    for j in range(TOKEN_TILE):
        o_ref[:, j * LANES:(j + 1) * LANES] = x_ref[pl.ds(j, TM_UNTILE, stride=TOKEN_TILE), :]


def _untile(y_tiles, first_token, n_tokens):
    first_block = first_token // TM_UNTILE
    return pl.pallas_call(
        _untile_kernel,
        grid=(n_tokens // TM_UNTILE,),
        in_specs=[pl.BlockSpec((TM_UNTILE * TOKEN_TILE, LANES), lambda i: (first_block + i, 0))],
        out_specs=pl.BlockSpec((TM_UNTILE, D_MODEL), lambda i: (i, 0)),
        out_shape=jax.ShapeDtypeStruct((n_tokens, D_MODEL), F32),
        compiler_params=_params(("arbitrary",)),
        name="untile",
    )(y_tiles)


def _pair_tables():
    lo, hi = [], []
    for a in range(EXPERTS_PER_GROUP):
        for b in range(a + 1, EXPERTS_PER_GROUP):
            lo.append(a)
            hi.append(b)
    return np.asarray(lo, np.int32), np.asarray(hi, np.int32)


def _block_tables(rinfo):
    cls = rinfo[0].astype(jnp.int32)
    t = cls.shape[0]
    n_blocks = t // ROW_BLOCK + N_CLASSES
    sorted_cls, order = lax.sort((cls, jnp.arange(t, dtype=jnp.int32)), num_keys=1)
    class_ids = jnp.arange(N_CLASSES + 1, dtype=jnp.int32)
    starts = jnp.sum((sorted_cls[:, None] < class_ids[None, :]).astype(jnp.int32), axis=0)
    counts = starts[1:] - starts[:-1]
    nblk = (counts + ROW_BLOCK - 1) // ROW_BLOCK
    blk_end = jnp.cumsum(nblk)
    blk_start = blk_end - nblk
    used = blk_end[-1]
    b = jnp.arange(n_blocks, dtype=jnp.int32)
    b_eff = jnp.minimum(b, used - 1)
    c = jnp.sum((blk_end[None, :] <= b_eff[:, None]).astype(jnp.int32), axis=1)
    c = jnp.minimum(c, N_CLASSES - 1)
    off = b_eff - blk_start[c]
    src = starts[c] + off * ROW_BLOCK
    nvalid = jnp.where(b < used, jnp.clip(counts[c] - off * ROW_BLOCK, 0, ROW_BLOCK), 0).astype(jnp.int32)
    pair_lo, pair_hi = _pair_tables()
    grp = c // N_PAIRS
    ea = (grp * EXPERTS_PER_GROUP + jnp.asarray(pair_lo)[c % N_PAIRS]).astype(jnp.int32)
    eb = (grp * EXPERTS_PER_GROUP + jnp.asarray(pair_hi)[c % N_PAIRS]).astype(jnp.int32)
    rows = jnp.clip(src[:, None] + jnp.arange(ROW_BLOCK, dtype=jnp.int32)[None, :], 0, t - 1)
    row_tok = order[rows]
    row_w = jnp.concatenate([rinfo[1][row_tok][:, None, :], rinfo[2][row_tok][:, None, :],
                             jnp.zeros((n_blocks, TOKEN_TILE - 2, ROW_BLOCK), F32)], axis=1)
    return ea, eb, nvalid, row_tok.reshape(n_blocks, 1, ROW_BLOCK), row_w


def _rope_tables(seq):
    inv = 1.0 / (ROPE_THETA ** (jnp.arange(0, HEAD_DIM, 2, dtype=F32) / HEAD_DIM))
    ang = jnp.arange(seq, dtype=F32)[:, None] * inv[None, :]
    cos, sin = jnp.cos(ang), jnp.sin(ang)
    cos128 = jnp.concatenate([cos, cos, cos, cos], axis=-1)
    sin128 = jnp.concatenate([-sin, sin, -sin, sin], axis=-1)
    return cos128, sin128


def _prepare_weights(norm_mix, w_in, q_norm, k_norm, attn_sink, conv_w, conv_b, a_log_fwd, a_log_bwd,
                     dt_bias_fwd, dt_bias_bwd, d_skip, ssm_norm, w_out_attn, w_out_ssm, w_o, norm_ffn,
                     w_router_group, b_router_group, w_router_expert, b_router_expert, w_gate, w_up, w_down):
    o_q = 0
    o_k = o_q + ATTN_WIDTH
    o_v = o_k + KV_WIDTH
    o_z = o_v + KV_WIDTH
    o_xbc = o_z + D_INNER
    o_dtf = o_xbc + CONV_DIM
    o_dtb = o_dtf + N_SSM_HEADS
    o_ga = o_dtb + N_SSM_HEADS
    o_gs = o_ga + D_MODEL
    w = w_in.astype(BF16)
    w_r = jnp.concatenate([
        w[:, o_z:o_z + D_INNER], w[:, o_ga:o_gs + D_MODEL], w[:, o_xbc:o_xbc + CONV_DIM],
        w[:, o_q:o_q + ATTN_WIDTH], w[:, o_k:o_k + KV_WIDTH], w[:, o_v:o_v + KV_WIDTH],
        w[:, o_dtf:o_dtb + N_SSM_HEADS], jnp.zeros((D_MODEL, LANES - 2 * N_SSM_HEADS), w.dtype)], axis=1)
    pad64 = jnp.zeros((LANES - 2 * N_SSM_HEADS,), F32)
    eye = np.kron(np.eye(2, dtype=np.float32), np.ones((HEAD_DIM, HEAD_DIM), np.float32))
    idx = np.arange(CHUNK)
    w_router = jnp.concatenate([w_router_group, w_router_expert,
                                jnp.zeros((D_MODEL, LANES - N_EXPERT_GROUPS - N_EXPERTS), F32)], axis=1)
    wr1 = w_router.astype(BF16)
    return dict(
        norm_mix=norm_mix.reshape(1, D_MODEL),
        w_in=w_r,
        qg128=jnp.tile(q_norm, 2).reshape(1, LANES),
        kg128=jnp.tile(k_norm, 2).reshape(1, LANES),
        seg=jnp.asarray(eye, BF16),
        sink=attn_sink.astype(F32),
        conv_w=conv_w,
        conv_b=conv_b.reshape(1, CONV_DIM),
        alog128=jnp.concatenate([a_log_fwd, a_log_bwd, pad64]).reshape(1, LANES),
        bias128=jnp.concatenate([dt_bias_fwd, dt_bias_bwd, pad64]).reshape(1, LANES),
        tri_l=jnp.asarray(idx[:, None] >= idx[None, :], BF16),
        tri_u=jnp.asarray(idx[:, None] <= idx[None, :], BF16),
        dskip=jnp.repeat(d_skip, SSM_HEAD_DIM).reshape(1, D_INNER),
        ssm_norm=ssm_norm.reshape(1, D_INNER),
        wa=w_out_attn.astype(BF16), ws=w_out_ssm.astype(BF16), wo=w_o.astype(BF16),
        norm_ffn=norm_ffn.reshape(1, D_MODEL),
        wr1=wr1, wr2=(w_router - wr1.astype(F32)).astype(BF16),
        br=jnp.concatenate([b_router_group, b_router_expert,
                            jnp.zeros((LANES - N_EXPERT_GROUPS - N_EXPERTS,), F32)]).reshape(1, LANES),
        wg=w_gate.astype(BF16), wu=w_up.astype(BF16), wd=w_down.astype(BF16),
    )


def _mixer(x, p):
    batch, seq, _ = x.shape
    x2d = x.reshape(batch * seq, D_MODEL)
    cos128, sin128 = _rope_tables(seq)
    proj, dt, qr, kdup, vdup = _inproj(x2d, p['norm_mix'], p['w_in'], cos128, sin128, p['qg128'], p['kg128'],
                                       p['seg'], seq)
    attn = _attention(qr, kdup, vdup, p['sink'], batch, seq)
    xc = _conv(proj, p['conv_w'], p['conv_b'], batch, seq)
    hb = _ssd_bwd_states(xc, dt, p['bias128'], p['alog128'], p['tri_l'], batch, seq)
    ssm = _ssd_main(xc, proj, dt, hb, p['bias128'], p['alog128'], p['tri_l'], p['tri_u'], p['dskip'],
                    p['ssm_norm'], batch, seq)
    return attn, ssm, proj, x2d


def kernel(x_prompt, x_sample, norm_mix, w_in, q_norm, k_norm, attn_sink, conv_w, conv_b, a_log_fwd, a_log_bwd,
           dt_bias_fwd, dt_bias_bwd, d_skip, ssm_norm, w_out_attn, w_out_ssm, w_o, norm_ffn, w_router_group,
           b_router_group, w_router_expert, b_router_expert, w_gate, w_up, w_down):
    assert norm_mix.shape[0] == 1, "single-layer encoder"
    p = _prepare_weights(norm_mix[0], w_in[0], q_norm[0], k_norm[0], attn_sink[0], conv_w[0], conv_b[0],
                         a_log_fwd[0], a_log_bwd[0], dt_bias_fwd[0], dt_bias_bwd[0], d_skip[0], ssm_norm[0],
                         w_out_attn[0], w_out_ssm[0], w_o[0], norm_ffn[0], w_router_group[0], b_router_group[0],
                         w_router_expert[0], b_router_expert[0], w_gate[0], w_up[0], w_down[0])
    x2t, rinfo = _outproj(_mixer(x_prompt, p), _mixer(x_sample, p), p['wa'], p['ws'], p['wo'], p['norm_ffn'],
                          p['wr1'], p['wr2'], p['br'])
    ea, eb, nvalid, row_tok, row_w = _block_tables(rinfo)
    y = _moe(ea, eb, nvalid, row_tok, row_w, x2t, p['norm_ffn'], p['wg'], p['wu'], p['wd'])
    t_a = x_prompt.shape[0] * x_prompt.shape[1]
    t_b = x_sample.shape[0] * x_sample.shape[1]
    return _untile(y, 0, t_a).reshape(x_prompt.shape), _untile(y, t_a, t_b).reshape(x_sample.shape)
```

```python
import functools

import numpy as np
import jax
import jax.numpy as jnp
from jax import lax
from jax.experimental import pallas as pl
from jax.experimental.pallas import tpu as pltpu

F32 = jnp.float32
BF16 = jnp.bfloat16

D_MODEL = 1024
EPS = 1e-6
NEG_INF = -1e30
LOG2_E = 1.4426950408889634
N_Q_HEADS = 16
N_KV_HEADS = 4
HEAD_DIM = 64
ATTN_WIDTH = N_Q_HEADS * HEAD_DIM
KV_WIDTH = N_KV_HEADS * HEAD_DIM
ATTN_BLOCK = 128
ATTN_QB = 4
ROPE_THETA = 10000.0
D_INNER = 2 * D_MODEL
SSM_HEAD_DIM = 64
N_SSM_HEADS = D_INNER // SSM_HEAD_DIM
N_SSM_GROUPS = 4
HEADS_PER_GROUP = N_SSM_HEADS // N_SSM_GROUPS
D_STATE = 128
BC_WIDTH = N_SSM_GROUPS * D_STATE
CONV_DIM = D_INNER + 2 * BC_WIDTH
CONV_W = 7
CHUNK = 128
SSD_BWD_CHUNKS = 8
SSD_MAIN_CHUNKS = 4
N_EXPERT_GROUPS = 4
EXPERTS_PER_GROUP = 8
N_EXPERTS = N_EXPERT_GROUPS * EXPERTS_PER_GROUP
D_EXPERT = 512
N_PAIRS = EXPERTS_PER_GROUP * (EXPERTS_PER_GROUP - 1) // 2
N_CLASSES = N_EXPERT_GROUPS * N_PAIRS

LANES = 128
V7X_VMEM_LIMIT_BYTES = 56 * 1024 * 1024

COL_Z = 0
COL_GATE = COL_Z + D_INNER
COL_XS = COL_GATE + 2 * D_MODEL
COL_B = COL_XS + D_INNER
COL_C = COL_B + BC_WIDTH
COL_Q = COL_C + BC_WIDTH
COL_K = COL_Q + ATTN_WIDTH
COL_V = COL_K + KV_WIDTH
COL_DT = COL_V + KV_WIDTH
N_PROJ = COL_DT + LANES

TM_IN = 1024
NJ_IN = 3
TN_IN = N_PROJ // NJ_IN
CH_IN = 512
TM_OUT = 512
TM_UNTILE = 1024
CONV_CT = 512
CONV_ROWS = 256
CONV_PITCH = 2
ROW_BLOCK = 128
MOE_ISSUE_PHASES = 8
TOKEN_TILE = D_MODEL // LANES


def _params(sem):
    return pltpu.CompilerParams(dimension_semantics=sem, vmem_limit_bytes=V7X_VMEM_LIMIT_BYTES)


def _dot(a, b):
    return jnp.dot(a, b, preferred_element_type=F32)


def _dot_nt(a, b):
    return lax.dot_general(a, b, (((1,), (1,)), ((), ())), preferred_element_type=F32)


def _sigmoid(x):
    return 1.0 / (1.0 + jnp.exp(-x))


def _inproj_kernel(x_ref, g_ref, w_ref, cos_ref, sin_ref, qg_ref, kg_ref, seg_ref,
                   o_ref, dt_ref, qo_ref, ko_ref, vo_ref, h_ref):
    j = pl.program_id(1)

    @pl.when(j == 0)
    def _():
        x = x_ref[...]
        ms = jnp.mean(x * x, axis=-1, keepdims=True)
        h_ref[...] = (x * lax.rsqrt(ms + EPS) * g_ref[...]).astype(BF16)

    def project(c0, c1):
        acc = _dot(h_ref[...], w_ref[:, c0:c1])
        o_ref[:, c0:c1] = acc.astype(BF16)
        return acc

    @pl.when(j < NJ_IN - 1)
    def _():
        for c0 in range(0, TN_IN, CH_IN):
            project(c0, min(c0 + CH_IN, TN_IN))

    @pl.when(j == NJ_IN - 1)
    def _():
        base = (NJ_IN - 1) * TN_IN
        q0, k0, d0 = COL_Q - base, COL_K - base, COL_DT - base
        for c0 in range(0, q0, CH_IN):
            project(c0, min(c0 + CH_IN, q0))
        cos = cos_ref[...]
        sin = sin_ref[...]
        seg = seg_ref[...]
        lane = lax.broadcasted_iota(jnp.int32, (TM_IN, LANES), 1)
        first_half = (lane % HEAD_DIM) < (HEAD_DIM // 2)
        low = lane < HEAD_DIM

        def norm_rope(x, gain):
            ss = _dot((x * x).astype(BF16), seg)
            xn = x * lax.rsqrt(ss * (1.0 / HEAD_DIM) + EPS) * gain
            rot = jnp.where(first_half, pltpu.roll(xn, 96, 1), pltpu.roll(xn, 32, 1))
            return xn * cos + rot * sin

        def duplicate(y, dst_ref, s):
            ysw = pltpu.roll(y, HEAD_DIM, 1)
            dst_ref[:, (2 * s) * LANES:(2 * s + 1) * LANES] = jnp.where(low, y, ysw).astype(BF16)
            dst_ref[:, (2 * s + 1) * LANES:(2 * s + 2) * LANES] = jnp.where(low, ysw, y).astype(BF16)

        for c0 in range(q0, k0, CH_IN):
            acc = project(c0, c0 + CH_IN)
            for s in range(CH_IN // LANES):
                y = norm_rope(acc[:, s * LANES:(s + 1) * LANES], qg_ref[...]) * (HEAD_DIM ** -0.5 * LOG2_E)
                dst = c0 - q0 + s * LANES
                qo_ref[:, dst:dst + LANES] = y.astype(BF16)
        acc = project(k0, d0)
        for s in range(KV_WIDTH // LANES):
            duplicate(norm_rope(acc[:, s * LANES:(s + 1) * LANES], kg_ref[...]), ko_ref, s)
            duplicate(acc[:, KV_WIDTH + s * LANES:KV_WIDTH + (s + 1) * LANES], vo_ref, s)
        dt_ref[...] = project(d0, TN_IN)


def _inproj(x2d, gain, w_bf16, cos128, sin128, qg128, kg128, seg, seq):
    t = x2d.shape[0]
    nseq = seq // TM_IN

    def rows(i, j):
        return (i, 0)

    def const(i, j):
        return (0, 0)

    def pos(i, j):
        return (i % nseq, 0)

    return pl.pallas_call(
        _inproj_kernel,
        grid=(t // TM_IN, NJ_IN),
        in_specs=[
            pl.BlockSpec((TM_IN, D_MODEL), rows),
            pl.BlockSpec((1, D_MODEL), const),
            pl.BlockSpec((D_MODEL, TN_IN), lambda i, j: (0, j)),
            pl.BlockSpec((TM_IN, LANES), pos),
            pl.BlockSpec((TM_IN, LANES), pos),
            pl.BlockSpec((1, LANES), const),
            pl.BlockSpec((1, LANES), const),
            pl.BlockSpec((LANES, LANES), const),
        ],
        out_specs=[
            pl.BlockSpec((TM_IN, TN_IN), lambda i, j: (i, j)),
            pl.BlockSpec((TM_IN, LANES), rows),
            pl.BlockSpec((TM_IN, ATTN_WIDTH), rows),
            pl.BlockSpec((TM_IN, 2 * KV_WIDTH), rows),
            pl.BlockSpec((TM_IN, 2 * KV_WIDTH), rows),
        ],
        out_shape=[
            jax.ShapeDtypeStruct((t, N_PROJ), BF16),
            jax.ShapeDtypeStruct((t, LANES), F32),
            jax.ShapeDtypeStruct((t, ATTN_WIDTH), BF16),
            jax.ShapeDtypeStruct((t, 2 * KV_WIDTH), BF16),
            jax.ShapeDtypeStruct((t, 2 * KV_WIDTH), BF16),
        ],
        scratch_shapes=[pltpu.VMEM((TM_IN, D_MODEL), BF16)],
        compiler_params=_params(("arbitrary", "arbitrary")),
        name="inproj",
    )(x2d, gain, w_bf16, cos128, sin128, qg128, kg128, seg)


def _attn_kernel(sink_ref, q_ref, kp_ref, kc_ref, kn_ref, vp_ref, vc_ref, vn_ref, o_ref, *, n_steps):
    i = pl.program_id(1)
    nb = ATTN_BLOCK
    nk = 3 * nb
    qi = lax.broadcasted_iota(jnp.int32, (nb, nk), 0)
    si = lax.broadcasted_iota(jnp.int32, (nb, nk), 1)
    rel = qi - (si - nb)
    band = jnp.where(rel <= nb, jnp.where(rel >= -nb, 0.0, NEG_INF), NEG_INF)
    bias_first = jnp.where(si < nb, jnp.where(i > 0, band, NEG_INF), band)
    bias_last = jnp.where(si >= 2 * nb, jnp.where(i < n_steps - 1, band, NEG_INF), band)
    low_q = lax.broadcasted_iota(jnp.int32, (nb, LANES), 1) < HEAD_DIM
    low_k = lax.broadcasted_iota(jnp.int32, (nk, LANES), 1) < HEAD_DIM
    zero_q = jnp.zeros((nb, LANES), BF16)
    zero_k = jnp.zeros((nk, LANES), BF16)

    def window(p_ref, c_ref, n_ref, h, j):
        sl = slice(h * LANES, (h + 1) * LANES)
        rows = jnp.concatenate([p_ref[:, sl], c_ref[:, sl], n_ref[:, sl]], axis=0)
        return rows[j * nb:j * nb + nk]

    def scores(j, h):
        kd = window(kp_ref, kc_ref, kn_ref, h, j)
        slabs = [q_ref[j * nb:(j + 1) * nb, (2 * h + u) * LANES:(2 * h + u + 1) * LANES] for u in range(2)]
        q4 = jnp.concatenate([jnp.where(low_q, s_, zero_q) for s_ in slabs]
                             + [jnp.where(low_q, zero_q, s_) for s_ in slabs], axis=0)
        return _dot_nt(q4, kd)

    def finish(j, h, s4):
        vd = window(vp_ref, vc_ref, vn_ref, h, j)
        v_lo = jnp.where(low_k, vd, zero_k)
        v_hi = jnp.where(low_k, zero_k, vd)
        heads = (4 * h, 4 * h + 2, 4 * h + 1, 4 * h + 3)
        ps, invs = [], []
        for k, head in enumerate(heads):
            s = s4[k * nb:(k + 1) * nb]
            left = s[:, :nb] + (bias_first if j == 0 else band)[:, :nb]
            right = s[:, 2 * nb:] + (bias_last if j == ATTN_QB - 1 else band)[:, 2 * nb:]
            s = jnp.concatenate([left, s[:, nb:2 * nb], right], axis=1)
            snk = sink_ref[head] * LOG2_E
            m = jnp.maximum(jnp.max(s, axis=-1, keepdims=True), snk)
            p = jnp.exp2(s - m)
            den = jnp.sum(p, axis=-1, keepdims=True) + jnp.exp2(snk - m)
            ps.append(p.astype(BF16))
            invs.append(1.0 / den)
        o = _dot(jnp.concatenate(ps[:2], axis=0), v_lo) + _dot(jnp.concatenate(ps[2:], axis=0), v_hi)
        for u in range(2):
            ou = o[u * nb:(u + 1) * nb] * jnp.where(low_q, invs[u], invs[2 + u])
            o_ref[j * nb:(j + 1) * nb, (2 * h + u) * LANES:(2 * h + u + 1) * LANES] = ou.astype(BF16)

    tasks = [(j, h) for j in range(ATTN_QB) for h in range(N_KV_HEADS)]
    pending = scores(*tasks[0])
    for n, task in enumerate(tasks):
        following = scores(*tasks[n + 1]) if n + 1 < len(tasks) else None
        finish(*task, pending)
        pending = following


def _attention(qr, kdup, vdup, sink, batch, seq):
    t = qr.shape[0]
    nq = seq // ATTN_BLOCK
    n_steps = nq // ATTN_QB
    rows = ATTN_QB * ATTN_BLOCK

    def prev(b, i, s):
        return (b * nq + jnp.maximum(i * ATTN_QB - 1, 0), 0)

    def cur(b, i, s):
        return (b * n_steps + i, 0)

    def nxt(b, i, s):
        return (b * nq + jnp.minimum((i + 1) * ATTN_QB, nq - 1), 0)

    edge = (ATTN_BLOCK, 2 * KV_WIDTH)
    mid = (rows, 2 * KV_WIDTH)
    grid_spec = pltpu.PrefetchScalarGridSpec(
        num_scalar_prefetch=1,
        grid=(batch, n_steps),
        in_specs=[
            pl.BlockSpec((rows, ATTN_WIDTH), cur),
            pl.BlockSpec(edge, prev), pl.BlockSpec(mid, cur), pl.BlockSpec(edge, nxt),
            pl.BlockSpec(edge, prev), pl.BlockSpec(mid, cur), pl.BlockSpec(edge, nxt),
        ],
        out_specs=pl.BlockSpec((rows, ATTN_WIDTH), cur),
    )
    return pl.pallas_call(
        functools.partial(_attn_kernel, n_steps=n_steps),
        grid_spec=grid_spec,
        out_shape=jax.ShapeDtypeStruct((t, ATTN_WIDTH), BF16),
        compiler_params=_params(("arbitrary", "arbitrary")),
        name="attention",
    )(sink, qr, kdup, kdup, kdup, vdup, vdup, vdup)


def _conv_kernel(x_ref, w_ref, b_ref, o_ref, pad_ref, *, seq):
    halo = 8
    step = CONV_PITCH

    def rows(first, n):
        return pl.ds(step * (first + halo), n, stride=step)

    for h in range(CONV_CT // LANES):
        lanes = slice(h * LANES, (h + 1) * LANES)
        pad_ref[h, rows(-halo, halo), :] = jnp.zeros((halo, LANES), F32)
        pad_ref[h, rows(seq, halo), :] = jnp.zeros((halo, LANES), F32)
        for r in range(seq // CONV_ROWS):
            pad_ref[h, rows(r * CONV_ROWS, CONV_ROWS), :] = x_ref[r * CONV_ROWS:(r + 1) * CONV_ROWS, lanes].astype(F32)
    w = w_ref[...]
    bias = b_ref[...]
    for h in range(CONV_CT // LANES):
        lanes = slice(h * LANES, (h + 1) * LANES)
        for r in range(seq // CONV_ROWS):
            r0 = r * CONV_ROWS
            acc = jnp.broadcast_to(bias[:, lanes], (CONV_ROWS, LANES))
            for k in range(CONV_W):
                acc = acc + pad_ref[h, rows(r0 + k - CONV_W // 2, CONV_ROWS), :] * w[k:k + 1, lanes]
            o_ref[r0:r0 + CONV_ROWS, lanes] = (acc * _sigmoid(acc)).astype(BF16)


def _conv(proj, conv_w, conv_b, batch, seq):
    t = proj.shape[0]
    return pl.pallas_call(
        functools.partial(_conv_kernel, seq=seq),
        grid=(batch, CONV_DIM // CONV_CT),
        in_specs=[
            pl.BlockSpec((seq, CONV_CT), lambda b, c: (b, COL_XS // CONV_CT + c)),
            pl.BlockSpec((CONV_W, CONV_CT), lambda b, c: (0, c)),
            pl.BlockSpec((1, CONV_CT), lambda b, c: (0, c)),
        ],
        out_specs=pl.BlockSpec((seq, CONV_CT), lambda b, c: (b, c)),
        out_shape=jax.ShapeDtypeStruct((t, CONV_DIM), BF16),
        scratch_shapes=[pltpu.VMEM((CONV_CT // LANES, CONV_PITCH * (seq + 16), LANES), F32)],
        compiler_params=_params(("arbitrary", "arbitrary")),
        name="conv",
    )(proj, conv_w, conv_b)


def _split3(a):
    a1 = a.astype(BF16)
    r1 = a - a1.astype(F32)
    a2 = r1.astype(BF16)
    a3 = (r1 - a2.astype(F32)).astype(BF16)
    return a1, a2, a3


def _tri_matmul(tri, a):
    a1, a2, a3 = _split3(a)
    return _dot(tri, a1) + _dot(tri, a2) + _dot(tri, a3)


def _softplus(x):
    return jnp.maximum(x, 0.0) + jnp.log(1.0 + jnp.exp(-jnp.abs(x)))


def _dt_and_rate(dt_ref, bias_ref, alog_ref):
    dt = _softplus(dt_ref[...] + bias_ref[...])
    rate = dt * (-LOG2_E * jnp.exp(alog_ref[...]))
    return dt, rate


def _head_rows(mat, first, rows):
    n = mat.shape[1]
    return jnp.concatenate(
        [jnp.broadcast_to(mat[first + e:first + e + 1, :], (rows, n)) for e in range(HEADS_PER_GROUP)], axis=0)


def _ssd_bwd_state_kernel(xs_ref, b_ref, dt_ref, bias_ref, alog_ref, tl_ref, hb_ref, st_ref):
    c = pl.program_id(1)

    @pl.when(c == 0)
    def _():
        st_ref[...] = jnp.zeros_like(st_ref)

    for ci in reversed(range(SSD_BWD_CHUNKS)):
        rows = pl.ds(ci * CHUNK, CHUNK)
        _ssd_bwd_chunk(xs_ref.at[rows, :], b_ref.at[rows, :], dt_ref.at[rows, :], bias_ref, alog_ref, tl_ref,
                       hb_ref.at[ci], st_ref)


def _ssd_bwd_chunk(xs_ref, b_ref, dt_ref, bias_ref, alog_ref, tl_ref, hb_ref, st_ref):
    hb_ref[...] = st_ref[...].astype(BF16)
    dt, rate = _dt_and_rate(dt_ref, bias_ref, alog_ref)
    pre = _tri_matmul(tl_ref[...], rate)
    pre_t = pre.T
    excl_t = (pre - rate).T
    total = jnp.broadcast_to(pre_t[:, CHUNK - 1:CHUNK], (LANES, CHUNK))
    w_t = dt.T * jnp.exp2(excl_t)
    dec = jnp.exp2(total)
    off = N_SSM_HEADS
    for g in range(N_SSM_GROUPS):
        xs_t = xs_ref[:, g * 512:(g + 1) * 512].astype(F32).T
        xd = (xs_t * _head_rows(w_t, off + g * HEADS_PER_GROUP, SSM_HEAD_DIM)).astype(BF16)
        upd = _dot(xd, b_ref[:, g * D_STATE:(g + 1) * D_STATE])
        st_ref[g] = _head_rows(dec, off + g * HEADS_PER_GROUP, SSM_HEAD_DIM) * st_ref[g] + upd


def _ssd_bwd_states(xc, dt, bias128, alog128, tri_l, batch, seq):
    nc = seq // CHUNK
    ns = nc // SSD_BWD_CHUNKS
    rows = SSD_BWD_CHUNKS * CHUNK

    def rev(b, c):
        return (b * ns + ns - 1 - c, 0)

    return pl.pallas_call(
        _ssd_bwd_state_kernel,
        grid=(batch, ns),
        in_specs=[
            pl.BlockSpec((rows, D_INNER), rev),
            pl.BlockSpec((rows, BC_WIDTH), lambda b, c: (b * ns + ns - 1 - c, D_INNER // BC_WIDTH)),
            pl.BlockSpec((rows, LANES), rev),
            pl.BlockSpec((1, LANES), lambda b, c: (0, 0)),
            pl.BlockSpec((1, LANES), lambda b, c: (0, 0)),
            pl.BlockSpec((CHUNK, CHUNK), lambda b, c: (0, 0)),
        ],
        out_specs=pl.BlockSpec((SSD_BWD_CHUNKS, N_SSM_GROUPS, 512, D_STATE),
                               lambda b, c: (b * ns + ns - 1 - c, 0, 0, 0)),
        out_shape=jax.ShapeDtypeStruct((batch * nc, N_SSM_GROUPS, 512, D_STATE), BF16),
        scratch_shapes=[pltpu.VMEM((N_SSM_GROUPS, 512, D_STATE), F32)],
        compiler_params=_params(("arbitrary", "arbitrary")),
        name="ssd_bwd_states",
    )(xc, xc, dt, bias128, alog128, tri_l)


def _ssd_main_kernel(xc_ref, z_ref, dt_ref, hb_ref, bias_ref, alog_ref, tl_ref, tu_ref, dskip_ref, gain_ref,
                     o_ref, hf_ref, y_ref):
    c = pl.program_id(1)

    @pl.when(c == 0)
    def _():
        hf_ref[...] = jnp.zeros_like(hf_ref)

    for ci in range(SSD_MAIN_CHUNKS):
        rows = pl.ds(ci * CHUNK, CHUNK)
        _ssd_main_chunk(xc_ref.at[rows, :], z_ref.at[rows, :], dt_ref.at[rows, :], hb_ref.at[ci], bias_ref, alog_ref,
                        tl_ref, tu_ref, dskip_ref, gain_ref, o_ref.at[rows, :], hf_ref, y_ref)


def _ssd_main_chunk(xc_ref, z_ref, dt_ref, hb_ref, bias_ref, alog_ref, tl_ref, tu_ref, dskip_ref, gain_ref,
                    o_ref, hf_ref, y_ref):
    dt, rate = _dt_and_rate(dt_ref, bias_ref, alog_ref)
    lane = lax.broadcasted_iota(jnp.int32, (CHUNK, LANES), 1)
    cum = jnp.where(lane < N_SSM_HEADS, _tri_matmul(tl_ref[...], rate), _tri_matmul(tu_ref[...], rate))
    cum_t = cum.T
    dt_t = dt.T
    src_t = cum_t - jnp.log2(dt_t)
    row = lax.broadcasted_iota(jnp.int32, (CHUNK, CHUNK), 0)
    col = lax.broadcasted_iota(jnp.int32, (CHUNK, CHUNK), 1)
    lower = row >= col
    diag = row == col
    low = lane < SSM_HEAD_DIM
    zero_x = jnp.zeros((CHUNK, LANES), BF16)
    nb = N_SSM_HEADS

    def lane_bcast(mat, idx):
        return jnp.broadcast_to(mat[:, idx:idx + 1], (CHUNK, CHUNK))

    def sub_bcast(mat, idx):
        return jnp.broadcast_to(mat[idx:idx + 1, :], (CHUNK, CHUNK))

    def head_matrix(e, cb):
        col_f = lane_bcast(cum, e)
        col_b = lane_bcast(cum, nb + e)
        decay = jnp.exp2(jnp.where(lower, col_f - sub_bcast(src_t, e), col_b - sub_bcast(src_t, nb + e)))
        decay = decay + jnp.where(diag, sub_bcast(dt_t, nb + e), 0.0)
        return (decay * cb).astype(BF16), col_f, col_b

    for g in range(N_SSM_GROUPS):
        bg = xc_ref[:, D_INNER + g * D_STATE:D_INNER + (g + 1) * D_STATE]
        cg = xc_ref[:, D_INNER + BC_WIDTH + g * D_STATE:D_INNER + BC_WIDTH + (g + 1) * D_STATE]
        cb = _dot_nt(cg, bg)
        y_in_f = _dot_nt(cg, hf_ref[g].astype(BF16))
        y_in_b = _dot_nt(cg, hb_ref[g])
        for jp in range(HEADS_PER_GROUP // 2):
            e0 = g * HEADS_PER_GROUP + 2 * jp
            cols = slice(e0 * SSM_HEAD_DIM, e0 * SSM_HEAD_DIM + LANES)
            loc = slice(jp * LANES, (jp + 1) * LANES)
            xs_pair = xc_ref[:, cols]
            m0, cf0, cb0 = head_matrix(e0, cb)
            m1, cf1, cb1 = head_matrix(e0 + 1, cb)
            y = _dot(m0, jnp.where(low, xs_pair, zero_x)) + _dot(m1, jnp.where(low, zero_x, xs_pair))
            y = y + y_in_f[:, loc] * jnp.exp2(jnp.where(low, cf0, cf1))
            y = y + y_in_b[:, loc] * jnp.exp2(jnp.where(low, cb0, cb1))
            y_ref[:, cols] = y + dskip_ref[:, cols] * xs_pair.astype(F32)

    z = z_ref[...].astype(F32)
    y = y_ref[...] * (z * _sigmoid(z))
    ms = jnp.mean(y * y, axis=-1, keepdims=True)
    o_ref[...] = (y * lax.rsqrt(ms + EPS) * gain_ref[...]).astype(BF16)

    last = jnp.broadcast_to(cum_t[:, CHUNK - 1:CHUNK], (LANES, CHUNK))
    w_t = jnp.exp2(last - src_t)
    dec = jnp.exp2(last)
    for g in range(N_SSM_GROUPS):
        xs_t = xc_ref[:, g * 512:(g + 1) * 512].astype(F32).T
        xd = (xs_t * _head_rows(w_t, g * HEADS_PER_GROUP, SSM_HEAD_DIM)).astype(BF16)
        upd = _dot(xd, xc_ref[:, D_INNER + g * D_STATE:D_INNER + (g + 1) * D_STATE])
        hf_ref[g] = _head_rows(dec, g * HEADS_PER_GROUP, SSM_HEAD_DIM) * hf_ref[g] + upd


def _ssd_main(xc, proj, dt, hb, bias128, alog128, tri_l, tri_u, dskip, gain, batch, seq):
    t = xc.shape[0]
    ns = seq // CHUNK // SSD_MAIN_CHUNKS
    rows = SSD_MAIN_CHUNKS * CHUNK

    def tok(b, c):
        return (b * ns + c, 0)

    def const(b, c):
        return (0, 0)

    return pl.pallas_call(
        _ssd_main_kernel,
        grid=(batch, ns),
        in_specs=[
            pl.BlockSpec((rows, CONV_DIM), tok),
            pl.BlockSpec((rows, D_INNER), tok),
            pl.BlockSpec((rows, LANES), tok),
            pl.BlockSpec((SSD_MAIN_CHUNKS, N_SSM_GROUPS, 512, D_STATE), lambda b, c: (b * ns + c, 0, 0, 0)),
            pl.BlockSpec((1, LANES), const),
            pl.BlockSpec((1, LANES), const),
            pl.BlockSpec((CHUNK, CHUNK), const),
            pl.BlockSpec((CHUNK, CHUNK), const),
            pl.BlockSpec((1, D_INNER), const),
            pl.BlockSpec((1, D_INNER), const),
        ],
        out_specs=pl.BlockSpec((rows, D_INNER), tok),
        out_shape=jax.ShapeDtypeStruct((t, D_INNER), BF16),
        scratch_shapes=[pltpu.VMEM((N_SSM_GROUPS, 512, D_STATE), F32), pltpu.VMEM((CHUNK, D_INNER), F32)],
        compiler_params=_params(("arbitrary", "arbitrary")),
        name="ssd_main",
    )(xc, proj, dt, hb, bias128, alog128, tri_l, tri_u, dskip, gain)


def _outproj_kernel(attn_a, ssm_a, gate_a, x_a, attn_b, ssm_b, gate_b, x_b, wa_ref, ws_ref, wo_ref, gn_ref,
                    wr1_ref, wr2_ref, br_ref, o_ref, r_ref, *, n_a):
    i = pl.program_id(0)

    @pl.when(i < n_a)
    def _():
        _outproj_tile(attn_a, ssm_a, gate_a, x_a, wa_ref, ws_ref, wo_ref, gn_ref, wr1_ref, wr2_ref, br_ref,
                      o_ref, r_ref)

    @pl.when(i >= n_a)
    def _():
        _outproj_tile(attn_b, ssm_b, gate_b, x_b, wa_ref, ws_ref, wo_ref, gn_ref, wr1_ref, wr2_ref, br_ref,
                      o_ref, r_ref)


def _outproj_tile(attn_ref, ssm_ref, gate_ref, x_ref, wa_ref, ws_ref, wo_ref, gn_ref, wr1_ref, wr2_ref, br_ref,
                  o_ref, r_ref):
    a_out = _dot(attn_ref[...], wa_ref[...])
    s_out = _dot(ssm_ref[...], ws_ref[...])
    ga = gate_ref[:, :D_MODEL].astype(F32)
    gs = gate_ref[:, D_MODEL:].astype(F32)
    merged = _sigmoid(ga) * a_out + _sigmoid(gs) * s_out
    x2 = x_ref[...] + _dot(merged.astype(BF16), wo_ref[...])
    for j in range(TOKEN_TILE):
        o_ref[pl.ds(j, TM_OUT, stride=TOKEN_TILE), :] = x2[:, j * LANES:(j + 1) * LANES]

    ms = jnp.mean(x2 * x2, axis=-1, keepdims=True)
    hn = x2 * lax.rsqrt(ms + EPS) * gn_ref[...]
    h1 = hn.astype(BF16)
    h2 = (hn - h1.astype(F32)).astype(BF16)
    lg = _dot(h1, wr1_ref[...]) + _dot(h2, wr1_ref[...]) + _dot(h1, wr2_ref[...]) + br_ref[...]

    lane = lax.broadcasted_iota(jnp.int32, (TM_OUT, LANES), 1).astype(F32)
    big = float(LANES)

    def rmax(v):
        return jnp.max(v, axis=-1, keepdims=True)

    def first_lane(mask):
        return jnp.min(jnp.where(mask, lane, big), axis=-1, keepdims=True)

    gl = jnp.where(lane < N_EXPERT_GROUPS, lg, NEG_INF)
    gmax = rmax(gl)
    g_w = 1.0 / jnp.sum(jnp.exp(gl - gmax), axis=-1, keepdims=True)
    gidx = first_lane(gl == gmax)
    base = N_EXPERT_GROUPS + EXPERTS_PER_GROUP * gidx
    el = jnp.where(lane >= base, jnp.where(lane < base + EXPERTS_PER_GROUP, lg, NEG_INF), NEG_INF)
    m1 = rmax(el)
    i1 = first_lane(el == m1)
    el2 = jnp.where(lane == i1, NEG_INF, el)
    m2 = rmax(el2)
    i2 = first_lane(el2 == m2)
    r = jnp.exp(m2 - m1)
    w1 = g_w / (1.0 + r)
    w2 = w1 * r
    j1 = i1 - base
    j2 = i2 - base
    swap = j1 > j2
    e_lo = jnp.where(swap, j2, j1)
    e_hi = jnp.where(swap, j1, j2)
    w_lo = jnp.where(swap, w2, w1)
    w_hi = jnp.where(swap, w1, w2)
    pair = e_lo * (EXPERTS_PER_GROUP - 1) - e_lo * (e_lo - 1.0) * 0.5 + (e_hi - e_lo - 1.0)
    cls = gidx * N_PAIRS + pair
    rows = jnp.where(lane == 0.0, cls, jnp.where(lane == 1.0, w_lo, jnp.where(lane == 2.0, w_hi, 0.0)))
    r_ref[...] = rows.T[:TOKEN_TILE, :]


def _outproj(group_a, group_b, wa, ws, wo, gn, wr1, wr2, br):
    n_a = group_a[3].shape[0] // TM_OUT
    n_b = group_b[3].shape[0] // TM_OUT
    t = (n_a + n_b) * TM_OUT

    def first(i):
        return (jnp.minimum(i, n_a - 1), 0)

    def second(i):
        return (jnp.maximum(i - n_a, 0), 0)

    def const(i):
        return (0, 0)

    def group_specs(tok):
        return [
            pl.BlockSpec((TM_OUT, ATTN_WIDTH), tok),
            pl.BlockSpec((TM_OUT, D_INNER), tok),
            pl.BlockSpec((TM_OUT, 2 * D_MODEL), lambda i: (tok(i)[0], COL_GATE // (2 * D_MODEL))),
            pl.BlockSpec((TM_OUT, D_MODEL), tok),
        ]

    resident = dict(pipeline_mode=pl.Buffered(1))
    return pl.pallas_call(
        functools.partial(_outproj_kernel, n_a=n_a),
        grid=(n_a + n_b,),
        in_specs=group_specs(first) + group_specs(second) + [
            pl.BlockSpec((ATTN_WIDTH, D_MODEL), const, **resident),
            pl.BlockSpec((D_INNER, D_MODEL), const, **resident),
            pl.BlockSpec((D_MODEL, D_MODEL), const, **resident),
            pl.BlockSpec((1, D_MODEL), const),
            pl.BlockSpec((D_MODEL, LANES), const),
            pl.BlockSpec((D_MODEL, LANES), const),
            pl.BlockSpec((1, LANES), const),
        ],
        out_specs=[
            pl.BlockSpec((TM_OUT * TOKEN_TILE, LANES), lambda i: (i, 0)),
            pl.BlockSpec((TOKEN_TILE, TM_OUT), lambda i: (0, i)),
        ],
        out_shape=[
            jax.ShapeDtypeStruct((t * TOKEN_TILE, LANES), F32),
            jax.ShapeDtypeStruct((TOKEN_TILE, t), F32),
        ],
        compiler_params=_params(("arbitrary",)),
        name="outproj_router",
    )(*group_a, *group_b, wa, ws, wo, gn, wr1, wr2, br)


def _moe_kernel(ea_ref, eb_ref, nv_ref, tokc_ref, tokn_ref, roww_ref, x_ref, gn_ref,
                wga_ref, wua_ref, wda_ref, wgb_ref, wub_ref, wdb_ref, o_ref,
                xg_ref, st_ref, gsem, ssem, *, last_step):
    i = pl.program_id(0)
    slot = i % 2
    other = 1 - slot

    def tile(idx):
        return pl.ds(pl.multiple_of(idx * TOKEN_TILE, TOKEN_TILE), TOKEN_TILE)

    def gather_copy(tok, r, s):
        return pltpu.make_async_copy(x_ref.at[tile(tok), :], xg_ref.at[s, tile(r), :], gsem.at[s])

    def scatter_copy(tok, r, s):
        return pltpu.make_async_copy(st_ref.at[s, tile(r), :], o_ref.at[tile(tok), :], ssem.at[s])

    def for_rows(n, fn):
        n8 = lax.shift_right_logical(n, 3)

        def body8(g, _):
            for u in range(8):
                fn(g * 8 + u)
            return 0

        def body1(r, _):
            fn(r)
            return 0

        lax.fori_loop(0, n8, body8, 0)
        lax.fori_loop(n8 * 8, n, body1, 0)

    def start_scatters(tok_ref, n, s):
        for_rows(n, lambda r: scatter_copy(tok_ref[0, 0, r], r, s).start())

    def wait_scatters(n, s):
        for_rows(n, lambda r: scatter_copy(0, 0, s).wait())

    per_phase = ROW_BLOCK // MOE_ISSUE_PHASES

    def issue_next(phase):
        for r in range(phase * per_phase, (phase + 1) * per_phase):
            gather_copy(tokn_ref[0, 0, r], r, other).start()

    @pl.when(i == 0)
    def _():
        for r in range(ROW_BLOCK):
            gather_copy(tokc_ref[0, 0, r], r, 0).start()

    @pl.when(jnp.logical_or(i == 0, nv_ref[jnp.maximum(i - 1, 0)] > 0))
    def _():
        for r in range(ROW_BLOCK):
            gather_copy(0, 0, slot).wait()

    @pl.when(i >= 2)
    def _():
        wait_scatters(nv_ref[jnp.maximum(i - 2, 0)], slot)

    @pl.when(nv_ref[i] > 0)
    def _():
        x = jnp.concatenate(
            [xg_ref[slot, pl.ds(j, ROW_BLOCK, stride=TOKEN_TILE), :] for j in range(TOKEN_TILE)], axis=1)
        w_cols = jnp.concatenate([roww_ref[0], jnp.zeros((LANES - TOKEN_TILE, ROW_BLOCK), F32)], axis=0).T
        w_lo = w_cols[:, 0:1]
        w_hi = w_cols[:, 1:2]
        ms = jnp.mean(x * x, axis=-1, keepdims=True)
        hn = (x * lax.rsqrt(ms + EPS) * gn_ref[...]).astype(BF16)
        issue_next(0)

        def expert(wg_ref, wu_ref, wd_ref, phase):
            gte = _dot(hn, wg_ref[0])
            issue_next(phase)
            up = _dot(hn, wu_ref[0])
            issue_next(phase + 1)
            y = _dot((gte * _sigmoid(gte) * up).astype(BF16), wd_ref[0])
            issue_next(phase + 2)
            return y

        ya = expert(wga_ref, wua_ref, wda_ref, 1)
        yb = expert(wgb_ref, wub_ref, wdb_ref, 4)
        out = x + w_lo * ya + w_hi * yb
        for j in range(TOKEN_TILE):
            st_ref[slot, pl.ds(j, ROW_BLOCK, stride=TOKEN_TILE), :] = out[:, j * LANES:(j + 1) * LANES]
        issue_next(7)

    start_scatters(tokc_ref, nv_ref[i], slot)

    @pl.when(i == last_step)
    def _():
        wait_scatters(nv_ref[jnp.maximum(i - 1, 0)], other)
        wait_scatters(nv_ref[i], slot)


def _moe(ea, eb, nvalid, row_tok, row_w, x2t, gn, wg, wu, wd):
    n_blocks = row_tok.shape[0]

    def wa(i, ea, eb, nv):
        return (ea[i], 0, 0)

    def wb(i, ea, eb, nv):
        return (eb[i], 0, 0)

    any_spec = pl.BlockSpec(memory_space=pl.ANY)
    grid_spec = pltpu.PrefetchScalarGridSpec(
        num_scalar_prefetch=3,
        grid=(n_blocks,),
        in_specs=[
            pl.BlockSpec((1, 1, ROW_BLOCK), lambda i, ea, eb, nv: (i, 0, 0), memory_space=pltpu.SMEM),
            pl.BlockSpec((1, 1, ROW_BLOCK), lambda i, ea, eb, nv: (jnp.minimum(i + 1, n_blocks - 1), 0, 0),
                         memory_space=pltpu.SMEM),
            pl.BlockSpec((1, TOKEN_TILE, ROW_BLOCK), lambda i, ea, eb, nv: (i, 0, 0)),
            any_spec,
            pl.BlockSpec((1, D_MODEL), lambda i, ea, eb, nv: (0, 0)),
            pl.BlockSpec((1, D_MODEL, D_EXPERT), wa), pl.BlockSpec((1, D_MODEL, D_EXPERT), wa),
            pl.BlockSpec((1, D_EXPERT, D_MODEL), wa),
            pl.BlockSpec((1, D_MODEL, D_EXPERT), wb), pl.BlockSpec((1, D_MODEL, D_EXPERT), wb),
            pl.BlockSpec((1, D_EXPERT, D_MODEL), wb),
        ],
        out_specs=any_spec,
        scratch_shapes=[
            pltpu.VMEM((2, ROW_BLOCK * TOKEN_TILE, LANES), F32),
            pltpu.VMEM((2, ROW_BLOCK * TOKEN_TILE, LANES), F32),
            pltpu.SemaphoreType.DMA((2,)),
            pltpu.SemaphoreType.DMA((2,)),
        ],
    )
    return pl.pallas_call(
        functools.partial(_moe_kernel, last_step=n_blocks - 1),
        grid_spec=grid_spec,
        out_shape=jax.ShapeDtypeStruct(x2t.shape, F32),
        compiler_params=_params(("arbitrary",)),
        name="moe",
    )(ea, eb, nvalid, row_tok, row_tok, row_w, x2t, gn, wg, wu, wd, wg, wu, wd)


def _untile_kernel(x_ref, o_ref):
    for j in range(TOKEN_TILE):
        o_ref[:, j * LANES:(j + 1) * LANES] = x_ref[pl.ds(j, TM_UNTILE, stride=TOKEN_TILE), :]


def _untile(y_tiles, first_token, n_tokens):
    first_block = first_token // TM_UNTILE
    return pl.pallas_call(
        _untile_kernel,
        grid=(n_tokens // TM_UNTILE,),
        in_specs=[pl.BlockSpec((TM_UNTILE * TOKEN_TILE, LANES), lambda i: (first_block + i, 0))],
        out_specs=pl.BlockSpec((TM_UNTILE, D_MODEL), lambda i: (i, 0)),
        out_shape=jax.ShapeDtypeStruct((n_tokens, D_MODEL), F32),
        compiler_params=_params(("arbitrary",)),
        name="untile",
    )(y_tiles)


def _pair_tables():
    lo, hi = [], []
    for a in range(EXPERTS_PER_GROUP):
        for b in range(a + 1, EXPERTS_PER_GROUP):
            lo.append(a)
            hi.append(b)
    return np.asarray(lo, np.int32), np.asarray(hi, np.int32)


def _block_tables(rinfo):
    cls = rinfo[0].astype(jnp.int32)
    t = cls.shape[0]
    n_blocks = t // ROW_BLOCK + N_CLASSES + 1
    sorted_cls, order = lax.sort((cls, jnp.arange(t, dtype=jnp.int32)), num_keys=1)
    class_ids = jnp.arange(N_CLASSES + 1, dtype=jnp.int32)
    starts = jnp.sum((sorted_cls[:, None] < class_ids[None, :]).astype(jnp.int32), axis=0)
    counts = starts[1:] - starts[:-1]
    nblk = (counts + ROW_BLOCK - 1) // ROW_BLOCK
    blk_end = jnp.cumsum(nblk)
    blk_start = blk_end - nblk
    used = blk_end[-1]
    b = jnp.arange(n_blocks, dtype=jnp.int32)
    b_eff = jnp.minimum(b, used - 1)
    c = jnp.sum((blk_end[None, :] <= b_eff[:, None]).astype(jnp.int32), axis=1)
    c = jnp.minimum(c, N_CLASSES - 1)
    off = b_eff - blk_start[c]
    src = starts[c] + off * ROW_BLOCK
    nvalid = jnp.where(b < used, jnp.clip(counts[c] - off * ROW_BLOCK, 0, ROW_BLOCK), 0).astype(jnp.int32)
    pair_lo, pair_hi = _pair_tables()
    grp = c // N_PAIRS
    ea = (grp * EXPERTS_PER_GROUP + jnp.asarray(pair_lo)[c % N_PAIRS]).astype(jnp.int32)
    eb = (grp * EXPERTS_PER_GROUP + jnp.asarray(pair_hi)[c % N_PAIRS]).astype(jnp.int32)
    rows = jnp.clip(src[:, None] + jnp.arange(ROW_BLOCK, dtype=jnp.int32)[None, :], 0, t - 1)
    row_tok = order[rows]
    row_w = jnp.concatenate([rinfo[1][row_tok][:, None, :], rinfo[2][row_tok][:, None, :],
                             jnp.zeros((n_blocks, TOKEN_TILE - 2, ROW_BLOCK), F32)], axis=1)
    return ea, eb, nvalid, row_tok.reshape(n_blocks, 1, ROW_BLOCK), row_w


def _rope_tables(seq):
    inv = 1.0 / (ROPE_THETA ** (jnp.arange(0, HEAD_DIM, 2, dtype=F32) / HEAD_DIM))
    ang = jnp.arange(seq, dtype=F32)[:, None] * inv[None, :]
    cos, sin = jnp.cos(ang), jnp.sin(ang)
    cos128 = jnp.concatenate([cos, cos, cos, cos], axis=-1)
    sin128 = jnp.concatenate([-sin, sin, -sin, sin], axis=-1)
    return cos128, sin128


def _prepare_weights(norm_mix, w_in, q_norm, k_norm, attn_sink, conv_w, conv_b, a_log_fwd, a_log_bwd,
                     dt_bias_fwd, dt_bias_bwd, d_skip, ssm_norm, w_out_attn, w_out_ssm, w_o, norm_ffn,
                     w_router_group, b_router_group, w_router_expert, b_router_expert, w_gate, w_up, w_down):
    o_q = 0
    o_k = o_q + ATTN_WIDTH
    o_v = o_k + KV_WIDTH
    o_z = o_v + KV_WIDTH
    o_xbc = o_z + D_INNER
    o_dtf = o_xbc + CONV_DIM
    o_dtb = o_dtf + N_SSM_HEADS
    o_ga = o_dtb + N_SSM_HEADS
    o_gs = o_ga + D_MODEL
    w = w_in.astype(BF16)
    w_r = jnp.concatenate([
        w[:, o_z:o_z + D_INNER], w[:, o_ga:o_gs + D_MODEL], w[:, o_xbc:o_xbc + CONV_DIM],
        w[:, o_q:o_q + ATTN_WIDTH], w[:, o_k:o_k + KV_WIDTH], w[:, o_v:o_v + KV_WIDTH],
        w[:, o_dtf:o_dtb + N_SSM_HEADS], jnp.zeros((D_MODEL, LANES - 2 * N_SSM_HEADS), w.dtype)], axis=1)
    pad64 = jnp.zeros((LANES - 2 * N_SSM_HEADS,), F32)
    eye = np.kron(np.eye(2, dtype=np.float32), np.ones((HEAD_DIM, HEAD_DIM), np.float32))
    idx = np.arange(CHUNK)
    w_router = jnp.concatenate([w_router_group, w_router_expert,
                                jnp.zeros((D_MODEL, LANES - N_EXPERT_GROUPS - N_EXPERTS), F32)], axis=1)
    wr1 = w_router.astype(BF16)
    return dict(
        norm_mix=norm_mix.reshape(1, D_MODEL),
        w_in=w_r,
        qg128=jnp.tile(q_norm, 2).reshape(1, LANES),
        kg128=jnp.tile(k_norm, 2).reshape(1, LANES),
        seg=jnp.asarray(eye, BF16),
        sink=attn_sink.astype(F32),
        conv_w=conv_w,
        conv_b=conv_b.reshape(1, CONV_DIM),
        alog128=jnp.concatenate([a_log_fwd, a_log_bwd, pad64]).reshape(1, LANES),
        bias128=jnp.concatenate([dt_bias_fwd, dt_bias_bwd, pad64]).reshape(1, LANES),
        tri_l=jnp.asarray(idx[:, None] >= idx[None, :], BF16),
        tri_u=jnp.asarray(idx[:, None] <= idx[None, :], BF16),
        dskip=jnp.repeat(d_skip, SSM_HEAD_DIM).reshape(1, D_INNER),
        ssm_norm=ssm_norm.reshape(1, D_INNER),
        wa=w_out_attn.astype(BF16), ws=w_out_ssm.astype(BF16), wo=w_o.astype(BF16),
        norm_ffn=norm_ffn.reshape(1, D_MODEL),
        wr1=wr1, wr2=(w_router - wr1.astype(F32)).astype(BF16),
        br=jnp.concatenate([b_router_group, b_router_expert,
                            jnp.zeros((LANES - N_EXPERT_GROUPS - N_EXPERTS,), F32)]).reshape(1, LANES),
        wg=w_gate.astype(BF16), wu=w_up.astype(BF16), wd=w_down.astype(BF16),
    )


def _mixer(x, p):
    batch, seq, _ = x.shape
    step_rows = (TM_IN, ATTN_QB * ATTN_BLOCK, SSD_BWD_CHUNKS * CHUNK, SSD_MAIN_CHUNKS * CHUNK, CONV_ROWS)
    assert all(seq % rows == 0 for rows in step_rows), "sequence length must be a multiple of every row tile"
    x2d = x.reshape(batch * seq, D_MODEL)
    cos128, sin128 = _rope_tables(seq)
    proj, dt, qr, kdup, vdup = _inproj(x2d, p['norm_mix'], p['w_in'], cos128, sin128, p['qg128'], p['kg128'],
                                       p['seg'], seq)
    attn = _attention(qr, kdup, vdup, p['sink'], batch, seq)
    xc = _conv(proj, p['conv_w'], p['conv_b'], batch, seq)
    hb = _ssd_bwd_states(xc, dt, p['bias128'], p['alog128'], p['tri_l'], batch, seq)
    ssm = _ssd_main(xc, proj, dt, hb, p['bias128'], p['alog128'], p['tri_l'], p['tri_u'], p['dskip'],
                    p['ssm_norm'], batch, seq)
    return attn, ssm, proj, x2d


def kernel(x_prompt, x_sample, norm_mix, w_in, q_norm, k_norm, attn_sink, conv_w, conv_b, a_log_fwd, a_log_bwd,
           dt_bias_fwd, dt_bias_bwd, d_skip, ssm_norm, w_out_attn, w_out_ssm, w_o, norm_ffn, w_router_group,
           b_router_group, w_router_expert, b_router_expert, w_gate, w_up, w_down):
    assert norm_mix.shape[0] == 1, "single-layer encoder"
    p = _prepare_weights(norm_mix[0], w_in[0], q_norm[0], k_norm[0], attn_sink[0], conv_w[0], conv_b[0],
                         a_log_fwd[0], a_log_bwd[0], dt_bias_fwd[0], dt_bias_bwd[0], d_skip[0], ssm_norm[0],
                         w_out_attn[0], w_out_ssm[0], w_o[0], norm_ffn[0], w_router_group[0], b_router_group[0],
                         w_router_expert[0], b_router_expert[0], w_gate[0], w_up[0], w_down[0])
    x2t, rinfo = _outproj(_mixer(x_prompt, p), _mixer(x_sample, p), p['wa'], p['ws'], p['wo'], p['norm_ffn'],
                          p['wr1'], p['wr2'], p['br'])
    ea, eb, nvalid, row_tok, row_w = _block_tables(rinfo)
    y = _moe(ea, eb, nvalid, row_tok, row_w, x2t, p['norm_ffn'], p['wg'], p['wu'], p['wd'])
    t_a = x_prompt.shape[0] * x_prompt.shape[1]
    t_b = x_sample.shape[0] * x_sample.shape[1]
    assert t_a % TM_UNTILE == 0 and t_b % TM_UNTILE == 0 and TM_UNTILE % TM_OUT == 0
    return _untile(y, 0, t_a).reshape(x_prompt.shape), _untile(y, t_a, t_b).reshape(x_sample.shape)
```

```python
import functools

import numpy as np
import jax
import jax.numpy as jnp
from jax import lax
from jax.experimental import pallas as pl
from jax.experimental.pallas import tpu as pltpu

F32 = jnp.float32
BF16 = jnp.bfloat16

D_MODEL = 1024
EPS = 1e-6
NEG_INF = -1e30
LOG2_E = 1.4426950408889634
N_Q_HEADS = 16
N_KV_HEADS = 4
HEAD_DIM = 64
ATTN_WIDTH = N_Q_HEADS * HEAD_DIM
KV_WIDTH = N_KV_HEADS * HEAD_DIM
ATTN_BLOCK = 128
ATTN_QB = 4
ROPE_THETA = 10000.0
D_INNER = 2 * D_MODEL
SSM_HEAD_DIM = 64
N_SSM_HEADS = D_INNER // SSM_HEAD_DIM
N_SSM_GROUPS = 4
HEADS_PER_GROUP = N_SSM_HEADS // N_SSM_GROUPS
D_STATE = 128
BC_WIDTH = N_SSM_GROUPS * D_STATE
CONV_DIM = D_INNER + 2 * BC_WIDTH
CONV_W = 7
CHUNK = 128
SSD_BWD_CHUNKS = 8
SSD_MAIN_CHUNKS = 4
N_EXPERT_GROUPS = 4
EXPERTS_PER_GROUP = 8
N_EXPERTS = N_EXPERT_GROUPS * EXPERTS_PER_GROUP
D_EXPERT = 512
N_PAIRS = EXPERTS_PER_GROUP * (EXPERTS_PER_GROUP - 1) // 2
N_CLASSES = N_EXPERT_GROUPS * N_PAIRS

LANES = 128
V7X_VMEM_LIMIT_BYTES = 56 * 1024 * 1024

COL_Z = 0
COL_GATE = COL_Z + D_INNER
COL_XS = COL_GATE + 2 * D_MODEL
COL_B = COL_XS + D_INNER
COL_C = COL_B + BC_WIDTH
COL_Q = COL_C + BC_WIDTH
COL_K = COL_Q + ATTN_WIDTH
COL_V = COL_K + KV_WIDTH
COL_DT = COL_V + KV_WIDTH
N_PROJ = COL_DT + LANES

TM_IN = 1024
NJ_IN = 3
TN_IN = N_PROJ // NJ_IN
CH_IN = 512
TM_OUT = 512
TM_UNTILE = 1024
CONV_CT = 512
CONV_ROWS = 256
CONV_PITCH = 2
ROW_BLOCK = 128
TOKEN_TILE = D_MODEL // LANES


def _params(sem):
    return pltpu.CompilerParams(dimension_semantics=sem, vmem_limit_bytes=V7X_VMEM_LIMIT_BYTES)


def _dot(a, b):
    return jnp.dot(a, b, preferred_element_type=F32)


def _dot_nt(a, b):
    return lax.dot_general(a, b, (((1,), (1,)), ((), ())), preferred_element_type=F32)


def _sigmoid(x):
    return 1.0 / (1.0 + jnp.exp(-x))


def _inproj_kernel(x_ref, g_ref, w_ref, cos_ref, sin_ref, qg_ref, kg_ref, seg_ref,
                   o_ref, dt_ref, qo_ref, ko_ref, vo_ref, h_ref):
    j = pl.program_id(1)

    @pl.when(j == 0)
    def _():
        x = x_ref[...]
        ms = jnp.mean(x * x, axis=-1, keepdims=True)
        h_ref[...] = (x * lax.rsqrt(ms + EPS) * g_ref[...]).astype(BF16)

    def project(c0, c1):
        acc = _dot(h_ref[...], w_ref[:, c0:c1])
        o_ref[:, c0:c1] = acc.astype(BF16)
        return acc

    @pl.when(j < NJ_IN - 1)
    def _():
        for c0 in range(0, TN_IN, CH_IN):
            project(c0, min(c0 + CH_IN, TN_IN))

    @pl.when(j == NJ_IN - 1)
    def _():
        base = (NJ_IN - 1) * TN_IN
        q0, k0, d0 = COL_Q - base, COL_K - base, COL_DT - base
        for c0 in range(0, q0, CH_IN):
            project(c0, min(c0 + CH_IN, q0))
        cos = cos_ref[...]
        sin = sin_ref[...]
        seg = seg_ref[...]
        lane = lax.broadcasted_iota(jnp.int32, (TM_IN, LANES), 1)
        first_half = (lane % HEAD_DIM) < (HEAD_DIM // 2)
        low = lane < HEAD_DIM

        def norm_rope(x, gain):
            ss = _dot((x * x).astype(BF16), seg)
            xn = x * lax.rsqrt(ss * (1.0 / HEAD_DIM) + EPS) * gain
            rot = jnp.where(first_half, pltpu.roll(xn, 96, 1), pltpu.roll(xn, 32, 1))
            return xn * cos + rot * sin

        def duplicate(y, dst_ref, s):
            ysw = pltpu.roll(y, HEAD_DIM, 1)
            dst_ref[:, (2 * s) * LANES:(2 * s + 1) * LANES] = jnp.where(low, y, ysw).astype(BF16)
            dst_ref[:, (2 * s + 1) * LANES:(2 * s + 2) * LANES] = jnp.where(low, ysw, y).astype(BF16)

        for c0 in range(q0, k0, CH_IN):
            acc = project(c0, c0 + CH_IN)
            for s in range(CH_IN // LANES):
                y = norm_rope(acc[:, s * LANES:(s + 1) * LANES], qg_ref[...]) * (HEAD_DIM ** -0.5 * LOG2_E)
                dst = c0 - q0 + s * LANES
                qo_ref[:, dst:dst + LANES] = y.astype(BF16)
        acc = project(k0, d0)
        for s in range(KV_WIDTH // LANES):
            duplicate(norm_rope(acc[:, s * LANES:(s + 1) * LANES], kg_ref[...]), ko_ref, s)
            duplicate(acc[:, KV_WIDTH + s * LANES:KV_WIDTH + (s + 1) * LANES], vo_ref, s)
        dt_ref[...] = project(d0, TN_IN)


def _inproj(x2d, gain, w_bf16, cos128, sin128, qg128, kg128, seg, seq):
    t = x2d.shape[0]
    nseq = seq // TM_IN

    def rows(i, j):
        return (i, 0)

    def const(i, j):
        return (0, 0)

    def pos(i, j):
        return (i % nseq, 0)

    return pl.pallas_call(
        _inproj_kernel,
        grid=(t // TM_IN, NJ_IN),
        in_specs=[
            pl.BlockSpec((TM_IN, D_MODEL), rows),
            pl.BlockSpec((1, D_MODEL), const),
            pl.BlockSpec((D_MODEL, TN_IN), lambda i, j: (0, j)),
            pl.BlockSpec((TM_IN, LANES), pos),
            pl.BlockSpec((TM_IN, LANES), pos),
            pl.BlockSpec((1, LANES), const),
            pl.BlockSpec((1, LANES), const),
            pl.BlockSpec((LANES, LANES), const),
        ],
        out_specs=[
            pl.BlockSpec((TM_IN, TN_IN), lambda i, j: (i, j)),
            pl.BlockSpec((TM_IN, LANES), rows),
            pl.BlockSpec((TM_IN, ATTN_WIDTH), rows),
            pl.BlockSpec((TM_IN, 2 * KV_WIDTH), rows),
            pl.BlockSpec((TM_IN, 2 * KV_WIDTH), rows),
        ],
        out_shape=[
            jax.ShapeDtypeStruct((t, N_PROJ), BF16),
            jax.ShapeDtypeStruct((t, LANES), F32),
            jax.ShapeDtypeStruct((t, ATTN_WIDTH), BF16),
            jax.ShapeDtypeStruct((t, 2 * KV_WIDTH), BF16),
            jax.ShapeDtypeStruct((t, 2 * KV_WIDTH), BF16),
        ],
        scratch_shapes=[pltpu.VMEM((TM_IN, D_MODEL), BF16)],
        compiler_params=_params(("arbitrary", "arbitrary")),
        name="inproj",
    )(x2d, gain, w_bf16, cos128, sin128, qg128, kg128, seg)


def _attn_kernel(sink_ref, q_ref, kp_ref, kc_ref, kn_ref, vp_ref, vc_ref, vn_ref, o_ref, *, n_steps):
    i = pl.program_id(1)
    nb = ATTN_BLOCK
    nk = 3 * nb
    qi = lax.broadcasted_iota(jnp.int32, (nb, nk), 0)
    si = lax.broadcasted_iota(jnp.int32, (nb, nk), 1)
    rel = qi - (si - nb)
    band = jnp.where(rel <= nb, jnp.where(rel >= -nb, 0.0, NEG_INF), NEG_INF)
    bias_first = jnp.where(si < nb, jnp.where(i > 0, band, NEG_INF), band)
    bias_last = jnp.where(si >= 2 * nb, jnp.where(i < n_steps - 1, band, NEG_INF), band)
    low_q = lax.broadcasted_iota(jnp.int32, (nb, LANES), 1) < HEAD_DIM
    low_k = lax.broadcasted_iota(jnp.int32, (nk, LANES), 1) < HEAD_DIM
    zero_q = jnp.zeros((nb, LANES), BF16)
    zero_k = jnp.zeros((nk, LANES), BF16)

    def window(p_ref, c_ref, n_ref, h, j):
        sl = slice(h * LANES, (h + 1) * LANES)
        rows = jnp.concatenate([p_ref[:, sl], c_ref[:, sl], n_ref[:, sl]], axis=0)
        return rows[j * nb:j * nb + nk]

    def scores(j, h):
        kd = window(kp_ref, kc_ref, kn_ref, h, j)
        slabs = [q_ref[j * nb:(j + 1) * nb, (2 * h + u) * LANES:(2 * h + u + 1) * LANES] for u in range(2)]
        q4 = jnp.concatenate([jnp.where(low_q, s_, zero_q) for s_ in slabs]
                             + [jnp.where(low_q, zero_q, s_) for s_ in slabs], axis=0)
        return _dot_nt(q4, kd)

    def finish(j, h, s4):
        vd = window(vp_ref, vc_ref, vn_ref, h, j)
        v_lo = jnp.where(low_k, vd, zero_k)
        v_hi = jnp.where(low_k, zero_k, vd)
        heads = (4 * h, 4 * h + 2, 4 * h + 1, 4 * h + 3)
        ps, invs = [], []
        for k, head in enumerate(heads):
            s = s4[k * nb:(k + 1) * nb]
            left = s[:, :nb] + (bias_first if j == 0 else band)[:, :nb]
            right = s[:, 2 * nb:] + (bias_last if j == ATTN_QB - 1 else band)[:, 2 * nb:]
            s = jnp.concatenate([left, s[:, nb:2 * nb], right], axis=1)
            snk = sink_ref[head] * LOG2_E
            m = jnp.maximum(jnp.max(s, axis=-1, keepdims=True), snk)
            p = jnp.exp2(s - m)
            den = jnp.sum(p, axis=-1, keepdims=True) + jnp.exp2(snk - m)
            ps.append(p.astype(BF16))
            invs.append(1.0 / den)
        o = _dot(jnp.concatenate(ps[:2], axis=0), v_lo) + _dot(jnp.concatenate(ps[2:], axis=0), v_hi)
        for u in range(2):
            ou = o[u * nb:(u + 1) * nb] * jnp.where(low_q, invs[u], invs[2 + u])
            o_ref[j * nb:(j + 1) * nb, (2 * h + u) * LANES:(2 * h + u + 1) * LANES] = ou.astype(BF16)

    tasks = [(j, h) for j in range(ATTN_QB) for h in range(N_KV_HEADS)]
    pending = scores(*tasks[0])
    for n, task in enumerate(tasks):
        following = scores(*tasks[n + 1]) if n + 1 < len(tasks) else None
        finish(*task, pending)
        pending = following


def _attention(qr, kdup, vdup, sink, batch, seq):
    t = qr.shape[0]
    nq = seq // ATTN_BLOCK
    n_steps = nq // ATTN_QB
    rows = ATTN_QB * ATTN_BLOCK

    def prev(b, i, s):
        return (b * nq + jnp.maximum(i * ATTN_QB - 1, 0), 0)

    def cur(b, i, s):
        return (b * n_steps + i, 0)

    def nxt(b, i, s):
        return (b * nq + jnp.minimum((i + 1) * ATTN_QB, nq - 1), 0)

    edge = (ATTN_BLOCK, 2 * KV_WIDTH)
    mid = (rows, 2 * KV_WIDTH)
    grid_spec = pltpu.PrefetchScalarGridSpec(
        num_scalar_prefetch=1,
        grid=(batch, n_steps),
        in_specs=[
            pl.BlockSpec((rows, ATTN_WIDTH), cur),
            pl.BlockSpec(edge, prev), pl.BlockSpec(mid, cur), pl.BlockSpec(edge, nxt),
            pl.BlockSpec(edge, prev), pl.BlockSpec(mid, cur), pl.BlockSpec(edge, nxt),
        ],
        out_specs=pl.BlockSpec((rows, ATTN_WIDTH), cur),
    )
    return pl.pallas_call(
        functools.partial(_attn_kernel, n_steps=n_steps),
        grid_spec=grid_spec,
        out_shape=jax.ShapeDtypeStruct((t, ATTN_WIDTH), BF16),
        compiler_params=_params(("arbitrary", "arbitrary")),
        name="attention",
    )(sink, qr, kdup, kdup, kdup, vdup, vdup, vdup)


def _conv_kernel(x_ref, w_ref, b_ref, o_ref, pad_ref, *, seq):
    halo = 8
    step = CONV_PITCH

    def rows(first, n):
        return pl.ds(step * (first + halo), n, stride=step)

    for h in range(CONV_CT // LANES):
        lanes = slice(h * LANES, (h + 1) * LANES)
        pad_ref[h, rows(-halo, halo), :] = jnp.zeros((halo, LANES), F32)
        pad_ref[h, rows(seq, halo), :] = jnp.zeros((halo, LANES), F32)
        for r in range(seq // CONV_ROWS):
            pad_ref[h, rows(r * CONV_ROWS, CONV_ROWS), :] = x_ref[r * CONV_ROWS:(r + 1) * CONV_ROWS, lanes].astype(F32)
    w = w_ref[...]
    bias = b_ref[...]
    for h in range(CONV_CT // LANES):
        lanes = slice(h * LANES, (h + 1) * LANES)
        for r in range(seq // CONV_ROWS):
            r0 = r * CONV_ROWS
            acc = jnp.broadcast_to(bias[:, lanes], (CONV_ROWS, LANES))
            for k in range(CONV_W):
                acc = acc + pad_ref[h, rows(r0 + k - CONV_W // 2, CONV_ROWS), :] * w[k:k + 1, lanes]
            o_ref[r0:r0 + CONV_ROWS, lanes] = (acc * _sigmoid(acc)).astype(BF16)


def _conv(proj, conv_w, conv_b, batch, seq):
    t = proj.shape[0]
    return pl.pallas_call(
        functools.partial(_conv_kernel, seq=seq),
        grid=(batch, CONV_DIM // CONV_CT),
        in_specs=[
            pl.BlockSpec((seq, CONV_CT), lambda b, c: (b, COL_XS // CONV_CT + c)),
            pl.BlockSpec((CONV_W, CONV_CT), lambda b, c: (0, c)),
            pl.BlockSpec((1, CONV_CT), lambda b, c: (0, c)),
        ],
        out_specs=pl.BlockSpec((seq, CONV_CT), lambda b, c: (b, c)),
        out_shape=jax.ShapeDtypeStruct((t, CONV_DIM), BF16),
        scratch_shapes=[pltpu.VMEM((CONV_CT // LANES, CONV_PITCH * (seq + 16), LANES), F32)],
        compiler_params=_params(("arbitrary", "arbitrary")),
        name="conv",
    )(proj, conv_w, conv_b)


def _split3(a):
    a1 = a.astype(BF16)
    r1 = a - a1.astype(F32)
    a2 = r1.astype(BF16)
    a3 = (r1 - a2.astype(F32)).astype(BF16)
    return a1, a2, a3


def _tri_matmul(tri, a):
    a1, a2, a3 = _split3(a)
    return _dot(tri, a1) + _dot(tri, a2) + _dot(tri, a3)


def _softplus(x):
    return jnp.maximum(x, 0.0) + jnp.log(1.0 + jnp.exp(-jnp.abs(x)))


def _dt_and_rate(dt_ref, bias_ref, alog_ref):
    dt = _softplus(dt_ref[...] + bias_ref[...])
    rate = dt * (-LOG2_E * jnp.exp(alog_ref[...]))
    return dt, rate


def _head_rows(mat, first, rows):
    n = mat.shape[1]
    return jnp.concatenate(
        [jnp.broadcast_to(mat[first + e:first + e + 1, :], (rows, n)) for e in range(HEADS_PER_GROUP)], axis=0)


def _ssd_bwd_state_kernel(xs_ref, b_ref, dt_ref, bias_ref, alog_ref, tl_ref, hb_ref, st_ref):
    c = pl.program_id(1)

    @pl.when(c == 0)
    def _():
        st_ref[...] = jnp.zeros_like(st_ref)

    for ci in reversed(range(SSD_BWD_CHUNKS)):
        rows = pl.ds(ci * CHUNK, CHUNK)
        _ssd_bwd_chunk(xs_ref.at[rows, :], b_ref.at[rows, :], dt_ref.at[rows, :], bias_ref, alog_ref, tl_ref,
                       hb_ref.at[ci], st_ref)


def _ssd_bwd_chunk(xs_ref, b_ref, dt_ref, bias_ref, alog_ref, tl_ref, hb_ref, st_ref):
    hb_ref[...] = st_ref[...].astype(BF16)
    dt, rate = _dt_and_rate(dt_ref, bias_ref, alog_ref)
    pre = _tri_matmul(tl_ref[...], rate)
    pre_t = pre.T
    excl_t = (pre - rate).T
    total = jnp.broadcast_to(pre_t[:, CHUNK - 1:CHUNK], (LANES, CHUNK))
    w_t = dt.T * jnp.exp2(excl_t)
    dec = jnp.exp2(total)
    off = N_SSM_HEADS
    for g in range(N_SSM_GROUPS):
        xs_t = xs_ref[:, g * 512:(g + 1) * 512].astype(F32).T
        xd = (xs_t * _head_rows(w_t, off + g * HEADS_PER_GROUP, SSM_HEAD_DIM)).astype(BF16)
        upd = _dot(xd, b_ref[:, g * D_STATE:(g + 1) * D_STATE])
        st_ref[g] = _head_rows(dec, off + g * HEADS_PER_GROUP, SSM_HEAD_DIM) * st_ref[g] + upd


def _ssd_bwd_states(xc, dt, bias128, alog128, tri_l, batch, seq):
    nc = seq // CHUNK
    ns = nc // SSD_BWD_CHUNKS
    rows = SSD_BWD_CHUNKS * CHUNK

    def rev(b, c):
        return (b * ns + ns - 1 - c, 0)

    return pl.pallas_call(
        _ssd_bwd_state_kernel,
        grid=(batch, ns),
        in_specs=[
            pl.BlockSpec((rows, D_INNER), rev),
            pl.BlockSpec((rows, BC_WIDTH), lambda b, c: (b * ns + ns - 1 - c, D_INNER // BC_WIDTH)),
            pl.BlockSpec((rows, LANES), rev),
            pl.BlockSpec((1, LANES), lambda b, c: (0, 0)),
            pl.BlockSpec((1, LANES), lambda b, c: (0, 0)),
            pl.BlockSpec((CHUNK, CHUNK), lambda b, c: (0, 0)),
        ],
        out_specs=pl.BlockSpec((SSD_BWD_CHUNKS, N_SSM_GROUPS, 512, D_STATE),
                               lambda b, c: (b * ns + ns - 1 - c, 0, 0, 0)),
        out_shape=jax.ShapeDtypeStruct((batch * nc, N_SSM_GROUPS, 512, D_STATE), BF16),
        scratch_shapes=[pltpu.VMEM((N_SSM_GROUPS, 512, D_STATE), F32)],
        compiler_params=_params(("arbitrary", "arbitrary")),
        name="ssd_bwd_states",
    )(xc, xc, dt, bias128, alog128, tri_l)


def _ssd_main_kernel(xc_ref, z_ref, dt_ref, hb_ref, bias_ref, alog_ref, tl_ref, tu_ref, dskip_ref, gain_ref,
                     o_ref, hf_ref, y_ref):
    c = pl.program_id(1)

    @pl.when(c == 0)
    def _():
        hf_ref[...] = jnp.zeros_like(hf_ref)

    for ci in range(SSD_MAIN_CHUNKS):
        rows = pl.ds(ci * CHUNK, CHUNK)
        _ssd_main_chunk(xc_ref.at[rows, :], z_ref.at[rows, :], dt_ref.at[rows, :], hb_ref.at[ci], bias_ref, alog_ref,
                        tl_ref, tu_ref, dskip_ref, gain_ref, o_ref.at[rows, :], hf_ref, y_ref)


def _ssd_main_chunk(xc_ref, z_ref, dt_ref, hb_ref, bias_ref, alog_ref, tl_ref, tu_ref, dskip_ref, gain_ref,
                    o_ref, hf_ref, y_ref):
    dt, rate = _dt_and_rate(dt_ref, bias_ref, alog_ref)
    lane = lax.broadcasted_iota(jnp.int32, (CHUNK, LANES), 1)
    cum = jnp.where(lane < N_SSM_HEADS, _tri_matmul(tl_ref[...], rate), _tri_matmul(tu_ref[...], rate))
    cum_t = cum.T
    dt_t = dt.T
    src_t = cum_t - jnp.log2(dt_t)
    row = lax.broadcasted_iota(jnp.int32, (CHUNK, CHUNK), 0)
    col = lax.broadcasted_iota(jnp.int32, (CHUNK, CHUNK), 1)
    lower = row >= col
    diag = row == col
    low = lane < SSM_HEAD_DIM
    zero_x = jnp.zeros((CHUNK, LANES), BF16)
    nb = N_SSM_HEADS

    def lane_bcast(mat, idx):
        return jnp.broadcast_to(mat[:, idx:idx + 1], (CHUNK, CHUNK))

    def sub_bcast(mat, idx):
        return jnp.broadcast_to(mat[idx:idx + 1, :], (CHUNK, CHUNK))

    def head_matrix(e, cb):
        col_f = lane_bcast(cum, e)
        col_b = lane_bcast(cum, nb + e)
        decay = jnp.exp2(jnp.where(lower, col_f - sub_bcast(src_t, e), col_b - sub_bcast(src_t, nb + e)))
        decay = decay + jnp.where(diag, sub_bcast(dt_t, nb + e), 0.0)
        return (decay * cb).astype(BF16), col_f, col_b

    for g in range(N_SSM_GROUPS):
        bg = xc_ref[:, D_INNER + g * D_STATE:D_INNER + (g + 1) * D_STATE]
        cg = xc_ref[:, D_INNER + BC_WIDTH + g * D_STATE:D_INNER + BC_WIDTH + (g + 1) * D_STATE]
        cb = _dot_nt(cg, bg)
        y_in_f = _dot_nt(cg, hf_ref[g].astype(BF16))
        y_in_b = _dot_nt(cg, hb_ref[g])
        for jp in range(HEADS_PER_GROUP // 2):
            e0 = g * HEADS_PER_GROUP + 2 * jp
            cols = slice(e0 * SSM_HEAD_DIM, e0 * SSM_HEAD_DIM + LANES)
            loc = slice(jp * LANES, (jp + 1) * LANES)
            xs_pair = xc_ref[:, cols]
            m0, cf0, cb0 = head_matrix(e0, cb)
            m1, cf1, cb1 = head_matrix(e0 + 1, cb)
            y = _dot(m0, jnp.where(low, xs_pair, zero_x)) + _dot(m1, jnp.where(low, zero_x, xs_pair))
            y = y + y_in_f[:, loc] * jnp.exp2(jnp.where(low, cf0, cf1))
            y = y + y_in_b[:, loc] * jnp.exp2(jnp.where(low, cb0, cb1))
            y_ref[:, cols] = y + dskip_ref[:, cols] * xs_pair.astype(F32)

    z = z_ref[...].astype(F32)
    y = y_ref[...] * (z * _sigmoid(z))
    ms = jnp.mean(y * y, axis=-1, keepdims=True)
    o_ref[...] = (y * lax.rsqrt(ms + EPS) * gain_ref[...]).astype(BF16)

    last = jnp.broadcast_to(cum_t[:, CHUNK - 1:CHUNK], (LANES, CHUNK))
    w_t = jnp.exp2(last - src_t)
    dec = jnp.exp2(last)
    for g in range(N_SSM_GROUPS):
        xs_t = xc_ref[:, g * 512:(g + 1) * 512].astype(F32).T
        xd = (xs_t * _head_rows(w_t, g * HEADS_PER_GROUP, SSM_HEAD_DIM)).astype(BF16)
        upd = _dot(xd, xc_ref[:, D_INNER + g * D_STATE:D_INNER + (g + 1) * D_STATE])
        hf_ref[g] = _head_rows(dec, g * HEADS_PER_GROUP, SSM_HEAD_DIM) * hf_ref[g] + upd


def _ssd_main(xc, proj, dt, hb, bias128, alog128, tri_l, tri_u, dskip, gain, batch, seq):
    t = xc.shape[0]
    ns = seq // CHUNK // SSD_MAIN_CHUNKS
    rows = SSD_MAIN_CHUNKS * CHUNK

    def tok(b, c):
        return (b * ns + c, 0)

    def const(b, c):
        return (0, 0)

    return pl.pallas_call(
        _ssd_main_kernel,
        grid=(batch, ns),
        in_specs=[
            pl.BlockSpec((rows, CONV_DIM), tok),
            pl.BlockSpec((rows, D_INNER), tok),
            pl.BlockSpec((rows, LANES), tok),
            pl.BlockSpec((SSD_MAIN_CHUNKS, N_SSM_GROUPS, 512, D_STATE), lambda b, c: (b * ns + c, 0, 0, 0)),
            pl.BlockSpec((1, LANES), const),
            pl.BlockSpec((1, LANES), const),
            pl.BlockSpec((CHUNK, CHUNK), const),
            pl.BlockSpec((CHUNK, CHUNK), const),
            pl.BlockSpec((1, D_INNER), const),
            pl.BlockSpec((1, D_INNER), const),
        ],
        out_specs=pl.BlockSpec((rows, D_INNER), tok),
        out_shape=jax.ShapeDtypeStruct((t, D_INNER), BF16),
        scratch_shapes=[pltpu.VMEM((N_SSM_GROUPS, 512, D_STATE), F32), pltpu.VMEM((CHUNK, D_INNER), F32)],
        compiler_params=_params(("arbitrary", "arbitrary")),
        name="ssd_main",
    )(xc, proj, dt, hb, bias128, alog128, tri_l, tri_u, dskip, gain)


def _outproj_kernel(attn_a, ssm_a, gate_a, x_a, attn_b, ssm_b, gate_b, x_b, wa_ref, ws_ref, wo_ref, gn_ref,
                    wr1_ref, wr2_ref, br_ref, o_ref, r_ref, *, n_a):
    i = pl.program_id(0)

    @pl.when(i < n_a)
    def _():
        _outproj_tile(attn_a, ssm_a, gate_a, x_a, wa_ref, ws_ref, wo_ref, gn_ref, wr1_ref, wr2_ref, br_ref,
                      o_ref, r_ref)

    @pl.when(i >= n_a)
    def _():
        _outproj_tile(attn_b, ssm_b, gate_b, x_b, wa_ref, ws_ref, wo_ref, gn_ref, wr1_ref, wr2_ref, br_ref,
                      o_ref, r_ref)


def _outproj_tile(attn_ref, ssm_ref, gate_ref, x_ref, wa_ref, ws_ref, wo_ref, gn_ref, wr1_ref, wr2_ref, br_ref,
                  o_ref, r_ref):
    a_out = _dot(attn_ref[...], wa_ref[...])
    s_out = _dot(ssm_ref[...], ws_ref[...])
    ga = gate_ref[:, :D_MODEL].astype(F32)
    gs = gate_ref[:, D_MODEL:].astype(F32)
    merged = _sigmoid(ga) * a_out + _sigmoid(gs) * s_out
    x2 = x_ref[...] + _dot(merged.astype(BF16), wo_ref[...])
    for j in range(TOKEN_TILE):
        o_ref[pl.ds(j, TM_OUT, stride=TOKEN_TILE), :] = x2[:, j * LANES:(j + 1) * LANES]

    ms = jnp.mean(x2 * x2, axis=-1, keepdims=True)
    hn = x2 * lax.rsqrt(ms + EPS) * gn_ref[...]
    h1 = hn.astype(BF16)
    h2 = (hn - h1.astype(F32)).astype(BF16)
    lg = _dot(h1, wr1_ref[...]) + _dot(h2, wr1_ref[...]) + _dot(h1, wr2_ref[...]) + br_ref[...]

    lane = lax.broadcasted_iota(jnp.int32, (TM_OUT, LANES), 1).astype(F32)
    big = float(LANES)

    def rmax(v):
        return jnp.max(v, axis=-1, keepdims=True)

    def first_lane(mask):
        return jnp.min(jnp.where(mask, lane, big), axis=-1, keepdims=True)

    gl = jnp.where(lane < N_EXPERT_GROUPS, lg, NEG_INF)
    gmax = rmax(gl)
    g_w = 1.0 / jnp.sum(jnp.exp(gl - gmax), axis=-1, keepdims=True)
    gidx = first_lane(gl == gmax)
    base = N_EXPERT_GROUPS + EXPERTS_PER_GROUP * gidx
    el = jnp.where(lane >= base, jnp.where(lane < base + EXPERTS_PER_GROUP, lg, NEG_INF), NEG_INF)
    m1 = rmax(el)
    i1 = first_lane(el == m1)
    el2 = jnp.where(lane == i1, NEG_INF, el)
    m2 = rmax(el2)
    i2 = first_lane(el2 == m2)
    r = jnp.exp(m2 - m1)
    w1 = g_w / (1.0 + r)
    w2 = w1 * r
    j1 = i1 - base
    j2 = i2 - base
    swap = j1 > j2
    e_lo = jnp.where(swap, j2, j1)
    e_hi = jnp.where(swap, j1, j2)
    w_lo = jnp.where(swap, w2, w1)
    w_hi = jnp.where(swap, w1, w2)
    pair = e_lo * (EXPERTS_PER_GROUP - 1) - e_lo * (e_lo - 1.0) * 0.5 + (e_hi - e_lo - 1.0)
    cls = gidx * N_PAIRS + pair
    rows = jnp.where(lane == 0.0, cls, jnp.where(lane == 1.0, w_lo, jnp.where(lane == 2.0, w_hi, 0.0)))
    r_ref[...] = rows.T[:TOKEN_TILE, :]


def _outproj(group_a, group_b, wa, ws, wo, gn, wr1, wr2, br):
    n_a = group_a[3].shape[0] // TM_OUT
    n_b = group_b[3].shape[0] // TM_OUT
    t = (n_a + n_b) * TM_OUT

    def first(i):
        return (jnp.minimum(i, n_a - 1), 0)

    def second(i):
        return (jnp.maximum(i - n_a, 0), 0)

    def const(i):
        return (0, 0)

    def group_specs(tok):
        return [
            pl.BlockSpec((TM_OUT, ATTN_WIDTH), tok),
            pl.BlockSpec((TM_OUT, D_INNER), tok),
            pl.BlockSpec((TM_OUT, 2 * D_MODEL), lambda i: (tok(i)[0], COL_GATE // (2 * D_MODEL))),
            pl.BlockSpec((TM_OUT, D_MODEL), tok),
        ]

    resident = dict(pipeline_mode=pl.Buffered(1))
    return pl.pallas_call(
        functools.partial(_outproj_kernel, n_a=n_a),
        grid=(n_a + n_b,),
        in_specs=group_specs(first) + group_specs(second) + [
            pl.BlockSpec((ATTN_WIDTH, D_MODEL), const, **resident),
            pl.BlockSpec((D_INNER, D_MODEL), const, **resident),
            pl.BlockSpec((D_MODEL, D_MODEL), const, **resident),
            pl.BlockSpec((1, D_MODEL), const),
            pl.BlockSpec((D_MODEL, LANES), const),
            pl.BlockSpec((D_MODEL, LANES), const),
            pl.BlockSpec((1, LANES), const),
        ],
        out_specs=[
            pl.BlockSpec((TM_OUT * TOKEN_TILE, LANES), lambda i: (i, 0)),
            pl.BlockSpec((TOKEN_TILE, TM_OUT), lambda i: (0, i)),
        ],
        out_shape=[
            jax.ShapeDtypeStruct((t * TOKEN_TILE, LANES), F32),
            jax.ShapeDtypeStruct((TOKEN_TILE, t), F32),
        ],
        compiler_params=_params(("arbitrary",)),
        name="outproj_router",
    )(*group_a, *group_b, wa, ws, wo, gn, wr1, wr2, br)


def _moe_kernel(ea_ref, eb_ref, nv_ref, tokc_ref, tokn_ref, roww_ref, x_ref, gn_ref,
                wga_ref, wua_ref, wda_ref, wgb_ref, wub_ref, wdb_ref, o_ref,
                xg_ref, st_ref, gsem, ssem, *, n_blocks):
    i = pl.program_id(0)
    slot = i % 2
    other = 1 - slot

    def tile(idx):
        return pl.ds(pl.multiple_of(idx * TOKEN_TILE, TOKEN_TILE), TOKEN_TILE)

    def gather_copy(tok, r, s):
        return pltpu.make_async_copy(x_ref.at[tile(tok), :], xg_ref.at[s, tile(r), :], gsem.at[s])

    def scatter_copy(tok, r, s):
        return pltpu.make_async_copy(st_ref.at[s, tile(r), :], o_ref.at[tile(tok), :], ssem.at[s])

    def for_rows(n, fn):
        n8 = lax.shift_right_logical(n, 3)

        def body8(g, _):
            for u in range(8):
                fn(g * 8 + u)
            return 0

        def body1(r, _):
            fn(r)
            return 0

        lax.fori_loop(0, n8, body8, 0)
        lax.fori_loop(n8 * 8, n, body1, 0)

    def start_gathers(tok_ref, n, s):
        for_rows(n, lambda r: gather_copy(tok_ref[0, 0, r], r, s).start())

    def wait_gathers(n, s):
        for_rows(n, lambda r: gather_copy(0, 0, s).wait())

    def start_scatters(tok_ref, n, s):
        for_rows(n, lambda r: scatter_copy(tok_ref[0, 0, r], r, s).start())

    def wait_scatters(n, s):
        for_rows(n, lambda r: scatter_copy(0, 0, s).wait())

    @pl.when(i == 0)
    def _():
        xg_ref[...] = jnp.zeros_like(xg_ref)
        start_gathers(tokc_ref, nv_ref[0], 0)

    @pl.when(i + 1 < n_blocks)
    def _():
        start_gathers(tokn_ref, nv_ref[jnp.minimum(i + 1, n_blocks - 1)], other)

    wait_gathers(nv_ref[i], slot)

    @pl.when(i >= 2)
    def _():
        wait_scatters(nv_ref[jnp.maximum(i - 2, 0)], slot)

    @pl.when(nv_ref[i] > 0)
    def _():
        x = jnp.concatenate(
            [xg_ref[slot, pl.ds(j, ROW_BLOCK, stride=TOKEN_TILE), :] for j in range(TOKEN_TILE)], axis=1)
        w_cols = jnp.concatenate([roww_ref[0], jnp.zeros((LANES - TOKEN_TILE, ROW_BLOCK), F32)], axis=0).T
        w_lo = w_cols[:, 0:1]
        w_hi = w_cols[:, 1:2]
        ms = jnp.mean(x * x, axis=-1, keepdims=True)
        hn = (x * lax.rsqrt(ms + EPS) * gn_ref[...]).astype(BF16)

        gate_a, up_a = _dot(hn, wga_ref[0]), _dot(hn, wua_ref[0])
        gate_b, up_b = _dot(hn, wgb_ref[0]), _dot(hn, wub_ref[0])
        h_a = (gate_a * _sigmoid(gate_a) * up_a).astype(BF16)
        h_b = (gate_b * _sigmoid(gate_b) * up_b).astype(BF16)
        out = x + w_lo * _dot(h_a, wda_ref[0]) + w_hi * _dot(h_b, wdb_ref[0])
        for j in range(TOKEN_TILE):
            st_ref[slot, pl.ds(j, ROW_BLOCK, stride=TOKEN_TILE), :] = out[:, j * LANES:(j + 1) * LANES]

    start_scatters(tokc_ref, nv_ref[i], slot)

    @pl.when(i == n_blocks - 1)
    def _():
        wait_scatters(nv_ref[jnp.maximum(i - 1, 0)], other)
        wait_scatters(nv_ref[i], slot)


def _moe(ea, eb, nvalid, row_tok, row_w, x2t, gn, wg, wu, wd):
    n_blocks = row_tok.shape[0]

    def wa(i, ea, eb, nv):
        return (ea[i], 0, 0)

    def wb(i, ea, eb, nv):
        return (eb[i], 0, 0)

    any_spec = pl.BlockSpec(memory_space=pl.ANY)
    grid_spec = pltpu.PrefetchScalarGridSpec(
        num_scalar_prefetch=3,
        grid=(n_blocks,),
        in_specs=[
            pl.BlockSpec((1, 1, ROW_BLOCK), lambda i, ea, eb, nv: (i, 0, 0), memory_space=pltpu.SMEM),
            pl.BlockSpec((1, 1, ROW_BLOCK), lambda i, ea, eb, nv: (jnp.minimum(i + 1, n_blocks - 1), 0, 0),
                         memory_space=pltpu.SMEM),
            pl.BlockSpec((1, TOKEN_TILE, ROW_BLOCK), lambda i, ea, eb, nv: (i, 0, 0)),
            any_spec,
            pl.BlockSpec((1, D_MODEL), lambda i, ea, eb, nv: (0, 0)),
            pl.BlockSpec((1, D_MODEL, D_EXPERT), wa), pl.BlockSpec((1, D_MODEL, D_EXPERT), wa),
            pl.BlockSpec((1, D_EXPERT, D_MODEL), wa),
            pl.BlockSpec((1, D_MODEL, D_EXPERT), wb), pl.BlockSpec((1, D_MODEL, D_EXPERT), wb),
            pl.BlockSpec((1, D_EXPERT, D_MODEL), wb),
        ],
        out_specs=any_spec,
        scratch_shapes=[
            pltpu.VMEM((2, ROW_BLOCK * TOKEN_TILE, LANES), F32),
            pltpu.VMEM((2, ROW_BLOCK * TOKEN_TILE, LANES), F32),
            pltpu.SemaphoreType.DMA((2,)),
            pltpu.SemaphoreType.DMA((2,)),
        ],
    )
    return pl.pallas_call(
        functools.partial(_moe_kernel, n_blocks=n_blocks),
        grid_spec=grid_spec,
        out_shape=jax.ShapeDtypeStruct(x2t.shape, F32),
        compiler_params=_params(("arbitrary",)),
        name="moe",
    )(ea, eb, nvalid, row_tok, row_tok, row_w, x2t, gn, wg, wu, wd, wg, wu, wd)


def _untile_kernel(x_ref, o_ref):
    for j in range(TOKEN_TILE):
        o_ref[:, j * LANES:(j + 1) * LANES] = x_ref[pl.ds(j, TM_UNTILE, stride=TOKEN_TILE), :]


def _untile(y_tiles, first_token, n_tokens):
    first_block = first_token // TM_UNTILE
    return pl.pallas_call(
        _untile_kernel,
        grid=(n_tokens // TM_UNTILE,),
        in_specs=[pl.BlockSpec((TM_UNTILE * TOKEN_TILE, LANES), lambda i: (first_block + i, 0))],
        out_specs=pl.BlockSpec((TM_UNTILE, D_MODEL), lambda i: (i, 0)),
        out_shape=jax.ShapeDtypeStruct((n_tokens, D_MODEL), F32),
        compiler_params=_params(("arbitrary",)),
        name="untile",
    )(y_tiles)


def _pair_tables():
    lo, hi = [], []
    for a in range(EXPERTS_PER_GROUP):
        for b in range(a + 1, EXPERTS_PER_GROUP):
            lo.append(a)
            hi.append(b)
    return np.asarray(lo, np.int32), np.asarray(hi, np.int32)


def _block_tables(rinfo):
    cls = rinfo[0].astype(jnp.int32)
    t = cls.shape[0]
    n_blocks = t // ROW_BLOCK + N_CLASSES
    sorted_cls, order = lax.sort((cls, jnp.arange(t, dtype=jnp.int32)), num_keys=1)
    class_ids = jnp.arange(N_CLASSES + 1, dtype=jnp.int32)
    starts = jnp.sum((sorted_cls[:, None] < class_ids[None, :]).astype(jnp.int32), axis=0)
    counts = starts[1:] - starts[:-1]
    nblk = (counts + ROW_BLOCK - 1) // ROW_BLOCK
    blk_end = jnp.cumsum(nblk)
    blk_start = blk_end - nblk
    used = blk_end[-1]
    b = jnp.arange(n_blocks, dtype=jnp.int32)
    b_eff = jnp.minimum(b, used - 1)
    c = jnp.sum((blk_end[None, :] <= b_eff[:, None]).astype(jnp.int32), axis=1)
    c = jnp.minimum(c, N_CLASSES - 1)
    off = b_eff - blk_start[c]
    src = starts[c] + off * ROW_BLOCK
    nvalid = jnp.where(b < used, jnp.clip(counts[c] - off * ROW_BLOCK, 0, ROW_BLOCK), 0).astype(jnp.int32)
    pair_lo, pair_hi = _pair_tables()
    grp = c // N_PAIRS
    ea = (grp * EXPERTS_PER_GROUP + jnp.asarray(pair_lo)[c % N_PAIRS]).astype(jnp.int32)
    eb = (grp * EXPERTS_PER_GROUP + jnp.asarray(pair_hi)[c % N_PAIRS]).astype(jnp.int32)
    rows = jnp.clip(src[:, None] + jnp.arange(ROW_BLOCK, dtype=jnp.int32)[None, :], 0, t - 1)
    row_tok = order[rows]
    row_w = jnp.concatenate([rinfo[1][row_tok][:, None, :], rinfo[2][row_tok][:, None, :],
                             jnp.zeros((n_blocks, TOKEN_TILE - 2, ROW_BLOCK), F32)], axis=1)
    return ea, eb, nvalid, row_tok.reshape(n_blocks, 1, ROW_BLOCK), row_w


def _rope_tables(seq):
    inv = 1.0 / (ROPE_THETA ** (jnp.arange(0, HEAD_DIM, 2, dtype=F32) / HEAD_DIM))
    ang = jnp.arange(seq, dtype=F32)[:, None] * inv[None, :]
    cos, sin = jnp.cos(ang), jnp.sin(ang)
    cos128 = jnp.concatenate([cos, cos, cos, cos], axis=-1)
    sin128 = jnp.concatenate([-sin, sin, -sin, sin], axis=-1)
    return cos128, sin128


def _prepare_weights(norm_mix, w_in, q_norm, k_norm, attn_sink, conv_w, conv_b, a_log_fwd, a_log_bwd,
                     dt_bias_fwd, dt_bias_bwd, d_skip, ssm_norm, w_out_attn, w_out_ssm, w_o, norm_ffn,
                     w_router_group, b_router_group, w_router_expert, b_router_expert, w_gate, w_up, w_down):
    o_q = 0
    o_k = o_q + ATTN_WIDTH
    o_v = o_k + KV_WIDTH
    o_z = o_v + KV_WIDTH
    o_xbc = o_z + D_INNER
    o_dtf = o_xbc + CONV_DIM
    o_dtb = o_dtf + N_SSM_HEADS
    o_ga = o_dtb + N_SSM_HEADS
    o_gs = o_ga + D_MODEL
    w = w_in.astype(BF16)
    w_r = jnp.concatenate([
        w[:, o_z:o_z + D_INNER], w[:, o_ga:o_gs + D_MODEL], w[:, o_xbc:o_xbc + CONV_DIM],
        w[:, o_q:o_q + ATTN_WIDTH], w[:, o_k:o_k + KV_WIDTH], w[:, o_v:o_v + KV_WIDTH],
        w[:, o_dtf:o_dtb + N_SSM_HEADS], jnp.zeros((D_MODEL, LANES - 2 * N_SSM_HEADS), w.dtype)], axis=1)
    pad64 = jnp.zeros((LANES - 2 * N_SSM_HEADS,), F32)
    eye = np.kron(np.eye(2, dtype=np.float32), np.ones((HEAD_DIM, HEAD_DIM), np.float32))
    idx = np.arange(CHUNK)
    w_router = jnp.concatenate([w_router_group, w_router_expert,
                                jnp.zeros((D_MODEL, LANES - N_EXPERT_GROUPS - N_EXPERTS), F32)], axis=1)
    wr1 = w_router.astype(BF16)
    return dict(
        norm_mix=norm_mix.reshape(1, D_MODEL),
        w_in=w_r,
        qg128=jnp.tile(q_norm, 2).reshape(1, LANES),
        kg128=jnp.tile(k_norm, 2).reshape(1, LANES),
        seg=jnp.asarray(eye, BF16),
        sink=attn_sink.astype(F32),
        conv_w=conv_w,
        conv_b=conv_b.reshape(1, CONV_DIM),
        alog128=jnp.concatenate([a_log_fwd, a_log_bwd, pad64]).reshape(1, LANES),
        bias128=jnp.concatenate([dt_bias_fwd, dt_bias_bwd, pad64]).reshape(1, LANES),
        tri_l=jnp.asarray(idx[:, None] >= idx[None, :], BF16),
        tri_u=jnp.asarray(idx[:, None] <= idx[None, :], BF16),
        dskip=jnp.repeat(d_skip, SSM_HEAD_DIM).reshape(1, D_INNER),
        ssm_norm=ssm_norm.reshape(1, D_INNER),
        wa=w_out_attn.astype(BF16), ws=w_out_ssm.astype(BF16), wo=w_o.astype(BF16),
        norm_ffn=norm_ffn.reshape(1, D_MODEL),
        wr1=wr1, wr2=(w_router - wr1.astype(F32)).astype(BF16),
        br=jnp.concatenate([b_router_group, b_router_expert,
                            jnp.zeros((LANES - N_EXPERT_GROUPS - N_EXPERTS,), F32)]).reshape(1, LANES),
        wg=w_gate.astype(BF16), wu=w_up.astype(BF16), wd=w_down.astype(BF16),
    )


def _mixer(x, p):
    batch, seq, _ = x.shape
    step_rows = (TM_IN, ATTN_QB * ATTN_BLOCK, SSD_BWD_CHUNKS * CHUNK, SSD_MAIN_CHUNKS * CHUNK, CONV_ROWS)
    assert all(seq % rows == 0 for rows in step_rows), "sequence length must be a multiple of every row tile"
    x2d = x.reshape(batch * seq, D_MODEL)
    cos128, sin128 = _rope_tables(seq)
    proj, dt, qr, kdup, vdup = _inproj(x2d, p['norm_mix'], p['w_in'], cos128, sin128, p['qg128'], p['kg128'],
                                       p['seg'], seq)
    attn = _attention(qr, kdup, vdup, p['sink'], batch, seq)
    xc = _conv(proj, p['conv_w'], p['conv_b'], batch, seq)
    hb = _ssd_bwd_states(xc, dt, p['bias128'], p['alog128'], p['tri_l'], batch, seq)
    ssm = _ssd_main(xc, proj, dt, hb, p['bias128'], p['alog128'], p['tri_l'], p['tri_u'], p['dskip'],
                    p['ssm_norm'], batch, seq)
    return attn, ssm, proj, x2d


def kernel(x_prompt, x_sample, norm_mix, w_in, q_norm, k_norm, attn_sink, conv_w, conv_b, a_log_fwd, a_log_bwd,
           dt_bias_fwd, dt_bias_bwd, d_skip, ssm_norm, w_out_attn, w_out_ssm, w_o, norm_ffn, w_router_group,
           b_router_group, w_router_expert, b_router_expert, w_gate, w_up, w_down):
    assert norm_mix.shape[0] == 1, "single-layer encoder"
    p = _prepare_weights(norm_mix[0], w_in[0], q_norm[0], k_norm[0], attn_sink[0], conv_w[0], conv_b[0],
                         a_log_fwd[0], a_log_bwd[0], dt_bias_fwd[0], dt_bias_bwd[0], d_skip[0], ssm_norm[0],
                         w_out_attn[0], w_out_ssm[0], w_o[0], norm_ffn[0], w_router_group[0], b_router_group[0],
                         w_router_expert[0], b_router_expert[0], w_gate[0], w_up[0], w_down[0])
    x2t, rinfo = _outproj(_mixer(x_prompt, p), _mixer(x_sample, p), p['wa'], p['ws'], p['wo'], p['norm_ffn'],
                          p['wr1'], p['wr2'], p['br'])
    ea, eb, nvalid, row_tok, row_w = _block_tables(rinfo)
    y = _moe(ea, eb, nvalid, row_tok, row_w, x2t, p['norm_ffn'], p['wg'], p['wu'], p['wd'])
    t_a = x_prompt.shape[0] * x_prompt.shape[1]
    t_b = x_sample.shape[0] * x_sample.shape[1]
    assert t_a % TM_UNTILE == 0 and t_b % TM_UNTILE == 0 and TM_UNTILE % TM_OUT == 0
    return _untile(y, 0, t_a).reshape(x_prompt.shape), _untile(y, t_a, t_b).reshape(x_sample.shape)
```

```python
import functools

import numpy as np
import jax
import jax.numpy as jnp
from jax import lax
from jax.experimental import pallas as pl
from jax.experimental.pallas import tpu as pltpu

F32 = jnp.float32
BF16 = jnp.bfloat16

D_MODEL = 1024
EPS = 1e-6
NEG_INF = -1e30
LOG2_E = 1.4426950408889634
N_Q_HEADS = 16
N_KV_HEADS = 4
HEAD_DIM = 64
ATTN_WIDTH = N_Q_HEADS * HEAD_DIM
KV_WIDTH = N_KV_HEADS * HEAD_DIM
ATTN_BLOCK = 128
ATTN_QB = 4
ROPE_THETA = 10000.0
D_INNER = 2 * D_MODEL
SSM_HEAD_DIM = 64
N_SSM_HEADS = D_INNER // SSM_HEAD_DIM
N_SSM_GROUPS = 4
HEADS_PER_GROUP = N_SSM_HEADS // N_SSM_GROUPS
D_STATE = 128
BC_WIDTH = N_SSM_GROUPS * D_STATE
CONV_DIM = D_INNER + 2 * BC_WIDTH
CONV_W = 7
CHUNK = 128
SSD_BWD_CHUNKS = 8
SSD_MAIN_CHUNKS = 4
N_EXPERT_GROUPS = 4
EXPERTS_PER_GROUP = 8
N_EXPERTS = N_EXPERT_GROUPS * EXPERTS_PER_GROUP
D_EXPERT = 512
N_PAIRS = EXPERTS_PER_GROUP * (EXPERTS_PER_GROUP - 1) // 2
N_CLASSES = N_EXPERT_GROUPS * N_PAIRS

LANES = 128
V7X_VMEM_LIMIT_BYTES = 56 * 1024 * 1024

COL_Z = 0
COL_GATE = COL_Z + D_INNER
COL_XS = COL_GATE + 2 * D_MODEL
COL_B = COL_XS + D_INNER
COL_C = COL_B + BC_WIDTH
COL_Q = COL_C + BC_WIDTH
COL_K = COL_Q + ATTN_WIDTH
COL_V = COL_K + KV_WIDTH
COL_DT = COL_V + KV_WIDTH
N_PROJ = COL_DT + LANES

TM_IN = 1024
NJ_IN = 3
TN_IN = N_PROJ // NJ_IN
CH_IN = 512
TM_OUT = 512
OUT_SPLIT = 2
TM_UNTILE = 1024
CONV_CT = 512
CONV_ROWS = 256
CONV_PITCH = 2
ROW_BLOCK = 128
TOKEN_TILE = D_MODEL // LANES


def _params(sem):
    return pltpu.CompilerParams(dimension_semantics=sem, vmem_limit_bytes=V7X_VMEM_LIMIT_BYTES)


def _dot(a, b):
    return jnp.dot(a, b, preferred_element_type=F32)


def _dot_nt(a, b):
    return lax.dot_general(a, b, (((1,), (1,)), ((), ())), preferred_element_type=F32)


def _sigmoid(x):
    return 1.0 / (1.0 + jnp.exp(-x))


def _inproj_kernel(x_ref, g_ref, w_ref, cos_ref, sin_ref, qg_ref, kg_ref, seg_ref,
                   o_ref, dt_ref, qo_ref, ko_ref, vo_ref, h_ref):
    j = pl.program_id(1)

    @pl.when(j == 0)
    def _():
        x = x_ref[...]
        ms = jnp.mean(x * x, axis=-1, keepdims=True)
        h_ref[...] = (x * lax.rsqrt(ms + EPS) * g_ref[...]).astype(BF16)

    def project(c0, c1):
        acc = _dot(h_ref[...], w_ref[:, c0:c1])
        o_ref[:, c0:c1] = acc.astype(BF16)
        return acc

    @pl.when(j < NJ_IN - 1)
    def _():
        for c0 in range(0, TN_IN, CH_IN):
            project(c0, min(c0 + CH_IN, TN_IN))

    @pl.when(j == NJ_IN - 1)
    def _():
        base = (NJ_IN - 1) * TN_IN
        q0, k0, d0 = COL_Q - base, COL_K - base, COL_DT - base
        for c0 in range(0, q0, CH_IN):
            project(c0, min(c0 + CH_IN, q0))
        cos = cos_ref[...]
        sin = sin_ref[...]
        seg = seg_ref[...]
        lane = lax.broadcasted_iota(jnp.int32, (TM_IN, LANES), 1)
        first_half = (lane % HEAD_DIM) < (HEAD_DIM // 2)
        low = lane < HEAD_DIM

        def norm_rope(x, gain):
            ss = _dot((x * x).astype(BF16), seg)
            xn = x * lax.rsqrt(ss * (1.0 / HEAD_DIM) + EPS) * gain
            rot = jnp.where(first_half, pltpu.roll(xn, 96, 1), pltpu.roll(xn, 32, 1))
            return xn * cos + rot * sin

        def duplicate(y, dst_ref, s):
            ysw = pltpu.roll(y, HEAD_DIM, 1)
            dst_ref[:, (2 * s) * LANES:(2 * s + 1) * LANES] = jnp.where(low, y, ysw).astype(BF16)
            dst_ref[:, (2 * s + 1) * LANES:(2 * s + 2) * LANES] = jnp.where(low, ysw, y).astype(BF16)

        for c0 in range(q0, k0, CH_IN):
            acc = project(c0, c0 + CH_IN)
            for s in range(CH_IN // LANES):
                y = norm_rope(acc[:, s * LANES:(s + 1) * LANES], qg_ref[...]) * (HEAD_DIM ** -0.5 * LOG2_E)
                dst = c0 - q0 + s * LANES
                qo_ref[:, dst:dst + LANES] = y.astype(BF16)
        acc = project(k0, d0)
        for s in range(KV_WIDTH // LANES):
            duplicate(norm_rope(acc[:, s * LANES:(s + 1) * LANES], kg_ref[...]), ko_ref, s)
            duplicate(acc[:, KV_WIDTH + s * LANES:KV_WIDTH + (s + 1) * LANES], vo_ref, s)
        dt_ref[...] = project(d0, TN_IN)


def _inproj(x2d, gain, w_bf16, cos128, sin128, qg128, kg128, seg, seq):
    t = x2d.shape[0]
    nseq = seq // TM_IN

    def rows(i, j):
        return (i, 0)

    def const(i, j):
        return (0, 0)

    def pos(i, j):
        return (i % nseq, 0)

    return pl.pallas_call(
        _inproj_kernel,
        grid=(t // TM_IN, NJ_IN),
        in_specs=[
            pl.BlockSpec((TM_IN, D_MODEL), rows),
            pl.BlockSpec((1, D_MODEL), const),
            pl.BlockSpec((D_MODEL, TN_IN), lambda i, j: (0, j)),
            pl.BlockSpec((TM_IN, LANES), pos),
            pl.BlockSpec((TM_IN, LANES), pos),
            pl.BlockSpec((1, LANES), const),
            pl.BlockSpec((1, LANES), const),
            pl.BlockSpec((LANES, LANES), const),
        ],
        out_specs=[
            pl.BlockSpec((TM_IN, TN_IN), lambda i, j: (i, j)),
            pl.BlockSpec((TM_IN, LANES), rows),
            pl.BlockSpec((TM_IN, ATTN_WIDTH), rows),
            pl.BlockSpec((TM_IN, 2 * KV_WIDTH), rows),
            pl.BlockSpec((TM_IN, 2 * KV_WIDTH), rows),
        ],
        out_shape=[
            jax.ShapeDtypeStruct((t, N_PROJ), BF16),
            jax.ShapeDtypeStruct((t, LANES), F32),
            jax.ShapeDtypeStruct((t, ATTN_WIDTH), BF16),
            jax.ShapeDtypeStruct((t, 2 * KV_WIDTH), BF16),
            jax.ShapeDtypeStruct((t, 2 * KV_WIDTH), BF16),
        ],
        scratch_shapes=[pltpu.VMEM((TM_IN, D_MODEL), BF16)],
        compiler_params=_params(("arbitrary", "arbitrary")),
        name="inproj",
    )(x2d, gain, w_bf16, cos128, sin128, qg128, kg128, seg)


def _attn_kernel(sink_ref, q_ref, kp_ref, kc_ref, kn_ref, vp_ref, vc_ref, vn_ref, o_ref, *, n_steps):
    i = pl.program_id(1)
    nb = ATTN_BLOCK
    nk = 3 * nb
    qi = lax.broadcasted_iota(jnp.int32, (nb, nk), 0)
    si = lax.broadcasted_iota(jnp.int32, (nb, nk), 1)
    rel = qi - (si - nb)
    band = jnp.where(rel <= nb, jnp.where(rel >= -nb, 0.0, NEG_INF), NEG_INF)
    bias_first = jnp.where(si < nb, jnp.where(i > 0, band, NEG_INF), band)
    bias_last = jnp.where(si >= 2 * nb, jnp.where(i < n_steps - 1, band, NEG_INF), band)
    low_q = lax.broadcasted_iota(jnp.int32, (nb, LANES), 1) < HEAD_DIM
    low_k = lax.broadcasted_iota(jnp.int32, (nk, LANES), 1) < HEAD_DIM
    zero_q = jnp.zeros((nb, LANES), BF16)
    zero_k = jnp.zeros((nk, LANES), BF16)

    def window(p_ref, c_ref, n_ref, h, j):
        sl = slice(h * LANES, (h + 1) * LANES)
        rows = jnp.concatenate([p_ref[:, sl], c_ref[:, sl], n_ref[:, sl]], axis=0)
        return rows[j * nb:j * nb + nk]

    def scores(j, h):
        kd = window(kp_ref, kc_ref, kn_ref, h, j)
        slabs = [q_ref[j * nb:(j + 1) * nb, (2 * h + u) * LANES:(2 * h + u + 1) * LANES] for u in range(2)]
        q4 = jnp.concatenate([jnp.where(low_q, s_, zero_q) for s_ in slabs]
                             + [jnp.where(low_q, zero_q, s_) for s_ in slabs], axis=0)
        return _dot_nt(q4, kd)

    def finish(j, h, s4):
        vd = window(vp_ref, vc_ref, vn_ref, h, j)
        v_lo = jnp.where(low_k, vd, zero_k)
        v_hi = jnp.where(low_k, zero_k, vd)
        heads = (4 * h, 4 * h + 2, 4 * h + 1, 4 * h + 3)
        ps, invs = [], []
        for k, head in enumerate(heads):
            s = s4[k * nb:(k + 1) * nb]
            left = s[:, :nb] + (bias_first if j == 0 else band)[:, :nb]
            right = s[:, 2 * nb:] + (bias_last if j == ATTN_QB - 1 else band)[:, 2 * nb:]
            s = jnp.concatenate([left, s[:, nb:2 * nb], right], axis=1)
            snk = sink_ref[head] * LOG2_E
            m = jnp.maximum(jnp.max(s, axis=-1, keepdims=True), snk)
            p = jnp.exp2(s - m)
            den = jnp.sum(p, axis=-1, keepdims=True) + jnp.exp2(snk - m)
            ps.append(p.astype(BF16))
            invs.append(1.0 / den)
        o = _dot(jnp.concatenate(ps[:2], axis=0), v_lo) + _dot(jnp.concatenate(ps[2:], axis=0), v_hi)
        for u in range(2):
            ou = o[u * nb:(u + 1) * nb] * jnp.where(low_q, invs[u], invs[2 + u])
            o_ref[j * nb:(j + 1) * nb, (2 * h + u) * LANES:(2 * h + u + 1) * LANES] = ou.astype(BF16)

    tasks = [(j, h) for j in range(ATTN_QB) for h in range(N_KV_HEADS)]
    pending = scores(*tasks[0])
    for n, task in enumerate(tasks):
        following = scores(*tasks[n + 1]) if n + 1 < len(tasks) else None
        finish(*task, pending)
        pending = following


def _attention(qr, kdup, vdup, sink, batch, seq):
    t = qr.shape[0]
    nq = seq // ATTN_BLOCK
    n_steps = nq // ATTN_QB
    rows = ATTN_QB * ATTN_BLOCK

    def prev(b, i, s):
        return (b * nq + jnp.maximum(i * ATTN_QB - 1, 0), 0)

    def cur(b, i, s):
        return (b * n_steps + i, 0)

    def nxt(b, i, s):
        return (b * nq + jnp.minimum((i + 1) * ATTN_QB, nq - 1), 0)

    edge = (ATTN_BLOCK, 2 * KV_WIDTH)
    mid = (rows, 2 * KV_WIDTH)
    grid_spec = pltpu.PrefetchScalarGridSpec(
        num_scalar_prefetch=1,
        grid=(batch, n_steps),
        in_specs=[
            pl.BlockSpec((rows, ATTN_WIDTH), cur),
            pl.BlockSpec(edge, prev), pl.BlockSpec(mid, cur), pl.BlockSpec(edge, nxt),
            pl.BlockSpec(edge, prev), pl.BlockSpec(mid, cur), pl.BlockSpec(edge, nxt),
        ],
        out_specs=pl.BlockSpec((rows, ATTN_WIDTH), cur),
    )
    return pl.pallas_call(
        functools.partial(_attn_kernel, n_steps=n_steps),
        grid_spec=grid_spec,
        out_shape=jax.ShapeDtypeStruct((t, ATTN_WIDTH), BF16),
        compiler_params=_params(("arbitrary", "arbitrary")),
        name="attention",
    )(sink, qr, kdup, kdup, kdup, vdup, vdup, vdup)


def _conv_kernel(x_ref, w_ref, b_ref, o_ref, pad_ref, *, seq):
    halo = 8
    step = CONV_PITCH

    def rows(first, n):
        return pl.ds(step * (first + halo), n, stride=step)

    for h in range(CONV_CT // LANES):
        lanes = slice(h * LANES, (h + 1) * LANES)
        pad_ref[h, rows(-halo, halo), :] = jnp.zeros((halo, LANES), F32)
        pad_ref[h, rows(seq, halo), :] = jnp.zeros((halo, LANES), F32)
        for r in range(seq // CONV_ROWS):
            pad_ref[h, rows(r * CONV_ROWS, CONV_ROWS), :] = x_ref[r * CONV_ROWS:(r + 1) * CONV_ROWS, lanes].astype(F32)
    w = w_ref[...]
    bias = b_ref[...]
    for h in range(CONV_CT // LANES):
        lanes = slice(h * LANES, (h + 1) * LANES)
        for r in range(seq // CONV_ROWS):
            r0 = r * CONV_ROWS
            acc = jnp.broadcast_to(bias[:, lanes], (CONV_ROWS, LANES))
            for k in range(CONV_W):
                acc = acc + pad_ref[h, rows(r0 + k - CONV_W // 2, CONV_ROWS), :] * w[k:k + 1, lanes]
            o_ref[r0:r0 + CONV_ROWS, lanes] = (acc * _sigmoid(acc)).astype(BF16)


def _conv(proj, conv_w, conv_b, batch, seq):
    t = proj.shape[0]
    return pl.pallas_call(
        functools.partial(_conv_kernel, seq=seq),
        grid=(batch, CONV_DIM // CONV_CT),
        in_specs=[
            pl.BlockSpec((seq, CONV_CT), lambda b, c: (b, COL_XS // CONV_CT + c)),
            pl.BlockSpec((CONV_W, CONV_CT), lambda b, c: (0, c)),
            pl.BlockSpec((1, CONV_CT), lambda b, c: (0, c)),
        ],
        out_specs=pl.BlockSpec((seq, CONV_CT), lambda b, c: (b, c)),
        out_shape=jax.ShapeDtypeStruct((t, CONV_DIM), BF16),
        scratch_shapes=[pltpu.VMEM((CONV_CT // LANES, CONV_PITCH * (seq + 16), LANES), F32)],
        compiler_params=_params(("arbitrary", "arbitrary")),
        name="conv",
    )(proj, conv_w, conv_b)


def _split3(a):
    a1 = a.astype(BF16)
    r1 = a - a1.astype(F32)
    a2 = r1.astype(BF16)
    a3 = (r1 - a2.astype(F32)).astype(BF16)
    return a1, a2, a3


def _tri_matmul(tri, a):
    a1, a2, a3 = _split3(a)
    return _dot(tri, a1) + _dot(tri, a2) + _dot(tri, a3)


def _softplus(x):
    return jnp.maximum(x, 0.0) + jnp.log(1.0 + jnp.exp(-jnp.abs(x)))


def _dt_and_rate(dt_ref, bias_ref, alog_ref):
    dt = _softplus(dt_ref[...] + bias_ref[...])
    rate = dt * (-LOG2_E * jnp.exp(alog_ref[...]))
    return dt, rate


def _head_rows(mat, first, rows):
    n = mat.shape[1]
    return jnp.concatenate(
        [jnp.broadcast_to(mat[first + e:first + e + 1, :], (rows, n)) for e in range(HEADS_PER_GROUP)], axis=0)


def _ssd_bwd_state_kernel(xs_ref, b_ref, dt_ref, bias_ref, alog_ref, tl_ref, hb_ref, st_ref):
    c = pl.program_id(1)

    @pl.when(c == 0)
    def _():
        st_ref[...] = jnp.zeros_like(st_ref)

    for ci in reversed(range(SSD_BWD_CHUNKS)):
        rows = pl.ds(ci * CHUNK, CHUNK)
        _ssd_bwd_chunk(xs_ref.at[rows, :], b_ref.at[rows, :], dt_ref.at[rows, :], bias_ref, alog_ref, tl_ref,
                       hb_ref.at[ci], st_ref)


def _ssd_bwd_chunk(xs_ref, b_ref, dt_ref, bias_ref, alog_ref, tl_ref, hb_ref, st_ref):
    hb_ref[...] = st_ref[...].astype(BF16)
    dt, rate = _dt_and_rate(dt_ref, bias_ref, alog_ref)
    pre = _tri_matmul(tl_ref[...], rate)
    pre_t = pre.T
    excl_t = (pre - rate).T
    total = jnp.broadcast_to(pre_t[:, CHUNK - 1:CHUNK], (LANES, CHUNK))
    w_t = dt.T * jnp.exp2(excl_t)
    dec = jnp.exp2(total)
    off = N_SSM_HEADS
    for g in range(N_SSM_GROUPS):
        xs_t = xs_ref[:, g * 512:(g + 1) * 512].astype(F32).T
        xd = (xs_t * _head_rows(w_t, off + g * HEADS_PER_GROUP, SSM_HEAD_DIM)).astype(BF16)
        upd = _dot(xd, b_ref[:, g * D_STATE:(g + 1) * D_STATE])
        st_ref[g] = _head_rows(dec, off + g * HEADS_PER_GROUP, SSM_HEAD_DIM) * st_ref[g] + upd


def _ssd_bwd_states(xc, dt, bias128, alog128, tri_l, batch, seq):
    nc = seq // CHUNK
    ns = nc // SSD_BWD_CHUNKS
    rows = SSD_BWD_CHUNKS * CHUNK

    def rev(b, c):
        return (b * ns + ns - 1 - c, 0)

    return pl.pallas_call(
        _ssd_bwd_state_kernel,
        grid=(batch, ns),
        in_specs=[
            pl.BlockSpec((rows, D_INNER), rev),
            pl.BlockSpec((rows, BC_WIDTH), lambda b, c: (b * ns + ns - 1 - c, D_INNER // BC_WIDTH)),
            pl.BlockSpec((rows, LANES), rev),
            pl.BlockSpec((1, LANES), lambda b, c: (0, 0)),
            pl.BlockSpec((1, LANES), lambda b, c: (0, 0)),
            pl.BlockSpec((CHUNK, CHUNK), lambda b, c: (0, 0)),
        ],
        out_specs=pl.BlockSpec((SSD_BWD_CHUNKS, N_SSM_GROUPS, 512, D_STATE),
                               lambda b, c: (b * ns + ns - 1 - c, 0, 0, 0)),
        out_shape=jax.ShapeDtypeStruct((batch * nc, N_SSM_GROUPS, 512, D_STATE), BF16),
        scratch_shapes=[pltpu.VMEM((N_SSM_GROUPS, 512, D_STATE), F32)],
        compiler_params=_params(("arbitrary", "arbitrary")),
        name="ssd_bwd_states",
    )(xc, xc, dt, bias128, alog128, tri_l)


def _ssd_main_kernel(xc_ref, z_ref, dt_ref, hb_ref, bias_ref, alog_ref, tl_ref, tu_ref, dskip_ref, gain_ref,
                     o_ref, hf_ref, y_ref):
    c = pl.program_id(1)

    @pl.when(c == 0)
    def _():
        hf_ref[...] = jnp.zeros_like(hf_ref)

    for ci in range(SSD_MAIN_CHUNKS):
        rows = pl.ds(ci * CHUNK, CHUNK)
        _ssd_main_chunk(xc_ref.at[rows, :], z_ref.at[rows, :], dt_ref.at[rows, :], hb_ref.at[ci], bias_ref, alog_ref,
                        tl_ref, tu_ref, dskip_ref, gain_ref, o_ref.at[rows, :], hf_ref, y_ref)


def _ssd_main_chunk(xc_ref, z_ref, dt_ref, hb_ref, bias_ref, alog_ref, tl_ref, tu_ref, dskip_ref, gain_ref,
                    o_ref, hf_ref, y_ref):
    dt, rate = _dt_and_rate(dt_ref, bias_ref, alog_ref)
    lane = lax.broadcasted_iota(jnp.int32, (CHUNK, LANES), 1)
    cum = jnp.where(lane < N_SSM_HEADS, _tri_matmul(tl_ref[...], rate), _tri_matmul(tu_ref[...], rate))
    cum_t = cum.T
    dt_t = dt.T
    src_t = cum_t - jnp.log2(dt_t)
    row = lax.broadcasted_iota(jnp.int32, (CHUNK, CHUNK), 0)
    col = lax.broadcasted_iota(jnp.int32, (CHUNK, CHUNK), 1)
    lower = row >= col
    diag = row == col
    low = lane < SSM_HEAD_DIM
    zero_x = jnp.zeros((CHUNK, LANES), BF16)
    nb = N_SSM_HEADS

    def lane_bcast(mat, idx):
        return jnp.broadcast_to(mat[:, idx:idx + 1], (CHUNK, CHUNK))

    def sub_bcast(mat, idx):
        return jnp.broadcast_to(mat[idx:idx + 1, :], (CHUNK, CHUNK))

    def head_matrix(e, cb):
        col_f = lane_bcast(cum, e)
        col_b = lane_bcast(cum, nb + e)
        decay = jnp.exp2(jnp.where(lower, col_f - sub_bcast(src_t, e), col_b - sub_bcast(src_t, nb + e)))
        decay = decay + jnp.where(diag, sub_bcast(dt_t, nb + e), 0.0)
        return (decay * cb).astype(BF16), col_f, col_b

    def group_matmuls(g):
        bg = xc_ref[:, D_INNER + g * D_STATE:D_INNER + (g + 1) * D_STATE]
        cg = xc_ref[:, D_INNER + BC_WIDTH + g * D_STATE:D_INNER + BC_WIDTH + (g + 1) * D_STATE]
        return _dot_nt(cg, bg), _dot_nt(cg, hf_ref[g].astype(BF16)), _dot_nt(cg, hb_ref[g])

    pending = group_matmuls(0)
    for g in range(N_SSM_GROUPS):
        cb, y_in_f, y_in_b = pending
        if g + 1 < N_SSM_GROUPS:
            pending = group_matmuls(g + 1)
        for jp in range(HEADS_PER_GROUP // 2):
            e0 = g * HEADS_PER_GROUP + 2 * jp
            cols = slice(e0 * SSM_HEAD_DIM, e0 * SSM_HEAD_DIM + LANES)
            loc = slice(jp * LANES, (jp + 1) * LANES)
            xs_pair = xc_ref[:, cols]
            m0, cf0, cb0 = head_matrix(e0, cb)
            m1, cf1, cb1 = head_matrix(e0 + 1, cb)
            y = _dot(m0, jnp.where(low, xs_pair, zero_x)) + _dot(m1, jnp.where(low, zero_x, xs_pair))
            y = y + y_in_f[:, loc] * jnp.exp2(jnp.where(low, cf0, cf1))
            y = y + y_in_b[:, loc] * jnp.exp2(jnp.where(low, cb0, cb1))
            y_ref[:, cols] = y + dskip_ref[:, cols] * xs_pair.astype(F32)

    z = z_ref[...].astype(F32)
    y = y_ref[...] * (z * _sigmoid(z))
    ms = jnp.mean(y * y, axis=-1, keepdims=True)
    o_ref[...] = (y * lax.rsqrt(ms + EPS) * gain_ref[...]).astype(BF16)

    last = jnp.broadcast_to(cum_t[:, CHUNK - 1:CHUNK], (LANES, CHUNK))
    w_t = jnp.exp2(last - src_t)
    dec = jnp.exp2(last)
    for g in range(N_SSM_GROUPS):
        xs_t = xc_ref[:, g * 512:(g + 1) * 512].astype(F32).T
        xd = (xs_t * _head_rows(w_t, g * HEADS_PER_GROUP, SSM_HEAD_DIM)).astype(BF16)
        upd = _dot(xd, xc_ref[:, D_INNER + g * D_STATE:D_INNER + (g + 1) * D_STATE])
        hf_ref[g] = _head_rows(dec, g * HEADS_PER_GROUP, SSM_HEAD_DIM) * hf_ref[g] + upd


def _ssd_main(xc, proj, dt, hb, bias128, alog128, tri_l, tri_u, dskip, gain, batch, seq):
    t = xc.shape[0]
    ns = seq // CHUNK // SSD_MAIN_CHUNKS
    rows = SSD_MAIN_CHUNKS * CHUNK

    def tok(b, c):
        return (b * ns + c, 0)

    def const(b, c):
        return (0, 0)

    return pl.pallas_call(
        _ssd_main_kernel,
        grid=(batch, ns),
        in_specs=[
            pl.BlockSpec((rows, CONV_DIM), tok),
            pl.BlockSpec((rows, D_INNER), tok),
            pl.BlockSpec((rows, LANES), tok),
            pl.BlockSpec((SSD_MAIN_CHUNKS, N_SSM_GROUPS, 512, D_STATE), lambda b, c: (b * ns + c, 0, 0, 0)),
            pl.BlockSpec((1, LANES), const),
            pl.BlockSpec((1, LANES), const),
            pl.BlockSpec((CHUNK, CHUNK), const),
            pl.BlockSpec((CHUNK, CHUNK), const),
            pl.BlockSpec((1, D_INNER), const),
            pl.BlockSpec((1, D_INNER), const),
        ],
        out_specs=pl.BlockSpec((rows, D_INNER), tok),
        out_shape=jax.ShapeDtypeStruct((t, D_INNER), BF16),
        scratch_shapes=[pltpu.VMEM((N_SSM_GROUPS, 512, D_STATE), F32), pltpu.VMEM((CHUNK, D_INNER), F32)],
        compiler_params=_params(("arbitrary", "arbitrary")),
        name="ssd_main",
    )(xc, proj, dt, hb, bias128, alog128, tri_l, tri_u, dskip, gain)


def _outproj_kernel(attn_a, ssm_a, gate_a, x_a, attn_b, ssm_b, gate_b, x_b, wa_ref, ws_ref, wo_ref, gn_ref,
                    wr1_ref, wr2_ref, br_ref, o_ref, r_ref, *, n_a):
    i = pl.program_id(0)

    @pl.when(i < n_a)
    def _():
        _outproj_tile(attn_a, ssm_a, gate_a, x_a, wa_ref, ws_ref, wo_ref, gn_ref, wr1_ref, wr2_ref, br_ref,
                      o_ref, r_ref)

    @pl.when(i >= n_a)
    def _():
        _outproj_tile(attn_b, ssm_b, gate_b, x_b, wa_ref, ws_ref, wo_ref, gn_ref, wr1_ref, wr2_ref, br_ref,
                      o_ref, r_ref)


def _outproj_tile(attn_ref, ssm_ref, gate_ref, x_ref, wa_ref, ws_ref, wo_ref, gn_ref, wr1_ref, wr2_ref, br_ref,
                  o_ref, r_ref):
    a_out = _dot(attn_ref[...], wa_ref[...])
    s_out = _dot(ssm_ref[...], ws_ref[...])
    n = TM_OUT // OUT_SPLIT
    x2s = []
    for h in range(OUT_SPLIT):
        rows = slice(h * n, (h + 1) * n)
        ga = gate_ref[rows, :D_MODEL].astype(F32)
        gs = gate_ref[rows, D_MODEL:].astype(F32)
        merged = _sigmoid(ga) * a_out[rows] + _sigmoid(gs) * s_out[rows]
        x2s.append(x_ref[rows, :] + _dot(merged.astype(BF16), wo_ref[...]))
    for h, x2 in enumerate(x2s):
        for j in range(TOKEN_TILE):
            o_ref[pl.ds(h * n * TOKEN_TILE + j, n, stride=TOKEN_TILE), :] = x2[:, j * LANES:(j + 1) * LANES]
        r_ref[:, h * n:(h + 1) * n] = _route(x2, gn_ref, wr1_ref, wr2_ref, br_ref)


def _route(x2, gn_ref, wr1_ref, wr2_ref, br_ref):
    n = x2.shape[0]
    ms = jnp.mean(x2 * x2, axis=-1, keepdims=True)
    hn = x2 * lax.rsqrt(ms + EPS) * gn_ref[...]
    h1 = hn.astype(BF16)
    h2 = (hn - h1.astype(F32)).astype(BF16)
    lg = _dot(h1, wr1_ref[...]) + _dot(h2, wr1_ref[...]) + _dot(h1, wr2_ref[...]) + br_ref[...]

    lane = lax.broadcasted_iota(jnp.int32, (n, LANES), 1).astype(F32)
    big = float(LANES)

    def rmax(v):
        return jnp.max(v, axis=-1, keepdims=True)

    def first_lane(mask):
        return jnp.min(jnp.where(mask, lane, big), axis=-1, keepdims=True)

    gl = jnp.where(lane < N_EXPERT_GROUPS, lg, NEG_INF)
    gmax = rmax(gl)
    g_w = 1.0 / jnp.sum(jnp.exp(gl - gmax), axis=-1, keepdims=True)
    gidx = first_lane(gl == gmax)
    base = N_EXPERT_GROUPS + EXPERTS_PER_GROUP * gidx
    el = jnp.where(lane >= base, jnp.where(lane < base + EXPERTS_PER_GROUP, lg, NEG_INF), NEG_INF)
    m1 = rmax(el)
    i1 = first_lane(el == m1)
    el2 = jnp.where(lane == i1, NEG_INF, el)
    m2 = rmax(el2)
    i2 = first_lane(el2 == m2)
    r = jnp.exp(m2 - m1)
    w1 = g_w / (1.0 + r)
    w2 = w1 * r
    j1 = i1 - base
    j2 = i2 - base
    swap = j1 > j2
    e_lo = jnp.where(swap, j2, j1)
    e_hi = jnp.where(swap, j1, j2)
    w_lo = jnp.where(swap, w2, w1)
    w_hi = jnp.where(swap, w1, w2)
    pair = e_lo * (EXPERTS_PER_GROUP - 1) - e_lo * (e_lo - 1.0) * 0.5 + (e_hi - e_lo - 1.0)
    cls = gidx * N_PAIRS + pair
    rows = jnp.where(lane == 0.0, cls, jnp.where(lane == 1.0, w_lo, jnp.where(lane == 2.0, w_hi, 0.0)))
    return rows.T[:TOKEN_TILE, :]


def _outproj(group_a, group_b, wa, ws, wo, gn, wr1, wr2, br):
    n_a = group_a[3].shape[0] // TM_OUT
    n_b = group_b[3].shape[0] // TM_OUT
    t = (n_a + n_b) * TM_OUT

    def first(i):
        return (jnp.minimum(i, n_a - 1), 0)

    def second(i):
        return (jnp.maximum(i - n_a, 0), 0)

    def const(i):
        return (0, 0)

    def group_specs(tok):
        return [
            pl.BlockSpec((TM_OUT, ATTN_WIDTH), tok),
            pl.BlockSpec((TM_OUT, D_INNER), tok),
            pl.BlockSpec((TM_OUT, 2 * D_MODEL), lambda i: (tok(i)[0], COL_GATE // (2 * D_MODEL))),
            pl.BlockSpec((TM_OUT, D_MODEL), tok),
        ]

    resident = dict(pipeline_mode=pl.Buffered(1))
    return pl.pallas_call(
        functools.partial(_outproj_kernel, n_a=n_a),
        grid=(n_a + n_b,),
        in_specs=group_specs(first) + group_specs(second) + [
            pl.BlockSpec((ATTN_WIDTH, D_MODEL), const, **resident),
            pl.BlockSpec((D_INNER, D_MODEL), const, **resident),
            pl.BlockSpec((D_MODEL, D_MODEL), const, **resident),
            pl.BlockSpec((1, D_MODEL), const),
            pl.BlockSpec((D_MODEL, LANES), const),
            pl.BlockSpec((D_MODEL, LANES), const),
            pl.BlockSpec((1, LANES), const),
        ],
        out_specs=[
            pl.BlockSpec((TM_OUT * TOKEN_TILE, LANES), lambda i: (i, 0)),
            pl.BlockSpec((TOKEN_TILE, TM_OUT), lambda i: (0, i)),
        ],
        out_shape=[
            jax.ShapeDtypeStruct((t * TOKEN_TILE, LANES), F32),
            jax.ShapeDtypeStruct((TOKEN_TILE, t), F32),
        ],
        compiler_params=_params(("arbitrary",)),
        name="outproj_router",
    )(*group_a, *group_b, wa, ws, wo, gn, wr1, wr2, br)


def _moe_kernel(ea_ref, eb_ref, nv_ref, tokc_ref, tokn_ref, roww_ref, x_ref, gn_ref,
                wga_ref, wua_ref, wda_ref, wgb_ref, wub_ref, wdb_ref, o_ref,
                xg_ref, st_ref, gsem, ssem, *, n_blocks):
    i = pl.program_id(0)
    slot = i % 2
    other = 1 - slot

    def tile(idx):
        return pl.ds(pl.multiple_of(idx * TOKEN_TILE, TOKEN_TILE), TOKEN_TILE)

    def gather_copy(tok, r, s):
        return pltpu.make_async_copy(x_ref.at[tile(tok), :], xg_ref.at[s, tile(r), :], gsem.at[s])

    def scatter_copy(tok, r, s):
        return pltpu.make_async_copy(st_ref.at[s, tile(r), :], o_ref.at[tile(tok), :], ssem.at[s])

    def for_rows(n, fn):
        n8 = lax.shift_right_logical(n, 3)

        def body8(g, _):
            for u in range(8):
                fn(g * 8 + u)
            return 0

        def body1(r, _):
            fn(r)
            return 0

        lax.fori_loop(0, n8, body8, 0)
        lax.fori_loop(n8 * 8, n, body1, 0)

    def start_gathers(tok_ref, n, s):
        for_rows(n, lambda r: gather_copy(tok_ref[0, 0, r], r, s).start())

    def wait_gathers(n, s):
        for_rows(n, lambda r: gather_copy(0, 0, s).wait())

    def start_scatters(tok_ref, n, s):
        for_rows(n, lambda r: scatter_copy(tok_ref[0, 0, r], r, s).start())

    def wait_scatters(n, s):
        for_rows(n, lambda r: scatter_copy(0, 0, s).wait())

    @pl.when(i == 0)
    def _():
        xg_ref[...] = jnp.zeros_like(xg_ref)
        start_gathers(tokc_ref, nv_ref[0], 0)

    @pl.when(i + 1 < n_blocks)
    def _():
        start_gathers(tokn_ref, nv_ref[jnp.minimum(i + 1, n_blocks - 1)], other)

    wait_gathers(nv_ref[i], slot)

    @pl.when(i >= 2)
    def _():
        wait_scatters(nv_ref[jnp.maximum(i - 2, 0)], slot)

    @pl.when(nv_ref[i] > 0)
    def _():
        x = jnp.concatenate(
            [xg_ref[slot, pl.ds(j, ROW_BLOCK, stride=TOKEN_TILE), :] for j in range(TOKEN_TILE)], axis=1)
        w_cols = jnp.concatenate([roww_ref[0], jnp.zeros((LANES - TOKEN_TILE, ROW_BLOCK), F32)], axis=0).T
        w_lo = w_cols[:, 0:1]
        w_hi = w_cols[:, 1:2]
        ms = jnp.mean(x * x, axis=-1, keepdims=True)
        hn = (x * lax.rsqrt(ms + EPS) * gn_ref[...]).astype(BF16)

        gate_a, up_a = _dot(hn, wga_ref[0]), _dot(hn, wua_ref[0])
        gate_b, up_b = _dot(hn, wgb_ref[0]), _dot(hn, wub_ref[0])
        h_a = (gate_a * _sigmoid(gate_a) * up_a).astype(BF16)
        h_b = (gate_b * _sigmoid(gate_b) * up_b).astype(BF16)
        out = x + w_lo * _dot(h_a, wda_ref[0]) + w_hi * _dot(h_b, wdb_ref[0])
        for j in range(TOKEN_TILE):
            st_ref[slot, pl.ds(j, ROW_BLOCK, stride=TOKEN_TILE), :] = out[:, j * LANES:(j + 1) * LANES]

    start_scatters(tokc_ref, nv_ref[i], slot)

    @pl.when(i == n_blocks - 1)
    def _():
        wait_scatters(nv_ref[jnp.maximum(i - 1, 0)], other)
        wait_scatters(nv_ref[i], slot)


def _moe(ea, eb, nvalid, row_tok, row_w, x2t, gn, wg, wu, wd):
    n_blocks = row_tok.shape[0]

    def wa(i, ea, eb, nv):
        return (ea[i], 0, 0)

    def wb(i, ea, eb, nv):
        return (eb[i], 0, 0)

    any_spec = pl.BlockSpec(memory_space=pl.ANY)
    grid_spec = pltpu.PrefetchScalarGridSpec(
        num_scalar_prefetch=3,
        grid=(n_blocks,),
        in_specs=[
            pl.BlockSpec((1, 1, ROW_BLOCK), lambda i, ea, eb, nv: (i, 0, 0), memory_space=pltpu.SMEM),
            pl.BlockSpec((1, 1, ROW_BLOCK), lambda i, ea, eb, nv: (jnp.minimum(i + 1, n_blocks - 1), 0, 0),
                         memory_space=pltpu.SMEM),
            pl.BlockSpec((1, TOKEN_TILE, ROW_BLOCK), lambda i, ea, eb, nv: (i, 0, 0)),
            any_spec,
            pl.BlockSpec((1, D_MODEL), lambda i, ea, eb, nv: (0, 0)),
            pl.BlockSpec((1, D_MODEL, D_EXPERT), wa), pl.BlockSpec((1, D_MODEL, D_EXPERT), wa),
            pl.BlockSpec((1, D_EXPERT, D_MODEL), wa),
            pl.BlockSpec((1, D_MODEL, D_EXPERT), wb), pl.BlockSpec((1, D_MODEL, D_EXPERT), wb),
            pl.BlockSpec((1, D_EXPERT, D_MODEL), wb),
        ],
        out_specs=any_spec,
        scratch_shapes=[
            pltpu.VMEM((2, ROW_BLOCK * TOKEN_TILE, LANES), F32),
            pltpu.VMEM((2, ROW_BLOCK * TOKEN_TILE, LANES), F32),
            pltpu.SemaphoreType.DMA((2,)),
            pltpu.SemaphoreType.DMA((2,)),
        ],
    )
    return pl.pallas_call(
        functools.partial(_moe_kernel, n_blocks=n_blocks),
        grid_spec=grid_spec,
        out_shape=jax.ShapeDtypeStruct(x2t.shape, F32),
        compiler_params=_params(("arbitrary",)),
        name="moe",
    )(ea, eb, nvalid, row_tok, row_tok, row_w, x2t, gn, wg, wu, wd, wg, wu, wd)


def _untile_kernel(x_ref, o_ref):
    for j in range(TOKEN_TILE):
        o_ref[:, j * LANES:(j + 1) * LANES] = x_ref[pl.ds(j, TM_UNTILE, stride=TOKEN_TILE), :]


def _untile(y_tiles, first_token, n_tokens):
    first_block = first_token // TM_UNTILE
    return pl.pallas_call(
        _untile_kernel,
        grid=(n_tokens // TM_UNTILE,),
        in_specs=[pl.BlockSpec((TM_UNTILE * TOKEN_TILE, LANES), lambda i: (first_block + i, 0))],
        out_specs=pl.BlockSpec((TM_UNTILE, D_MODEL), lambda i: (i, 0)),
        out_shape=jax.ShapeDtypeStruct((n_tokens, D_MODEL), F32),
        compiler_params=_params(("arbitrary",)),
        name="untile",
    )(y_tiles)


def _pair_tables():
    lo, hi = [], []
    for a in range(EXPERTS_PER_GROUP):
        for b in range(a + 1, EXPERTS_PER_GROUP):
            lo.append(a)
            hi.append(b)
    return np.asarray(lo, np.int32), np.asarray(hi, np.int32)


def _block_tables(rinfo):
    cls = rinfo[0].astype(jnp.int32)
    t = cls.shape[0]
    n_blocks = t // ROW_BLOCK + N_CLASSES
    sorted_cls, order = lax.sort((cls, jnp.arange(t, dtype=jnp.int32)), num_keys=1)
    class_ids = jnp.arange(N_CLASSES + 1, dtype=jnp.int32)
    starts = jnp.sum((sorted_cls[:, None] < class_ids[None, :]).astype(jnp.int32), axis=0)
    counts = starts[1:] - starts[:-1]
    nblk = (counts + ROW_BLOCK - 1) // ROW_BLOCK
    blk_end = jnp.cumsum(nblk)
    blk_start = blk_end - nblk
    used = blk_end[-1]
    b = jnp.arange(n_blocks, dtype=jnp.int32)
    b_eff = jnp.minimum(b, used - 1)
    c = jnp.sum((blk_end[None, :] <= b_eff[:, None]).astype(jnp.int32), axis=1)
    c = jnp.minimum(c, N_CLASSES - 1)
    off = b_eff - blk_start[c]
    src = starts[c] + off * ROW_BLOCK
    nvalid = jnp.where(b < used, jnp.clip(counts[c] - off * ROW_BLOCK, 0, ROW_BLOCK), 0).astype(jnp.int32)
    pair_lo, pair_hi = _pair_tables()
    grp = c // N_PAIRS
    ea = (grp * EXPERTS_PER_GROUP + jnp.asarray(pair_lo)[c % N_PAIRS]).astype(jnp.int32)
    eb = (grp * EXPERTS_PER_GROUP + jnp.asarray(pair_hi)[c % N_PAIRS]).astype(jnp.int32)
    rows = jnp.clip(src[:, None] + jnp.arange(ROW_BLOCK, dtype=jnp.int32)[None, :], 0, t - 1)
    row_tok = order[rows]
    row_w = jnp.concatenate([rinfo[1][row_tok][:, None, :], rinfo[2][row_tok][:, None, :],
                             jnp.zeros((n_blocks, TOKEN_TILE - 2, ROW_BLOCK), F32)], axis=1)
    return ea, eb, nvalid, row_tok.reshape(n_blocks, 1, ROW_BLOCK), row_w


def _rope_tables(seq):
    inv = 1.0 / (ROPE_THETA ** (jnp.arange(0, HEAD_DIM, 2, dtype=F32) / HEAD_DIM))
    ang = jnp.arange(seq, dtype=F32)[:, None] * inv[None, :]
    cos, sin = jnp.cos(ang), jnp.sin(ang)
    cos128 = jnp.concatenate([cos, cos, cos, cos], axis=-1)
    sin128 = jnp.concatenate([-sin, sin, -sin, sin], axis=-1)
    return cos128, sin128


def _prepare_weights(norm_mix, w_in, q_norm, k_norm, attn_sink, conv_w, conv_b, a_log_fwd, a_log_bwd,
                     dt_bias_fwd, dt_bias_bwd, d_skip, ssm_norm, w_out_attn, w_out_ssm, w_o, norm_ffn,
                     w_router_group, b_router_group, w_router_expert, b_router_expert, w_gate, w_up, w_down):
    o_q = 0
    o_k = o_q + ATTN_WIDTH
    o_v = o_k + KV_WIDTH
    o_z = o_v + KV_WIDTH
    o_xbc = o_z + D_INNER
    o_dtf = o_xbc + CONV_DIM
    o_dtb = o_dtf + N_SSM_HEADS
    o_ga = o_dtb + N_SSM_HEADS
    o_gs = o_ga + D_MODEL
    w = w_in.astype(BF16)
    w_r = jnp.concatenate([
        w[:, o_z:o_z + D_INNER], w[:, o_ga:o_gs + D_MODEL], w[:, o_xbc:o_xbc + CONV_DIM],
        w[:, o_q:o_q + ATTN_WIDTH], w[:, o_k:o_k + KV_WIDTH], w[:, o_v:o_v + KV_WIDTH],
        w[:, o_dtf:o_dtb + N_SSM_HEADS], jnp.zeros((D_MODEL, LANES - 2 * N_SSM_HEADS), w.dtype)], axis=1)
    pad64 = jnp.zeros((LANES - 2 * N_SSM_HEADS,), F32)
    eye = np.kron(np.eye(2, dtype=np.float32), np.ones((HEAD_DIM, HEAD_DIM), np.float32))
    idx = np.arange(CHUNK)
    w_router = jnp.concatenate([w_router_group, w_router_expert,
                                jnp.zeros((D_MODEL, LANES - N_EXPERT_GROUPS - N_EXPERTS), F32)], axis=1)
    wr1 = w_router.astype(BF16)
    return dict(
        norm_mix=norm_mix.reshape(1, D_MODEL),
        w_in=w_r,
        qg128=jnp.tile(q_norm, 2).reshape(1, LANES),
        kg128=jnp.tile(k_norm, 2).reshape(1, LANES),
        seg=jnp.asarray(eye, BF16),
        sink=attn_sink.astype(F32),
        conv_w=conv_w,
        conv_b=conv_b.reshape(1, CONV_DIM),
        alog128=jnp.concatenate([a_log_fwd, a_log_bwd, pad64]).reshape(1, LANES),
        bias128=jnp.concatenate([dt_bias_fwd, dt_bias_bwd, pad64]).reshape(1, LANES),
        tri_l=jnp.asarray(idx[:, None] >= idx[None, :], BF16),
        tri_u=jnp.asarray(idx[:, None] <= idx[None, :], BF16),
        dskip=jnp.repeat(d_skip, SSM_HEAD_DIM).reshape(1, D_INNER),
        ssm_norm=ssm_norm.reshape(1, D_INNER),
        wa=w_out_attn.astype(BF16), ws=w_out_ssm.astype(BF16), wo=w_o.astype(BF16),
        norm_ffn=norm_ffn.reshape(1, D_MODEL),
        wr1=wr1, wr2=(w_router - wr1.astype(F32)).astype(BF16),
        br=jnp.concatenate([b_router_group, b_router_expert,
                            jnp.zeros((LANES - N_EXPERT_GROUPS - N_EXPERTS,), F32)]).reshape(1, LANES),
        wg=w_gate.astype(BF16), wu=w_up.astype(BF16), wd=w_down.astype(BF16),
    )


def _mixer(x, p):
    batch, seq, _ = x.shape
    step_rows = (TM_IN, ATTN_QB * ATTN_BLOCK, SSD_BWD_CHUNKS * CHUNK, SSD_MAIN_CHUNKS * CHUNK, CONV_ROWS)
    assert all(seq % rows == 0 for rows in step_rows), "sequence length must be a multiple of every row tile"
    x2d = x.reshape(batch * seq, D_MODEL)
    cos128, sin128 = _rope_tables(seq)
    proj, dt, qr, kdup, vdup = _inproj(x2d, p['norm_mix'], p['w_in'], cos128, sin128, p['qg128'], p['kg128'],
                                       p['seg'], seq)
    attn = _attention(qr, kdup, vdup, p['sink'], batch, seq)
    xc = _conv(proj, p['conv_w'], p['conv_b'], batch, seq)
    hb = _ssd_bwd_states(xc, dt, p['bias128'], p['alog128'], p['tri_l'], batch, seq)
    ssm = _ssd_main(xc, proj, dt, hb, p['bias128'], p['alog128'], p['tri_l'], p['tri_u'], p['dskip'],
                    p['ssm_norm'], batch, seq)
    return attn, ssm, proj, x2d


def kernel(x_prompt, x_sample, norm_mix, w_in, q_norm, k_norm, attn_sink, conv_w, conv_b, a_log_fwd, a_log_bwd,
           dt_bias_fwd, dt_bias_bwd, d_skip, ssm_norm, w_out_attn, w_out_ssm, w_o, norm_ffn, w_router_group,
           b_router_group, w_router_expert, b_router_expert, w_gate, w_up, w_down):
    assert norm_mix.shape[0] == 1, "single-layer encoder"
    p = _prepare_weights(norm_mix[0], w_in[0], q_norm[0], k_norm[0], attn_sink[0], conv_w[0], conv_b[0],
                         a_log_fwd[0], a_log_bwd[0], dt_bias_fwd[0], dt_bias_bwd[0], d_skip[0], ssm_norm[0],
                         w_out_attn[0], w_out_ssm[0], w_o[0], norm_ffn[0], w_router_group[0], b_router_group[0],
                         w_router_expert[0], b_router_expert[0], w_gate[0], w_up[0], w_down[0])
    x2t, rinfo = _outproj(_mixer(x_prompt, p), _mixer(x_sample, p), p['wa'], p['ws'], p['wo'], p['norm_ffn'],
                          p['wr1'], p['wr2'], p['br'])
    ea, eb, nvalid, row_tok, row_w = _block_tables(rinfo)
    y = _moe(ea, eb, nvalid, row_tok, row_w, x2t, p['norm_ffn'], p['wg'], p['wu'], p['wd'])
    t_a = x_prompt.shape[0] * x_prompt.shape[1]
    t_b = x_sample.shape[0] * x_sample.shape[1]
    assert t_a % TM_UNTILE == 0 and t_b % TM_UNTILE == 0 and TM_UNTILE % TM_OUT == 0
    return _untile(y, 0, t_a).reshape(x_prompt.shape), _untile(y, t_a, t_b).reshape(x_sample.shape)
```

```python
import functools

import numpy as np
import jax
import jax.numpy as jnp
from jax import lax
from jax.experimental import pallas as pl
from jax.experimental.pallas import tpu as pltpu

F32 = jnp.float32
BF16 = jnp.bfloat16

D_MODEL = 1024
EPS = 1e-6
NEG_INF = -1e30
LOG2_E = 1.4426950408889634
N_Q_HEADS = 16
N_KV_HEADS = 4
HEAD_DIM = 64
ATTN_WIDTH = N_Q_HEADS * HEAD_DIM
KV_WIDTH = N_KV_HEADS * HEAD_DIM
ATTN_BLOCK = 128
ATTN_QB = 4
ROPE_THETA = 10000.0
D_INNER = 2 * D_MODEL
SSM_HEAD_DIM = 64
N_SSM_HEADS = D_INNER // SSM_HEAD_DIM
N_SSM_GROUPS = 4
HEADS_PER_GROUP = N_SSM_HEADS // N_SSM_GROUPS
D_STATE = 128
BC_WIDTH = N_SSM_GROUPS * D_STATE
CONV_DIM = D_INNER + 2 * BC_WIDTH
CONV_W = 7
CHUNK = 128
SSD_BWD_CHUNKS = 8
SSD_MAIN_CHUNKS = 4
N_EXPERT_GROUPS = 4
EXPERTS_PER_GROUP = 8
N_EXPERTS = N_EXPERT_GROUPS * EXPERTS_PER_GROUP
D_EXPERT = 512
N_PAIRS = EXPERTS_PER_GROUP * (EXPERTS_PER_GROUP - 1) // 2
N_CLASSES = N_EXPERT_GROUPS * N_PAIRS

LANES = 128
V7X_VMEM_LIMIT_BYTES = 56 * 1024 * 1024

COL_Z = 0
COL_GATE = COL_Z + D_INNER
COL_XS = COL_GATE + 2 * D_MODEL
COL_B = COL_XS + D_INNER
COL_C = COL_B + BC_WIDTH
COL_Q = COL_C + BC_WIDTH
COL_K = COL_Q + ATTN_WIDTH
COL_V = COL_K + KV_WIDTH
COL_DT = COL_V + KV_WIDTH
N_PROJ = COL_DT + LANES

TM_IN = 1024
NJ_IN = 3
TN_IN = N_PROJ // NJ_IN
CH_IN = 512
TM_OUT = 512
OUT_SPLIT = 2
TM_UNTILE = 1024
CONV_CT = 512
CONV_ROWS = 256
CONV_PITCH = 2
ROW_BLOCK = 128
TOKEN_TILE = D_MODEL // LANES


def _params(sem):
    return pltpu.CompilerParams(dimension_semantics=sem, vmem_limit_bytes=V7X_VMEM_LIMIT_BYTES)


def _dot(a, b):
    return jnp.dot(a, b, preferred_element_type=F32)


def _dot_nt(a, b):
    return lax.dot_general(a, b, (((1,), (1,)), ((), ())), preferred_element_type=F32)


def _sigmoid(x):
    return 1.0 / (1.0 + jnp.exp(-x))


def _inproj_kernel(x_ref, g_ref, w_ref, cos_ref, sin_ref, qg_ref, kg_ref, seg_ref,
                   o_ref, dt_ref, qo_ref, ko_ref, vo_ref, h_ref):
    j = pl.program_id(1)

    @pl.when(j == 0)
    def _():
        x = x_ref[...]
        ms = jnp.mean(x * x, axis=-1, keepdims=True)
        h_ref[...] = (x * lax.rsqrt(ms + EPS) * g_ref[...]).astype(BF16)

    def project(c0, c1):
        acc = _dot(h_ref[...], w_ref[:, c0:c1])
        o_ref[:, c0:c1] = acc.astype(BF16)
        return acc

    @pl.when(j < NJ_IN - 1)
    def _():
        for c0 in range(0, TN_IN, CH_IN):
            project(c0, min(c0 + CH_IN, TN_IN))

    @pl.when(j == NJ_IN - 1)
    def _():
        base = (NJ_IN - 1) * TN_IN
        q0, k0, d0 = COL_Q - base, COL_K - base, COL_DT - base
        for c0 in range(0, q0, CH_IN):
            project(c0, min(c0 + CH_IN, q0))
        cos = cos_ref[...]
        sin = sin_ref[...]
        seg = seg_ref[...]
        lane = lax.broadcasted_iota(jnp.int32, (TM_IN, LANES), 1)
        first_half = (lane % HEAD_DIM) < (HEAD_DIM // 2)
        low = lane < HEAD_DIM

        def norm_rope(x, gain):
            ss = _dot((x * x).astype(BF16), seg)
            xn = x * lax.rsqrt(ss * (1.0 / HEAD_DIM) + EPS) * gain
            rot = jnp.where(first_half, pltpu.roll(xn, 96, 1), pltpu.roll(xn, 32, 1))
            return xn * cos + rot * sin

        def duplicate(y, dst_ref, s):
            ysw = pltpu.roll(y, HEAD_DIM, 1)
            dst_ref[:, (2 * s) * LANES:(2 * s + 1) * LANES] = jnp.where(low, y, ysw).astype(BF16)
            dst_ref[:, (2 * s + 1) * LANES:(2 * s + 2) * LANES] = jnp.where(low, ysw, y).astype(BF16)

        for c0 in range(q0, k0, CH_IN):
            acc = project(c0, c0 + CH_IN)
            for s in range(CH_IN // LANES):
                y = norm_rope(acc[:, s * LANES:(s + 1) * LANES], qg_ref[...]) * (HEAD_DIM ** -0.5 * LOG2_E)
                dst = c0 - q0 + s * LANES
                qo_ref[:, dst:dst + LANES] = y.astype(BF16)
        acc = project(k0, d0)
        for s in range(KV_WIDTH // LANES):
            duplicate(norm_rope(acc[:, s * LANES:(s + 1) * LANES], kg_ref[...]), ko_ref, s)
            duplicate(acc[:, KV_WIDTH + s * LANES:KV_WIDTH + (s + 1) * LANES], vo_ref, s)
        dt_ref[...] = project(d0, TN_IN)


def _inproj(x2d, gain, w_bf16, cos128, sin128, qg128, kg128, seg, seq):
    t = x2d.shape[0]
    nseq = seq // TM_IN

    def rows(i, j):
        return (i, 0)

    def const(i, j):
        return (0, 0)

    def pos(i, j):
        return (i % nseq, 0)

    return pl.pallas_call(
        _inproj_kernel,
        grid=(t // TM_IN, NJ_IN),
        in_specs=[
            pl.BlockSpec((TM_IN, D_MODEL), rows),
            pl.BlockSpec((1, D_MODEL), const),
            pl.BlockSpec((D_MODEL, TN_IN), lambda i, j: (0, j)),
            pl.BlockSpec((TM_IN, LANES), pos),
            pl.BlockSpec((TM_IN, LANES), pos),
            pl.BlockSpec((1, LANES), const),
            pl.BlockSpec((1, LANES), const),
            pl.BlockSpec((LANES, LANES), const),
        ],
        out_specs=[
            pl.BlockSpec((TM_IN, TN_IN), lambda i, j: (i, j)),
            pl.BlockSpec((TM_IN, LANES), rows),
            pl.BlockSpec((TM_IN, ATTN_WIDTH), rows),
            pl.BlockSpec((TM_IN, 2 * KV_WIDTH), rows),
            pl.BlockSpec((TM_IN, 2 * KV_WIDTH), rows),
        ],
        out_shape=[
            jax.ShapeDtypeStruct((t, N_PROJ), BF16),
            jax.ShapeDtypeStruct((t, LANES), F32),
            jax.ShapeDtypeStruct((t, ATTN_WIDTH), BF16),
            jax.ShapeDtypeStruct((t, 2 * KV_WIDTH), BF16),
            jax.ShapeDtypeStruct((t, 2 * KV_WIDTH), BF16),
        ],
        scratch_shapes=[pltpu.VMEM((TM_IN, D_MODEL), BF16)],
        compiler_params=_params(("arbitrary", "arbitrary")),
        name="inproj",
    )(x2d, gain, w_bf16, cos128, sin128, qg128, kg128, seg)


def _attn_kernel(sink_ref, q_ref, kp_ref, kc_ref, kn_ref, vp_ref, vc_ref, vn_ref, o_ref, *, n_steps):
    i = pl.program_id(1)
    nb = ATTN_BLOCK
    nk = 3 * nb
    qi = lax.broadcasted_iota(jnp.int32, (nb, nk), 0)
    si = lax.broadcasted_iota(jnp.int32, (nb, nk), 1)
    rel = qi - (si - nb)
    band = jnp.where(rel <= nb, jnp.where(rel >= -nb, 0.0, NEG_INF), NEG_INF)
    bias_first = jnp.where(si < nb, jnp.where(i > 0, band, NEG_INF), band)
    bias_last = jnp.where(si >= 2 * nb, jnp.where(i < n_steps - 1, band, NEG_INF), band)
    low_q = lax.broadcasted_iota(jnp.int32, (nb, LANES), 1) < HEAD_DIM
    low_k = lax.broadcasted_iota(jnp.int32, (nk, LANES), 1) < HEAD_DIM
    zero_q = jnp.zeros((nb, LANES), BF16)
    zero_k = jnp.zeros((nk, LANES), BF16)

    def window(p_ref, c_ref, n_ref, h, j):
        sl = slice(h * LANES, (h + 1) * LANES)
        rows = jnp.concatenate([p_ref[:, sl], c_ref[:, sl], n_ref[:, sl]], axis=0)
        return rows[j * nb:j * nb + nk]

    def scores(j, h):
        kd = window(kp_ref, kc_ref, kn_ref, h, j)
        slabs = [q_ref[j * nb:(j + 1) * nb, (2 * h + u) * LANES:(2 * h + u + 1) * LANES] for u in range(2)]
        q4 = jnp.concatenate([jnp.where(low_q, s_, zero_q) for s_ in slabs]
                             + [jnp.where(low_q, zero_q, s_) for s_ in slabs], axis=0)
        return _dot_nt(q4, kd)

    def finish(j, h, s4):
        vd = window(vp_ref, vc_ref, vn_ref, h, j)
        v_lo = jnp.where(low_k, vd, zero_k)
        v_hi = jnp.where(low_k, zero_k, vd)
        heads = (4 * h, 4 * h + 2, 4 * h + 1, 4 * h + 3)
        ps, invs = [], []
        for k, head in enumerate(heads):
            s = s4[k * nb:(k + 1) * nb]
            left = s[:, :nb] + (bias_first if j == 0 else band)[:, :nb]
            right = s[:, 2 * nb:] + (bias_last if j == ATTN_QB - 1 else band)[:, 2 * nb:]
            s = jnp.concatenate([left, s[:, nb:2 * nb], right], axis=1)
            snk = sink_ref[head] * LOG2_E
            m = jnp.maximum(jnp.max(s, axis=-1, keepdims=True), snk)
            p = jnp.exp2(s - m)
            den = jnp.sum(p, axis=-1, keepdims=True) + jnp.exp2(snk - m)
            ps.append(p.astype(BF16))
            invs.append(1.0 / den)
        o = _dot(jnp.concatenate(ps[:2], axis=0), v_lo) + _dot(jnp.concatenate(ps[2:], axis=0), v_hi)
        for u in range(2):
            ou = o[u * nb:(u + 1) * nb] * jnp.where(low_q, invs[u], invs[2 + u])
            o_ref[j * nb:(j + 1) * nb, (2 * h + u) * LANES:(2 * h + u + 1) * LANES] = ou.astype(BF16)

    tasks = [(j, h) for j in range(ATTN_QB) for h in range(N_KV_HEADS)]
    pending = scores(*tasks[0])
    for n, task in enumerate(tasks):
        following = scores(*tasks[n + 1]) if n + 1 < len(tasks) else None
        finish(*task, pending)
        pending = following


def _attention(qr, kdup, vdup, sink, batch, seq):
    t = qr.shape[0]
    nq = seq // ATTN_BLOCK
    n_steps = nq // ATTN_QB
    rows = ATTN_QB * ATTN_BLOCK

    def prev(b, i, s):
        return (b * nq + jnp.maximum(i * ATTN_QB - 1, 0), 0)

    def cur(b, i, s):
        return (b * n_steps + i, 0)

    def nxt(b, i, s):
        return (b * nq + jnp.minimum((i + 1) * ATTN_QB, nq - 1), 0)

    edge = (ATTN_BLOCK, 2 * KV_WIDTH)
    mid = (rows, 2 * KV_WIDTH)
    grid_spec = pltpu.PrefetchScalarGridSpec(
        num_scalar_prefetch=1,
        grid=(batch, n_steps),
        in_specs=[
            pl.BlockSpec((rows, ATTN_WIDTH), cur),
            pl.BlockSpec(edge, prev), pl.BlockSpec(mid, cur), pl.BlockSpec(edge, nxt),
            pl.BlockSpec(edge, prev), pl.BlockSpec(mid, cur), pl.BlockSpec(edge, nxt),
        ],
        out_specs=pl.BlockSpec((rows, ATTN_WIDTH), cur),
    )
    return pl.pallas_call(
        functools.partial(_attn_kernel, n_steps=n_steps),
        grid_spec=grid_spec,
        out_shape=jax.ShapeDtypeStruct((t, ATTN_WIDTH), BF16),
        compiler_params=_params(("arbitrary", "arbitrary")),
        name="attention",
    )(sink, qr, kdup, kdup, kdup, vdup, vdup, vdup)


def _conv_kernel(x_ref, w_ref, b_ref, o_ref, pad_ref, *, seq):
    halo = 8
    step = CONV_PITCH

    def rows(first, n):
        return pl.ds(step * (first + halo), n, stride=step)

    for h in range(CONV_CT // LANES):
        lanes = slice(h * LANES, (h + 1) * LANES)
        pad_ref[h, rows(-halo, halo), :] = jnp.zeros((halo, LANES), F32)
        pad_ref[h, rows(seq, halo), :] = jnp.zeros((halo, LANES), F32)
        for r in range(seq // CONV_ROWS):
            pad_ref[h, rows(r * CONV_ROWS, CONV_ROWS), :] = x_ref[r * CONV_ROWS:(r + 1) * CONV_ROWS, lanes].astype(F32)
    w = w_ref[...]
    bias = b_ref[...]
    for h in range(CONV_CT // LANES):
        lanes = slice(h * LANES, (h + 1) * LANES)
        for r in range(seq // CONV_ROWS):
            r0 = r * CONV_ROWS
            acc = jnp.broadcast_to(bias[:, lanes], (CONV_ROWS, LANES))
            for k in range(CONV_W):
                acc = acc + pad_ref[h, rows(r0 + k - CONV_W // 2, CONV_ROWS), :] * w[k:k + 1, lanes]
            o_ref[r0:r0 + CONV_ROWS, lanes] = (acc * _sigmoid(acc)).astype(BF16)


def _conv(proj, conv_w, conv_b, batch, seq):
    t = proj.shape[0]
    return pl.pallas_call(
        functools.partial(_conv_kernel, seq=seq),
        grid=(batch, CONV_DIM // CONV_CT),
        in_specs=[
            pl.BlockSpec((seq, CONV_CT), lambda b, c: (b, COL_XS // CONV_CT + c)),
            pl.BlockSpec((CONV_W, CONV_CT), lambda b, c: (0, c)),
            pl.BlockSpec((1, CONV_CT), lambda b, c: (0, c)),
        ],
        out_specs=pl.BlockSpec((seq, CONV_CT), lambda b, c: (b, c)),
        out_shape=jax.ShapeDtypeStruct((t, CONV_DIM), BF16),
        scratch_shapes=[pltpu.VMEM((CONV_CT // LANES, CONV_PITCH * (seq + 16), LANES), F32)],
        compiler_params=_params(("arbitrary", "arbitrary")),
        name="conv",
    )(proj, conv_w, conv_b)


def _split3(a):
    a1 = a.astype(BF16)
    r1 = a - a1.astype(F32)
    a2 = r1.astype(BF16)
    a3 = (r1 - a2.astype(F32)).astype(BF16)
    return a1, a2, a3


def _tri_matmul(tri, a):
    a1, a2, a3 = _split3(a)
    return _dot(tri, a1) + _dot(tri, a2) + _dot(tri, a3)


def _softplus(x):
    return jnp.maximum(x, 0.0) + jnp.log(1.0 + jnp.exp(-jnp.abs(x)))


def _dt_and_rate(dt_ref, bias_ref, alog_ref):
    dt = _softplus(dt_ref[...] + bias_ref[...])
    rate = dt * (-LOG2_E * jnp.exp(alog_ref[...]))
    return dt, rate


def _head_rows(mat, first, rows):
    n = mat.shape[1]
    return jnp.concatenate(
        [jnp.broadcast_to(mat[first + e:first + e + 1, :], (rows, n)) for e in range(HEADS_PER_GROUP)], axis=0)


def _ssd_bwd_state_kernel(xs_ref, b_ref, dt_ref, bias_ref, alog_ref, tl_ref, hb_ref, st_ref):
    c = pl.program_id(1)

    @pl.when(c == 0)
    def _():
        st_ref[...] = jnp.zeros_like(st_ref)

    for ci in reversed(range(SSD_BWD_CHUNKS)):
        rows = pl.ds(ci * CHUNK, CHUNK)
        _ssd_bwd_chunk(xs_ref.at[rows, :], b_ref.at[rows, :], dt_ref.at[rows, :], bias_ref, alog_ref, tl_ref,
                       hb_ref.at[ci], st_ref)


def _ssd_bwd_chunk(xs_ref, b_ref, dt_ref, bias_ref, alog_ref, tl_ref, hb_ref, st_ref):
    hb_ref[...] = st_ref[...].astype(BF16)
    dt, rate = _dt_and_rate(dt_ref, bias_ref, alog_ref)
    pre = _tri_matmul(tl_ref[...], rate)
    pre_t = pre.T
    excl_t = (pre - rate).T
    total = jnp.broadcast_to(pre_t[:, CHUNK - 1:CHUNK], (LANES, CHUNK))
    w_t = dt.T * jnp.exp2(excl_t)
    dec = jnp.exp2(total)
    off = N_SSM_HEADS
    for g in range(N_SSM_GROUPS):
        xs_t = xs_ref[:, g * 512:(g + 1) * 512].astype(F32).T
        xd = (xs_t * _head_rows(w_t, off + g * HEADS_PER_GROUP, SSM_HEAD_DIM)).astype(BF16)
        upd = _dot(xd, b_ref[:, g * D_STATE:(g + 1) * D_STATE])
        st_ref[g] = _head_rows(dec, off + g * HEADS_PER_GROUP, SSM_HEAD_DIM) * st_ref[g] + upd


def _ssd_bwd_states(xc, dt, bias128, alog128, tri_l, batch, seq):
    nc = seq // CHUNK
    ns = nc // SSD_BWD_CHUNKS
    rows = SSD_BWD_CHUNKS * CHUNK

    def rev(b, c):
        return (b * ns + ns - 1 - c, 0)

    return pl.pallas_call(
        _ssd_bwd_state_kernel,
        grid=(batch, ns),
        in_specs=[
            pl.BlockSpec((rows, D_INNER), rev),
            pl.BlockSpec((rows, BC_WIDTH), lambda b, c: (b * ns + ns - 1 - c, D_INNER // BC_WIDTH)),
            pl.BlockSpec((rows, LANES), rev),
            pl.BlockSpec((1, LANES), lambda b, c: (0, 0)),
            pl.BlockSpec((1, LANES), lambda b, c: (0, 0)),
            pl.BlockSpec((CHUNK, CHUNK), lambda b, c: (0, 0)),
        ],
        out_specs=pl.BlockSpec((SSD_BWD_CHUNKS, N_SSM_GROUPS, 512, D_STATE),
                               lambda b, c: (b * ns + ns - 1 - c, 0, 0, 0)),
        out_shape=jax.ShapeDtypeStruct((batch * nc, N_SSM_GROUPS, 512, D_STATE), BF16),
        scratch_shapes=[pltpu.VMEM((N_SSM_GROUPS, 512, D_STATE), F32)],
        compiler_params=_params(("arbitrary", "arbitrary")),
        name="ssd_bwd_states",
    )(xc, xc, dt, bias128, alog128, tri_l)


def _ssd_main_kernel(xc_ref, z_ref, dt_ref, hb_ref, bias_ref, alog_ref, tl_ref, tu_ref, dskip_ref, gain_ref,
                     o_ref, hf_ref, y_ref):
    c = pl.program_id(1)

    @pl.when(c == 0)
    def _():
        hf_ref[...] = jnp.zeros_like(hf_ref)

    for ci in range(SSD_MAIN_CHUNKS):
        rows = pl.ds(ci * CHUNK, CHUNK)
        _ssd_main_chunk(xc_ref.at[rows, :], z_ref.at[rows, :], dt_ref.at[rows, :], hb_ref.at[ci], bias_ref, alog_ref,
                        tl_ref, tu_ref, dskip_ref, gain_ref, o_ref.at[rows, :], hf_ref, y_ref)


def _ssd_main_chunk(xc_ref, z_ref, dt_ref, hb_ref, bias_ref, alog_ref, tl_ref, tu_ref, dskip_ref, gain_ref,
                    o_ref, hf_ref, y_ref):
    dt, rate = _dt_and_rate(dt_ref, bias_ref, alog_ref)
    lane = lax.broadcasted_iota(jnp.int32, (CHUNK, LANES), 1)
    cum = jnp.where(lane < N_SSM_HEADS, _tri_matmul(tl_ref[...], rate), _tri_matmul(tu_ref[...], rate))
    cum_t = cum.T
    dt_t = dt.T
    src_t = cum_t - jnp.log2(dt_t)
    row = lax.broadcasted_iota(jnp.int32, (CHUNK, CHUNK), 0)
    col = lax.broadcasted_iota(jnp.int32, (CHUNK, CHUNK), 1)
    lower = row >= col
    diag = row == col
    low = lane < SSM_HEAD_DIM
    zero_x = jnp.zeros((CHUNK, LANES), BF16)
    nb = N_SSM_HEADS

    def lane_bcast(mat, idx):
        return jnp.broadcast_to(mat[:, idx:idx + 1], (CHUNK, CHUNK))

    def sub_bcast(mat, idx):
        return jnp.broadcast_to(mat[idx:idx + 1, :], (CHUNK, CHUNK))

    def head_matrix(e, cb):
        col_f = lane_bcast(cum, e)
        col_b = lane_bcast(cum, nb + e)
        decay = jnp.exp2(jnp.where(lower, col_f - sub_bcast(src_t, e), col_b - sub_bcast(src_t, nb + e)))
        decay = decay + jnp.where(diag, sub_bcast(dt_t, nb + e), 0.0)
        return (decay * cb).astype(BF16), col_f, col_b

    def group_matmuls(g):
        bg = xc_ref[:, D_INNER + g * D_STATE:D_INNER + (g + 1) * D_STATE]
        cg = xc_ref[:, D_INNER + BC_WIDTH + g * D_STATE:D_INNER + BC_WIDTH + (g + 1) * D_STATE]
        return _dot_nt(cg, bg), _dot_nt(cg, hf_ref[g].astype(BF16)), _dot_nt(cg, hb_ref[g])

    pending = group_matmuls(0)
    for g in range(N_SSM_GROUPS):
        cb, y_in_f, y_in_b = pending
        if g + 1 < N_SSM_GROUPS:
            pending = group_matmuls(g + 1)
        for jp in range(HEADS_PER_GROUP // 2):
            e0 = g * HEADS_PER_GROUP + 2 * jp
            cols = slice(e0 * SSM_HEAD_DIM, e0 * SSM_HEAD_DIM + LANES)
            loc = slice(jp * LANES, (jp + 1) * LANES)
            xs_pair = xc_ref[:, cols]
            m0, cf0, cb0 = head_matrix(e0, cb)
            m1, cf1, cb1 = head_matrix(e0 + 1, cb)
            y = _dot(m0, jnp.where(low, xs_pair, zero_x)) + _dot(m1, jnp.where(low, zero_x, xs_pair))
            y = y + y_in_f[:, loc] * jnp.exp2(jnp.where(low, cf0, cf1))
            y = y + y_in_b[:, loc] * jnp.exp2(jnp.where(low, cb0, cb1))
            y_ref[:, cols] = y + dskip_ref[:, cols] * xs_pair.astype(F32)

    z = z_ref[...].astype(F32)
    y = y_ref[...] * (z * _sigmoid(z))
    ms = jnp.mean(y * y, axis=-1, keepdims=True)
    o_ref[...] = (y * lax.rsqrt(ms + EPS) * gain_ref[...]).astype(BF16)

    last = jnp.broadcast_to(cum_t[:, CHUNK - 1:CHUNK], (LANES, CHUNK))
    w_t = jnp.exp2(last - src_t)
    dec = jnp.exp2(last)
    for g in range(N_SSM_GROUPS):
        xs_t = xc_ref[:, g * 512:(g + 1) * 512].astype(F32).T
        xd = (xs_t * _head_rows(w_t, g * HEADS_PER_GROUP, SSM_HEAD_DIM)).astype(BF16)
        upd = _dot(xd, xc_ref[:, D_INNER + g * D_STATE:D_INNER + (g + 1) * D_STATE])
        hf_ref[g] = _head_rows(dec, g * HEADS_PER_GROUP, SSM_HEAD_DIM) * hf_ref[g] + upd


def _ssd_main(xc, proj, dt, hb, bias128, alog128, tri_l, tri_u, dskip, gain, batch, seq):
    t = xc.shape[0]
    ns = seq // CHUNK // SSD_MAIN_CHUNKS
    rows = SSD_MAIN_CHUNKS * CHUNK

    def tok(b, c):
        return (b * ns + c, 0)

    def const(b, c):
        return (0, 0)

    return pl.pallas_call(
        _ssd_main_kernel,
        grid=(batch, ns),
        in_specs=[
            pl.BlockSpec((rows, CONV_DIM), tok),
            pl.BlockSpec((rows, D_INNER), tok),
            pl.BlockSpec((rows, LANES), tok),
            pl.BlockSpec((SSD_MAIN_CHUNKS, N_SSM_GROUPS, 512, D_STATE), lambda b, c: (b * ns + c, 0, 0, 0)),
            pl.BlockSpec((1, LANES), const),
            pl.BlockSpec((1, LANES), const),
            pl.BlockSpec((CHUNK, CHUNK), const),
            pl.BlockSpec((CHUNK, CHUNK), const),
            pl.BlockSpec((1, D_INNER), const),
            pl.BlockSpec((1, D_INNER), const),
        ],
        out_specs=pl.BlockSpec((rows, D_INNER), tok),
        out_shape=jax.ShapeDtypeStruct((t, D_INNER), BF16),
        scratch_shapes=[pltpu.VMEM((N_SSM_GROUPS, 512, D_STATE), F32), pltpu.VMEM((CHUNK, D_INNER), F32)],
        compiler_params=_params(("arbitrary", "arbitrary")),
        name="ssd_main",
    )(xc, proj, dt, hb, bias128, alog128, tri_l, tri_u, dskip, gain)


def _outproj_kernel(attn_a, ssm_a, gate_a, x_a, attn_b, ssm_b, gate_b, x_b, wa_ref, ws_ref, wo_ref, gn_ref,
                    wr1_ref, wr2_ref, br_ref, o_ref, r_ref, *, n_a):
    i = pl.program_id(0)

    @pl.when(i < n_a)
    def _():
        _outproj_tile(attn_a, ssm_a, gate_a, x_a, wa_ref, ws_ref, wo_ref, gn_ref, wr1_ref, wr2_ref, br_ref,
                      o_ref, r_ref)

    @pl.when(i >= n_a)
    def _():
        _outproj_tile(attn_b, ssm_b, gate_b, x_b, wa_ref, ws_ref, wo_ref, gn_ref, wr1_ref, wr2_ref, br_ref,
                      o_ref, r_ref)


def _outproj_tile(attn_ref, ssm_ref, gate_ref, x_ref, wa_ref, ws_ref, wo_ref, gn_ref, wr1_ref, wr2_ref, br_ref,
                  o_ref, r_ref):
    a_out = _dot(attn_ref[...], wa_ref[...])
    s_out = _dot(ssm_ref[...], ws_ref[...])
    n = TM_OUT // OUT_SPLIT
    x2s = []
    for h in range(OUT_SPLIT):
        rows = slice(h * n, (h + 1) * n)
        ga = gate_ref[rows, :D_MODEL].astype(F32)
        gs = gate_ref[rows, D_MODEL:].astype(F32)
        merged = _sigmoid(ga) * a_out[rows] + _sigmoid(gs) * s_out[rows]
        x2s.append(x_ref[rows, :] + _dot(merged.astype(BF16), wo_ref[...]))
    for h, x2 in enumerate(x2s):
        for j in range(TOKEN_TILE):
            o_ref[pl.ds(h * n * TOKEN_TILE + j, n, stride=TOKEN_TILE), :] = x2[:, j * LANES:(j + 1) * LANES]
        r_ref[:, h * n:(h + 1) * n] = _route(x2, gn_ref, wr1_ref, wr2_ref, br_ref)


def _route(x2, gn_ref, wr1_ref, wr2_ref, br_ref):
    n = x2.shape[0]
    ms = jnp.mean(x2 * x2, axis=-1, keepdims=True)
    hn = x2 * lax.rsqrt(ms + EPS) * gn_ref[...]
    h1 = hn.astype(BF16)
    h2 = (hn - h1.astype(F32)).astype(BF16)
    lg = _dot(h1, wr1_ref[...]) + _dot(h2, wr1_ref[...]) + _dot(h1, wr2_ref[...]) + br_ref[...]

    lane = lax.broadcasted_iota(jnp.int32, (n, LANES), 1).astype(F32)
    big = float(LANES)

    def rmax(v):
        return jnp.max(v, axis=-1, keepdims=True)

    def first_lane(mask):
        return jnp.min(jnp.where(mask, lane, big), axis=-1, keepdims=True)

    gl = jnp.where(lane < N_EXPERT_GROUPS, lg, NEG_INF)
    gmax = rmax(gl)
    g_w = 1.0 / jnp.sum(jnp.exp(gl - gmax), axis=-1, keepdims=True)
    gidx = first_lane(gl == gmax)
    base = N_EXPERT_GROUPS + EXPERTS_PER_GROUP * gidx
    el = jnp.where(lane >= base, jnp.where(lane < base + EXPERTS_PER_GROUP, lg, NEG_INF), NEG_INF)
    m1 = rmax(el)
    i1 = first_lane(el == m1)
    el2 = jnp.where(lane == i1, NEG_INF, el)
    m2 = rmax(el2)
    i2 = first_lane(el2 == m2)
    r = jnp.exp(m2 - m1)
    w1 = g_w / (1.0 + r)
    w2 = w1 * r
    j1 = i1 - base
    j2 = i2 - base
    swap = j1 > j2
    e_lo = jnp.where(swap, j2, j1)
    e_hi = jnp.where(swap, j1, j2)
    w_lo = jnp.where(swap, w2, w1)
    w_hi = jnp.where(swap, w1, w2)
    pair = e_lo * (EXPERTS_PER_GROUP - 1) - e_lo * (e_lo - 1.0) * 0.5 + (e_hi - e_lo - 1.0)
    cls = gidx * N_PAIRS + pair
    rows = jnp.where(lane == 0.0, cls, jnp.where(lane == 1.0, w_lo, jnp.where(lane == 2.0, w_hi, 0.0)))
    return rows.T[:TOKEN_TILE, :]


def _outproj(group_a, group_b, wa, ws, wo, gn, wr1, wr2, br):
    n_a = group_a[3].shape[0] // TM_OUT
    n_b = group_b[3].shape[0] // TM_OUT
    t = (n_a + n_b) * TM_OUT

    def first(i):
        return (jnp.minimum(i, n_a - 1), 0)

    def second(i):
        return (jnp.maximum(i - n_a, 0), 0)

    def const(i):
        return (0, 0)

    def group_specs(tok):
        return [
            pl.BlockSpec((TM_OUT, ATTN_WIDTH), tok),
            pl.BlockSpec((TM_OUT, D_INNER), tok),
            pl.BlockSpec((TM_OUT, 2 * D_MODEL), lambda i: (tok(i)[0], COL_GATE // (2 * D_MODEL))),
            pl.BlockSpec((TM_OUT, D_MODEL), tok),
        ]

    resident = dict(pipeline_mode=pl.Buffered(1))
    return pl.pallas_call(
        functools.partial(_outproj_kernel, n_a=n_a),
        grid=(n_a + n_b,),
        in_specs=group_specs(first) + group_specs(second) + [
            pl.BlockSpec((ATTN_WIDTH, D_MODEL), const, **resident),
            pl.BlockSpec((D_INNER, D_MODEL), const, **resident),
            pl.BlockSpec((D_MODEL, D_MODEL), const, **resident),
            pl.BlockSpec((1, D_MODEL), const),
            pl.BlockSpec((D_MODEL, LANES), const),
            pl.BlockSpec((D_MODEL, LANES), const),
            pl.BlockSpec((1, LANES), const),
        ],
        out_specs=[
            pl.BlockSpec((TM_OUT * TOKEN_TILE, LANES), lambda i: (i, 0)),
            pl.BlockSpec((TOKEN_TILE, TM_OUT), lambda i: (0, i)),
        ],
        out_shape=[
            jax.ShapeDtypeStruct((t * TOKEN_TILE, LANES), F32),
            jax.ShapeDtypeStruct((TOKEN_TILE, t), F32),
        ],
        compiler_params=_params(("arbitrary",)),
        name="outproj_router",
    )(*group_a, *group_b, wa, ws, wo, gn, wr1, wr2, br)


def _moe_kernel(ea_ref, eb_ref, nv_ref, tokc_ref, tokn_ref, roww_ref, x_ref, gn_ref,
                wga_ref, wua_ref, wda_ref, wgb_ref, wub_ref, wdb_ref, o_ref,
                xg_ref, st_ref, gsem, ssem, *, n_blocks):
    i = pl.program_id(0)
    slot = i % 2
    other = 1 - slot

    def tile(idx):
        return pl.ds(pl.multiple_of(idx * TOKEN_TILE, TOKEN_TILE), TOKEN_TILE)

    def gather_copy(tok, r, s):
        return pltpu.make_async_copy(x_ref.at[tile(tok), :], xg_ref.at[s, tile(r), :], gsem.at[s])

    def scatter_copy(tok, r, s):
        return pltpu.make_async_copy(st_ref.at[s, tile(r), :], o_ref.at[tile(tok), :], ssem.at[s])

    def for_rows(n, fn):
        n8 = lax.shift_right_logical(n, 3)

        def body8(g, _):
            for u in range(8):
                fn(g * 8 + u, u % 2)
            return 0

        def body1(r, _):
            fn(r, 0)
            return 0

        lax.fori_loop(0, n8, body8, 0)
        lax.fori_loop(n8 * 8, n, body1, 0)

    def start_gathers(tok_ref, n, s):
        for_rows(n, lambda r, parity: gather_copy(tok_ref[0, 0, r], r, s).start())

    def wait_gathers(n, s):
        for_rows(n, lambda r, parity: gather_copy(0, 0, s).wait())

    def start_scatters(tok_ref, n, s):
        for_rows(n, lambda r, parity: scatter_copy(tok_ref[0, 0, r], r, s).start(priority=parity))

    def wait_scatters(n, s):
        for_rows(n, lambda r, parity: scatter_copy(0, 0, s).wait())

    @pl.when(i == 0)
    def _():
        xg_ref[...] = jnp.zeros_like(xg_ref)
        start_gathers(tokc_ref, nv_ref[0], 0)

    @pl.when(i + 1 < n_blocks)
    def _():
        start_gathers(tokn_ref, nv_ref[jnp.minimum(i + 1, n_blocks - 1)], other)

    wait_gathers(nv_ref[i], slot)

    @pl.when(i >= 2)
    def _():
        wait_scatters(nv_ref[jnp.maximum(i - 2, 0)], slot)

    @pl.when(nv_ref[i] > 0)
    def _():
        x = jnp.concatenate(
            [xg_ref[slot, pl.ds(j, ROW_BLOCK, stride=TOKEN_TILE), :] for j in range(TOKEN_TILE)], axis=1)
        w_cols = jnp.concatenate([roww_ref[0], jnp.zeros((LANES - TOKEN_TILE, ROW_BLOCK), F32)], axis=0).T
        w_lo = w_cols[:, 0:1]
        w_hi = w_cols[:, 1:2]
        ms = jnp.mean(x * x, axis=-1, keepdims=True)
        hn = (x * lax.rsqrt(ms + EPS) * gn_ref[...]).astype(BF16)

        gate_a, up_a = _dot(hn, wga_ref[0]), _dot(hn, wua_ref[0])
        gate_b, up_b = _dot(hn, wgb_ref[0]), _dot(hn, wub_ref[0])
        h_a = (gate_a * _sigmoid(gate_a) * up_a).astype(BF16)
        h_b = (gate_b * _sigmoid(gate_b) * up_b).astype(BF16)
        out = x + w_lo * _dot(h_a, wda_ref[0]) + w_hi * _dot(h_b, wdb_ref[0])
        for j in range(TOKEN_TILE):
            st_ref[slot, pl.ds(j, ROW_BLOCK, stride=TOKEN_TILE), :] = out[:, j * LANES:(j + 1) * LANES]

    start_scatters(tokc_ref, nv_ref[i], slot)

    @pl.when(i == n_blocks - 1)
    def _():
        wait_scatters(nv_ref[jnp.maximum(i - 1, 0)], other)
        wait_scatters(nv_ref[i], slot)


def _moe(ea, eb, nvalid, row_tok, row_w, x2t, gn, wg, wu, wd):
    n_blocks = row_tok.shape[0]

    def wa(i, ea, eb, nv):
        return (ea[i], 0, 0)

    def wb(i, ea, eb, nv):
        return (eb[i], 0, 0)

    any_spec = pl.BlockSpec(memory_space=pl.ANY)
    grid_spec = pltpu.PrefetchScalarGridSpec(
        num_scalar_prefetch=3,
        grid=(n_blocks,),
        in_specs=[
            pl.BlockSpec((1, 1, ROW_BLOCK), lambda i, ea, eb, nv: (i, 0, 0), memory_space=pltpu.SMEM),
            pl.BlockSpec((1, 1, ROW_BLOCK), lambda i, ea, eb, nv: (jnp.minimum(i + 1, n_blocks - 1), 0, 0),
                         memory_space=pltpu.SMEM),
            pl.BlockSpec((1, TOKEN_TILE, ROW_BLOCK), lambda i, ea, eb, nv: (i, 0, 0)),
            any_spec,
            pl.BlockSpec((1, D_MODEL), lambda i, ea, eb, nv: (0, 0)),
            pl.BlockSpec((1, D_MODEL, D_EXPERT), wa), pl.BlockSpec((1, D_MODEL, D_EXPERT), wa),
            pl.BlockSpec((1, D_EXPERT, D_MODEL), wa),
            pl.BlockSpec((1, D_MODEL, D_EXPERT), wb), pl.BlockSpec((1, D_MODEL, D_EXPERT), wb),
            pl.BlockSpec((1, D_EXPERT, D_MODEL), wb),
        ],
        out_specs=any_spec,
        scratch_shapes=[
            pltpu.VMEM((2, ROW_BLOCK * TOKEN_TILE, LANES), F32),
            pltpu.VMEM((2, ROW_BLOCK * TOKEN_TILE, LANES), F32),
            pltpu.SemaphoreType.DMA((2,)),
            pltpu.SemaphoreType.DMA((2,)),
        ],
    )
    return pl.pallas_call(
        functools.partial(_moe_kernel, n_blocks=n_blocks),
        grid_spec=grid_spec,
        out_shape=jax.ShapeDtypeStruct(x2t.shape, F32),
        compiler_params=_params(("arbitrary",)),
        name="moe",
    )(ea, eb, nvalid, row_tok, row_tok, row_w, x2t, gn, wg, wu, wd, wg, wu, wd)


def _untile_kernel(x_ref, o_ref):
    for j in range(TOKEN_TILE):
        o_ref[:, j * LANES:(j + 1) * LANES] = x_ref[pl.ds(j, TM_UNTILE, stride=TOKEN_TILE), :]


def _untile(y_tiles, first_token, n_tokens):
    first_block = first_token // TM_UNTILE
    return pl.pallas_call(
        _untile_kernel,
        grid=(n_tokens // TM_UNTILE,),
        in_specs=[pl.BlockSpec((TM_UNTILE * TOKEN_TILE, LANES), lambda i: (first_block + i, 0))],
        out_specs=pl.BlockSpec((TM_UNTILE, D_MODEL), lambda i: (i, 0)),
        out_shape=jax.ShapeDtypeStruct((n_tokens, D_MODEL), F32),
        compiler_params=_params(("arbitrary",)),
        name="untile",
    )(y_tiles)


def _pair_tables():
    lo, hi = [], []
    for a in range(EXPERTS_PER_GROUP):
        for b in range(a + 1, EXPERTS_PER_GROUP):
            lo.append(a)
            hi.append(b)
    return np.asarray(lo, np.int32), np.asarray(hi, np.int32)


def _block_tables(rinfo):
    cls = rinfo[0].astype(jnp.int32)
    t = cls.shape[0]
    n_blocks = t // ROW_BLOCK + N_CLASSES
    sorted_cls, order = lax.sort((cls, jnp.arange(t, dtype=jnp.int32)), num_keys=1)
    class_ids = jnp.arange(N_CLASSES + 1, dtype=jnp.int32)
    starts = jnp.sum((sorted_cls[:, None] < class_ids[None, :]).astype(jnp.int32), axis=0)
    counts = starts[1:] - starts[:-1]
    nblk = (counts + ROW_BLOCK - 1) // ROW_BLOCK
    blk_end = jnp.cumsum(nblk)
    blk_start = blk_end - nblk
    used = blk_end[-1]
    b = jnp.arange(n_blocks, dtype=jnp.int32)
    b_eff = jnp.minimum(b, used - 1)
    c = jnp.sum((blk_end[None, :] <= b_eff[:, None]).astype(jnp.int32), axis=1)
    c = jnp.minimum(c, N_CLASSES - 1)
    off = b_eff - blk_start[c]
    src = starts[c] + off * ROW_BLOCK
    nvalid = jnp.where(b < used, jnp.clip(counts[c] - off * ROW_BLOCK, 0, ROW_BLOCK), 0).astype(jnp.int32)
    pair_lo, pair_hi = _pair_tables()
    grp = c // N_PAIRS
    ea = (grp * EXPERTS_PER_GROUP + jnp.asarray(pair_lo)[c % N_PAIRS]).astype(jnp.int32)
    eb = (grp * EXPERTS_PER_GROUP + jnp.asarray(pair_hi)[c % N_PAIRS]).astype(jnp.int32)
    rows = jnp.clip(src[:, None] + jnp.arange(ROW_BLOCK, dtype=jnp.int32)[None, :], 0, t - 1)
    row_tok = order[rows]
    row_w = jnp.concatenate([rinfo[1][row_tok][:, None, :], rinfo[2][row_tok][:, None, :],
                             jnp.zeros((n_blocks, TOKEN_TILE - 2, ROW_BLOCK), F32)], axis=1)
    return ea, eb, nvalid, row_tok.reshape(n_blocks, 1, ROW_BLOCK), row_w


def _rope_tables(seq):
    inv = 1.0 / (ROPE_THETA ** (jnp.arange(0, HEAD_DIM, 2, dtype=F32) / HEAD_DIM))
    ang = jnp.arange(seq, dtype=F32)[:, None] * inv[None, :]
    cos, sin = jnp.cos(ang), jnp.sin(ang)
    cos128 = jnp.concatenate([cos, cos, cos, cos], axis=-1)
    sin128 = jnp.concatenate([-sin, sin, -sin, sin], axis=-1)
    return cos128, sin128


def _prepare_weights(norm_mix, w_in, q_norm, k_norm, attn_sink, conv_w, conv_b, a_log_fwd, a_log_bwd,
                     dt_bias_fwd, dt_bias_bwd, d_skip, ssm_norm, w_out_attn, w_out_ssm, w_o, norm_ffn,
                     w_router_group, b_router_group, w_router_expert, b_router_expert, w_gate, w_up, w_down):
    o_q = 0
    o_k = o_q + ATTN_WIDTH
    o_v = o_k + KV_WIDTH
    o_z = o_v + KV_WIDTH
    o_xbc = o_z + D_INNER
    o_dtf = o_xbc + CONV_DIM
    o_dtb = o_dtf + N_SSM_HEADS
    o_ga = o_dtb + N_SSM_HEADS
    o_gs = o_ga + D_MODEL
    w = w_in.astype(BF16)
    w_r = jnp.concatenate([
        w[:, o_z:o_z + D_INNER], w[:, o_ga:o_gs + D_MODEL], w[:, o_xbc:o_xbc + CONV_DIM],
        w[:, o_q:o_q + ATTN_WIDTH], w[:, o_k:o_k + KV_WIDTH], w[:, o_v:o_v + KV_WIDTH],
        w[:, o_dtf:o_dtb + N_SSM_HEADS], jnp.zeros((D_MODEL, LANES - 2 * N_SSM_HEADS), w.dtype)], axis=1)
    pad64 = jnp.zeros((LANES - 2 * N_SSM_HEADS,), F32)
    eye = np.kron(np.eye(2, dtype=np.float32), np.ones((HEAD_DIM, HEAD_DIM), np.float32))
    idx = np.arange(CHUNK)
    w_router = jnp.concatenate([w_router_group, w_router_expert,
                                jnp.zeros((D_MODEL, LANES - N_EXPERT_GROUPS - N_EXPERTS), F32)], axis=1)
    wr1 = w_router.astype(BF16)
    return dict(
        norm_mix=norm_mix.reshape(1, D_MODEL),
        w_in=w_r,
        qg128=jnp.tile(q_norm, 2).reshape(1, LANES),
        kg128=jnp.tile(k_norm, 2).reshape(1, LANES),
        seg=jnp.asarray(eye, BF16),
        sink=attn_sink.astype(F32),
        conv_w=conv_w,
        conv_b=conv_b.reshape(1, CONV_DIM),
        alog128=jnp.concatenate([a_log_fwd, a_log_bwd, pad64]).reshape(1, LANES),
        bias128=jnp.concatenate([dt_bias_fwd, dt_bias_bwd, pad64]).reshape(1, LANES),
        tri_l=jnp.asarray(idx[:, None] >= idx[None, :], BF16),
        tri_u=jnp.asarray(idx[:, None] <= idx[None, :], BF16),
        dskip=jnp.repeat(d_skip, SSM_HEAD_DIM).reshape(1, D_INNER),
        ssm_norm=ssm_norm.reshape(1, D_INNER),
        wa=w_out_attn.astype(BF16), ws=w_out_ssm.astype(BF16), wo=w_o.astype(BF16),
        norm_ffn=norm_ffn.reshape(1, D_MODEL),
        wr1=wr1, wr2=(w_router - wr1.astype(F32)).astype(BF16),
        br=jnp.concatenate([b_router_group, b_router_expert,
                            jnp.zeros((LANES - N_EXPERT_GROUPS - N_EXPERTS,), F32)]).reshape(1, LANES),
        wg=w_gate.astype(BF16), wu=w_up.astype(BF16), wd=w_down.astype(BF16),
    )


def _mixer(x, p):
    batch, seq, _ = x.shape
    step_rows = (TM_IN, ATTN_QB * ATTN_BLOCK, SSD_BWD_CHUNKS * CHUNK, SSD_MAIN_CHUNKS * CHUNK, CONV_ROWS)
    assert all(seq % rows == 0 for rows in step_rows), "sequence length must be a multiple of every row tile"
    x2d = x.reshape(batch * seq, D_MODEL)
    cos128, sin128 = _rope_tables(seq)
    proj, dt, qr, kdup, vdup = _inproj(x2d, p['norm_mix'], p['w_in'], cos128, sin128, p['qg128'], p['kg128'],
                                       p['seg'], seq)
    attn = _attention(qr, kdup, vdup, p['sink'], batch, seq)
    xc = _conv(proj, p['conv_w'], p['conv_b'], batch, seq)
    hb = _ssd_bwd_states(xc, dt, p['bias128'], p['alog128'], p['tri_l'], batch, seq)
    ssm = _ssd_main(xc, proj, dt, hb, p['bias128'], p['alog128'], p['tri_l'], p['tri_u'], p['dskip'],
                    p['ssm_norm'], batch, seq)
    return attn, ssm, proj, x2d


def kernel(x_prompt, x_sample, norm_mix, w_in, q_norm, k_norm, attn_sink, conv_w, conv_b, a_log_fwd, a_log_bwd,
           dt_bias_fwd, dt_bias_bwd, d_skip, ssm_norm, w_out_attn, w_out_ssm, w_o, norm_ffn, w_router_group,
           b_router_group, w_router_expert, b_router_expert, w_gate, w_up, w_down):
    assert norm_mix.shape[0] == 1, "single-layer encoder"
    p = _prepare_weights(norm_mix[0], w_in[0], q_norm[0], k_norm[0], attn_sink[0], conv_w[0], conv_b[0],
                         a_log_fwd[0], a_log_bwd[0], dt_bias_fwd[0], dt_bias_bwd[0], d_skip[0], ssm_norm[0],
                         w_out_attn[0], w_out_ssm[0], w_o[0], norm_ffn[0], w_router_group[0], b_router_group[0],
                         w_router_expert[0], b_router_expert[0], w_gate[0], w_up[0], w_down[0])
    x2t, rinfo = _outproj(_mixer(x_prompt, p), _mixer(x_sample, p), p['wa'], p['ws'], p['wo'], p['norm_ffn'],
                          p['wr1'], p['wr2'], p['br'])
    ea, eb, nvalid, row_tok, row_w = _block_tables(rinfo)
    y = _moe(ea, eb, nvalid, row_tok, row_w, x2t, p['norm_ffn'], p['wg'], p['wu'], p['wd'])
    t_a = x_prompt.shape[0] * x_prompt.shape[1]
    t_b = x_sample.shape[0] * x_sample.shape[1]
    assert t_a % TM_UNTILE == 0 and t_b % TM_UNTILE == 0 and TM_UNTILE % TM_OUT == 0
    return _untile(y, 0, t_a).reshape(x_prompt.shape), _untile(y, t_a, t_b).reshape(x_sample.shape)
```

```python
import functools

import numpy as np
import jax
import jax.numpy as jnp
from jax import lax
from jax.experimental import pallas as pl
from jax.experimental.pallas import tpu as pltpu

F32 = jnp.float32
BF16 = jnp.bfloat16

D_MODEL = 1024
EPS = 1e-6
NEG_INF = -1e30
LOG2_E = 1.4426950408889634
N_Q_HEADS = 16
N_KV_HEADS = 4
HEAD_DIM = 64
ATTN_WIDTH = N_Q_HEADS * HEAD_DIM
KV_WIDTH = N_KV_HEADS * HEAD_DIM
ATTN_BLOCK = 128
ATTN_QB = 4
ROPE_THETA = 10000.0
D_INNER = 2 * D_MODEL
SSM_HEAD_DIM = 64
N_SSM_HEADS = D_INNER // SSM_HEAD_DIM
N_SSM_GROUPS = 4
HEADS_PER_GROUP = N_SSM_HEADS // N_SSM_GROUPS
D_STATE = 128
BC_WIDTH = N_SSM_GROUPS * D_STATE
CONV_DIM = D_INNER + 2 * BC_WIDTH
CONV_W = 7
CHUNK = 128
SSD_BWD_CHUNKS = 8
SSD_MAIN_CHUNKS = 4
N_EXPERT_GROUPS = 4
EXPERTS_PER_GROUP = 8
N_EXPERTS = N_EXPERT_GROUPS * EXPERTS_PER_GROUP
D_EXPERT = 512
N_PAIRS = EXPERTS_PER_GROUP * (EXPERTS_PER_GROUP - 1) // 2
N_CLASSES = N_EXPERT_GROUPS * N_PAIRS

LANES = 128
V7X_VMEM_LIMIT_BYTES = 56 * 1024 * 1024

COL_Z = 0
COL_GATE = COL_Z + D_INNER
COL_XS = COL_GATE + 2 * D_MODEL
COL_B = COL_XS + D_INNER
COL_C = COL_B + BC_WIDTH
COL_Q = COL_C + BC_WIDTH
COL_K = COL_Q + ATTN_WIDTH
COL_V = COL_K + KV_WIDTH
COL_DT = COL_V + KV_WIDTH
N_PROJ = COL_DT + LANES

TM_IN = 1024
NJ_IN = 3
TN_IN = N_PROJ // NJ_IN
CH_IN = 512
TM_OUT = 512
OUT_SPLIT = 2
TM_UNTILE = 1024
CONV_CT = 512
CONV_ROWS = 256
CONV_PITCH = 2
ROW_BLOCK = 128
TOKEN_TILE = D_MODEL // LANES


def _params(sem):
    return pltpu.CompilerParams(dimension_semantics=sem, vmem_limit_bytes=V7X_VMEM_LIMIT_BYTES)


def _dot(a, b):
    return jnp.dot(a, b, preferred_element_type=F32)


def _dot_nt(a, b):
    return lax.dot_general(a, b, (((1,), (1,)), ((), ())), preferred_element_type=F32)


def _sigmoid(x):
    return 1.0 / (1.0 + jnp.exp(-x))


def _inproj_kernel(x_ref, g_ref, w_ref, cos_ref, sin_ref, qg_ref, kg_ref, seg_ref,
                   o_ref, dt_ref, qo_ref, ko_ref, vo_ref, h_ref):
    j = pl.program_id(1)

    @pl.when(j == 0)
    def _():
        x = x_ref[...]
        ms = jnp.mean(x * x, axis=-1, keepdims=True)
        h_ref[...] = (x * lax.rsqrt(ms + EPS) * g_ref[...]).astype(BF16)

    def project(c0, c1):
        acc = _dot(h_ref[...], w_ref[:, c0:c1])
        o_ref[:, c0:c1] = acc.astype(BF16)
        return acc

    @pl.when(j < NJ_IN - 1)
    def _():
        for c0 in range(0, TN_IN, CH_IN):
            project(c0, min(c0 + CH_IN, TN_IN))

    @pl.when(j == NJ_IN - 1)
    def _():
        base = (NJ_IN - 1) * TN_IN
        q0, k0, d0 = COL_Q - base, COL_K - base, COL_DT - base
        for c0 in range(0, q0, CH_IN):
            project(c0, min(c0 + CH_IN, q0))
        cos = cos_ref[...]
        sin = sin_ref[...]
        seg = seg_ref[...]
        lane = lax.broadcasted_iota(jnp.int32, (TM_IN, LANES), 1)
        first_half = (lane % HEAD_DIM) < (HEAD_DIM // 2)
        low = lane < HEAD_DIM

        def norm_rope(x, gain):
            ss = _dot((x * x).astype(BF16), seg)
            xn = x * lax.rsqrt(ss * (1.0 / HEAD_DIM) + EPS) * gain
            rot = jnp.where(first_half, pltpu.roll(xn, 96, 1), pltpu.roll(xn, 32, 1))
            return xn * cos + rot * sin

        def duplicate(y, dst_ref, s):
            ysw = pltpu.roll(y, HEAD_DIM, 1)
            dst_ref[:, (2 * s) * LANES:(2 * s + 1) * LANES] = jnp.where(low, y, ysw).astype(BF16)
            dst_ref[:, (2 * s + 1) * LANES:(2 * s + 2) * LANES] = jnp.where(low, ysw, y).astype(BF16)

        for c0 in range(q0, k0, CH_IN):
            acc = project(c0, c0 + CH_IN)
            for s in range(CH_IN // LANES):
                y = norm_rope(acc[:, s * LANES:(s + 1) * LANES], qg_ref[...]) * (HEAD_DIM ** -0.5 * LOG2_E)
                dst = c0 - q0 + s * LANES
                qo_ref[:, dst:dst + LANES] = y.astype(BF16)
        acc = project(k0, d0)
        for s in range(KV_WIDTH // LANES):
            duplicate(norm_rope(acc[:, s * LANES:(s + 1) * LANES], kg_ref[...]), ko_ref, s)
            duplicate(acc[:, KV_WIDTH + s * LANES:KV_WIDTH + (s + 1) * LANES], vo_ref, s)
        dt_ref[...] = project(d0, TN_IN)


def _inproj(x2d, gain, w_bf16, cos128, sin128, qg128, kg128, seg, seq):
    t = x2d.shape[0]
    nseq = seq // TM_IN

    def rows(i, j):
        return (i, 0)

    def const(i, j):
        return (0, 0)

    def pos(i, j):
        return (i % nseq, 0)

    return pl.pallas_call(
        _inproj_kernel,
        grid=(t // TM_IN, NJ_IN),
        in_specs=[
            pl.BlockSpec((TM_IN, D_MODEL), rows),
            pl.BlockSpec((1, D_MODEL), const),
            pl.BlockSpec((D_MODEL, TN_IN), lambda i, j: (0, j)),
            pl.BlockSpec((TM_IN, LANES), pos),
            pl.BlockSpec((TM_IN, LANES), pos),
            pl.BlockSpec((1, LANES), const),
            pl.BlockSpec((1, LANES), const),
            pl.BlockSpec((LANES, LANES), const),
        ],
        out_specs=[
            pl.BlockSpec((TM_IN, TN_IN), lambda i, j: (i, j)),
            pl.BlockSpec((TM_IN, LANES), rows),
            pl.BlockSpec((TM_IN, ATTN_WIDTH), rows),
            pl.BlockSpec((TM_IN, 2 * KV_WIDTH), rows),
            pl.BlockSpec((TM_IN, 2 * KV_WIDTH), rows),
        ],
        out_shape=[
            jax.ShapeDtypeStruct((t, N_PROJ), BF16),
            jax.ShapeDtypeStruct((t, LANES), F32),
            jax.ShapeDtypeStruct((t, ATTN_WIDTH), BF16),
            jax.ShapeDtypeStruct((t, 2 * KV_WIDTH), BF16),
            jax.ShapeDtypeStruct((t, 2 * KV_WIDTH), BF16),
        ],
        scratch_shapes=[pltpu.VMEM((TM_IN, D_MODEL), BF16)],
        compiler_params=_params(("arbitrary", "arbitrary")),
        name="inproj",
    )(x2d, gain, w_bf16, cos128, sin128, qg128, kg128, seg)


def _attn_kernel(sink_ref, q_ref, kp_ref, kc_ref, kn_ref, vp_ref, vc_ref, vn_ref, o_ref, *, n_steps):
    i = pl.program_id(1)
    nb = ATTN_BLOCK
    nk = 3 * nb
    qi = lax.broadcasted_iota(jnp.int32, (nb, nk), 0)
    si = lax.broadcasted_iota(jnp.int32, (nb, nk), 1)
    rel = qi - (si - nb)
    band = jnp.where(rel <= nb, jnp.where(rel >= -nb, 0.0, NEG_INF), NEG_INF)
    bias_first = jnp.where(si < nb, jnp.where(i > 0, band, NEG_INF), band)
    bias_last = jnp.where(si >= 2 * nb, jnp.where(i < n_steps - 1, band, NEG_INF), band)
    low_q = lax.broadcasted_iota(jnp.int32, (nb, LANES), 1) < HEAD_DIM
    low_k = lax.broadcasted_iota(jnp.int32, (nk, LANES), 1) < HEAD_DIM
    zero_q = jnp.zeros((nb, LANES), BF16)
    zero_k = jnp.zeros((nk, LANES), BF16)

    def window(p_ref, c_ref, n_ref, h, j):
        sl = slice(h * LANES, (h + 1) * LANES)
        rows = jnp.concatenate([p_ref[:, sl], c_ref[:, sl], n_ref[:, sl]], axis=0)
        return rows[j * nb:j * nb + nk]

    def scores(j, h):
        kd = window(kp_ref, kc_ref, kn_ref, h, j)
        slabs = [q_ref[j * nb:(j + 1) * nb, (2 * h + u) * LANES:(2 * h + u + 1) * LANES] for u in range(2)]
        q4 = jnp.concatenate([jnp.where(low_q, s_, zero_q) for s_ in slabs]
                             + [jnp.where(low_q, zero_q, s_) for s_ in slabs], axis=0)
        return _dot_nt(q4, kd)

    def finish(j, h, s4):
        vd = window(vp_ref, vc_ref, vn_ref, h, j)
        v_lo = jnp.where(low_k, vd, zero_k)
        v_hi = jnp.where(low_k, zero_k, vd)
        heads = (4 * h, 4 * h + 2, 4 * h + 1, 4 * h + 3)
        ps, invs = [], []
        for k, head in enumerate(heads):
            s = s4[k * nb:(k + 1) * nb]
            left = s[:, :nb] + (bias_first if j == 0 else band)[:, :nb]
            right = s[:, 2 * nb:] + (bias_last if j == ATTN_QB - 1 else band)[:, 2 * nb:]
            s = jnp.concatenate([left, s[:, nb:2 * nb], right], axis=1)
            snk = sink_ref[head] * LOG2_E
            m = jnp.maximum(jnp.max(s, axis=-1, keepdims=True), snk)
            p = jnp.exp2(s - m)
            den = jnp.sum(p, axis=-1, keepdims=True) + jnp.exp2(snk - m)
            ps.append(p.astype(BF16))
            invs.append(1.0 / den)
        o = _dot(jnp.concatenate(ps[:2], axis=0), v_lo) + _dot(jnp.concatenate(ps[2:], axis=0), v_hi)
        for u in range(2):
            ou = o[u * nb:(u + 1) * nb] * jnp.where(low_q, invs[u], invs[2 + u])
            o_ref[j * nb:(j + 1) * nb, (2 * h + u) * LANES:(2 * h + u + 1) * LANES] = ou.astype(BF16)

    tasks = [(j, h) for j in range(ATTN_QB) for h in range(N_KV_HEADS)]
    pending = scores(*tasks[0])
    for n, task in enumerate(tasks):
        following = scores(*tasks[n + 1]) if n + 1 < len(tasks) else None
        finish(*task, pending)
        pending = following


def _attention(qr, kdup, vdup, sink, batch, seq):
    t = qr.shape[0]
    nq = seq // ATTN_BLOCK
    n_steps = nq // ATTN_QB
    rows = ATTN_QB * ATTN_BLOCK

    def prev(b, i, s):
        return (b * nq + jnp.maximum(i * ATTN_QB - 1, 0), 0)

    def cur(b, i, s):
        return (b * n_steps + i, 0)

    def nxt(b, i, s):
        return (b * nq + jnp.minimum((i + 1) * ATTN_QB, nq - 1), 0)

    edge = (ATTN_BLOCK, 2 * KV_WIDTH)
    mid = (rows, 2 * KV_WIDTH)
    grid_spec = pltpu.PrefetchScalarGridSpec(
        num_scalar_prefetch=1,
        grid=(batch, n_steps),
        in_specs=[
            pl.BlockSpec((rows, ATTN_WIDTH), cur),
            pl.BlockSpec(edge, prev), pl.BlockSpec(mid, cur), pl.BlockSpec(edge, nxt),
            pl.BlockSpec(edge, prev), pl.BlockSpec(mid, cur), pl.BlockSpec(edge, nxt),
        ],
        out_specs=pl.BlockSpec((rows, ATTN_WIDTH), cur),
    )
    return pl.pallas_call(
        functools.partial(_attn_kernel, n_steps=n_steps),
        grid_spec=grid_spec,
        out_shape=jax.ShapeDtypeStruct((t, ATTN_WIDTH), BF16),
        compiler_params=_params(("arbitrary", "arbitrary")),
        name="attention",
    )(sink, qr, kdup, kdup, kdup, vdup, vdup, vdup)


def _conv_kernel(x_ref, w_ref, b_ref, o_ref, pad_ref, *, seq):
    halo = 8
    step = CONV_PITCH

    def rows(first, n):
        return pl.ds(step * (first + halo), n, stride=step)

    for h in range(CONV_CT // LANES):
        lanes = slice(h * LANES, (h + 1) * LANES)
        pad_ref[h, rows(-halo, halo), :] = jnp.zeros((halo, LANES), F32)
        pad_ref[h, rows(seq, halo), :] = jnp.zeros((halo, LANES), F32)
        for r in range(seq // CONV_ROWS):
            pad_ref[h, rows(r * CONV_ROWS, CONV_ROWS), :] = x_ref[r * CONV_ROWS:(r + 1) * CONV_ROWS, lanes].astype(F32)
    w = w_ref[...]
    bias = b_ref[...]
    for h in range(CONV_CT // LANES):
        lanes = slice(h * LANES, (h + 1) * LANES)
        for r in range(seq // CONV_ROWS):
            r0 = r * CONV_ROWS
            acc = jnp.broadcast_to(bias[:, lanes], (CONV_ROWS, LANES))
            for k in range(CONV_W):
                acc = acc + pad_ref[h, rows(r0 + k - CONV_W // 2, CONV_ROWS), :] * w[k:k + 1, lanes]
            o_ref[r0:r0 + CONV_ROWS, lanes] = (acc * _sigmoid(acc)).astype(BF16)


def _conv(proj, conv_w, conv_b, batch, seq):
    t = proj.shape[0]
    return pl.pallas_call(
        functools.partial(_conv_kernel, seq=seq),
        grid=(batch, CONV_DIM // CONV_CT),
        in_specs=[
            pl.BlockSpec((seq, CONV_CT), lambda b, c: (b, COL_XS // CONV_CT + c)),
            pl.BlockSpec((CONV_W, CONV_CT), lambda b, c: (0, c)),
            pl.BlockSpec((1, CONV_CT), lambda b, c: (0, c)),
        ],
        out_specs=pl.BlockSpec((seq, CONV_CT), lambda b, c: (b, c)),
        out_shape=jax.ShapeDtypeStruct((t, CONV_DIM), BF16),
        scratch_shapes=[pltpu.VMEM((CONV_CT // LANES, CONV_PITCH * (seq + 16), LANES), F32)],
        compiler_params=_params(("arbitrary", "arbitrary")),
        name="conv",
    )(proj, conv_w, conv_b)


def _split3(a):
    a1 = a.astype(BF16)
    r1 = a - a1.astype(F32)
    a2 = r1.astype(BF16)
    a3 = (r1 - a2.astype(F32)).astype(BF16)
    return a1, a2, a3


def _tri_matmul(tri, a):
    a1, a2, a3 = _split3(a)
    return _dot(tri, a1) + _dot(tri, a2) + _dot(tri, a3)


def _softplus(x):
    return jnp.maximum(x, 0.0) + jnp.log(1.0 + jnp.exp(-jnp.abs(x)))


def _dt_and_rate(dt_ref, bias_ref, alog_ref):
    dt = _softplus(dt_ref[...] + bias_ref[...])
    rate = dt * (-LOG2_E * jnp.exp(alog_ref[...]))
    return dt, rate


def _head_rows(mat, first, rows):
    n = mat.shape[1]
    return jnp.concatenate(
        [jnp.broadcast_to(mat[first + e:first + e + 1, :], (rows, n)) for e in range(HEADS_PER_GROUP)], axis=0)


def _ssd_bwd_state_kernel(xs_ref, b_ref, dt_ref, bias_ref, alog_ref, tl_ref, hb_ref, st_ref):
    c = pl.program_id(1)

    @pl.when(c == 0)
    def _():
        st_ref[...] = jnp.zeros_like(st_ref)

    for ci in reversed(range(SSD_BWD_CHUNKS)):
        rows = pl.ds(ci * CHUNK, CHUNK)
        _ssd_bwd_chunk(xs_ref.at[rows, :], b_ref.at[rows, :], dt_ref.at[rows, :], bias_ref, alog_ref, tl_ref,
                       hb_ref.at[ci], st_ref)


def _ssd_bwd_chunk(xs_ref, b_ref, dt_ref, bias_ref, alog_ref, tl_ref, hb_ref, st_ref):
    hb_ref[...] = st_ref[...].astype(BF16)
    dt, rate = _dt_and_rate(dt_ref, bias_ref, alog_ref)
    pre = _tri_matmul(tl_ref[...], rate)
    pre_t = pre.T
    excl_t = (pre - rate).T
    total = jnp.broadcast_to(pre_t[:, CHUNK - 1:CHUNK], (LANES, CHUNK))
    w_t = dt.T * jnp.exp2(excl_t)
    dec = jnp.exp2(total)
    off = N_SSM_HEADS
    for g in range(N_SSM_GROUPS):
        xs_t = xs_ref[:, g * 512:(g + 1) * 512].astype(F32).T
        xd = (xs_t * _head_rows(w_t, off + g * HEADS_PER_GROUP, SSM_HEAD_DIM)).astype(BF16)
        upd = _dot(xd, b_ref[:, g * D_STATE:(g + 1) * D_STATE])
        st_ref[g] = _head_rows(dec, off + g * HEADS_PER_GROUP, SSM_HEAD_DIM) * st_ref[g] + upd


def _ssd_bwd_states(xc, dt, bias128, alog128, tri_l, batch, seq):
    nc = seq // CHUNK
    ns = nc // SSD_BWD_CHUNKS
    rows = SSD_BWD_CHUNKS * CHUNK

    def rev(b, c):
        return (b * ns + ns - 1 - c, 0)

    return pl.pallas_call(
        _ssd_bwd_state_kernel,
        grid=(batch, ns),
        in_specs=[
            pl.BlockSpec((rows, D_INNER), rev),
            pl.BlockSpec((rows, BC_WIDTH), lambda b, c: (b * ns + ns - 1 - c, D_INNER // BC_WIDTH)),
            pl.BlockSpec((rows, LANES), rev),
            pl.BlockSpec((1, LANES), lambda b, c: (0, 0)),
            pl.BlockSpec((1, LANES), lambda b, c: (0, 0)),
            pl.BlockSpec((CHUNK, CHUNK), lambda b, c: (0, 0)),
        ],
        out_specs=pl.BlockSpec((SSD_BWD_CHUNKS, N_SSM_GROUPS, 512, D_STATE),
                               lambda b, c: (b * ns + ns - 1 - c, 0, 0, 0)),
        out_shape=jax.ShapeDtypeStruct((batch * nc, N_SSM_GROUPS, 512, D_STATE), BF16),
        scratch_shapes=[pltpu.VMEM((N_SSM_GROUPS, 512, D_STATE), F32)],
        compiler_params=_params(("arbitrary", "arbitrary")),
        name="ssd_bwd_states",
    )(xc, xc, dt, bias128, alog128, tri_l)


def _ssd_main_kernel(xc_ref, z_ref, dt_ref, hb_ref, bias_ref, alog_ref, tl_ref, tu_ref, dskip_ref, gain_ref,
                     o_ref, hf_ref, y_ref):
    c = pl.program_id(1)

    @pl.when(c == 0)
    def _():
        hf_ref[...] = jnp.zeros_like(hf_ref)

    for ci in range(SSD_MAIN_CHUNKS):
        rows = pl.ds(ci * CHUNK, CHUNK)
        _ssd_main_chunk(xc_ref.at[rows, :], z_ref.at[rows, :], dt_ref.at[rows, :], hb_ref.at[ci], bias_ref, alog_ref,
                        tl_ref, tu_ref, dskip_ref, gain_ref, o_ref.at[rows, :], hf_ref, y_ref)


def _ssd_main_chunk(xc_ref, z_ref, dt_ref, hb_ref, bias_ref, alog_ref, tl_ref, tu_ref, dskip_ref, gain_ref,
                    o_ref, hf_ref, y_ref):
    dt, rate = _dt_and_rate(dt_ref, bias_ref, alog_ref)
    lane = lax.broadcasted_iota(jnp.int32, (CHUNK, LANES), 1)
    cum = jnp.where(lane < N_SSM_HEADS, _tri_matmul(tl_ref[...], rate), _tri_matmul(tu_ref[...], rate))
    cum_t = cum.T
    dt_t = dt.T
    src_t = cum_t - jnp.log2(dt_t)
    row = lax.broadcasted_iota(jnp.int32, (CHUNK, CHUNK), 0)
    col = lax.broadcasted_iota(jnp.int32, (CHUNK, CHUNK), 1)
    lower = row >= col
    diag = row == col
    low = lane < SSM_HEAD_DIM
    zero_x = jnp.zeros((CHUNK, LANES), BF16)
    nb = N_SSM_HEADS

    def lane_bcast(mat, idx):
        return jnp.broadcast_to(mat[:, idx:idx + 1], (CHUNK, CHUNK))

    def sub_bcast(mat, idx):
        return jnp.broadcast_to(mat[idx:idx + 1, :], (CHUNK, CHUNK))

    def head_matrix(e, cb):
        col_f = lane_bcast(cum, e)
        col_b = lane_bcast(cum, nb + e)
        decay = jnp.exp2(jnp.where(lower, col_f - sub_bcast(src_t, e), col_b - sub_bcast(src_t, nb + e)))
        decay = decay + jnp.where(diag, sub_bcast(dt_t, nb + e), 0.0)
        return (decay * cb).astype(BF16), col_f, col_b

    def group_matmuls(g):
        bg = xc_ref[:, D_INNER + g * D_STATE:D_INNER + (g + 1) * D_STATE]
        cg = xc_ref[:, D_INNER + BC_WIDTH + g * D_STATE:D_INNER + BC_WIDTH + (g + 1) * D_STATE]
        return _dot_nt(cg, bg), _dot_nt(cg, hf_ref[g].astype(BF16)), _dot_nt(cg, hb_ref[g])

    pending = group_matmuls(0)
    for g in range(N_SSM_GROUPS):
        cb, y_in_f, y_in_b = pending
        if g + 1 < N_SSM_GROUPS:
            pending = group_matmuls(g + 1)
        for jp in range(HEADS_PER_GROUP // 2):
            e0 = g * HEADS_PER_GROUP + 2 * jp
            cols = slice(e0 * SSM_HEAD_DIM, e0 * SSM_HEAD_DIM + LANES)
            loc = slice(jp * LANES, (jp + 1) * LANES)
            xs_pair = xc_ref[:, cols]
            m0, cf0, cb0 = head_matrix(e0, cb)
            m1, cf1, cb1 = head_matrix(e0 + 1, cb)
            y = _dot(m0, jnp.where(low, xs_pair, zero_x)) + _dot(m1, jnp.where(low, zero_x, xs_pair))
            y = y + y_in_f[:, loc] * jnp.exp2(jnp.where(low, cf0, cf1))
            y = y + y_in_b[:, loc] * jnp.exp2(jnp.where(low, cb0, cb1))
            y_ref[:, cols] = y + dskip_ref[:, cols] * xs_pair.astype(F32)

    z = z_ref[...].astype(F32)
    y = y_ref[...] * (z * _sigmoid(z))
    ms = jnp.mean(y * y, axis=-1, keepdims=True)
    o_ref[...] = (y * lax.rsqrt(ms + EPS) * gain_ref[...]).astype(BF16)

    last = jnp.broadcast_to(cum_t[:, CHUNK - 1:CHUNK], (LANES, CHUNK))
    w_t = jnp.exp2(last - src_t)
    dec = jnp.exp2(last)
    for g in range(N_SSM_GROUPS):
        xs_t = xc_ref[:, g * 512:(g + 1) * 512].astype(F32).T
        xd = (xs_t * _head_rows(w_t, g * HEADS_PER_GROUP, SSM_HEAD_DIM)).astype(BF16)
        upd = _dot(xd, xc_ref[:, D_INNER + g * D_STATE:D_INNER + (g + 1) * D_STATE])
        hf_ref[g] = _head_rows(dec, g * HEADS_PER_GROUP, SSM_HEAD_DIM) * hf_ref[g] + upd


def _ssd_main(xc, proj, dt, hb, bias128, alog128, tri_l, tri_u, dskip, gain, batch, seq):
    t = xc.shape[0]
    ns = seq // CHUNK // SSD_MAIN_CHUNKS
    rows = SSD_MAIN_CHUNKS * CHUNK

    def tok(b, c):
        return (b * ns + c, 0)

    def const(b, c):
        return (0, 0)

    return pl.pallas_call(
        _ssd_main_kernel,
        grid=(batch, ns),
        in_specs=[
            pl.BlockSpec((rows, CONV_DIM), tok),
            pl.BlockSpec((rows, D_INNER), tok),
            pl.BlockSpec((rows, LANES), tok),
            pl.BlockSpec((SSD_MAIN_CHUNKS, N_SSM_GROUPS, 512, D_STATE), lambda b, c: (b * ns + c, 0, 0, 0)),
            pl.BlockSpec((1, LANES), const),
            pl.BlockSpec((1, LANES), const),
            pl.BlockSpec((CHUNK, CHUNK), const),
            pl.BlockSpec((CHUNK, CHUNK), const),
            pl.BlockSpec((1, D_INNER), const),
            pl.BlockSpec((1, D_INNER), const),
        ],
        out_specs=pl.BlockSpec((rows, D_INNER), tok),
        out_shape=jax.ShapeDtypeStruct((t, D_INNER), BF16),
        scratch_shapes=[pltpu.VMEM((N_SSM_GROUPS, 512, D_STATE), F32), pltpu.VMEM((CHUNK, D_INNER), F32)],
        compiler_params=_params(("arbitrary", "arbitrary")),
        name="ssd_main",
    )(xc, proj, dt, hb, bias128, alog128, tri_l, tri_u, dskip, gain)


def _outproj_kernel(attn_a, ssm_a, gate_a, x_a, attn_b, ssm_b, gate_b, x_b, wa_ref, ws_ref, wo_ref, gn_ref,
                    wr1_ref, wr2_ref, br_ref, o_ref, r_ref, *, n_a):
    i = pl.program_id(0)

    @pl.when(i < n_a)
    def _():
        _outproj_tile(attn_a, ssm_a, gate_a, x_a, wa_ref, ws_ref, wo_ref, gn_ref, wr1_ref, wr2_ref, br_ref,
                      o_ref, r_ref)

    @pl.when(i >= n_a)
    def _():
        _outproj_tile(attn_b, ssm_b, gate_b, x_b, wa_ref, ws_ref, wo_ref, gn_ref, wr1_ref, wr2_ref, br_ref,
                      o_ref, r_ref)


def _outproj_tile(attn_ref, ssm_ref, gate_ref, x_ref, wa_ref, ws_ref, wo_ref, gn_ref, wr1_ref, wr2_ref, br_ref,
                  o_ref, r_ref):
    a_out = _dot(attn_ref[...], wa_ref[...])
    s_out = _dot(ssm_ref[...], ws_ref[...])
    n = TM_OUT // OUT_SPLIT
    x2s = []
    for h in range(OUT_SPLIT):
        rows = slice(h * n, (h + 1) * n)
        ga = gate_ref[rows, :D_MODEL].astype(F32)
        gs = gate_ref[rows, D_MODEL:].astype(F32)
        merged = _sigmoid(ga) * a_out[rows] + _sigmoid(gs) * s_out[rows]
        x2s.append(x_ref[rows, :] + _dot(merged.astype(BF16), wo_ref[...]))
    for h, x2 in enumerate(x2s):
        for j in range(TOKEN_TILE):
            o_ref[pl.ds(h * n * TOKEN_TILE + j, n, stride=TOKEN_TILE), :] = x2[:, j * LANES:(j + 1) * LANES]
        r_ref[:, h * n:(h + 1) * n] = _route(x2, gn_ref, wr1_ref, wr2_ref, br_ref)


def _route(x2, gn_ref, wr1_ref, wr2_ref, br_ref):
    n = x2.shape[0]
    ms = jnp.mean(x2 * x2, axis=-1, keepdims=True)
    hn = x2 * lax.rsqrt(ms + EPS) * gn_ref[...]
    h1 = hn.astype(BF16)
    h2 = (hn - h1.astype(F32)).astype(BF16)
    lg = _dot(h1, wr1_ref[...]) + _dot(h2, wr1_ref[...]) + _dot(h1, wr2_ref[...]) + br_ref[...]

    lane = lax.broadcasted_iota(jnp.int32, (n, LANES), 1).astype(F32)
    big = float(LANES)

    def rmax(v):
        return jnp.max(v, axis=-1, keepdims=True)

    def first_lane(mask):
        return jnp.min(jnp.where(mask, lane, big), axis=-1, keepdims=True)

    gl = jnp.where(lane < N_EXPERT_GROUPS, lg, NEG_INF)
    gmax = rmax(gl)
    g_w = 1.0 / jnp.sum(jnp.exp(gl - gmax), axis=-1, keepdims=True)
    gidx = first_lane(gl == gmax)
    base = N_EXPERT_GROUPS + EXPERTS_PER_GROUP * gidx
    el = jnp.where(lane >= base, jnp.where(lane < base + EXPERTS_PER_GROUP, lg, NEG_INF), NEG_INF)
    m1 = rmax(el)
    i1 = first_lane(el == m1)
    el2 = jnp.where(lane == i1, NEG_INF, el)
    m2 = rmax(el2)
    i2 = first_lane(el2 == m2)
    r = jnp.exp(m2 - m1)
    w1 = g_w / (1.0 + r)
    w2 = w1 * r
    j1 = i1 - base
    j2 = i2 - base
    swap = j1 > j2
    e_lo = jnp.where(swap, j2, j1)
    e_hi = jnp.where(swap, j1, j2)
    w_lo = jnp.where(swap, w2, w1)
    w_hi = jnp.where(swap, w1, w2)
    pair = e_lo * (EXPERTS_PER_GROUP - 1) - e_lo * (e_lo - 1.0) * 0.5 + (e_hi - e_lo - 1.0)
    cls = gidx * N_PAIRS + pair
    rows = jnp.where(lane == 0.0, cls, jnp.where(lane == 1.0, w_lo, jnp.where(lane == 2.0, w_hi, 0.0)))
    return rows.T[:TOKEN_TILE, :]


def _outproj(group_a, group_b, wa, ws, wo, gn, wr1, wr2, br):
    n_a = group_a[3].shape[0] // TM_OUT
    n_b = group_b[3].shape[0] // TM_OUT
    t = (n_a + n_b) * TM_OUT

    def first(i):
        return (jnp.minimum(i, n_a - 1), 0)

    def second(i):
        return (jnp.maximum(i - n_a, 0), 0)

    def const(i):
        return (0, 0)

    def group_specs(tok):
        return [
            pl.BlockSpec((TM_OUT, ATTN_WIDTH), tok),
            pl.BlockSpec((TM_OUT, D_INNER), tok),
            pl.BlockSpec((TM_OUT, 2 * D_MODEL), lambda i: (tok(i)[0], COL_GATE // (2 * D_MODEL))),
            pl.BlockSpec((TM_OUT, D_MODEL), tok),
        ]

    resident = dict(pipeline_mode=pl.Buffered(1))
    return pl.pallas_call(
        functools.partial(_outproj_kernel, n_a=n_a),
        grid=(n_a + n_b,),
        in_specs=group_specs(first) + group_specs(second) + [
            pl.BlockSpec((ATTN_WIDTH, D_MODEL), const, **resident),
            pl.BlockSpec((D_INNER, D_MODEL), const, **resident),
            pl.BlockSpec((D_MODEL, D_MODEL), const, **resident),
            pl.BlockSpec((1, D_MODEL), const),
            pl.BlockSpec((D_MODEL, LANES), const),
            pl.BlockSpec((D_MODEL, LANES), const),
            pl.BlockSpec((1, LANES), const),
        ],
        out_specs=[
            pl.BlockSpec((TM_OUT * TOKEN_TILE, LANES), lambda i: (i, 0)),
            pl.BlockSpec((TOKEN_TILE, TM_OUT), lambda i: (0, i)),
        ],
        out_shape=[
            jax.ShapeDtypeStruct((t * TOKEN_TILE, LANES), F32),
            jax.ShapeDtypeStruct((TOKEN_TILE, t), F32),
        ],
        compiler_params=_params(("arbitrary",)),
        name="outproj_router",
    )(*group_a, *group_b, wa, ws, wo, gn, wr1, wr2, br)


def _moe_kernel(ea_ref, eb_ref, nv_ref, tokc_ref, tokn_ref, roww_ref, x_ref, gn_ref,
                wga_ref, wua_ref, wda_ref, wgb_ref, wub_ref, wdb_ref, o_ref,
                xg_ref, st_ref, gsem, ssem, *, n_blocks):
    i = pl.program_id(0)
    slot = i % 2
    other = 1 - slot

    def tile(idx):
        return pl.ds(pl.multiple_of(idx * TOKEN_TILE, TOKEN_TILE), TOKEN_TILE)

    def gather_copy(tok, r, s):
        return pltpu.make_async_copy(x_ref.at[tile(tok), :], xg_ref.at[s, tile(r), :], gsem.at[s])

    def scatter_copy(tok, r, s):
        return pltpu.make_async_copy(st_ref.at[s, tile(r), :], o_ref.at[tile(tok), :], ssem.at[s])

    def for_rows(n, fn):
        n8 = lax.shift_right_logical(n, 3)

        def body8(g, _):
            for u in range(8):
                fn(g * 8 + u, u % 2)
            return 0

        def body1(r, _):
            fn(r, 0)
            return 0

        lax.fori_loop(0, n8, body8, 0)
        lax.fori_loop(n8 * 8, n, body1, 0)

    def start_gathers(tok_ref, n, s):
        for_rows(n, lambda r, parity: gather_copy(tok_ref[0, 0, r], r, s).start(priority=parity))

    def wait_gathers(n, s):
        for_rows(n, lambda r, parity: gather_copy(0, 0, s).wait())

    def start_scatters(tok_ref, n, s):
        for_rows(n, lambda r, parity: scatter_copy(tok_ref[0, 0, r], r, s).start(priority=parity))

    def wait_scatters(n, s):
        for_rows(n, lambda r, parity: scatter_copy(0, 0, s).wait())

    @pl.when(i == 0)
    def _():
        xg_ref[...] = jnp.zeros_like(xg_ref)
        start_gathers(tokc_ref, nv_ref[0], 0)

    @pl.when(i + 1 < n_blocks)
    def _():
        start_gathers(tokn_ref, nv_ref[jnp.minimum(i + 1, n_blocks - 1)], other)

    wait_gathers(nv_ref[i], slot)

    @pl.when(i >= 2)
    def _():
        wait_scatters(nv_ref[jnp.maximum(i - 2, 0)], slot)

    @pl.when(nv_ref[i] > 0)
    def _():
        x = jnp.concatenate(
            [xg_ref[slot, pl.ds(j, ROW_BLOCK, stride=TOKEN_TILE), :] for j in range(TOKEN_TILE)], axis=1)
        w_cols = jnp.concatenate([roww_ref[0], jnp.zeros((LANES - TOKEN_TILE, ROW_BLOCK), F32)], axis=0).T
        w_lo = w_cols[:, 0:1]
        w_hi = w_cols[:, 1:2]
        ms = jnp.mean(x * x, axis=-1, keepdims=True)
        hn = (x * lax.rsqrt(ms + EPS) * gn_ref[...]).astype(BF16)

        gate_a, up_a = _dot(hn, wga_ref[0]), _dot(hn, wua_ref[0])
        gate_b, up_b = _dot(hn, wgb_ref[0]), _dot(hn, wub_ref[0])
        h_a = (gate_a * _sigmoid(gate_a) * up_a).astype(BF16)
        h_b = (gate_b * _sigmoid(gate_b) * up_b).astype(BF16)
        out = x + w_lo * _dot(h_a, wda_ref[0]) + w_hi * _dot(h_b, wdb_ref[0])
        for j in range(TOKEN_TILE):
            st_ref[slot, pl.ds(j, ROW_BLOCK, stride=TOKEN_TILE), :] = out[:, j * LANES:(j + 1) * LANES]

    start_scatters(tokc_ref, nv_ref[i], slot)

    @pl.when(i == n_blocks - 1)
    def _():
        wait_scatters(nv_ref[jnp.maximum(i - 1, 0)], other)
        wait_scatters(nv_ref[i], slot)


def _moe(ea, eb, nvalid, row_tok, row_w, x2t, gn, wg, wu, wd):
    n_blocks = row_tok.shape[0]

    def wa(i, ea, eb, nv):
        return (ea[i], 0, 0)

    def wb(i, ea, eb, nv):
        return (eb[i], 0, 0)

    any_spec = pl.BlockSpec(memory_space=pl.ANY)
    grid_spec = pltpu.PrefetchScalarGridSpec(
        num_scalar_prefetch=3,
        grid=(n_blocks,),
        in_specs=[
            pl.BlockSpec((1, 1, ROW_BLOCK), lambda i, ea, eb, nv: (i, 0, 0), memory_space=pltpu.SMEM),
            pl.BlockSpec((1, 1, ROW_BLOCK), lambda i, ea, eb, nv: (jnp.minimum(i + 1, n_blocks - 1), 0, 0),
                         memory_space=pltpu.SMEM),
            pl.BlockSpec((1, TOKEN_TILE, ROW_BLOCK), lambda i, ea, eb, nv: (i, 0, 0)),
            any_spec,
            pl.BlockSpec((1, D_MODEL), lambda i, ea, eb, nv: (0, 0)),
            pl.BlockSpec((1, D_MODEL, D_EXPERT), wa), pl.BlockSpec((1, D_MODEL, D_EXPERT), wa),
            pl.BlockSpec((1, D_EXPERT, D_MODEL), wa),
            pl.BlockSpec((1, D_MODEL, D_EXPERT), wb), pl.BlockSpec((1, D_MODEL, D_EXPERT), wb),
            pl.BlockSpec((1, D_EXPERT, D_MODEL), wb),
        ],
        out_specs=any_spec,
        scratch_shapes=[
            pltpu.VMEM((2, ROW_BLOCK * TOKEN_TILE, LANES), F32),
            pltpu.VMEM((2, ROW_BLOCK * TOKEN_TILE, LANES), F32),
            pltpu.SemaphoreType.DMA((2,)),
            pltpu.SemaphoreType.DMA((2,)),
        ],
    )
    return pl.pallas_call(
        functools.partial(_moe_kernel, n_blocks=n_blocks),
        grid_spec=grid_spec,
        out_shape=jax.ShapeDtypeStruct(x2t.shape, F32),
        compiler_params=_params(("arbitrary",)),
        name="moe",
    )(ea, eb, nvalid, row_tok, row_tok, row_w, x2t, gn, wg, wu, wd, wg, wu, wd)


def _untile_kernel(x_ref, o_ref):
    for j in range(TOKEN_TILE):
        o_ref[:, j * LANES:(j + 1) * LANES] = x_ref[pl.ds(j, TM_UNTILE, stride=TOKEN_TILE), :]


def _untile(y_tiles, first_token, n_tokens):
    first_block = first_token // TM_UNTILE
    return pl.pallas_call(
        _untile_kernel,
        grid=(n_tokens // TM_UNTILE,),
        in_specs=[pl.BlockSpec((TM_UNTILE * TOKEN_TILE, LANES), lambda i: (first_block + i, 0))],
        out_specs=pl.BlockSpec((TM_UNTILE, D_MODEL), lambda i: (i, 0)),
        out_shape=jax.ShapeDtypeStruct((n_tokens, D_MODEL), F32),
        compiler_params=_params(("arbitrary",)),
        name="untile",
    )(y_tiles)


def _pair_tables():
    lo, hi = [], []
    for a in range(EXPERTS_PER_GROUP):
        for b in range(a + 1, EXPERTS_PER_GROUP):
            lo.append(a)
            hi.append(b)
    return np.asarray(lo, np.int32), np.asarray(hi, np.int32)


def _block_tables(rinfo):
    cls = rinfo[0].astype(jnp.int32)
    t = cls.shape[0]
    n_blocks = t // ROW_BLOCK + N_CLASSES
    sorted_cls, order = lax.sort((cls, jnp.arange(t, dtype=jnp.int32)), num_keys=1)
    class_ids = jnp.arange(N_CLASSES + 1, dtype=jnp.int32)
    starts = jnp.sum((sorted_cls[:, None] < class_ids[None, :]).astype(jnp.int32), axis=0)
    counts = starts[1:] - starts[:-1]
    nblk = (counts + ROW_BLOCK - 1) // ROW_BLOCK
    blk_end = jnp.cumsum(nblk)
    blk_start = blk_end - nblk
    used = blk_end[-1]
    b = jnp.arange(n_blocks, dtype=jnp.int32)
    b_eff = jnp.minimum(b, used - 1)
    c = jnp.sum((blk_end[None, :] <= b_eff[:, None]).astype(jnp.int32), axis=1)
    c = jnp.minimum(c, N_CLASSES - 1)
    off = b_eff - blk_start[c]
    src = starts[c] + off * ROW_BLOCK
    nvalid = jnp.where(b < used, jnp.clip(counts[c] - off * ROW_BLOCK, 0, ROW_BLOCK), 0).astype(jnp.int32)
    pair_lo, pair_hi = _pair_tables()
    grp = c // N_PAIRS
    ea = (grp * EXPERTS_PER_GROUP + jnp.asarray(pair_lo)[c % N_PAIRS]).astype(jnp.int32)
    eb = (grp * EXPERTS_PER_GROUP + jnp.asarray(pair_hi)[c % N_PAIRS]).astype(jnp.int32)
    rows = jnp.clip(src[:, None] + jnp.arange(ROW_BLOCK, dtype=jnp.int32)[None, :], 0, t - 1)
    row_tok = order[rows]
    row_w = jnp.concatenate([rinfo[1][row_tok][:, None, :], rinfo[2][row_tok][:, None, :],
                             jnp.zeros((n_blocks, TOKEN_TILE - 2, ROW_BLOCK), F32)], axis=1)
    return ea, eb, nvalid, row_tok.reshape(n_blocks, 1, ROW_BLOCK), row_w


def _rope_tables(seq):
    inv = 1.0 / (ROPE_THETA ** (jnp.arange(0, HEAD_DIM, 2, dtype=F32) / HEAD_DIM))
    ang = jnp.arange(seq, dtype=F32)[:, None] * inv[None, :]
    cos, sin = jnp.cos(ang), jnp.sin(ang)
    cos128 = jnp.concatenate([cos, cos, cos, cos], axis=-1)
    sin128 = jnp.concatenate([-sin, sin, -sin, sin], axis=-1)
    return cos128, sin128


def _prepare_weights(norm_mix, w_in, q_norm, k_norm, attn_sink, conv_w, conv_b, a_log_fwd, a_log_bwd,
                     dt_bias_fwd, dt_bias_bwd, d_skip, ssm_norm, w_out_attn, w_out_ssm, w_o, norm_ffn,
                     w_router_group, b_router_group, w_router_expert, b_router_expert, w_gate, w_up, w_down):
    o_q = 0
    o_k = o_q + ATTN_WIDTH
    o_v = o_k + KV_WIDTH
    o_z = o_v + KV_WIDTH
    o_xbc = o_z + D_INNER
    o_dtf = o_xbc + CONV_DIM
    o_dtb = o_dtf + N_SSM_HEADS
    o_ga = o_dtb + N_SSM_HEADS
    o_gs = o_ga + D_MODEL
    w = w_in.astype(BF16)
    w_r = jnp.concatenate([
        w[:, o_z:o_z + D_INNER], w[:, o_ga:o_gs + D_MODEL], w[:, o_xbc:o_xbc + CONV_DIM],
        w[:, o_q:o_q + ATTN_WIDTH], w[:, o_k:o_k + KV_WIDTH], w[:, o_v:o_v + KV_WIDTH],
        w[:, o_dtf:o_dtb + N_SSM_HEADS], jnp.zeros((D_MODEL, LANES - 2 * N_SSM_HEADS), w.dtype)], axis=1)
    pad64 = jnp.zeros((LANES - 2 * N_SSM_HEADS,), F32)
    eye = np.kron(np.eye(2, dtype=np.float32), np.ones((HEAD_DIM, HEAD_DIM), np.float32))
    idx = np.arange(CHUNK)
    w_router = jnp.concatenate([w_router_group, w_router_expert,
                                jnp.zeros((D_MODEL, LANES - N_EXPERT_GROUPS - N_EXPERTS), F32)], axis=1)
    wr1 = w_router.astype(BF16)
    return dict(
        norm_mix=norm_mix.reshape(1, D_MODEL),
        w_in=w_r,
        qg128=jnp.tile(q_norm, 2).reshape(1, LANES),
        kg128=jnp.tile(k_norm, 2).reshape(1, LANES),
        seg=jnp.asarray(eye, BF16),
        sink=attn_sink.astype(F32),
        conv_w=conv_w,
        conv_b=conv_b.reshape(1, CONV_DIM),
        alog128=jnp.concatenate([a_log_fwd, a_log_bwd, pad64]).reshape(1, LANES),
        bias128=jnp.concatenate([dt_bias_fwd, dt_bias_bwd, pad64]).reshape(1, LANES),
        tri_l=jnp.asarray(idx[:, None] >= idx[None, :], BF16),
        tri_u=jnp.asarray(idx[:, None] <= idx[None, :], BF16),
        dskip=jnp.repeat(d_skip, SSM_HEAD_DIM).reshape(1, D_INNER),
        ssm_norm=ssm_norm.reshape(1, D_INNER),
        wa=w_out_attn.astype(BF16), ws=w_out_ssm.astype(BF16), wo=w_o.astype(BF16),
        norm_ffn=norm_ffn.reshape(1, D_MODEL),
        wr1=wr1, wr2=(w_router - wr1.astype(F32)).astype(BF16),
        br=jnp.concatenate([b_router_group, b_router_expert,
                            jnp.zeros((LANES - N_EXPERT_GROUPS - N_EXPERTS,), F32)]).reshape(1, LANES),
        wg=w_gate.astype(BF16), wu=w_up.astype(BF16), wd=w_down.astype(BF16),
    )


def _mixer(x, p):
    batch, seq, _ = x.shape
    step_rows = (TM_IN, ATTN_QB * ATTN_BLOCK, SSD_BWD_CHUNKS * CHUNK, SSD_MAIN_CHUNKS * CHUNK, CONV_ROWS)
    assert all(seq % rows == 0 for rows in step_rows), "sequence length must be a multiple of every row tile"
    x2d = x.reshape(batch * seq, D_MODEL)
    cos128, sin128 = _rope_tables(seq)
    proj, dt, qr, kdup, vdup = _inproj(x2d, p['norm_mix'], p['w_in'], cos128, sin128, p['qg128'], p['kg128'],
                                       p['seg'], seq)
    attn = _attention(qr, kdup, vdup, p['sink'], batch, seq)
    xc = _conv(proj, p['conv_w'], p['conv_b'], batch, seq)
    hb = _ssd_bwd_states(xc, dt, p['bias128'], p['alog128'], p['tri_l'], batch, seq)
    ssm = _ssd_main(xc, proj, dt, hb, p['bias128'], p['alog128'], p['tri_l'], p['tri_u'], p['dskip'],
                    p['ssm_norm'], batch, seq)
    return attn, ssm, proj, x2d


def kernel(x_prompt, x_sample, norm_mix, w_in, q_norm, k_norm, attn_sink, conv_w, conv_b, a_log_fwd, a_log_bwd,
           dt_bias_fwd, dt_bias_bwd, d_skip, ssm_norm, w_out_attn, w_out_ssm, w_o, norm_ffn, w_router_group,
           b_router_group, w_router_expert, b_router_expert, w_gate, w_up, w_down):
    assert norm_mix.shape[0] == 1, "single-layer encoder"
    p = _prepare_weights(norm_mix[0], w_in[0], q_norm[0], k_norm[0], attn_sink[0], conv_w[0], conv_b[0],
                         a_log_fwd[0], a_log_bwd[0], dt_bias_fwd[0], dt_bias_bwd[0], d_skip[0], ssm_norm[0],
                         w_out_attn[0], w_out_ssm[0], w_o[0], norm_ffn[0], w_router_group[0], b_router_group[0],
                         w_router_expert[0], b_router_expert[0], w_gate[0], w_up[0], w_down[0])
    x2t, rinfo = _outproj(_mixer(x_prompt, p), _mixer(x_sample, p), p['wa'], p['ws'], p['wo'], p['norm_ffn'],
                          p['wr1'], p['wr2'], p['br'])
    ea, eb, nvalid, row_tok, row_w = _block_tables(rinfo)
    y = _moe(ea, eb, nvalid, row_tok, row_w, x2t, p['norm_ffn'], p['wg'], p['wu'], p['wd'])
    t_a = x_prompt.shape[0] * x_prompt.shape[1]
    t_b = x_sample.shape[0] * x_sample.shape[1]
    assert t_a % TM_UNTILE == 0 and t_b % TM_UNTILE == 0 and TM_UNTILE % TM_OUT == 0
    return _untile(y, 0, t_a).reshape(x_prompt.shape), _untile(y, t_a, t_b).reshape(x_sample.shape)
```
